```python
import jax, jax.numpy as jnp
from jax import lax
import numpy as np

D_MODEL = 1024
BATCH = 32
SEQ = 2048
DEPTH = 1
DEC_BATCH = 32
DEC_SEQ = 32
PAST_LEN = 2048

CHUNK = 64
Q_BLOCK = 128
ATT_WIDTH = D_MODEL // 2
HEAD_DIM = 64
N_HEADS = ATT_WIDTH // HEAD_DIM
IDX_HEADS = 8
IDX_DIM = 64
TOPK_MAX = 256
POOL_WIDTH = D_MODEL - ATT_WIDTH
POOL_WINDOWS = (2, 4, 8, 16)
POOL_GROUPS = 4
POOL_GROUP_WIDTH = POOL_WIDTH // POOL_GROUPS
POOL_PAST = 15
N_EXPERTS = 32
TOP_K = 4
D_FF = D_MODEL
SWIGLU_LIMIT = 7.0
SWIGLU_ALPHA = 1.702
LN_EPS = 1e-5
DEEPNORM_ALPHA = (2 * DEPTH) ** 0.25
DEEPNORM_BETA = (8 * DEPTH) ** -0.25
Q_OFF = 0
K_OFF = Q_OFF + N_HEADS * HEAD_DIM
V_OFF = K_OFF + HEAD_DIM
QI_OFF = V_OFF + HEAD_DIM
KI_OFF = QI_OFF + IDX_HEADS * IDX_DIM
WI_OFF = KI_OFF + IDX_DIM
U_OFF = WI_OFF + IDX_HEADS
IN_COLS = U_OFF + POOL_WIDTH

kernel_name = "hymba_dsa_pool_moe_deepnorm_stream"


def _layer_norm(x, g, b):
    xf = x.astype(jnp.float32)
    mu = jnp.mean(xf, axis=-1, keepdims=True)
    xc = xf - mu
    var = jnp.mean(xc * xc, axis=-1, keepdims=True)
    y = xc * lax.rsqrt(var + LN_EPS)
    return (y * g.astype(jnp.float32) + b.astype(jnp.float32)).astype(x.dtype)


def _split_proj(h, w_in):
    B, T, _ = h.shape
    p = h @ w_in
    q = p[..., Q_OFF:K_OFF].reshape(B, T, N_HEADS, HEAD_DIM)
    k = p[..., K_OFF:V_OFF]
    v = p[..., V_OFF:QI_OFF]
    qi = p[..., QI_OFF:KI_OFF].reshape(B, T, IDX_HEADS, IDX_DIM)
    ki = p[..., KI_OFF:WI_OFF]
    wi = p[..., WI_OFF:U_OFF]
    u = p[..., U_OFF:IN_COLS]
    return q, k, v, qi, ki, wi, u


def _dsa_attend(q, wi, qi, q_pos, k_all, v_all, ki_all, k_pos, topk):
    idx_logits = jnp.einsum('bqhd,bld->bqhl', qi, ki_all).astype(jnp.float32) * (IDX_DIM ** -0.5)
    idx_w = wi.astype(jnp.float32) * (IDX_HEADS ** -0.5)
    score = jnp.einsum('bqh,bqhl->bql', idx_w, jax.nn.relu(idx_logits))
    visible = (k_pos[None, :] // CHUNK) <= (q_pos[:, None] // CHUNK)
    score = jnp.where(visible[None], score, -jnp.inf)
    _, sel = lax.top_k(score, topk)
    gather = jax.vmap(lambda rows, ids: rows[ids])
    k_sel = gather(k_all, sel)
    v_sel = gather(v_all, sel)
    sel_ok = (k_pos[sel] // CHUNK) <= (q_pos[None, :, None] // CHUNK)
    logits = jnp.einsum('bqhd,bqjd->bqhj', q, k_sel).astype(jnp.float32) * (HEAD_DIM ** -0.5)
    logits = jnp.where(sel_ok[:, :, None, :], logits, -jnp.inf)
    probs = jax.nn.softmax(logits, axis=-1).astype(v_sel.dtype)
    out = jnp.einsum('bqhj,bqjd->bqhd', probs, v_sel)
    return out.reshape(out.shape[0], out.shape[1], N_HEADS * HEAD_DIM)


def _dsa_prompt(q, wi, qi, k, v, ki, topk):
    B, S = q.shape[0], q.shape[1]
    nb = S // Q_BLOCK
    pos = jnp.arange(S, dtype=jnp.int32)

    def blocks(a):
        return jnp.moveaxis(a.reshape((B, nb, Q_BLOCK) + a.shape[2:]), 1, 0)

    def one_block(args):
        qb, wb, qib, pb = args
        return _dsa_attend(qb, wb, qib, pb, k, v, ki, pos, topk)

    out = lax.map(one_block, (blocks(q), blocks(wi), blocks(qi), pos.reshape(nb, Q_BLOCK)))
    return jnp.moveaxis(out, 0, 1).reshape(B, S, N_HEADS * HEAD_DIM)


def _pool_mix(u, prefix, pos, pool_w, pool_scale):
    B, T, C = u.shape
    P = prefix.shape[1]
    ext = jnp.concatenate([prefix, u], axis=1).astype(jnp.float32)
    cs = jnp.concatenate([jnp.zeros((B, 1, C), jnp.float32), jnp.cumsum(ext, axis=1)], axis=1)
    means = []
    for g, w in enumerate(POOL_WINDOWS):
        c0, c1 = g * POOL_GROUP_WIDTH, (g + 1) * POOL_GROUP_WIDTH
        wsum = cs[:, P + 1:P + 1 + T, c0:c1] - cs[:, P + 1 - w:P + 1 - w + T, c0:c1]
        count = jnp.minimum(w, pos + 1).astype(jnp.float32)
        means.append(wsum / count[None, :, None])
    diff = (jnp.concatenate(means, axis=-1) - u.astype(jnp.float32)).astype(u.dtype)
    diff = diff.reshape(B, T, POOL_GROUPS, POOL_GROUP_WIDTH)
    y = jnp.einsum('btgc,gcd->btgd', diff, pool_w).reshape(B, T, C)
    return y * pool_scale


def _moe(h, w_router, b_router, w1, b1, w2, b2):
    B, T, D = h.shape
    xt = h.reshape(B * T, D)
    logits = (xt @ w_router + b_router).astype(jnp.float32)
    top_val, top_idx = lax.top_k(logits, TOP_K)
    top_w = jax.nn.softmax(top_val, axis=-1)
    gates = jnp.einsum('nk,nke->ne', top_w, jax.nn.one_hot(top_idx, N_EXPERTS, dtype=jnp.float32)).astype(h.dtype)
    out = jnp.zeros_like(xt)
    for e in range(N_EXPERTS):
        gu = xt @ w1[e] + b1[e]
        gate = jnp.minimum(gu[:, :D_FF], SWIGLU_LIMIT)
        up = jnp.clip(gu[:, D_FF:], -SWIGLU_LIMIT, SWIGLU_LIMIT)
        act = (up + 1) * (gate * jax.nn.sigmoid(SWIGLU_ALPHA * gate))
        out = out + gates[:, e:e + 1] * (act @ w2[e] + b2[e])
    return out.reshape(B, T, D)


def _block_tail(h, att, pool, w_o, ln1_g, ln1_b, w_router, b_router, w1, b1, w2, b2, ln2_g, ln2_b):
    mix = jnp.concatenate([att, pool], axis=-1) @ w_o
    h1 = _layer_norm(DEEPNORM_ALPHA * h + mix, ln1_g, ln1_b)
    m = _moe(h1, w_router, b_router, w1, b1, w2, b2)
    return _layer_norm(DEEPNORM_ALPHA * h1 + m, ln2_g, ln2_b)


def setup_inputs(seed: int = 0) -> dict:
    key = jax.random.key(seed)
    ks = jax.random.split(key, 24)
    f32 = jnp.float32
    nrm = lambda k, shape: jax.random.normal(k, shape, f32)
    w_in = nrm(ks[7], (DEPTH, D_MODEL, IN_COLS)) * D_MODEL ** -0.5
    w_in = w_in.at[:, :, V_OFF:QI_OFF].multiply(DEEPNORM_BETA)
    return {
        'x_prompt': nrm(ks[0], (BATCH, SEQ, D_MODEL)),
        'x_sample': nrm(ks[1], (DEC_BATCH, DEC_SEQ, D_MODEL)),
        'cache_k': nrm(ks[2], (DEPTH, DEC_BATCH, PAST_LEN, HEAD_DIM)),
        'cache_v': nrm(ks[3], (DEPTH, DEC_BATCH, PAST_LEN, HEAD_DIM)),
        'cache_kidx': nrm(ks[4], (DEPTH, DEC_BATCH, PAST_LEN, IDX_DIM)),
        'state_pool': nrm(ks[5], (DEPTH, DEC_BATCH, POOL_PAST, POOL_WIDTH)),
        'ln0_g': 1.0 + 0.02 * nrm(ks[6], (D_MODEL,)),
        'ln0_b': 0.02 * nrm(ks[8], (D_MODEL,)),
        'w_in': w_in,
        'w_o': nrm(ks[9], (DEPTH, D_MODEL, D_MODEL)) * (D_MODEL ** -0.5) * DEEPNORM_BETA,
        'pool_w': nrm(ks[10], (DEPTH, POOL_GROUPS, POOL_GROUP_WIDTH, POOL_GROUP_WIDTH)) * POOL_GROUP_WIDTH ** -0.5,
        'pool_scale': 1.0 + 0.02 * nrm(ks[11], (DEPTH, POOL_WIDTH)),
        'ln1_g': 1.0 + 0.02 * nrm(ks[12], (DEPTH, D_MODEL)),
        'ln1_b': 0.02 * nrm(ks[13], (DEPTH, D_MODEL)),
        'w_router': nrm(ks[14], (DEPTH, D_MODEL, N_EXPERTS)) * D_MODEL ** -0.5,
        'b_router': 0.01 * nrm(ks[15], (DEPTH, N_EXPERTS)),
        'w1': nrm(ks[16], (DEPTH, N_EXPERTS, D_MODEL, 2 * D_FF)) * D_MODEL ** -0.5,
        'b1': 0.01 * nrm(ks[17], (DEPTH, N_EXPERTS, 2 * D_FF)),
        'w2': nrm(ks[18], (DEPTH, N_EXPERTS, D_FF, D_MODEL)) * (D_FF ** -0.5) * DEEPNORM_BETA,
        'b2': 0.01 * nrm(ks[19], (DEPTH, N_EXPERTS, D_MODEL)),
        'ln2_g': 1.0 + 0.02 * nrm(ks[20], (DEPTH, D_MODEL)),
        'ln2_b': 0.02 * nrm(ks[21], (DEPTH, D_MODEL)),
    }


def reference(x_prompt, x_sample, cache_k, cache_v, cache_kidx, state_pool, ln0_g, ln0_b, w_in, w_o,
              pool_w, pool_scale, ln1_g, ln1_b, w_router, b_router, w1, b1, w2, b2, ln2_g, ln2_b):
    hp = _layer_norm(x_prompt, ln0_g, ln0_b)
    hs = _layer_norm(x_sample, ln0_g, ln0_b)
    Bp, S = hp.shape[0], hp.shape[1]
    T = hs.shape[1]
    L_past = cache_k.shape[2]
    topk_p = min(TOPK_MAX, S // 4)
    topk_s = min(TOPK_MAX, (L_past + T) // 4)
    pos_p = jnp.arange(S, dtype=jnp.int32)
    pos_s = L_past + jnp.arange(T, dtype=jnp.int32)
    pos_all = jnp.arange(L_past + T, dtype=jnp.int32)
    kp_l, vp_l, kip_l, pp_l, ks_l, vs_l, kis_l, ps_l = [], [], [], [], [], [], [], []
    for l in range(DEPTH):
        tail = (w_o[l], ln1_g[l], ln1_b[l], w_router[l], b_router[l], w1[l], b1[l], w2[l], b2[l], ln2_g[l], ln2_b[l])
        qp, kp, vp, qip, kip, wip, up = _split_proj(hp, w_in[l])
        att_p = _dsa_prompt(qp, wip, qip, kp, vp, kip, topk_p)
        zero_prefix = jnp.zeros((Bp, POOL_PAST, POOL_WIDTH), up.dtype)
        pool_p = _pool_mix(up, zero_prefix, pos_p, pool_w[l], pool_scale[l])
        kp_l.append(kp)
        vp_l.append(vp)
        kip_l.append(kip)
        pp_l.append(jnp.concatenate([zero_prefix, up], axis=1)[:, -POOL_PAST:])
        hp = _block_tail(hp, att_p, pool_p, *tail)
        qs, kn, vn, qis, kin, wis, us = _split_proj(hs, w_in[l])
        k_all = jnp.concatenate([cache_k[l], kn], axis=1)
        v_all = jnp.concatenate([cache_v[l], vn], axis=1)
        ki_all = jnp.concatenate([cache_kidx[l], kin], axis=1)
        att_s = _dsa_attend(qs, wis, qis, pos_s, k_all, v_all, ki_all, pos_all, topk_s)
        pool_s = _pool_mix(us, state_pool[l], pos_s, pool_w[l], pool_scale[l])
        ks_l.append(kn)
        vs_l.append(vn)
        kis_l.append(kin)
        ps_l.append(jnp.concatenate([state_pool[l], us], axis=1)[:, -POOL_PAST:])
        hs = _block_tail(hs, att_s, pool_s, *tail)
    return (hp, hs, jnp.stack(kp_l), jnp.stack(vp_l), jnp.stack(kip_l), jnp.stack(pp_l),
            jnp.stack(ks_l), jnp.stack(vs_l), jnp.stack(kis_l), jnp.stack(ps_l))
```

```python
import functools

import jax
import jax.numpy as jnp
from jax import lax
from jax.experimental import pallas as pl
from jax.experimental.pallas import tpu as pltpu

F32 = jnp.float32
BF16 = jnp.bfloat16
I32 = jnp.int32

CHUNK = 64
CHUNK_SHIFT = 6
assert 1 << CHUNK_SHIFT == CHUNK
N_HEADS = 8
HEAD_DIM = 64
IDX_HEADS = 8
IDX_DIM = 64
TOPK_MAX = 256
POOL_WINDOWS = (2, 4, 8, 16)
POOL_PAST = 15
N_EXPERTS = 32
TOP_K = 4
SWIGLU_LIMIT = 7.0
SWIGLU_ALPHA = 1.702
LN_EPS = 1e-5
DEPTH = 1
DEEPNORM_ALPHA = (2 * DEPTH) ** 0.25

LANES = 128
SUBLANES = 8
VMEM_LIMIT_BYTES = 56 * 1024 * 1024

NEG_BIG = -1e30
KEY_NEG_INF = -2139095041
KEY_POS_INF = 2139095040

NT_DIMS = (((1,), (1,)), ((), ()))


def _layer_norm(x, g, b):
    mu = jnp.mean(x, axis=-1, keepdims=True)
    xc = x - mu
    var = jnp.mean(xc * xc, axis=-1, keepdims=True)
    return xc * lax.rsqrt(var + LN_EPS) * g + b


def _params(sem):
    return pltpu.CompilerParams(dimension_semantics=sem, vmem_limit_bytes=VMEM_LIMIT_BYTES)


def _proj_kernel(x_ref, g_ref, b_ref, wa_ref, wb_ref, wt_ref,
                 q_ref, qi_ref, u_ref, k_ref, v_ref, ki_ref, kb_ref, kib_ref, vt_ref, wit_ref, *, lc):
    h = _layer_norm(x_ref[...], g_ref[...], b_ref[...])
    hb = h.astype(BF16)
    aw = N_HEADS * HEAD_DIM
    pa = jnp.dot(hb, wa_ref[...], preferred_element_type=F32)
    q_ref[...] = (pa[:, :aw] * (HEAD_DIM ** -0.5)).astype(BF16)
    qi_ref[...] = pa[:, aw:2 * aw].astype(BF16)
    u_ref[...] = pa[:, 2 * aw:]
    pb = jnp.dot(hb, wb_ref[...], preferred_element_type=F32)
    k = pb[:, 0:HEAD_DIM]
    v = pb[:, HEAD_DIM:2 * HEAD_DIM]
    ki = pb[:, 2 * HEAD_DIM:2 * HEAD_DIM + IDX_DIM]
    k_ref[...] = k
    v_ref[...] = v
    ki_ref[...] = ki
    kb_ref[...] = k.astype(BF16)
    kib_ref[...] = ki.astype(BF16)
    pt = lax.dot_general(wt_ref[...], hb, NT_DIMS, preferred_element_type=F32)
    for c in range(vt_ref.shape[0]):
        vt_ref[c] = pt[0:HEAD_DIM, c * lc:(c + 1) * lc].astype(BF16)
    wi = pt[HEAD_DIM:HEAD_DIM + IDX_HEADS, :]
    wit_ref[...] = (wi * (IDX_HEADS ** -0.5)) * (IDX_DIM ** -0.5)


def _proj(x2d, g, b, wa, wb, wt, *, tm, lc):
    n, d = x2d.shape
    tm = min(tm, n)
    aw = N_HEADS * HEAD_DIM
    uw = wa.shape[1] - 2 * aw
    row = lambda i: (i, 0)
    const = lambda i: (0, 0)
    out_shape = (
        jax.ShapeDtypeStruct((n, aw), BF16),
        jax.ShapeDtypeStruct((n, aw), BF16),
        jax.ShapeDtypeStruct((n, uw), F32),
        jax.ShapeDtypeStruct((n, HEAD_DIM), F32),
        jax.ShapeDtypeStruct((n, HEAD_DIM), F32),
        jax.ShapeDtypeStruct((n, IDX_DIM), F32),
        jax.ShapeDtypeStruct((n, HEAD_DIM), BF16),
        jax.ShapeDtypeStruct((n, IDX_DIM), BF16),
        jax.ShapeDtypeStruct((n // lc, HEAD_DIM, lc), BF16),
        jax.ShapeDtypeStruct((IDX_HEADS, n), F32),
    )
    out_specs = (
        pl.BlockSpec((tm, aw), row), pl.BlockSpec((tm, aw), row), pl.BlockSpec((tm, uw), row),
        pl.BlockSpec((tm, HEAD_DIM), row), pl.BlockSpec((tm, HEAD_DIM), row), pl.BlockSpec((tm, IDX_DIM), row),
        pl.BlockSpec((tm, HEAD_DIM), row), pl.BlockSpec((tm, IDX_DIM), row),
        pl.BlockSpec((tm // lc, HEAD_DIM, lc), lambda i: (i, 0, 0)),
        pl.BlockSpec((IDX_HEADS, tm), lambda i: (0, i)),
    )
    return pl.pallas_call(
        functools.partial(_proj_kernel, lc=lc),
        grid=(n // tm,),
        in_specs=[pl.BlockSpec((tm, d), row), pl.BlockSpec((1, d), const), pl.BlockSpec((1, d), const),
                  pl.BlockSpec(wa.shape, const), pl.BlockSpec(wb.shape, const), pl.BlockSpec(wt.shape, const)],
        out_specs=out_specs,
        out_shape=out_shape,
        compiler_params=_params(("parallel",)),
        name="proj",
    )(x2d, g, b, wa, wb, wt)


def _key_to_float(key):
    bits = jnp.where(key >= 0, key, key ^ jnp.int32(0x7FFFFFFF))
    return lax.bitcast_convert_type(bits, F32)


def _dsa_kernel(q_ref, qi_ref, wit_ref, kb_ref, kib_ref, vt_ref, o_ref,
                sc_ref, bias_ref, m_ref, l_ref, acc_ref,
                *, tq, lc, nk_static, causal, l_valid, q_pos0, topk):
    qb = pl.program_id(1)
    nk = qb * (tq // lc) + (tq // lc) if causal else nk_static
    q_chunk = (q_pos0 + qb * tq + lax.broadcasted_iota(I32, (lc, tq), 1)) >> CHUNK_SHIFT
    row_iota = lax.broadcasted_iota(I32, (lc, tq), 0)

    def score_chunk(kc, carry):
        off = pl.multiple_of(kc * lc, lc)
        kic = kib_ref[0, pl.ds(off, lc), :]
        acc = jnp.zeros((lc, tq), F32)
        for h in range(IDX_HEADS):
            s = lax.dot_general(kic, qi_ref[0, :, h * IDX_DIM:(h + 1) * IDX_DIM], NT_DIMS,
                                preferred_element_type=F32)
            acc = acc + wit_ref[h:h + 1, :] * jnp.maximum(s, 0.0)
        l_pos = off + row_iota
        visible = ((l_pos >> CHUNK_SHIFT) <= q_chunk) & (l_pos < l_valid)
        sc_ref[pl.ds(off, lc), :] = jnp.where(visible, acc, -jnp.inf)
        return carry

    lax.fori_loop(0, nk, score_chunk, 0)

    def count(pred):
        def body(kc, c8):
            off = pl.multiple_of(kc * lc, lc)
            hit = jnp.where(pred(sc_ref[pl.ds(off, lc), :]), 1.0, 0.0)
            return c8 + jnp.sum(hit.reshape(lc // SUBLANES, SUBLANES, tq), axis=0)
        c8 = lax.fori_loop(0, nk, body, jnp.zeros((SUBLANES, tq), F32))
        return jnp.sum(c8, axis=0, keepdims=True)

    def bisect(_, carry):
        lo, hi = carry
        mid = (lo >> 1) + (hi >> 1) + (lo & hi & 1)
        cand = _key_to_float(mid)
        ok = count(lambda blk: blk >= cand) >= topk
        return jnp.where(ok, mid, lo), jnp.where(ok, hi, mid)

    lo, _ = lax.fori_loop(0, 32, bisect,
                          (jnp.full((1, tq), KEY_NEG_INF, I32), jnp.full((1, tq), KEY_POS_INF + 1, I32)))
    thr = _key_to_float(lo)
    n_above = count(lambda blk: blk > thr)
    n_ties = topk - n_above

    tri = jnp.where(lax.broadcasted_iota(I32, (lc, lc), 0) >= lax.broadcasted_iota(I32, (lc, lc), 1),
                    1.0, 0.0).astype(BF16)

    def bias_chunk(kc, ties_before):
        off = pl.multiple_of(kc * lc, lc)
        blk = sc_ref[pl.ds(off, lc), :]
        tie = blk == thr
        tie_rank = jnp.dot(tri, jnp.where(tie, 1.0, 0.0).astype(BF16), preferred_element_type=F32) + ties_before
        bias = jnp.where(blk > thr, 0.0, jnp.where(tie, jnp.where(tie_rank <= n_ties, 0.0, NEG_BIG), NEG_BIG))
        bias_ref[pl.ds(off, lc), :] = jnp.where(blk == -jnp.inf, NEG_BIG, bias)
        return tie_rank[lc - 1:lc, :]

    lax.fori_loop(0, nk, bias_chunk, jnp.zeros((1, tq), F32))

    m_ref[...] = jnp.full(m_ref.shape, NEG_BIG, F32)
    l_ref[...] = jnp.zeros(l_ref.shape, F32)
    acc_ref[...] = jnp.zeros(acc_ref.shape, F32)

    def attend_chunk(kc, carry):
        off = pl.multiple_of(kc * lc, lc)
        kc_b = kb_ref[0, pl.ds(off, lc), :]
        vt_c = vt_ref[kc]
        bias = bias_ref[pl.ds(off, lc), :]
        for h in range(N_HEADS):
            hs = slice(h * HEAD_DIM, (h + 1) * HEAD_DIM)
            s = lax.dot_general(kc_b, q_ref[0, :, hs], NT_DIMS, preferred_element_type=F32) + bias
            m_old = m_ref[h:h + 1, :]
            m_new = jnp.maximum(m_old, jnp.max(s, axis=0, keepdims=True))
            alpha = jnp.exp(m_old - m_new)
            p = jnp.exp(s - m_new)
            l_ref[h:h + 1, :] = alpha * l_ref[h:h + 1, :] + jnp.sum(p, axis=0, keepdims=True)
            acc_ref[hs, :] = alpha * acc_ref[hs, :] + jnp.dot(vt_c, p.astype(BF16), preferred_element_type=F32)
            m_ref[h:h + 1, :] = m_new
        return carry

    lax.fori_loop(0, nk, attend_chunk, 0)

    for h in range(N_HEADS):
        hs = slice(h * HEAD_DIM, (h + 1) * HEAD_DIM)
        acc_ref[hs, :] = acc_ref[hs, :] / l_ref[h:h + 1, :]
    o_ref[0] = acc_ref[...].T.astype(BF16)


def _dsa(q, qi, wit, kb, kib, vt, *, tq, lc, causal, l_valid, q_pos0, topk):
    bsz, tq_tot, aw = q.shape
    l_tot = kb.shape[1]
    nq = tq_tot // tq
    nkc = l_tot // lc
    kern = functools.partial(_dsa_kernel, tq=tq, lc=lc, nk_static=nkc, causal=causal, l_valid=l_valid,
                             q_pos0=q_pos0, topk=topk)
    return pl.pallas_call(
        kern,
        grid=(bsz, nq),
        in_specs=[
            pl.BlockSpec((1, tq, aw), lambda b, i: (b, i, 0)),
            pl.BlockSpec((1, tq, aw), lambda b, i: (b, i, 0)),
            pl.BlockSpec((IDX_HEADS, tq), lambda b, i: (0, b * nq + i)),
            pl.BlockSpec((1, l_tot, HEAD_DIM), lambda b, i: (b, 0, 0)),
            pl.BlockSpec((1, l_tot, IDX_DIM), lambda b, i: (b, 0, 0)),
            pl.BlockSpec((nkc, HEAD_DIM, lc), lambda b, i: (b, 0, 0)),
        ],
        out_specs=pl.BlockSpec((1, tq, aw), lambda b, i: (b, i, 0)),
        out_shape=jax.ShapeDtypeStruct((bsz, tq_tot, aw), BF16),
        scratch_shapes=[
            pltpu.VMEM((l_tot, tq), F32),
            pltpu.VMEM((l_tot, tq), F32),
            pltpu.VMEM((N_HEADS, tq), F32),
            pltpu.VMEM((N_HEADS, tq), F32),
            pltpu.VMEM((aw, tq), F32),
        ],
        compiler_params=_params(("parallel", "parallel")),
        name="dsa",
    )(q, qi, wit, kb, kib, vt)


def _pool_kernel(u_ref, pre_ref, pw_ref, sc_ref, o_ref, ext_ref, *, t_len, tt, pos0):
    pad = pre_ref.shape[1]
    gw = pw_ref.shape[1]
    ext_ref[0:pad, :] = pre_ref[0]
    ext_ref[pad:pad + t_len, :] = u_ref[0]
    row = lax.broadcasted_iota(I32, (tt, gw), 0)
    for t in range(t_len // tt):
        r0 = t * tt
        for g, w in enumerate(POOL_WINDOWS):
            cols = slice(g * gw, (g + 1) * gw)
            cur = ext_ref[pad + r0:pad + r0 + tt, cols]
            wsum = cur
            for j in range(1, w):
                wsum = wsum + ext_ref[pad + r0 - j:pad + r0 - j + tt, cols]
            cnt = jnp.minimum(w, pos0 + r0 + 1 + row).astype(F32)
            diff = (wsum / cnt - cur).astype(BF16)
            y = jnp.dot(diff, pw_ref[g], preferred_element_type=F32)
            o_ref[0, r0:r0 + tt, cols] = (y * sc_ref[:, cols]).astype(BF16)


def _pool(u, prefix, pool_w_b, pool_scale, *, pos0):
    bsz, t_len, c = u.shape
    pad = prefix.shape[1]
    tt = min(t_len, 256)
    return pl.pallas_call(
        functools.partial(_pool_kernel, t_len=t_len, tt=tt, pos0=pos0),
        grid=(bsz,),
        in_specs=[pl.BlockSpec((1, t_len, c), lambda b: (b, 0, 0)),
                  pl.BlockSpec((1, pad, c), lambda b: (b, 0, 0)),
                  pl.BlockSpec(pool_w_b.shape, lambda b: (0, 0, 0)),
                  pl.BlockSpec((1, c), lambda b: (0, 0))],
        out_specs=pl.BlockSpec((1, t_len, c), lambda b: (b, 0, 0)),
        out_shape=jax.ShapeDtypeStruct((bsz, t_len, c), BF16),
        scratch_shapes=[pltpu.VMEM((pad + t_len, c), F32)],
        compiler_params=_params(("parallel",)),
        name="pool",
    )(u, prefix, pool_w_b, pool_scale)


def _tail_kernel(x_ref, att_ref, pool_ref, g0_ref, b0_ref, woa_ref, wob_ref, g1_ref, b1_ref,
                 wrh_ref, wrl_ref, br_ref,
                 h1_ref, eid_ref, gw_ref, rank_ref, cnt_ref, carry_ref, *, tm):
    @pl.when(pl.program_id(0) == 0)
    def _():
        carry_ref[...] = jnp.zeros(carry_ref.shape, F32)

    h = _layer_norm(x_ref[...], g0_ref[...], b0_ref[...])
    mix = (jnp.dot(att_ref[...], woa_ref[...], preferred_element_type=F32)
           + jnp.dot(pool_ref[...], wob_ref[...], preferred_element_type=F32))
    h1 = _layer_norm(DEEPNORM_ALPHA * h + mix, g1_ref[...], b1_ref[...])
    h1_ref[...] = h1

    hh = h1.astype(BF16)
    hl = (h1 - hh.astype(F32)).astype(BF16)
    logits = (lax.dot_general(wrh_ref[...], hh, NT_DIMS, preferred_element_type=F32)
              + lax.dot_general(wrh_ref[...], hl, NT_DIMS, preferred_element_type=F32)
              + lax.dot_general(wrl_ref[...], hh, NT_DIMS, preferred_element_type=F32)
              + br_ref[...])

    e_iota = lax.broadcasted_iota(I32, (N_EXPERTS, tm), 0).astype(F32)
    work = logits
    vals, hots = [], []
    for k in range(TOP_K):
        m = jnp.max(work, axis=0, keepdims=True)
        idx = jnp.min(jnp.where(work == m, e_iota, float(N_EXPERTS)), axis=0, keepdims=True)
        hot = e_iota == idx
        eid_ref[k:k + 1, :] = idx.astype(I32)
        vals.append(m)
        hots.append(jnp.where(hot, 1.0, 0.0))
        work = jnp.where(hot, -jnp.inf, work)
    exps = [jnp.exp(v - vals[0]) for v in vals]
    den = exps[0] + exps[1] + exps[2] + exps[3]
    for k in range(TOP_K):
        gw_ref[k:k + 1, :] = exps[k] / den

    hot_all = (hots[0] + hots[1] + hots[2] + hots[3]).astype(BF16)
    before = jnp.where(lax.broadcasted_iota(I32, (tm, tm), 0) < lax.broadcasted_iota(I32, (tm, tm), 1),
                       1.0, 0.0).astype(BF16)
    carry = carry_ref[...]
    prefix = jnp.dot(hot_all, before, preferred_element_type=F32) + jnp.tile(carry, (1, tm // LANES))
    for k in range(TOP_K):
        rank_ref[k:k + 1, :] = jnp.sum(hots[k] * prefix, axis=0, keepdims=True).astype(I32)
    total = carry + jnp.dot(hot_all, jnp.ones((tm, LANES), BF16), preferred_element_type=F32)
    carry_ref[...] = total
    cnt_ref[...] = total


def _tail(x2d, att, pool, g0, b0, woa, wob, g1, b1, wrh, wrl, br, *, tm):
    n, d = x2d.shape
    aw = att.shape[1]
    row = lambda i: (i, 0)
    const = lambda i: (0, 0)
    col = lambda i: (0, i)
    return pl.pallas_call(
        functools.partial(_tail_kernel, tm=tm),
        grid=(n // tm,),
        in_specs=[pl.BlockSpec((tm, d), row), pl.BlockSpec((tm, aw), row), pl.BlockSpec((tm, d - aw), row),
                  pl.BlockSpec((1, d), const), pl.BlockSpec((1, d), const),
                  pl.BlockSpec(woa.shape, const), pl.BlockSpec(wob.shape, const),
                  pl.BlockSpec((1, d), const), pl.BlockSpec((1, d), const),
                  pl.BlockSpec(wrh.shape, const), pl.BlockSpec(wrl.shape, const), pl.BlockSpec(br.shape, const)],
        out_specs=(pl.BlockSpec((tm, d), row), pl.BlockSpec((TOP_K, tm), col), pl.BlockSpec((TOP_K, tm), col),
                   pl.BlockSpec((TOP_K, tm), col), pl.BlockSpec((N_EXPERTS, LANES), const)),
        out_shape=(jax.ShapeDtypeStruct((n, d), F32), jax.ShapeDtypeStruct((TOP_K, n), I32),
                   jax.ShapeDtypeStruct((TOP_K, n), F32), jax.ShapeDtypeStruct((TOP_K, n), I32),
                   jax.ShapeDtypeStruct((N_EXPERTS, LANES), F32)),
        scratch_shapes=[pltpu.VMEM((N_EXPERTS, LANES), F32)],
        compiler_params=_params(("arbitrary",)),
        name="tail",
    )(x2d, att, pool, g0, b0, woa, wob, g1, b1, wrh, wrl, br)


def _positions_kernel(offs_ref, eid_ref, rank_ref, pos_ref):
    eid = eid_ref[...]
    pos = rank_ref[...]
    for e in range(N_EXPERTS):
        pos = pos + jnp.where(eid == e, offs_ref[e], 0)
    pos_ref[...] = pos


def _positions(offs, eid_t, rank_t, *, tn):
    k, n = eid_t.shape
    return pl.pallas_call(
        _positions_kernel,
        grid_spec=pltpu.PrefetchScalarGridSpec(
            num_scalar_prefetch=1,
            grid=(n // tn,),
            in_specs=[pl.BlockSpec((k, tn), lambda i, offs: (0, i)), pl.BlockSpec((k, tn), lambda i, offs: (0, i))],
            out_specs=pl.BlockSpec((k, tn), lambda i, offs: (0, i)),
        ),
        out_shape=jax.ShapeDtypeStruct((k, n), I32),
        compiler_params=_params(("parallel",)),
        name="positions",
    )(offs, eid_t, rank_t)


ROW_UNROLL = 8


def _pos_copy(pos_hbm, pos_smem, sem, tile, slot, width):
    return pltpu.make_async_copy(pos_hbm.at[pl.ds(pl.multiple_of(tile * width, width), width)],
                                 pos_smem.at[slot], sem.at[slot])


def _dispatch_kernel(pos_hbm, h1_ref, xs_in_hbm, xs_hbm, pos_smem, pos_sem, row_sem, *, td):
    del xs_in_hbm
    i = pl.program_id(0)
    n = pl.num_programs(0)
    slot = i % 2
    width = TOP_K * td

    @pl.when(i == 0)
    def _():
        _pos_copy(pos_hbm, pos_smem, pos_sem, 0, 0, width).start()

    _pos_copy(pos_hbm, pos_smem, pos_sem, i, slot, width).wait()

    @pl.when(i + 1 < n)
    def _():
        _pos_copy(pos_hbm, pos_smem, pos_sem, i + 1, 1 - slot, width).start()

    def row_copy(r, dst_row):
        return pltpu.make_async_copy(h1_ref.at[pl.ds(r, 1), :], xs_hbm.at[pl.ds(dst_row, 1), :], row_sem)

    def issue(blk, carry):
        for j in range(ROW_UNROLL):
            r = blk * ROW_UNROLL + j
            for k in range(TOP_K):
                row_copy(r, pos_smem[slot, k * td + r]).start()
        return carry

    lax.fori_loop(0, td // ROW_UNROLL, issue, 0)

    def drain(blk, carry):
        for _ in range(ROW_UNROLL * TOP_K):
            row_copy(0, 0).wait()
        return carry

    lax.fori_loop(0, td // ROW_UNROLL, drain, 0)


def _dispatch(pos_flat, h1, xs_zero, *, td):
    n, d = h1.shape
    return pl.pallas_call(
        functools.partial(_dispatch_kernel, td=td),
        grid=(n // td,),
        in_specs=[pl.BlockSpec(memory_space=pl.ANY), pl.BlockSpec((td, d), lambda i: (i, 0)),
                  pl.BlockSpec(memory_space=pl.ANY)],
        out_specs=pl.BlockSpec(memory_space=pl.ANY),
        out_shape=jax.ShapeDtypeStruct(xs_zero.shape, xs_zero.dtype),
        scratch_shapes=[pltpu.SMEM((2, TOP_K * td), I32), pltpu.SemaphoreType.DMA((2,)), pltpu.SemaphoreType.DMA(())],
        input_output_aliases={2: 0},
        compiler_params=_params(("arbitrary",)),
        name="dispatch",
    )(pos_flat, h1, xs_zero)


FF_CHUNK = 256


def _moe_kernel(te_ref, ta_ref, x_ref, w1_ref, b1_ref, w2_ref, b2_ref, y_ref):
    i = pl.program_id(0)
    d_ff = w2_ref.shape[1]

    @pl.when(ta_ref[i] > 0)
    def _():
        xb = x_ref[...].astype(BF16)
        y_ref[...] = jnp.broadcast_to(b2_ref[0], y_ref.shape)
        for j in range(d_ff // FF_CHUNK):
            gs = slice(j * FF_CHUNK, (j + 1) * FF_CHUNK)
            us = slice(d_ff + j * FF_CHUNK, d_ff + (j + 1) * FF_CHUNK)
            gate = jnp.dot(xb, w1_ref[0, :, gs], preferred_element_type=F32) + b1_ref[0, :, gs]
            up = jnp.dot(xb, w1_ref[0, :, us], preferred_element_type=F32) + b1_ref[0, :, us]
            gate = jnp.minimum(gate, SWIGLU_LIMIT)
            up = jnp.clip(up, -SWIGLU_LIMIT, SWIGLU_LIMIT)
            act = (up + 1.0) * (gate * jax.nn.sigmoid(SWIGLU_ALPHA * gate))
            y_ref[...] += jnp.dot(act.astype(BF16), w2_ref[0, gs, :], preferred_element_type=F32)

    @pl.when(ta_ref[i] == 0)
    def _():
        y_ref[...] = jnp.zeros(y_ref.shape, F32)


def _moe(tile_expert, tile_active, xs, w1b, b1, w2b, b2, *, tmoe):
    p, d = xs.shape
    e, _, ff2 = w1b.shape
    d_ff = ff2 // 2
    return pl.pallas_call(
        _moe_kernel,
        grid_spec=pltpu.PrefetchScalarGridSpec(
            num_scalar_prefetch=2,
            grid=(p // tmoe,),
            in_specs=[pl.BlockSpec((tmoe, d), lambda i, te, ta: (i, 0)),
                      pl.BlockSpec((1, d, ff2), lambda i, te, ta: (te[i], 0, 0)),
                      pl.BlockSpec((1, 1, ff2), lambda i, te, ta: (te[i], 0, 0)),
                      pl.BlockSpec((1, d_ff, d), lambda i, te, ta: (te[i], 0, 0)),
                      pl.BlockSpec((1, 1, d), lambda i, te, ta: (te[i], 0, 0))],
            out_specs=pl.BlockSpec((tmoe, d), lambda i, te, ta: (i, 0)),
        ),
        out_shape=jax.ShapeDtypeStruct((p, d), F32),
        compiler_params=_params(("parallel",)),
        name="moe",
    )(tile_expert, tile_active, xs, w1b, b1.reshape(e, 1, ff2), w2b, b2.reshape(e, 1, d))


def _combine_kernel(pos_hbm, h1_ref, gw_ref, g2_ref, b2_ref, ys_hbm, o_ref,
                    pos_smem, ybuf, pos_sem, row_sem, *, tc):
    i = pl.program_id(0)
    n = pl.num_programs(0)
    slot = i % 2
    width = TOP_K * tc

    def row_copy(tile_slot, k, r, src_row):
        return pltpu.make_async_copy(ys_hbm.at[pl.ds(src_row, 1), :], ybuf.at[tile_slot, k, pl.ds(r, 1), :],
                                     row_sem.at[tile_slot])

    def gather_tile(tile, tile_slot):
        _pos_copy(pos_hbm, pos_smem, pos_sem, tile, tile_slot, width).start()
        _pos_copy(pos_hbm, pos_smem, pos_sem, tile, tile_slot, width).wait()

        def issue(blk, carry):
            for j in range(ROW_UNROLL):
                r = blk * ROW_UNROLL + j
                for k in range(TOP_K):
                    row_copy(tile_slot, k, r, pos_smem[tile_slot, k * tc + r]).start()
            return carry

        lax.fori_loop(0, tc // ROW_UNROLL, issue, 0)

    @pl.when(i == 0)
    def _():
        gather_tile(0, 0)

    @pl.when(i + 1 < n)
    def _():
        gather_tile(i + 1, 1 - slot)

    def drain(blk, carry):
        for _ in range(ROW_UNROLL * TOP_K):
            row_copy(slot, 0, 0, 0).wait()
        return carry

    lax.fori_loop(0, tc // ROW_UNROLL, drain, 0)

    gw = gw_ref[...]
    m = gw[:, 0:1] * ybuf[slot, 0]
    for k in range(1, TOP_K):
        m = m + gw[:, k:k + 1] * ybuf[slot, k]
    o_ref[...] = _layer_norm(DEEPNORM_ALPHA * h1_ref[...] + m, g2_ref[...], b2_ref[...])


def _combine(pos_flat, h1, gw, g2, b2, ys, *, tc):
    n, d = h1.shape
    return pl.pallas_call(
        functools.partial(_combine_kernel, tc=tc),
        grid=(n // tc,),
        in_specs=[pl.BlockSpec(memory_space=pl.ANY), pl.BlockSpec((tc, d), lambda i: (i, 0)),
                  pl.BlockSpec((tc, TOP_K), lambda i: (i, 0)),
                  pl.BlockSpec((1, d), lambda i: (0, 0)), pl.BlockSpec((1, d), lambda i: (0, 0)),
                  pl.BlockSpec(memory_space=pl.ANY)],
        out_specs=pl.BlockSpec((tc, d), lambda i: (i, 0)),
        out_shape=jax.ShapeDtypeStruct((n, d), F32),
        scratch_shapes=[pltpu.SMEM((2, TOP_K * tc), I32), pltpu.VMEM((2, TOP_K, tc, d), F32),
                        pltpu.SemaphoreType.DMA((2,)), pltpu.SemaphoreType.DMA((2,))],
        compiler_params=_params(("arbitrary",)),
        name="combine",
    )(pos_flat, h1, gw, g2, b2, ys)


def _tile_major(a_t, tile):
    k, n = a_t.shape
    return a_t.reshape(k, n // tile, tile).transpose(1, 0, 2).reshape(-1)


def _block_tail(x2d, att, pool, wts, *, tm, tmoe, td, tc):
    n, d = x2d.shape
    tm, td, tc = min(tm, n), min(td, n), min(tc, n)
    h1, eid_t, gw_t, rank_t, cnt = _tail(x2d, att, pool, wts["g0"], wts["b0"], wts["woa"], wts["wob"],
                                         wts["g1"], wts["b1"], wts["wrh"], wts["wrl"], wts["br"], tm=tm)
    counts = cnt[:, 0].astype(I32)
    padded = ((counts + tmoe - 1) // tmoe) * tmoe
    ends = jnp.cumsum(padded)
    offs = ends - padded
    n_tiles = (TOP_K * n) // tmoe + N_EXPERTS
    tile_start = jnp.arange(n_tiles, dtype=I32) * tmoe
    tile_expert = jnp.minimum(jnp.searchsorted(ends, tile_start, side="right").astype(I32), N_EXPERTS - 1)
    tile_active = (tile_start < ends[-1]).astype(I32)

    pos_t = _positions(offs, eid_t, rank_t, tn=min(n, 8192))
    xs_zero = jnp.zeros((n_tiles * tmoe, d), F32)
    xs = _dispatch(_tile_major(pos_t, td), h1, xs_zero, td=td)
    ys = _moe(tile_expert, tile_active, xs, wts["w1b"], wts["b1e"], wts["w2b"], wts["b2e"], tmoe=tmoe)
    return _combine(_tile_major(pos_t, tc), h1, gw_t.T, wts["g2"], wts["b2"], ys, tc=tc)


def kernel(x_prompt, x_sample, cache_k, cache_v, cache_kidx, state_pool, ln0_g, ln0_b, w_in, w_o,
           pool_w, pool_scale, ln1_g, ln1_b, w_router, b_router, w1, b1, w2, b2, ln2_g, ln2_b):
    bp, s_len, d = x_prompt.shape
    bs, t_len, _ = x_sample.shape
    l_past = cache_k.shape[2]
    aw = N_HEADS * HEAD_DIM
    pw = d - aw
    lyr = 0
    lc = 256

    k_off = aw
    v_off = k_off + HEAD_DIM
    qi_off = v_off + HEAD_DIM
    ki_off = qi_off + IDX_HEADS * IDX_DIM
    wi_off = ki_off + IDX_DIM
    u_off = wi_off + IDX_HEADS
    win = w_in[lyr]
    wa = jnp.concatenate([win[:, 0:k_off], win[:, qi_off:ki_off], win[:, u_off:u_off + pw]], axis=1).astype(BF16)
    wb = jnp.concatenate([win[:, k_off:v_off], win[:, v_off:qi_off], win[:, ki_off:wi_off],
                          jnp.zeros((d, 64), F32)], axis=1).astype(BF16)
    wt = jnp.concatenate([win[:, v_off:qi_off].T, win[:, wi_off:u_off].T, jnp.zeros((8, d), F32)], axis=0).astype(BF16)

    g0 = ln0_g.reshape(1, d)
    b0 = ln0_b.reshape(1, d)
    wrt = w_router[lyr].T
    wrh = wrt.astype(BF16)
    wts = dict(
        g0=g0, b0=b0,
        woa=w_o[lyr][:aw].astype(BF16), wob=w_o[lyr][aw:].astype(BF16),
        g1=ln1_g[lyr].reshape(1, d), b1=ln1_b[lyr].reshape(1, d),
        wrh=wrh, wrl=(wrt - wrh.astype(F32)).astype(BF16), br=b_router[lyr].reshape(N_EXPERTS, 1),
        w1b=w1[lyr].astype(BF16), b1e=b1[lyr], w2b=w2[lyr].astype(BF16), b2e=b2[lyr],
        g2=ln2_g[lyr].reshape(1, d), b2=ln2_b[lyr].reshape(1, d),
    )
    pool_w_b = pool_w[lyr].astype(BF16)
    pool_sc = pool_scale[lyr].reshape(1, pw)

    xp = x_prompt.reshape(bp * s_len, d)
    q, qi, u, k, v, ki, kb, kib, vt, wit = _proj(xp, g0, b0, wa, wb, wt, tm=512, lc=lc)
    att_p = _dsa(q.reshape(bp, s_len, aw), qi.reshape(bp, s_len, aw), wit,
                 kb.reshape(bp, s_len, HEAD_DIM), kib.reshape(bp, s_len, IDX_DIM), vt,
                 tq=256, lc=lc, causal=True, l_valid=s_len, q_pos0=0, topk=min(TOPK_MAX, s_len // 4))
    u_p = u.reshape(bp, s_len, pw)
    pool_p = _pool(u_p, jnp.zeros((bp, POOL_PAST + 1, pw), F32), pool_w_b, pool_sc, pos0=0)
    y_p = _block_tail(xp, att_p.reshape(bp * s_len, aw), pool_p.reshape(bp * s_len, pw), wts,
                      tm=512, tmoe=512, td=512, tc=256)

    xs = x_sample.reshape(bs * t_len, d)
    qs, qis, us, kn, vn, kin, _, _, _, wits = _proj(xs, g0, b0, wa, wb, wt, tm=512, lc=lc)
    l_all = l_past + t_len
    l_pad = -(-l_all // lc) * lc
    tq_s = LANES
    pad_keys = lambda a: jnp.pad(a, ((0, 0), (0, l_pad - l_all), (0, 0)))
    k_all = pad_keys(jnp.concatenate([cache_k[lyr], kn.reshape(bs, t_len, HEAD_DIM)], axis=1))
    v_all = pad_keys(jnp.concatenate([cache_v[lyr], vn.reshape(bs, t_len, HEAD_DIM)], axis=1))
    ki_all = pad_keys(jnp.concatenate([cache_kidx[lyr], kin.reshape(bs, t_len, IDX_DIM)], axis=1))
    vt_all = v_all.reshape(bs, l_pad // lc, lc, HEAD_DIM).transpose(0, 1, 3, 2).reshape(-1, HEAD_DIM, lc)
    pad_q = lambda a: jnp.pad(a.reshape(bs, t_len, aw), ((0, 0), (0, tq_s - t_len), (0, 0)))
    wit_s = jnp.pad(wits.reshape(IDX_HEADS, bs, t_len), ((0, 0), (0, 0), (0, tq_s - t_len))).reshape(IDX_HEADS, -1)
    att_s = _dsa(pad_q(qs), pad_q(qis), wit_s, k_all.astype(BF16), ki_all.astype(BF16), vt_all.astype(BF16),
                 tq=tq_s, lc=lc, causal=False, l_valid=l_all, q_pos0=l_past, topk=min(TOPK_MAX, l_all // 4))
    att_s = att_s[:, :t_len].reshape(bs * t_len, aw)
    us3 = us.reshape(bs, t_len, pw)
    prefix_s = jnp.concatenate([jnp.zeros((bs, 1, pw), F32), state_pool[lyr]], axis=1)
    pool_s = _pool(us3, prefix_s, pool_w_b, pool_sc, pos0=l_past)
    y_s = _block_tail(xs, att_s, pool_s.reshape(bs * t_len, pw), wts, tm=512, tmoe=128, td=512, tc=256)

    pool_state_p = u_p[:, s_len - POOL_PAST:]
    pool_state_s = jnp.concatenate([state_pool[lyr], us3], axis=1)[:, -POOL_PAST:]
    return (y_p.reshape(bp, s_len, d), y_s.reshape(bs, t_len, d),
            k.reshape(1, bp, s_len, HEAD_DIM), v.reshape(1, bp, s_len, HEAD_DIM),
            ki.reshape(1, bp, s_len, IDX_DIM), pool_state_p[None],
            kn.reshape(1, bs, t_len, HEAD_DIM), vn.reshape(1, bs, t_len, HEAD_DIM),
            kin.reshape(1, bs, t_len, IDX_DIM), pool_state_s[None])
```

```python
import functools

import jax
import jax.numpy as jnp
from jax import lax
from jax.experimental import pallas as pl
from jax.experimental.pallas import tpu as pltpu

F32 = jnp.float32
BF16 = jnp.bfloat16
I32 = jnp.int32

CHUNK = 64
CHUNK_SHIFT = 6
assert 1 << CHUNK_SHIFT == CHUNK
N_HEADS = 8
HEAD_DIM = 64
IDX_HEADS = 8
IDX_DIM = 64
TOPK_MAX = 256
POOL_WINDOWS = (2, 4, 8, 16)
POOL_PAST = 15
N_EXPERTS = 32
TOP_K = 4
SWIGLU_LIMIT = 7.0
SWIGLU_ALPHA = 1.702
LN_EPS = 1e-5
DEPTH = 1
DEEPNORM_ALPHA = (2 * DEPTH) ** 0.25

LANES = 128
SUBLANES = 8
VMEM_LIMIT_BYTES = 56 * 1024 * 1024

NEG_BIG = -1e30
KEY_NEG_INF = -2139095041
KEY_POS_INF = 2139095040

NT_DIMS = (((1,), (1,)), ((), ()))


def _layer_norm(x, g, b):
    mu = jnp.mean(x, axis=-1, keepdims=True)
    xc = x - mu
    var = jnp.mean(xc * xc, axis=-1, keepdims=True)
    return xc * lax.rsqrt(var + LN_EPS) * g + b


def _params(sem):
    return pltpu.CompilerParams(dimension_semantics=sem, vmem_limit_bytes=VMEM_LIMIT_BYTES)


def _proj_kernel(x_ref, g_ref, b_ref, wa_ref, wb_ref, wt_ref,
                 q_ref, qi_ref, u_ref, k_ref, v_ref, ki_ref, kb_ref, kib_ref, vt_ref, wit_ref, *, lc):
    h = _layer_norm(x_ref[...], g_ref[...], b_ref[...])
    hb = h.astype(BF16)
    aw = N_HEADS * HEAD_DIM
    pa = jnp.dot(hb, wa_ref[...], preferred_element_type=F32)
    q_ref[...] = (pa[:, :aw] * (HEAD_DIM ** -0.5)).astype(BF16)
    qi_ref[...] = pa[:, aw:2 * aw].astype(BF16)
    u_ref[...] = pa[:, 2 * aw:]
    pb = jnp.dot(hb, wb_ref[...], preferred_element_type=F32)
    k = pb[:, 0:HEAD_DIM]
    v = pb[:, HEAD_DIM:2 * HEAD_DIM]
    ki = pb[:, 2 * HEAD_DIM:2 * HEAD_DIM + IDX_DIM]
    k_ref[...] = k
    v_ref[...] = v
    ki_ref[...] = ki
    kb_ref[...] = k.astype(BF16)
    kib_ref[...] = ki.astype(BF16)
    pt = lax.dot_general(wt_ref[...], hb, NT_DIMS, preferred_element_type=F32)
    for c in range(vt_ref.shape[0]):
        vt_ref[c] = pt[0:HEAD_DIM, c * lc:(c + 1) * lc].astype(BF16)
    wi = pt[HEAD_DIM:HEAD_DIM + IDX_HEADS, :]
    wit_ref[...] = (wi * (IDX_HEADS ** -0.5)) * (IDX_DIM ** -0.5)


def _proj(x2d, g, b, wa, wb, wt, *, tm, lc):
    n, d = x2d.shape
    tm = min(tm, n)
    aw = N_HEADS * HEAD_DIM
    uw = wa.shape[1] - 2 * aw
    row = lambda i: (i, 0)
    const = lambda i: (0, 0)
    out_shape = (
        jax.ShapeDtypeStruct((n, aw), BF16),
        jax.ShapeDtypeStruct((n, aw), BF16),
        jax.ShapeDtypeStruct((n, uw), F32),
        jax.ShapeDtypeStruct((n, HEAD_DIM), F32),
        jax.ShapeDtypeStruct((n, HEAD_DIM), F32),
        jax.ShapeDtypeStruct((n, IDX_DIM), F32),
        jax.ShapeDtypeStruct((n, HEAD_DIM), BF16),
        jax.ShapeDtypeStruct((n, IDX_DIM), BF16),
        jax.ShapeDtypeStruct((n // lc, HEAD_DIM, lc), BF16),
        jax.ShapeDtypeStruct((IDX_HEADS, n), F32),
    )
    out_specs = (
        pl.BlockSpec((tm, aw), row), pl.BlockSpec((tm, aw), row), pl.BlockSpec((tm, uw), row),
        pl.BlockSpec((tm, HEAD_DIM), row), pl.BlockSpec((tm, HEAD_DIM), row), pl.BlockSpec((tm, IDX_DIM), row),
        pl.BlockSpec((tm, HEAD_DIM), row), pl.BlockSpec((tm, IDX_DIM), row),
        pl.BlockSpec((tm // lc, HEAD_DIM, lc), lambda i: (i, 0, 0)),
        pl.BlockSpec((IDX_HEADS, tm), lambda i: (0, i)),
    )
    return pl.pallas_call(
        functools.partial(_proj_kernel, lc=lc),
        grid=(n // tm,),
        in_specs=[pl.BlockSpec((tm, d), row), pl.BlockSpec((1, d), const), pl.BlockSpec((1, d), const),
                  pl.BlockSpec(wa.shape, const), pl.BlockSpec(wb.shape, const), pl.BlockSpec(wt.shape, const)],
        out_specs=out_specs,
        out_shape=out_shape,
        compiler_params=_params(("parallel",)),
        name="proj",
    )(x2d, g, b, wa, wb, wt)


def _key_to_float(key):
    bits = jnp.where(key >= 0, key, key ^ jnp.int32(0x7FFFFFFF))
    return lax.bitcast_convert_type(bits, F32)


def _dsa_kernel(q_ref, qi_ref, wit_ref, kb_ref, kib_ref, vt_ref, o_ref,
                sc_ref, s_ref, out_ref, *acc_refs, tq, lc, nk_static, causal, l_valid, q_pos0, topk):
    qb = pl.program_id(1)
    nk = qb * (tq // lc) + (tq // lc) if causal else nk_static
    q_chunk = (q_pos0 + qb * tq + lax.broadcasted_iota(I32, (lc, tq), 1)) >> CHUNK_SHIFT
    row_iota = lax.broadcasted_iota(I32, (lc, tq), 0)

    def score_chunk(kc, carry):
        off = pl.multiple_of(kc * lc, lc)
        kic = kib_ref[0, pl.ds(off, lc), :]
        acc = jnp.zeros((lc, tq), F32)
        for h in range(IDX_HEADS):
            s = lax.dot_general(kic, qi_ref[0, :, h * IDX_DIM:(h + 1) * IDX_DIM], NT_DIMS,
                                preferred_element_type=F32)
            acc = acc + wit_ref[h:h + 1, :] * jnp.maximum(s, 0.0)
        l_pos = off + row_iota
        visible = ((l_pos >> CHUNK_SHIFT) <= q_chunk) & (l_pos < l_valid)
        sc_ref[pl.ds(off, lc), :] = jnp.where(visible, acc, -jnp.inf)
        return carry

    lax.fori_loop(0, nk, score_chunk, 0)

    def count(pred):
        ways = 4 * SUBLANES

        def body(kc, part):
            off = pl.multiple_of(kc * lc, lc)
            hit = jnp.where(pred(sc_ref[pl.ds(off, lc), :]), 1.0, 0.0)
            return part + jnp.sum(hit.reshape(lc // ways, ways, tq), axis=0)
        part = lax.fori_loop(0, nk, body, jnp.zeros((ways, tq), F32))
        return jnp.sum(part, axis=0, keepdims=True)

    def bisect(_, carry):
        lo, hi = carry
        mid = (lo >> 1) + (hi >> 1) + (lo & hi & 1)
        cand = _key_to_float(mid)
        ok = count(lambda blk: blk >= cand) >= topk
        return jnp.where(ok, mid, lo), jnp.where(ok, hi, mid)

    lo, _ = lax.fori_loop(0, 32, bisect,
                          (jnp.full((1, tq), KEY_NEG_INF, I32), jnp.full((1, tq), KEY_POS_INF + 1, I32)))
    thr = _key_to_float(lo)
    n_above = count(lambda blk: blk > thr)
    n_ties = topk - n_above

    tri = jnp.where(lax.broadcasted_iota(I32, (lc, lc), 0) >= lax.broadcasted_iota(I32, (lc, lc), 1),
                    1.0, 0.0).astype(BF16)

    def fold8(x):
        return x.reshape(lc // SUBLANES, SUBLANES, tq)

    def logits_chunk(kc, carry):
        ties_before, m8 = carry
        off = pl.multiple_of(kc * lc, lc)
        blk = sc_ref[pl.ds(off, lc), :]
        tie = blk == thr
        tie_rank = jnp.dot(tri, jnp.where(tie, 1.0, 0.0).astype(BF16), preferred_element_type=F32) + ties_before
        bias = jnp.where(blk > thr, 0.0, jnp.where(tie, jnp.where(tie_rank <= n_ties, 0.0, NEG_BIG), NEG_BIG))
        bias = jnp.where(blk == -jnp.inf, NEG_BIG, bias)
        kc_b = kb_ref[0, pl.ds(off, lc), :]
        m_rows = []
        for h in range(N_HEADS):
            s = lax.dot_general(kc_b, q_ref[0, :, h * HEAD_DIM:(h + 1) * HEAD_DIM], NT_DIMS,
                                preferred_element_type=F32) + bias
            s_ref[h, pl.ds(off, lc), :] = s
            m_rows.append(jnp.maximum(m8[h * SUBLANES:(h + 1) * SUBLANES], jnp.max(fold8(s), axis=0)))
        return tie_rank[lc - 1:lc, :], jnp.concatenate(m_rows, axis=0)

    _, m8 = lax.fori_loop(0, nk, logits_chunk,
                          (jnp.zeros((1, tq), F32), jnp.full((N_HEADS * SUBLANES, tq), NEG_BIG, F32)))
    m_all = jnp.max(m8.reshape(N_HEADS, SUBLANES, tq), axis=1)

    for acc_ref in acc_refs:
        acc_ref[...] = jnp.zeros(acc_ref.shape, F32)

    def pv_chunk(kc, l8):
        off = pl.multiple_of(kc * lc, lc)
        vt_c = vt_ref[kc]
        l_rows = []
        for h in range(N_HEADS):
            p = jnp.exp(s_ref[h, pl.ds(off, lc), :] - m_all[h:h + 1, :])
            l_rows.append(l8[h * SUBLANES:(h + 1) * SUBLANES] + jnp.sum(fold8(p), axis=0))
            acc_refs[h][...] += jnp.dot(vt_c, p.astype(BF16), preferred_element_type=F32)
        return jnp.concatenate(l_rows, axis=0)

    l8 = lax.fori_loop(0, nk, pv_chunk, jnp.zeros((N_HEADS * SUBLANES, tq), F32))
    l_all = jnp.sum(l8.reshape(N_HEADS, SUBLANES, tq), axis=1)

    for h in range(N_HEADS):
        out_ref[h * HEAD_DIM:(h + 1) * HEAD_DIM, :] = acc_refs[h][...] / l_all[h:h + 1, :]
    o_ref[0] = out_ref[...].T.astype(BF16)


def _dsa(q, qi, wit, kb, kib, vt, *, tq, lc, causal, l_valid, q_pos0, topk):
    bsz, tq_tot, aw = q.shape
    l_tot = kb.shape[1]
    nq = tq_tot // tq
    nkc = l_tot // lc
    kern = functools.partial(_dsa_kernel, tq=tq, lc=lc, nk_static=nkc, causal=causal, l_valid=l_valid,
                             q_pos0=q_pos0, topk=topk)
    return pl.pallas_call(
        kern,
        grid=(bsz, nq),
        in_specs=[
            pl.BlockSpec((1, tq, aw), lambda b, i: (b, i, 0)),
            pl.BlockSpec((1, tq, aw), lambda b, i: (b, i, 0)),
            pl.BlockSpec((IDX_HEADS, tq), lambda b, i: (0, b * nq + i)),
            pl.BlockSpec((1, l_tot, HEAD_DIM), lambda b, i: (b, 0, 0)),
            pl.BlockSpec((1, l_tot, IDX_DIM), lambda b, i: (b, 0, 0)),
            pl.BlockSpec((nkc, HEAD_DIM, lc), lambda b, i: (b, 0, 0)),
        ],
        out_specs=pl.BlockSpec((1, tq, aw), lambda b, i: (b, i, 0)),
        out_shape=jax.ShapeDtypeStruct((bsz, tq_tot, aw), BF16),
        scratch_shapes=[
            pltpu.VMEM((l_tot, tq), F32),
            pltpu.VMEM((N_HEADS, l_tot, tq), F32),
            pltpu.VMEM((aw, tq), F32),
        ] + [pltpu.VMEM((HEAD_DIM, tq), F32) for _ in range(N_HEADS)],
        compiler_params=_params(("parallel", "parallel")),
        name="dsa",
    )(q, qi, wit, kb, kib, vt)


def _pool_kernel(u_ref, pre_ref, pw_ref, sc_ref, o_ref, ext_ref, *, t_len, tt, pos0):
    pad = pre_ref.shape[1]
    gw = pw_ref.shape[1]
    ext_ref[0:pad, :] = pre_ref[0]
    ext_ref[pad:pad + t_len, :] = u_ref[0]
    row = lax.broadcasted_iota(I32, (tt, gw), 0)
    for t in range(t_len // tt):
        r0 = t * tt
        for g, w in enumerate(POOL_WINDOWS):
            cols = slice(g * gw, (g + 1) * gw)
            cur = ext_ref[pad + r0:pad + r0 + tt, cols]
            wsum = cur
            for j in range(1, w):
                wsum = wsum + ext_ref[pad + r0 - j:pad + r0 - j + tt, cols]
            cnt = jnp.minimum(w, pos0 + r0 + 1 + row).astype(F32)
            diff = (wsum / cnt - cur).astype(BF16)
            y = jnp.dot(diff, pw_ref[g], preferred_element_type=F32)
            o_ref[0, r0:r0 + tt, cols] = (y * sc_ref[:, cols]).astype(BF16)


def _pool(u, prefix, pool_w_b, pool_scale, *, pos0):
    bsz, t_len, c = u.shape
    pad = prefix.shape[1]
    tt = min(t_len, 256)
    return pl.pallas_call(
        functools.partial(_pool_kernel, t_len=t_len, tt=tt, pos0=pos0),
        grid=(bsz,),
        in_specs=[pl.BlockSpec((1, t_len, c), lambda b: (b, 0, 0)),
                  pl.BlockSpec((1, pad, c), lambda b: (b, 0, 0)),
                  pl.BlockSpec(pool_w_b.shape, lambda b: (0, 0, 0)),
                  pl.BlockSpec((1, c), lambda b: (0, 0))],
        out_specs=pl.BlockSpec((1, t_len, c), lambda b: (b, 0, 0)),
        out_shape=jax.ShapeDtypeStruct((bsz, t_len, c), BF16),
        scratch_shapes=[pltpu.VMEM((pad + t_len, c), F32)],
        compiler_params=_params(("parallel",)),
        name="pool",
    )(u, prefix, pool_w_b, pool_scale)


def _tail_kernel(x_ref, att_ref, pool_ref, g0_ref, b0_ref, woa_ref, wob_ref, g1_ref, b1_ref,
                 wrh_ref, wrl_ref, br_ref,
                 h1_ref, eid_ref, gw_ref, rank_ref, cnt_ref, carry_ref, *, tm):
    @pl.when(pl.program_id(0) == 0)
    def _():
        carry_ref[...] = jnp.zeros(carry_ref.shape, F32)

    h = _layer_norm(x_ref[...], g0_ref[...], b0_ref[...])
    mix = (jnp.dot(att_ref[...], woa_ref[...], preferred_element_type=F32)
           + jnp.dot(pool_ref[...], wob_ref[...], preferred_element_type=F32))
    h1 = _layer_norm(DEEPNORM_ALPHA * h + mix, g1_ref[...], b1_ref[...])
    h1_ref[...] = h1

    hh = h1.astype(BF16)
    hl = (h1 - hh.astype(F32)).astype(BF16)
    logits = (lax.dot_general(wrh_ref[...], hh, NT_DIMS, preferred_element_type=F32)
              + lax.dot_general(wrh_ref[...], hl, NT_DIMS, preferred_element_type=F32)
              + lax.dot_general(wrl_ref[...], hh, NT_DIMS, preferred_element_type=F32)
              + br_ref[...])

    e_iota = lax.broadcasted_iota(I32, (N_EXPERTS, tm), 0).astype(F32)
    work = logits
    vals, hots = [], []
    for k in range(TOP_K):
        m = jnp.max(work, axis=0, keepdims=True)
        idx = jnp.min(jnp.where(work == m, e_iota, float(N_EXPERTS)), axis=0, keepdims=True)
        hot = e_iota == idx
        eid_ref[k:k + 1, :] = idx.astype(I32)
        vals.append(m)
        hots.append(jnp.where(hot, 1.0, 0.0))
        work = jnp.where(hot, -jnp.inf, work)
    exps = [jnp.exp(v - vals[0]) for v in vals]
    den = exps[0] + exps[1] + exps[2] + exps[3]
    for k in range(TOP_K):
        gw_ref[k:k + 1, :] = exps[k] / den

    hot_all = (hots[0] + hots[1] + hots[2] + hots[3]).astype(BF16)
    before = jnp.where(lax.broadcasted_iota(I32, (tm, tm), 0) < lax.broadcasted_iota(I32, (tm, tm), 1),
                       1.0, 0.0).astype(BF16)
    carry = carry_ref[...]
    prefix = jnp.dot(hot_all, before, preferred_element_type=F32) + jnp.tile(carry, (1, tm // LANES))
    for k in range(TOP_K):
        rank_ref[k:k + 1, :] = jnp.sum(hots[k] * prefix, axis=0, keepdims=True).astype(I32)
    total = carry + jnp.dot(hot_all, jnp.ones((tm, LANES), BF16), preferred_element_type=F32)
    carry_ref[...] = total
    cnt_ref[...] = total


def _tail(x2d, att, pool, g0, b0, woa, wob, g1, b1, wrh, wrl, br, *, tm):
    n, d = x2d.shape
    aw = att.shape[1]
    row = lambda i: (i, 0)
    const = lambda i: (0, 0)
    col = lambda i: (0, i)
    return pl.pallas_call(
        functools.partial(_tail_kernel, tm=tm),
        grid=(n // tm,),
        in_specs=[pl.BlockSpec((tm, d), row), pl.BlockSpec((tm, aw), row), pl.BlockSpec((tm, d - aw), row),
                  pl.BlockSpec((1, d), const), pl.BlockSpec((1, d), const),
                  pl.BlockSpec(woa.shape, const), pl.BlockSpec(wob.shape, const),
                  pl.BlockSpec((1, d), const), pl.BlockSpec((1, d), const),
                  pl.BlockSpec(wrh.shape, const), pl.BlockSpec(wrl.shape, const), pl.BlockSpec(br.shape, const)],
        out_specs=(pl.BlockSpec((tm, d), row), pl.BlockSpec((TOP_K, tm), col), pl.BlockSpec((TOP_K, tm), col),
                   pl.BlockSpec((TOP_K, tm), col), pl.BlockSpec((N_EXPERTS, LANES), const)),
        out_shape=(jax.ShapeDtypeStruct((n, d), F32), jax.ShapeDtypeStruct((TOP_K, n), I32),
                   jax.ShapeDtypeStruct((TOP_K, n), F32), jax.ShapeDtypeStruct((TOP_K, n), I32),
                   jax.ShapeDtypeStruct((N_EXPERTS, LANES), F32)),
        scratch_shapes=[pltpu.VMEM((N_EXPERTS, LANES), F32)],
        compiler_params=_params(("arbitrary",)),
        name="tail",
    )(x2d, att, pool, g0, b0, woa, wob, g1, b1, wrh, wrl, br)


def _positions_kernel(offs_ref, eid_ref, rank_ref, pos_ref):
    eid = eid_ref[...]
    pos = rank_ref[...]
    for e in range(N_EXPERTS):
        pos = pos + jnp.where(eid == e, offs_ref[e], 0)
    pos_ref[...] = pos


def _positions(offs, eid_t, rank_t, *, tn):
    k, n = eid_t.shape
    return pl.pallas_call(
        _positions_kernel,
        grid_spec=pltpu.PrefetchScalarGridSpec(
            num_scalar_prefetch=1,
            grid=(n // tn,),
            in_specs=[pl.BlockSpec((k, tn), lambda i, offs: (0, i)), pl.BlockSpec((k, tn), lambda i, offs: (0, i))],
            out_specs=pl.BlockSpec((k, tn), lambda i, offs: (0, i)),
        ),
        out_shape=jax.ShapeDtypeStruct((k, n), I32),
        compiler_params=_params(("parallel",)),
        name="positions",
    )(offs, eid_t, rank_t)


ROW_UNROLL = 8


def _pos_copy(pos_hbm, pos_smem, sem, tile, slot, width):
    return pltpu.make_async_copy(pos_hbm.at[pl.ds(pl.multiple_of(tile * width, width), width)],
                                 pos_smem.at[slot], sem.at[slot])


def _dispatch_kernel(pos_hbm, h1_ref, xs_in_hbm, xs_hbm, pos_smem, pos_sem, row_sem, *, td):
    del xs_in_hbm
    i = pl.program_id(0)
    n = pl.num_programs(0)
    slot = i % 2
    width = TOP_K * td

    @pl.when(i == 0)
    def _():
        _pos_copy(pos_hbm, pos_smem, pos_sem, 0, 0, width).start()

    _pos_copy(pos_hbm, pos_smem, pos_sem, i, slot, width).wait()

    @pl.when(i + 1 < n)
    def _():
        _pos_copy(pos_hbm, pos_smem, pos_sem, i + 1, 1 - slot, width).start()

    def row_copy(r, dst_row):
        return pltpu.make_async_copy(h1_ref.at[pl.ds(r, 1), :], xs_hbm.at[pl.ds(dst_row, 1), :], row_sem)

    def issue(blk, carry):
        for j in range(ROW_UNROLL):
            r = blk * ROW_UNROLL + j
            for k in range(TOP_K):
                row_copy(r, pos_smem[slot, k * td + r]).start()
        return carry

    lax.fori_loop(0, td // ROW_UNROLL, issue, 0)

    def drain(blk, carry):
        for _ in range(ROW_UNROLL * TOP_K):
            row_copy(0, 0).wait()
        return carry

    lax.fori_loop(0, td // ROW_UNROLL, drain, 0)


def _dispatch(pos_flat, h1, xs_zero, *, td):
    n, d = h1.shape
    return pl.pallas_call(
        functools.partial(_dispatch_kernel, td=td),
        grid=(n // td,),
        in_specs=[pl.BlockSpec(memory_space=pl.ANY), pl.BlockSpec((td, d), lambda i: (i, 0)),
                  pl.BlockSpec(memory_space=pl.ANY)],
        out_specs=pl.BlockSpec(memory_space=pl.ANY),
        out_shape=jax.ShapeDtypeStruct(xs_zero.shape, xs_zero.dtype),
        scratch_shapes=[pltpu.SMEM((2, TOP_K * td), I32), pltpu.SemaphoreType.DMA((2,)), pltpu.SemaphoreType.DMA(())],
        input_output_aliases={2: 0},
        compiler_params=_params(("arbitrary",)),
        name="dispatch",
    )(pos_flat, h1, xs_zero)


FF_CHUNK = 256


def _moe_kernel(te_ref, ta_ref, x_ref, w1_ref, b1_ref, w2_ref, b2_ref, y_ref):
    i = pl.program_id(0)
    d_ff = w2_ref.shape[1]

    @pl.when(ta_ref[i] > 0)
    def _():
        xb = x_ref[...].astype(BF16)
        y_ref[...] = jnp.broadcast_to(b2_ref[0], y_ref.shape)
        for j in range(d_ff // FF_CHUNK):
            gs = slice(j * FF_CHUNK, (j + 1) * FF_CHUNK)
            us = slice(d_ff + j * FF_CHUNK, d_ff + (j + 1) * FF_CHUNK)
            gate = jnp.dot(xb, w1_ref[0, :, gs], preferred_element_type=F32) + b1_ref[0, :, gs]
            up = jnp.dot(xb, w1_ref[0, :, us], preferred_element_type=F32) + b1_ref[0, :, us]
            gate = jnp.minimum(gate, SWIGLU_LIMIT)
            up = jnp.clip(up, -SWIGLU_LIMIT, SWIGLU_LIMIT)
            act = (up + 1.0) * (gate * jax.nn.sigmoid(SWIGLU_ALPHA * gate))
            y_ref[...] += jnp.dot(act.astype(BF16), w2_ref[0, gs, :], preferred_element_type=F32)

    @pl.when(ta_ref[i] == 0)
    def _():
        y_ref[...] = jnp.zeros(y_ref.shape, F32)


def _moe(tile_expert, tile_active, xs, w1b, b1, w2b, b2, *, tmoe):
    p, d = xs.shape
    e, _, ff2 = w1b.shape
    d_ff = ff2 // 2
    return pl.pallas_call(
        _moe_kernel,
        grid_spec=pltpu.PrefetchScalarGridSpec(
            num_scalar_prefetch=2,
            grid=(p // tmoe,),
            in_specs=[pl.BlockSpec((tmoe, d), lambda i, te, ta: (i, 0)),
                      pl.BlockSpec((1, d, ff2), lambda i, te, ta: (te[i], 0, 0)),
                      pl.BlockSpec((1, 1, ff2), lambda i, te, ta: (te[i], 0, 0)),
                      pl.BlockSpec((1, d_ff, d), lambda i, te, ta: (te[i], 0, 0)),
                      pl.BlockSpec((1, 1, d), lambda i, te, ta: (te[i], 0, 0))],
            out_specs=pl.BlockSpec((tmoe, d), lambda i, te, ta: (i, 0)),
        ),
        out_shape=jax.ShapeDtypeStruct((p, d), F32),
        compiler_params=_params(("parallel",)),
        name="moe",
    )(tile_expert, tile_active, xs, w1b, b1.reshape(e, 1, ff2), w2b, b2.reshape(e, 1, d))


def _combine_kernel(pos_hbm, h1_ref, gw_ref, g2_ref, b2_ref, ys_hbm, o_ref,
                    pos_smem, ybuf, pos_sem, row_sem, *, tc):
    i = pl.program_id(0)
    n = pl.num_programs(0)
    slot = i % 2
    width = TOP_K * tc

    def row_copy(tile_slot, k, r, src_row):
        return pltpu.make_async_copy(ys_hbm.at[pl.ds(src_row, 1), :], ybuf.at[tile_slot, k, pl.ds(r, 1), :],
                                     row_sem.at[tile_slot])

    def gather_tile(tile, tile_slot):
        _pos_copy(pos_hbm, pos_smem, pos_sem, tile, tile_slot, width).start()
        _pos_copy(pos_hbm, pos_smem, pos_sem, tile, tile_slot, width).wait()

        def issue(blk, carry):
            for j in range(ROW_UNROLL):
                r = blk * ROW_UNROLL + j
                for k in range(TOP_K):
                    row_copy(tile_slot, k, r, pos_smem[tile_slot, k * tc + r]).start()
            return carry

        lax.fori_loop(0, tc // ROW_UNROLL, issue, 0)

    @pl.when(i == 0)
    def _():
        gather_tile(0, 0)

    @pl.when(i + 1 < n)
    def _():
        gather_tile(i + 1, 1 - slot)

    def drain(blk, carry):
        for _ in range(ROW_UNROLL * TOP_K):
            row_copy(slot, 0, 0, 0).wait()
        return carry

    lax.fori_loop(0, tc // ROW_UNROLL, drain, 0)

    gw = gw_ref[...]
    m = gw[:, 0:1] * ybuf[slot, 0]
    for k in range(1, TOP_K):
        m = m + gw[:, k:k + 1] * ybuf[slot, k]
    o_ref[...] = _layer_norm(DEEPNORM_ALPHA * h1_ref[...] + m, g2_ref[...], b2_ref[...])


def _combine(pos_flat, h1, gw, g2, b2, ys, *, tc):
    n, d = h1.shape
    return pl.pallas_call(
        functools.partial(_combine_kernel, tc=tc),
        grid=(n // tc,),
        in_specs=[pl.BlockSpec(memory_space=pl.ANY), pl.BlockSpec((tc, d), lambda i: (i, 0)),
                  pl.BlockSpec((tc, TOP_K), lambda i: (i, 0)),
                  pl.BlockSpec((1, d), lambda i: (0, 0)), pl.BlockSpec((1, d), lambda i: (0, 0)),
                  pl.BlockSpec(memory_space=pl.ANY)],
        out_specs=pl.BlockSpec((tc, d), lambda i: (i, 0)),
        out_shape=jax.ShapeDtypeStruct((n, d), F32),
        scratch_shapes=[pltpu.SMEM((2, TOP_K * tc), I32), pltpu.VMEM((2, TOP_K, tc, d), F32),
                        pltpu.SemaphoreType.DMA((2,)), pltpu.SemaphoreType.DMA((2,))],
        compiler_params=_params(("arbitrary",)),
        name="combine",
    )(pos_flat, h1, gw, g2, b2, ys)


def _tile_major(a_t, tile):
    k, n = a_t.shape
    return a_t.reshape(k, n // tile, tile).transpose(1, 0, 2).reshape(-1)


def _block_tail(x2d, att, pool, wts, *, tm, tmoe, td, tc):
    n, d = x2d.shape
    tm, td, tc = min(tm, n), min(td, n), min(tc, n)
    h1, eid_t, gw_t, rank_t, cnt = _tail(x2d, att, pool, wts["g0"], wts["b0"], wts["woa"], wts["wob"],
                                         wts["g1"], wts["b1"], wts["wrh"], wts["wrl"], wts["br"], tm=tm)
    counts = cnt[:, 0].astype(I32)
    padded = ((counts + tmoe - 1) // tmoe) * tmoe
    ends = jnp.cumsum(padded)
    offs = ends - padded
    n_tiles = (TOP_K * n) // tmoe + N_EXPERTS
    tile_start = jnp.arange(n_tiles, dtype=I32) * tmoe
    tile_expert = jnp.minimum(jnp.sum((ends[None, :] <= tile_start[:, None]).astype(I32), axis=1), N_EXPERTS - 1)
    tile_active = (tile_start < ends[-1]).astype(I32)

    pos_t = _positions(offs, eid_t, rank_t, tn=min(n, 8192))
    xs_zero = jnp.zeros((n_tiles * tmoe, d), F32)
    xs = _dispatch(_tile_major(pos_t, td), h1, xs_zero, td=td)
    ys = _moe(tile_expert, tile_active, xs, wts["w1b"], wts["b1e"], wts["w2b"], wts["b2e"], tmoe=tmoe)
    return _combine(_tile_major(pos_t, tc), h1, gw_t.T, wts["g2"], wts["b2"], ys, tc=tc)


def kernel(x_prompt, x_sample, cache_k, cache_v, cache_kidx, state_pool, ln0_g, ln0_b, w_in, w_o,
           pool_w, pool_scale, ln1_g, ln1_b, w_router, b_router, w1, b1, w2, b2, ln2_g, ln2_b):
    bp, s_len, d = x_prompt.shape
    bs, t_len, _ = x_sample.shape
    l_past = cache_k.shape[2]
    aw = N_HEADS * HEAD_DIM
    pw = d - aw
    lyr = 0
    lc = 256

    k_off = aw
    v_off = k_off + HEAD_DIM
    qi_off = v_off + HEAD_DIM
    ki_off = qi_off + IDX_HEADS * IDX_DIM
    wi_off = ki_off + IDX_DIM
    u_off = wi_off + IDX_HEADS
    win = w_in[lyr]
    wa = jnp.concatenate([win[:, 0:k_off], win[:, qi_off:ki_off], win[:, u_off:u_off + pw]], axis=1).astype(BF16)
    wb = jnp.concatenate([win[:, k_off:v_off], win[:, v_off:qi_off], win[:, ki_off:wi_off],
                          jnp.zeros((d, 64), F32)], axis=1).astype(BF16)
    wt = jnp.concatenate([win[:, v_off:qi_off].T, win[:, wi_off:u_off].T, jnp.zeros((8, d), F32)], axis=0).astype(BF16)

    g0 = ln0_g.reshape(1, d)
    b0 = ln0_b.reshape(1, d)
    wrt = w_router[lyr].T
    wrh = wrt.astype(BF16)
    wts = dict(
        g0=g0, b0=b0,
        woa=w_o[lyr][:aw].astype(BF16), wob=w_o[lyr][aw:].astype(BF16),
        g1=ln1_g[lyr].reshape(1, d), b1=ln1_b[lyr].reshape(1, d),
        wrh=wrh, wrl=(wrt - wrh.astype(F32)).astype(BF16), br=b_router[lyr].reshape(N_EXPERTS, 1),
        w1b=w1[lyr].astype(BF16), b1e=b1[lyr], w2b=w2[lyr].astype(BF16), b2e=b2[lyr],
        g2=ln2_g[lyr].reshape(1, d), b2=ln2_b[lyr].reshape(1, d),
    )
    pool_w_b = pool_w[lyr].astype(BF16)
    pool_sc = pool_scale[lyr].reshape(1, pw)

    xp = x_prompt.reshape(bp * s_len, d)
    q, qi, u, k, v, ki, kb, kib, vt, wit = _proj(xp, g0, b0, wa, wb, wt, tm=512, lc=lc)
    att_p = _dsa(q.reshape(bp, s_len, aw), qi.reshape(bp, s_len, aw), wit,
                 kb.reshape(bp, s_len, HEAD_DIM), kib.reshape(bp, s_len, IDX_DIM), vt,
                 tq=256, lc=lc, causal=True, l_valid=s_len, q_pos0=0, topk=min(TOPK_MAX, s_len // 4))
    u_p = u.reshape(bp, s_len, pw)
    pool_p = _pool(u_p, jnp.zeros((bp, POOL_PAST + 1, pw), F32), pool_w_b, pool_sc, pos0=0)
    y_p = _block_tail(xp, att_p.reshape(bp * s_len, aw), pool_p.reshape(bp * s_len, pw), wts,
                      tm=512, tmoe=512, td=512, tc=256)

    xs = x_sample.reshape(bs * t_len, d)
    qs, qis, us, kn, vn, kin, _, _, _, wits = _proj(xs, g0, b0, wa, wb, wt, tm=512, lc=lc)
    l_all = l_past + t_len
    l_pad = -(-l_all // lc) * lc
    tq_s = LANES
    pad_keys = lambda a: jnp.pad(a, ((0, 0), (0, l_pad - l_all), (0, 0)))
    k_all = pad_keys(jnp.concatenate([cache_k[lyr], kn.reshape(bs, t_len, HEAD_DIM)], axis=1))
    v_all = pad_keys(jnp.concatenate([cache_v[lyr], vn.reshape(bs, t_len, HEAD_DIM)], axis=1))
    ki_all = pad_keys(jnp.concatenate([cache_kidx[lyr], kin.reshape(bs, t_len, IDX_DIM)], axis=1))
    vt_all = v_all.reshape(bs, l_pad // lc, lc, HEAD_DIM).transpose(0, 1, 3, 2).reshape(-1, HEAD_DIM, lc)
    pad_q = lambda a: jnp.pad(a.reshape(bs, t_len, aw), ((0, 0), (0, tq_s - t_len), (0, 0)))
    wit_s = jnp.pad(wits.reshape(IDX_HEADS, bs, t_len), ((0, 0), (0, 0), (0, tq_s - t_len))).reshape(IDX_HEADS, -1)
    att_s = _dsa(pad_q(qs), pad_q(qis), wit_s, k_all.astype(BF16), ki_all.astype(BF16), vt_all.astype(BF16),
                 tq=tq_s, lc=lc, causal=False, l_valid=l_all, q_pos0=l_past, topk=min(TOPK_MAX, l_all // 4))
    att_s = att_s[:, :t_len].reshape(bs * t_len, aw)
    us3 = us.reshape(bs, t_len, pw)
    prefix_s = jnp.concatenate([jnp.zeros((bs, 1, pw), F32), state_pool[lyr]], axis=1)
    pool_s = _pool(us3, prefix_s, pool_w_b, pool_sc, pos0=l_past)
    y_s = _block_tail(xs, att_s, pool_s.reshape(bs * t_len, pw), wts, tm=512, tmoe=128, td=512, tc=256)

    pool_state_p = u_p[:, s_len - POOL_PAST:]
    pool_state_s = jnp.concatenate([state_pool[lyr], us3], axis=1)[:, -POOL_PAST:]
    return (y_p.reshape(bp, s_len, d), y_s.reshape(bs, t_len, d),
            k.reshape(1, bp, s_len, HEAD_DIM), v.reshape(1, bp, s_len, HEAD_DIM),
            ki.reshape(1, bp, s_len, IDX_DIM), pool_state_p[None],
            kn.reshape(1, bs, t_len, HEAD_DIM), vn.reshape(1, bs, t_len, HEAD_DIM),
            kin.reshape(1, bs, t_len, IDX_DIM), pool_state_s[None])
```

```python
import functools

import jax
import jax.numpy as jnp
from jax import lax
from jax.experimental import pallas as pl
from jax.experimental.pallas import tpu as pltpu

F32 = jnp.float32
BF16 = jnp.bfloat16
I32 = jnp.int32

CHUNK = 64
CHUNK_SHIFT = 6
assert 1 << CHUNK_SHIFT == CHUNK
N_HEADS = 8
HEAD_DIM = 64
IDX_HEADS = 8
IDX_DIM = 64
TOPK_MAX = 256
POOL_WINDOWS = (2, 4, 8, 16)
POOL_PAST = 15
N_EXPERTS = 32
TOP_K = 4
SWIGLU_LIMIT = 7.0
SWIGLU_ALPHA = 1.702
LN_EPS = 1e-5
DEPTH = 1
DEEPNORM_ALPHA = (2 * DEPTH) ** 0.25

LANES = 128
SUBLANES = 8
VMEM_LIMIT_BYTES = 56 * 1024 * 1024

NEG_BIG = -1e30
KEY_NEG_INF = -2139095041
KEY_POS_INF = 2139095040

NT_DIMS = (((1,), (1,)), ((), ()))


def _layer_norm(x, g, b):
    mu = jnp.mean(x, axis=-1, keepdims=True)
    xc = x - mu
    var = jnp.mean(xc * xc, axis=-1, keepdims=True)
    return xc * lax.rsqrt(var + LN_EPS) * g + b


def _params(sem):
    return pltpu.CompilerParams(dimension_semantics=sem, vmem_limit_bytes=VMEM_LIMIT_BYTES)


def _proj_kernel(x_ref, g_ref, b_ref, wa_ref, wb_ref, wt_ref,
                 q_ref, qi_ref, u_ref, k_ref, v_ref, ki_ref, kb_ref, kib_ref, vt_ref, wit_ref, *, lc):
    h = _layer_norm(x_ref[...], g_ref[...], b_ref[...])
    hb = h.astype(BF16)
    aw = N_HEADS * HEAD_DIM
    pa = jnp.dot(hb, wa_ref[...], preferred_element_type=F32)
    q_ref[...] = (pa[:, :aw] * (HEAD_DIM ** -0.5)).astype(BF16)
    qi_ref[...] = pa[:, aw:2 * aw].astype(BF16)
    u_ref[...] = pa[:, 2 * aw:]
    pb = jnp.dot(hb, wb_ref[...], preferred_element_type=F32)
    k = pb[:, 0:HEAD_DIM]
    v = pb[:, HEAD_DIM:2 * HEAD_DIM]
    ki = pb[:, 2 * HEAD_DIM:2 * HEAD_DIM + IDX_DIM]
    k_ref[...] = k
    v_ref[...] = v
    ki_ref[...] = ki
    kb_ref[...] = k.astype(BF16)
    kib_ref[...] = ki.astype(BF16)
    pt = lax.dot_general(wt_ref[...], hb, NT_DIMS, preferred_element_type=F32)
    for c in range(vt_ref.shape[0]):
        vt_ref[c] = pt[0:HEAD_DIM, c * lc:(c + 1) * lc].astype(BF16)
    wi = pt[HEAD_DIM:HEAD_DIM + IDX_HEADS, :]
    wit_ref[...] = (wi * (IDX_HEADS ** -0.5)) * (IDX_DIM ** -0.5)


def _proj(x2d, g, b, wa, wb, wt, *, tm, lc):
    n, d = x2d.shape
    tm = min(tm, n)
    aw = N_HEADS * HEAD_DIM
    uw = wa.shape[1] - 2 * aw
    row = lambda i: (i, 0)
    const = lambda i: (0, 0)
    out_shape = (
        jax.ShapeDtypeStruct((n, aw), BF16),
        jax.ShapeDtypeStruct((n, aw), BF16),
        jax.ShapeDtypeStruct((n, uw), F32),
        jax.ShapeDtypeStruct((n, HEAD_DIM), F32),
        jax.ShapeDtypeStruct((n, HEAD_DIM), F32),
        jax.ShapeDtypeStruct((n, IDX_DIM), F32),
        jax.ShapeDtypeStruct((n, HEAD_DIM), BF16),
        jax.ShapeDtypeStruct((n, IDX_DIM), BF16),
        jax.ShapeDtypeStruct((n // lc, HEAD_DIM, lc), BF16),
        jax.ShapeDtypeStruct((IDX_HEADS, n), F32),
    )
    out_specs = (
        pl.BlockSpec((tm, aw), row), pl.BlockSpec((tm, aw), row), pl.BlockSpec((tm, uw), row),
        pl.BlockSpec((tm, HEAD_DIM), row), pl.BlockSpec((tm, HEAD_DIM), row), pl.BlockSpec((tm, IDX_DIM), row),
        pl.BlockSpec((tm, HEAD_DIM), row), pl.BlockSpec((tm, IDX_DIM), row),
        pl.BlockSpec((tm // lc, HEAD_DIM, lc), lambda i: (i, 0, 0)),
        pl.BlockSpec((IDX_HEADS, tm), lambda i: (0, i)),
    )
    return pl.pallas_call(
        functools.partial(_proj_kernel, lc=lc),
        grid=(n // tm,),
        in_specs=[pl.BlockSpec((tm, d), row), pl.BlockSpec((1, d), const), pl.BlockSpec((1, d), const),
                  pl.BlockSpec(wa.shape, const), pl.BlockSpec(wb.shape, const), pl.BlockSpec(wt.shape, const)],
        out_specs=out_specs,
        out_shape=out_shape,
        compiler_params=_params(("parallel",)),
        name="proj",
    )(x2d, g, b, wa, wb, wt)


def _key_to_float(key):
    bits = jnp.where(key >= 0, key, key ^ jnp.int32(0x7FFFFFFF))
    return lax.bitcast_convert_type(bits, F32)


def _dsa_kernel(q_ref, qi_ref, wit_ref, kb_ref, kib_ref, vt_ref, o_ref,
                sc_ref, s_ref, out_ref, *acc_refs, tq, lc, nk_static, causal, l_valid, q_pos0, topk):
    qb = pl.program_id(1)
    nk = qb * (tq // lc) + (tq // lc) if causal else nk_static
    q_chunk = (q_pos0 + qb * tq + lax.broadcasted_iota(I32, (lc, tq), 1)) >> CHUNK_SHIFT
    row_iota = lax.broadcasted_iota(I32, (lc, tq), 0)

    def score_chunk(kc, carry):
        off = pl.multiple_of(kc * lc, lc)
        kic = kib_ref[0, pl.ds(off, lc), :]
        acc = jnp.zeros((lc, tq), F32)
        for h in range(IDX_HEADS):
            s = lax.dot_general(kic, qi_ref[0, :, h * IDX_DIM:(h + 1) * IDX_DIM], NT_DIMS,
                                preferred_element_type=F32)
            acc = acc + wit_ref[h:h + 1, :] * jnp.maximum(s, 0.0)
        l_pos = off + row_iota
        visible = ((l_pos >> CHUNK_SHIFT) <= q_chunk) & (l_pos < l_valid)
        sc_ref[pl.ds(off, lc), :] = jnp.where(visible, acc, -jnp.inf)
        return carry

    lax.fori_loop(0, nk, score_chunk, 0)

    def count(pred):
        ways = 4 * SUBLANES

        def body(kc, part):
            off = pl.multiple_of(kc * lc, lc)
            hit = jnp.where(pred(sc_ref[pl.ds(off, lc), :]), 1.0, 0.0)
            return part + jnp.sum(hit.reshape(lc // ways, ways, tq), axis=0)
        part = lax.fori_loop(0, nk, body, jnp.zeros((ways, tq), F32))
        return jnp.sum(part, axis=0, keepdims=True)

    def bisect(_, carry):
        lo, hi = carry
        mid = (lo >> 1) + (hi >> 1) + (lo & hi & 1)
        cand = _key_to_float(mid)
        ok = count(lambda blk: blk >= cand) >= topk
        return jnp.where(ok, mid, lo), jnp.where(ok, hi, mid)

    lo, _ = lax.fori_loop(0, 32, bisect,
                          (jnp.full((1, tq), KEY_NEG_INF, I32), jnp.full((1, tq), KEY_POS_INF + 1, I32)))
    thr = _key_to_float(lo)
    n_above = count(lambda blk: blk > thr)
    n_ties = topk - n_above

    tri = jnp.where(lax.broadcasted_iota(I32, (lc, lc), 0) >= lax.broadcasted_iota(I32, (lc, lc), 1),
                    1.0, 0.0).astype(BF16)

    def fold8(x):
        return x.reshape(lc // SUBLANES, SUBLANES, tq)

    def logits_chunk(kc, carry):
        ties_before, m8 = carry
        off = pl.multiple_of(kc * lc, lc)
        blk = sc_ref[pl.ds(off, lc), :]
        tie = blk == thr
        tie_rank = jnp.dot(tri, jnp.where(tie, 1.0, 0.0).astype(BF16), preferred_element_type=F32) + ties_before
        bias = jnp.where(blk > thr, 0.0, jnp.where(tie, jnp.where(tie_rank <= n_ties, 0.0, NEG_BIG), NEG_BIG))
        bias = jnp.where(blk == -jnp.inf, NEG_BIG, bias)
        kc_b = kb_ref[0, pl.ds(off, lc), :]
        m_rows = []
        for h in range(N_HEADS):
            s = lax.dot_general(kc_b, q_ref[0, :, h * HEAD_DIM:(h + 1) * HEAD_DIM], NT_DIMS,
                                preferred_element_type=F32) + bias
            s_ref[h, pl.ds(off, lc), :] = s
            m_rows.append(jnp.maximum(m8[h * SUBLANES:(h + 1) * SUBLANES], jnp.max(fold8(s), axis=0)))
        return tie_rank[lc - 1:lc, :], jnp.concatenate(m_rows, axis=0)

    _, m8 = lax.fori_loop(0, nk, logits_chunk,
                          (jnp.zeros((1, tq), F32), jnp.full((N_HEADS * SUBLANES, tq), NEG_BIG, F32)))
    m_all = jnp.max(m8.reshape(N_HEADS, SUBLANES, tq), axis=1)

    for acc_ref in acc_refs:
        acc_ref[...] = jnp.zeros(acc_ref.shape, F32)

    def pv_chunk(kc, l8):
        off = pl.multiple_of(kc * lc, lc)
        vt_c = vt_ref[kc]
        l_rows = []
        for h in range(N_HEADS):
            p = jnp.exp(s_ref[h, pl.ds(off, lc), :] - m_all[h:h + 1, :])
            l_rows.append(l8[h * SUBLANES:(h + 1) * SUBLANES] + jnp.sum(fold8(p), axis=0))
            acc_refs[h][...] += jnp.dot(vt_c, p.astype(BF16), preferred_element_type=F32)
        return jnp.concatenate(l_rows, axis=0)

    l8 = lax.fori_loop(0, nk, pv_chunk, jnp.zeros((N_HEADS * SUBLANES, tq), F32))
    l_all = jnp.sum(l8.reshape(N_HEADS, SUBLANES, tq), axis=1)

    for h in range(N_HEADS):
        out_ref[h * HEAD_DIM:(h + 1) * HEAD_DIM, :] = acc_refs[h][...] / l_all[h:h + 1, :]
    o_ref[0] = out_ref[...].T.astype(BF16)


def _dsa(q, qi, wit, kb, kib, vt, *, tq, lc, causal, l_valid, q_pos0, topk):
    bsz, tq_tot, aw = q.shape
    l_tot = kb.shape[1]
    nq = tq_tot // tq
    nkc = l_tot // lc
    kern = functools.partial(_dsa_kernel, tq=tq, lc=lc, nk_static=nkc, causal=causal, l_valid=l_valid,
                             q_pos0=q_pos0, topk=topk)
    return pl.pallas_call(
        kern,
        grid=(bsz, nq),
        in_specs=[
            pl.BlockSpec((1, tq, aw), lambda b, i: (b, i, 0)),
            pl.BlockSpec((1, tq, aw), lambda b, i: (b, i, 0)),
            pl.BlockSpec((IDX_HEADS, tq), lambda b, i: (0, b * nq + i)),
            pl.BlockSpec((1, l_tot, HEAD_DIM), lambda b, i: (b, 0, 0)),
            pl.BlockSpec((1, l_tot, IDX_DIM), lambda b, i: (b, 0, 0)),
            pl.BlockSpec((nkc, HEAD_DIM, lc), lambda b, i: (b, 0, 0)),
        ],
        out_specs=pl.BlockSpec((1, tq, aw), lambda b, i: (b, i, 0)),
        out_shape=jax.ShapeDtypeStruct((bsz, tq_tot, aw), BF16),
        scratch_shapes=[
            pltpu.VMEM((l_tot, tq), F32),
            pltpu.VMEM((N_HEADS, l_tot, tq), F32),
            pltpu.VMEM((aw, tq), F32),
        ] + [pltpu.VMEM((HEAD_DIM, tq), F32) for _ in range(N_HEADS)],
        compiler_params=_params(("parallel", "parallel")),
        name="dsa",
    )(q, qi, wit, kb, kib, vt)


def _pool_kernel(u_ref, pre_ref, pw_ref, sc_ref, o_ref, ext_ref, *, t_len, tt, pos0):
    pad = pre_ref.shape[1]
    gw = pw_ref.shape[1]
    ext_ref[0:pad, :] = pre_ref[0]
    ext_ref[pad:pad + t_len, :] = u_ref[0]
    row = lax.broadcasted_iota(I32, (tt, gw), 0)
    for t in range(t_len // tt):
        r0 = t * tt
        for g, w in enumerate(POOL_WINDOWS):
            cols = slice(g * gw, (g + 1) * gw)
            cur = ext_ref[pad + r0:pad + r0 + tt, cols]
            wsum = cur
            for j in range(1, w):
                wsum = wsum + ext_ref[pad + r0 - j:pad + r0 - j + tt, cols]
            cnt = jnp.minimum(w, pos0 + r0 + 1 + row).astype(F32)
            diff = (wsum / cnt - cur).astype(BF16)
            y = jnp.dot(diff, pw_ref[g], preferred_element_type=F32)
            o_ref[0, r0:r0 + tt, cols] = (y * sc_ref[:, cols]).astype(BF16)


def _pool(u, prefix, pool_w_b, pool_scale, *, pos0):
    bsz, t_len, c = u.shape
    pad = prefix.shape[1]
    tt = min(t_len, 256)
    return pl.pallas_call(
        functools.partial(_pool_kernel, t_len=t_len, tt=tt, pos0=pos0),
        grid=(bsz,),
        in_specs=[pl.BlockSpec((1, t_len, c), lambda b: (b, 0, 0)),
                  pl.BlockSpec((1, pad, c), lambda b: (b, 0, 0)),
                  pl.BlockSpec(pool_w_b.shape, lambda b: (0, 0, 0)),
                  pl.BlockSpec((1, c), lambda b: (0, 0))],
        out_specs=pl.BlockSpec((1, t_len, c), lambda b: (b, 0, 0)),
        out_shape=jax.ShapeDtypeStruct((bsz, t_len, c), BF16),
        scratch_shapes=[pltpu.VMEM((pad + t_len, c), F32)],
        compiler_params=_params(("parallel",)),
        name="pool",
    )(u, prefix, pool_w_b, pool_scale)


ROW_ALIGN = 16


def _tail_kernel(x_ref, att_ref, pool_ref, g0_ref, b0_ref, woa_ref, wob_ref, g1_ref, b1_ref,
                 wrh_ref, wrl_ref, br_ref,
                 h1_ref, cpos_ref, gw_ref, seg_ref, len_ref, dst_ref, used_ref, carry_ref, *, tm):
    @pl.when(pl.program_id(0) == 0)
    def _():
        carry_ref[...] = jnp.zeros(carry_ref.shape, F32)

    h = _layer_norm(x_ref[...], g0_ref[...], b0_ref[...])
    mix = (jnp.dot(att_ref[...], woa_ref[...], preferred_element_type=F32)
           + jnp.dot(pool_ref[...], wob_ref[...], preferred_element_type=F32))
    h1 = _layer_norm(DEEPNORM_ALPHA * h + mix, g1_ref[...], b1_ref[...])
    h1_ref[...] = h1

    hh = h1.astype(BF16)
    hl = (h1 - hh.astype(F32)).astype(BF16)
    logits = (lax.dot_general(wrh_ref[...], hh, NT_DIMS, preferred_element_type=F32)
              + lax.dot_general(wrh_ref[...], hl, NT_DIMS, preferred_element_type=F32)
              + lax.dot_general(wrl_ref[...], hh, NT_DIMS, preferred_element_type=F32)
              + br_ref[...])

    e_iota = lax.broadcasted_iota(I32, (N_EXPERTS, tm), 0).astype(F32)
    work = logits
    vals, hots = [], []
    for k in range(TOP_K):
        m = jnp.max(work, axis=0, keepdims=True)
        idx = jnp.min(jnp.where(work == m, e_iota, float(N_EXPERTS)), axis=0, keepdims=True)
        hot = e_iota == idx
        vals.append(m)
        hots.append(jnp.where(hot, 1.0, 0.0))
        work = jnp.where(hot, -jnp.inf, work)
    exps = [jnp.exp(v - vals[0]) for v in vals]
    den = exps[0] + exps[1] + exps[2] + exps[3]
    for k in range(TOP_K):
        gw_ref[k:k + 1, :] = exps[k] / den

    hot_all = (hots[0] + hots[1] + hots[2] + hots[3]).astype(BF16)
    count = jnp.dot(hot_all, jnp.ones((tm, LANES), BF16), preferred_element_type=F32)
    chunk_len = jnp.ceil(count * (1.0 / ROW_ALIGN)) * ROW_ALIGN
    lower = jnp.where(lax.broadcasted_iota(I32, (N_EXPERTS, N_EXPERTS), 1)
                      < lax.broadcasted_iota(I32, (N_EXPERTS, N_EXPERTS), 0), 1.0, 0.0).astype(BF16)
    seg_base = jnp.dot(lower, chunk_len.astype(BF16), preferred_element_type=F32)
    before = jnp.where(lax.broadcasted_iota(I32, (tm, tm), 0) < lax.broadcasted_iota(I32, (tm, tm), 1),
                       1.0, 0.0).astype(BF16)
    slot = jnp.dot(hot_all, before, preferred_element_type=F32) + jnp.tile(seg_base, (1, tm // LANES))
    for k in range(TOP_K):
        cpos_ref[k:k + 1, :] = jnp.sum(hots[k] * slot, axis=0, keepdims=True).astype(I32)
    carry = carry_ref[...]
    seg_ref[0] = seg_base
    len_ref[0] = chunk_len
    dst_ref[0] = carry
    carry_ref[...] = carry + chunk_len
    used_ref[...] = carry + chunk_len


def _tail(x2d, att, pool, g0, b0, woa, wob, g1, b1, wrh, wrl, br, *, tm):
    n, d = x2d.shape
    aw = att.shape[1]
    nt = n // tm
    row = lambda i: (i, 0)
    const = lambda i: (0, 0)
    col = lambda i: (0, i)
    tab = lambda i: (i, 0, 0)
    tab_shape = jax.ShapeDtypeStruct((nt, N_EXPERTS, LANES), F32)
    tab_spec = pl.BlockSpec((1, N_EXPERTS, LANES), tab)
    return pl.pallas_call(
        functools.partial(_tail_kernel, tm=tm),
        grid=(nt,),
        in_specs=[pl.BlockSpec((tm, d), row), pl.BlockSpec((tm, aw), row), pl.BlockSpec((tm, d - aw), row),
                  pl.BlockSpec((1, d), const), pl.BlockSpec((1, d), const),
                  pl.BlockSpec(woa.shape, const), pl.BlockSpec(wob.shape, const),
                  pl.BlockSpec((1, d), const), pl.BlockSpec((1, d), const),
                  pl.BlockSpec(wrh.shape, const), pl.BlockSpec(wrl.shape, const), pl.BlockSpec(br.shape, const)],
        out_specs=(pl.BlockSpec((tm, d), row), pl.BlockSpec((TOP_K, tm), col), pl.BlockSpec((TOP_K, tm), col),
                   tab_spec, tab_spec, tab_spec, pl.BlockSpec((N_EXPERTS, LANES), const)),
        out_shape=(jax.ShapeDtypeStruct((n, d), F32), jax.ShapeDtypeStruct((TOP_K, n), I32),
                   jax.ShapeDtypeStruct((TOP_K, n), F32), tab_shape, tab_shape, tab_shape,
                   jax.ShapeDtypeStruct((N_EXPERTS, LANES), F32)),
        scratch_shapes=[pltpu.VMEM((N_EXPERTS, LANES), F32)],
        compiler_params=_params(("arbitrary",)),
        name="tail",
    )(x2d, att, pool, g0, b0, woa, wob, g1, b1, wrh, wrl, br)


G_BLOCK = 256


def _stage_rows(tm):
    raw = TOP_K * tm + (ROW_ALIGN - 1) * N_EXPERTS
    return -(-raw // G_BLOCK) * G_BLOCK


def _chunk_copies(seg_ref, len_ref, dst_ref, tile, max_len, make_copy, act):
    def per_expert(e, carry):
        idx = tile * N_EXPERTS + e
        seg, ln, dst = seg_ref[idx], len_ref[idx], dst_ref[idx]
        size = ROW_ALIGN
        while size <= max_len:
            done = ln & (-2 * size)

            @pl.when((ln & size) != 0)
            def _(size=size, done=done):
                act(make_copy(pl.multiple_of(seg + done, ROW_ALIGN), pl.multiple_of(dst + done, ROW_ALIGN), size))
            size *= 2
        return carry

    lax.fori_loop(0, N_EXPERTS, per_expert, 0)


def _route_onehot(iota, pos_list, val_list):
    g = jnp.zeros(iota.shape, F32)
    for pos, val in zip(pos_list, val_list):
        g = jnp.where(iota == pos, val, g)
    return g.astype(BF16)


def _dispatch_kernel(seg_ref, len_ref, dst_ref, h1_ref, cpos_ref, xs_hbm, stage_ref, sem, *, tm):
    tile = pl.program_id(0)
    h1b = h1_ref[...].astype(BF16)
    pos_rows = [cpos_ref[k:k + 1, :] for k in range(TOP_K)]
    for blk in range(stage_ref.shape[0] // G_BLOCK):
        iota = blk * G_BLOCK + lax.broadcasted_iota(I32, (G_BLOCK, tm), 0)
        g = _route_onehot(iota, pos_rows, [1.0] * TOP_K)
        stage_ref[blk * G_BLOCK:(blk + 1) * G_BLOCK, :] = jnp.dot(g, h1b, preferred_element_type=F32).astype(BF16)

    def make_copy(seg, dst, size):
        return pltpu.make_async_copy(stage_ref.at[pl.ds(seg, size), :], xs_hbm.at[pl.ds(dst, size), :], sem)

    _chunk_copies(seg_ref, len_ref, dst_ref, tile, tm, make_copy, lambda c: c.start())
    _chunk_copies(seg_ref, len_ref, dst_ref, tile, tm, make_copy, lambda c: c.wait())


def _dispatch(seg, ln, dst, h1, cpos, *, tm, p_rows):
    n, d = h1.shape
    return pl.pallas_call(
        functools.partial(_dispatch_kernel, tm=tm),
        grid_spec=pltpu.PrefetchScalarGridSpec(
            num_scalar_prefetch=3,
            grid=(n // tm,),
            in_specs=[pl.BlockSpec((tm, d), lambda i, *_: (i, 0)), pl.BlockSpec((TOP_K, tm), lambda i, *_: (0, i))],
            out_specs=pl.BlockSpec(memory_space=pl.ANY),
            scratch_shapes=[pltpu.VMEM((_stage_rows(tm), d), BF16), pltpu.SemaphoreType.DMA(())],
        ),
        out_shape=jax.ShapeDtypeStruct((p_rows, d), BF16),
        compiler_params=_params(("arbitrary",)),
        name="dispatch",
    )(seg, ln, dst, h1, cpos)


FF_CHUNK = 256


def _moe_kernel(te_ref, tv_ref, x_ref, w1_ref, b1_ref, w2_ref, b2_ref, y_ref, acc_ref):
    i = pl.program_id(0)
    d_ff = w2_ref.shape[1]
    valid = tv_ref[i]

    @pl.when(valid > 0)
    def _():
        rows = lax.broadcasted_iota(I32, x_ref.shape, 0)
        xb = jnp.where(rows < valid, x_ref[...].astype(F32), 0.0).astype(BF16)
        acc_ref[...] = jnp.broadcast_to(b2_ref[0], acc_ref.shape)
        for j in range(d_ff // FF_CHUNK):
            gs = slice(j * FF_CHUNK, (j + 1) * FF_CHUNK)
            us = slice(d_ff + j * FF_CHUNK, d_ff + (j + 1) * FF_CHUNK)
            gate = jnp.dot(xb, w1_ref[0, :, gs], preferred_element_type=F32) + b1_ref[0, :, gs]
            up = jnp.dot(xb, w1_ref[0, :, us], preferred_element_type=F32) + b1_ref[0, :, us]
            gate = jnp.minimum(gate, SWIGLU_LIMIT)
            up = jnp.clip(up, -SWIGLU_LIMIT, SWIGLU_LIMIT)
            act = (up + 1.0) * (gate * jax.nn.sigmoid(SWIGLU_ALPHA * gate))
            acc_ref[...] += jnp.dot(act.astype(BF16), w2_ref[0, gs, :], preferred_element_type=F32)
        y_ref[...] = acc_ref[...].astype(BF16)

    @pl.when(valid <= 0)
    def _():
        y_ref[...] = jnp.zeros(y_ref.shape, BF16)


def _moe(tile_expert, tile_valid, xs, w1b, b1, w2b, b2, *, tmoe):
    p, d = xs.shape
    e, _, ff2 = w1b.shape
    d_ff = ff2 // 2
    return pl.pallas_call(
        _moe_kernel,
        grid_spec=pltpu.PrefetchScalarGridSpec(
            num_scalar_prefetch=2,
            grid=(p // tmoe,),
            in_specs=[pl.BlockSpec((tmoe, d), lambda i, te, tv: (i, 0)),
                      pl.BlockSpec((1, d, ff2), lambda i, te, tv: (te[i], 0, 0)),
                      pl.BlockSpec((1, 1, ff2), lambda i, te, tv: (te[i], 0, 0)),
                      pl.BlockSpec((1, d_ff, d), lambda i, te, tv: (te[i], 0, 0)),
                      pl.BlockSpec((1, 1, d), lambda i, te, tv: (te[i], 0, 0))],
            out_specs=pl.BlockSpec((tmoe, d), lambda i, te, tv: (i, 0)),
            scratch_shapes=[pltpu.VMEM((tmoe, d), F32)],
        ),
        out_shape=jax.ShapeDtypeStruct((p, d), BF16),
        compiler_params=_params(("parallel",)),
        name="moe",
    )(tile_expert, tile_valid, xs, w1b, b1.reshape(e, 1, ff2), w2b, b2.reshape(e, 1, d))


def _combine_kernel(seg_ref, len_ref, dst_ref, h1_ref, cpos_ref, gw_ref, g2_ref, b2_ref, ys_hbm, o_ref,
                    stage_ref, gate_ref, sem, *, tm):
    tile = pl.program_id(0)

    @pl.when(tile == 0)
    def _():
        stage_ref[...] = jnp.zeros(stage_ref.shape, BF16)

    def make_copy(seg, dst, size):
        return pltpu.make_async_copy(ys_hbm.at[pl.ds(dst, size), :], stage_ref.at[pl.ds(seg, size), :], sem)

    _chunk_copies(seg_ref, len_ref, dst_ref, tile, tm, make_copy, lambda c: c.start())
    cpos = cpos_ref[...]
    gw = gw_ref[...]
    pos_cols = [cpos[:, k:k + 1] for k in range(TOP_K)]
    gw_cols = [gw[:, k:k + 1] for k in range(TOP_K)]
    for blk in range(stage_ref.shape[0] // G_BLOCK):
        iota = blk * G_BLOCK + lax.broadcasted_iota(I32, (tm, G_BLOCK), 1)
        gate_ref[:, blk * G_BLOCK:(blk + 1) * G_BLOCK] = _route_onehot(iota, pos_cols, gw_cols)
    _chunk_copies(seg_ref, len_ref, dst_ref, tile, tm, make_copy, lambda c: c.wait())
    m = jnp.dot(gate_ref[...], stage_ref[...], preferred_element_type=F32)
    o_ref[...] = _layer_norm(DEEPNORM_ALPHA * h1_ref[...] + m, g2_ref[...], b2_ref[...])


def _combine(seg, ln, dst, h1, cpos_nt, gw_nt, g2, b2, ys, *, tm):
    n, d = h1.shape
    return pl.pallas_call(
        functools.partial(_combine_kernel, tm=tm),
        grid_spec=pltpu.PrefetchScalarGridSpec(
            num_scalar_prefetch=3,
            grid=(n // tm,),
            in_specs=[pl.BlockSpec((tm, d), lambda i, *_: (i, 0)),
                      pl.BlockSpec((tm, TOP_K), lambda i, *_: (i, 0)), pl.BlockSpec((tm, TOP_K), lambda i, *_: (i, 0)),
                      pl.BlockSpec((1, d), lambda i, *_: (0, 0)), pl.BlockSpec((1, d), lambda i, *_: (0, 0)),
                      pl.BlockSpec(memory_space=pl.ANY)],
            out_specs=pl.BlockSpec((tm, d), lambda i, *_: (i, 0)),
            scratch_shapes=[pltpu.VMEM((_stage_rows(tm), d), BF16), pltpu.VMEM((tm, _stage_rows(tm)), BF16),
                            pltpu.SemaphoreType.DMA(())],
        ),
        out_shape=jax.ShapeDtypeStruct((n, d), F32),
        compiler_params=_params(("arbitrary",)),
        name="combine",
    )(seg, ln, dst, h1, cpos_nt, gw_nt, g2, b2, ys)


def _block_tail(x2d, att, pool, wts, *, tm, tmoe):
    n, d = x2d.shape
    tm = min(tm, n)
    nt = n // tm
    h1, cpos, gw_t, seg_t, len_t, dst_t, used = _tail(x2d, att, pool, wts["g0"], wts["b0"], wts["woa"], wts["wob"],
                                                      wts["g1"], wts["b1"], wts["wrh"], wts["wrl"], wts["br"], tm=tm)
    used = used[:, 0].astype(I32)
    cap = ((used + tmoe - 1) // tmoe) * tmoe
    ends = jnp.cumsum(cap)
    offs = ends - cap
    n_tiles = -(-(TOP_K * n + (ROW_ALIGN - 1) * N_EXPERTS * nt) // tmoe) + N_EXPERTS
    tile_start = jnp.arange(n_tiles, dtype=I32) * tmoe
    tile_expert = jnp.minimum(jnp.sum((ends[None, :] <= tile_start[:, None]).astype(I32), axis=1), N_EXPERTS - 1)
    tile_valid = jnp.clip(offs[tile_expert] + used[tile_expert] - tile_start, 0, tmoe).astype(I32)
    tile_valid = jnp.where(tile_start < ends[-1], tile_valid, 0)
    seg = seg_t[:, :, 0].astype(I32).reshape(-1)
    ln = len_t[:, :, 0].astype(I32).reshape(-1)
    dst = (dst_t[:, :, 0].astype(I32) + offs[None, :]).reshape(-1)

    xs = _dispatch(seg, ln, dst, h1, cpos, tm=tm, p_rows=n_tiles * tmoe)
    ys = _moe(tile_expert, tile_valid, xs, wts["w1b"], wts["b1e"], wts["w2b"], wts["b2e"], tmoe=tmoe)
    return _combine(seg, ln, dst, h1, cpos.T, gw_t.T, wts["g2"], wts["b2"], ys, tm=tm)


def kernel(x_prompt, x_sample, cache_k, cache_v, cache_kidx, state_pool, ln0_g, ln0_b, w_in, w_o,
           pool_w, pool_scale, ln1_g, ln1_b, w_router, b_router, w1, b1, w2, b2, ln2_g, ln2_b):
    bp, s_len, d = x_prompt.shape
    bs, t_len, _ = x_sample.shape
    l_past = cache_k.shape[2]
    aw = N_HEADS * HEAD_DIM
    pw = d - aw
    lyr = 0
    lc = 256

    k_off = aw
    v_off = k_off + HEAD_DIM
    qi_off = v_off + HEAD_DIM
    ki_off = qi_off + IDX_HEADS * IDX_DIM
    wi_off = ki_off + IDX_DIM
    u_off = wi_off + IDX_HEADS
    win = w_in[lyr]
    wa = jnp.concatenate([win[:, 0:k_off], win[:, qi_off:ki_off], win[:, u_off:u_off + pw]], axis=1).astype(BF16)
    wb = jnp.concatenate([win[:, k_off:v_off], win[:, v_off:qi_off], win[:, ki_off:wi_off],
                          jnp.zeros((d, 64), F32)], axis=1).astype(BF16)
    wt = jnp.concatenate([win[:, v_off:qi_off].T, win[:, wi_off:u_off].T, jnp.zeros((8, d), F32)], axis=0).astype(BF16)

    g0 = ln0_g.reshape(1, d)
    b0 = ln0_b.reshape(1, d)
    wrt = w_router[lyr].T
    wrh = wrt.astype(BF16)
    wts = dict(
        g0=g0, b0=b0,
        woa=w_o[lyr][:aw].astype(BF16), wob=w_o[lyr][aw:].astype(BF16),
        g1=ln1_g[lyr].reshape(1, d), b1=ln1_b[lyr].reshape(1, d),
        wrh=wrh, wrl=(wrt - wrh.astype(F32)).astype(BF16), br=b_router[lyr].reshape(N_EXPERTS, 1),
        w1b=w1[lyr].astype(BF16), b1e=b1[lyr], w2b=w2[lyr].astype(BF16), b2e=b2[lyr],
        g2=ln2_g[lyr].reshape(1, d), b2=ln2_b[lyr].reshape(1, d),
    )
    pool_w_b = pool_w[lyr].astype(BF16)
    pool_sc = pool_scale[lyr].reshape(1, pw)

    xp = x_prompt.reshape(bp * s_len, d)
    q, qi, u, k, v, ki, kb, kib, vt, wit = _proj(xp, g0, b0, wa, wb, wt, tm=512, lc=lc)
    att_p = _dsa(q.reshape(bp, s_len, aw), qi.reshape(bp, s_len, aw), wit,
                 kb.reshape(bp, s_len, HEAD_DIM), kib.reshape(bp, s_len, IDX_DIM), vt,
                 tq=256, lc=lc, causal=True, l_valid=s_len, q_pos0=0, topk=min(TOPK_MAX, s_len // 4))
    u_p = u.reshape(bp, s_len, pw)
    pool_p = _pool(u_p, jnp.zeros((bp, POOL_PAST + 1, pw), F32), pool_w_b, pool_sc, pos0=0)
    y_p = _block_tail(xp, att_p.reshape(bp * s_len, aw), pool_p.reshape(bp * s_len, pw), wts,
                      tm=512, tmoe=512)

    xs = x_sample.reshape(bs * t_len, d)
    qs, qis, us, kn, vn, kin, _, _, _, wits = _proj(xs, g0, b0, wa, wb, wt, tm=512, lc=lc)
    l_all = l_past + t_len
    l_pad = -(-l_all // lc) * lc
    tq_s = LANES
    pad_keys = lambda a: jnp.pad(a, ((0, 0), (0, l_pad - l_all), (0, 0)))
    k_all = pad_keys(jnp.concatenate([cache_k[lyr], kn.reshape(bs, t_len, HEAD_DIM)], axis=1))
    v_all = pad_keys(jnp.concatenate([cache_v[lyr], vn.reshape(bs, t_len, HEAD_DIM)], axis=1))
    ki_all = pad_keys(jnp.concatenate([cache_kidx[lyr], kin.reshape(bs, t_len, IDX_DIM)], axis=1))
    vt_all = v_all.reshape(bs, l_pad // lc, lc, HEAD_DIM).transpose(0, 1, 3, 2).reshape(-1, HEAD_DIM, lc)
    pad_q = lambda a: jnp.pad(a.reshape(bs, t_len, aw), ((0, 0), (0, tq_s - t_len), (0, 0)))
    wit_s = jnp.pad(wits.reshape(IDX_HEADS, bs, t_len), ((0, 0), (0, 0), (0, tq_s - t_len))).reshape(IDX_HEADS, -1)
    att_s = _dsa(pad_q(qs), pad_q(qis), wit_s, k_all.astype(BF16), ki_all.astype(BF16), vt_all.astype(BF16),
                 tq=tq_s, lc=lc, causal=False, l_valid=l_all, q_pos0=l_past, topk=min(TOPK_MAX, l_all // 4))
    att_s = att_s[:, :t_len].reshape(bs * t_len, aw)
    us3 = us.reshape(bs, t_len, pw)
    prefix_s = jnp.concatenate([jnp.zeros((bs, 1, pw), F32), state_pool[lyr]], axis=1)
    pool_s = _pool(us3, prefix_s, pool_w_b, pool_sc, pos0=l_past)
    y_s = _block_tail(xs, att_s, pool_s.reshape(bs * t_len, pw), wts, tm=512, tmoe=128)

    pool_state_p = u_p[:, s_len - POOL_PAST:]
    pool_state_s = jnp.concatenate([state_pool[lyr], us3], axis=1)[:, -POOL_PAST:]
    return (y_p.reshape(bp, s_len, d), y_s.reshape(bs, t_len, d),
            k.reshape(1, bp, s_len, HEAD_DIM), v.reshape(1, bp, s_len, HEAD_DIM),
            ki.reshape(1, bp, s_len, IDX_DIM), pool_state_p[None],
            kn.reshape(1, bs, t_len, HEAD_DIM), vn.reshape(1, bs, t_len, HEAD_DIM),
            kin.reshape(1, bs, t_len, IDX_DIM), pool_state_s[None])
```

```python
import functools

import jax
import jax.numpy as jnp
from jax import lax
from jax.experimental import pallas as pl
from jax.experimental.pallas import tpu as pltpu

F32 = jnp.float32
BF16 = jnp.bfloat16
I32 = jnp.int32

CHUNK = 64
CHUNK_SHIFT = 6
assert 1 << CHUNK_SHIFT == CHUNK
N_HEADS = 8
HEAD_DIM = 64
IDX_HEADS = 8
IDX_DIM = 64
TOPK_MAX = 256
POOL_WINDOWS = (2, 4, 8, 16)
POOL_PAST = 15
N_EXPERTS = 32
TOP_K = 4
SWIGLU_LIMIT = 7.0
SWIGLU_ALPHA = 1.702
LN_EPS = 1e-5
DEPTH = 1
DEEPNORM_ALPHA = (2 * DEPTH) ** 0.25

LANES = 128
SUBLANES = 8
VMEM_LIMIT_BYTES = 56 * 1024 * 1024

NEG_BIG = -1e30
KEY_NEG_INF = -2139095041
KEY_POS_INF = 2139095040

NT_DIMS = (((1,), (1,)), ((), ()))


def _layer_norm(x, g, b):
    mu = jnp.mean(x, axis=-1, keepdims=True)
    xc = x - mu
    var = jnp.mean(xc * xc, axis=-1, keepdims=True)
    return xc * lax.rsqrt(var + LN_EPS) * g + b


def _params(sem):
    return pltpu.CompilerParams(dimension_semantics=sem, vmem_limit_bytes=VMEM_LIMIT_BYTES)


def _proj_kernel(x_ref, g_ref, b_ref, wa_ref, wb_ref, wt_ref,
                 q_ref, qi_ref, u_ref, k_ref, v_ref, ki_ref, kb_ref, kib_ref, vt_ref, wit_ref, *, lc):
    h = _layer_norm(x_ref[...], g_ref[...], b_ref[...])
    hb = h.astype(BF16)
    aw = N_HEADS * HEAD_DIM
    pa = jnp.dot(hb, wa_ref[...], preferred_element_type=F32)
    q_ref[...] = (pa[:, :aw] * (HEAD_DIM ** -0.5)).astype(BF16)
    qi_ref[...] = pa[:, aw:2 * aw].astype(BF16)
    u_ref[...] = pa[:, 2 * aw:]
    pb = jnp.dot(hb, wb_ref[...], preferred_element_type=F32)
    k = pb[:, 0:HEAD_DIM]
    v = pb[:, HEAD_DIM:2 * HEAD_DIM]
    ki = pb[:, 2 * HEAD_DIM:2 * HEAD_DIM + IDX_DIM]
    k_ref[...] = k
    v_ref[...] = v
    ki_ref[...] = ki
    kb_ref[...] = k.astype(BF16)
    kib_ref[...] = ki.astype(BF16)
    pt = lax.dot_general(wt_ref[...], hb, NT_DIMS, preferred_element_type=F32)
    for c in range(vt_ref.shape[0]):
        vt_ref[c] = pt[0:HEAD_DIM, c * lc:(c + 1) * lc].astype(BF16)
    wi = pt[HEAD_DIM:HEAD_DIM + IDX_HEADS, :]
    wit_ref[...] = (wi * (IDX_HEADS ** -0.5)) * (IDX_DIM ** -0.5)


def _proj(x2d, g, b, wa, wb, wt, *, tm, lc):
    n, d = x2d.shape
    tm = min(tm, n)
    aw = N_HEADS * HEAD_DIM
    uw = wa.shape[1] - 2 * aw
    row = lambda i: (i, 0)
    const = lambda i: (0, 0)
    out_shape = (
        jax.ShapeDtypeStruct((n, aw), BF16),
        jax.ShapeDtypeStruct((n, aw), BF16),
        jax.ShapeDtypeStruct((n, uw), F32),
        jax.ShapeDtypeStruct((n, HEAD_DIM), F32),
        jax.ShapeDtypeStruct((n, HEAD_DIM), F32),
        jax.ShapeDtypeStruct((n, IDX_DIM), F32),
        jax.ShapeDtypeStruct((n, HEAD_DIM), BF16),
        jax.ShapeDtypeStruct((n, IDX_DIM), BF16),
        jax.ShapeDtypeStruct((n // lc, HEAD_DIM, lc), BF16),
        jax.ShapeDtypeStruct((IDX_HEADS, n), F32),
    )
    out_specs = (
        pl.BlockSpec((tm, aw), row), pl.BlockSpec((tm, aw), row), pl.BlockSpec((tm, uw), row),
        pl.BlockSpec((tm, HEAD_DIM), row), pl.BlockSpec((tm, HEAD_DIM), row), pl.BlockSpec((tm, IDX_DIM), row),
        pl.BlockSpec((tm, HEAD_DIM), row), pl.BlockSpec((tm, IDX_DIM), row),
        pl.BlockSpec((tm // lc, HEAD_DIM, lc), lambda i: (i, 0, 0)),
        pl.BlockSpec((IDX_HEADS, tm), lambda i: (0, i)),
    )
    return pl.pallas_call(
        functools.partial(_proj_kernel, lc=lc),
        grid=(n // tm,),
        in_specs=[pl.BlockSpec((tm, d), row), pl.BlockSpec((1, d), const), pl.BlockSpec((1, d), const),
                  pl.BlockSpec(wa.shape, const), pl.BlockSpec(wb.shape, const), pl.BlockSpec(wt.shape, const)],
        out_specs=out_specs,
        out_shape=out_shape,
        compiler_params=_params(("parallel",)),
        name="proj",
    )(x2d, g, b, wa, wb, wt)


def _key_to_float(key):
    bits = jnp.where(key >= 0, key, key ^ jnp.int32(0x7FFFFFFF))
    return lax.bitcast_convert_type(bits, F32)


def _dsa_kernel(q_ref, qi_ref, wit_ref, kb_ref, kib_ref, vt_ref, o_ref,
                sc_ref, s_ref, out_ref, *acc_refs, tq, lc, nk_static, causal, l_valid, q_pos0, topk):
    qb = pl.program_id(1)
    nk = qb * (tq // lc) + (tq // lc) if causal else nk_static
    q_chunk = (q_pos0 + qb * tq + lax.broadcasted_iota(I32, (lc, tq), 1)) >> CHUNK_SHIFT
    row_iota = lax.broadcasted_iota(I32, (lc, tq), 0)

    def score_chunk(kc, carry):
        off = pl.multiple_of(kc * lc, lc)
        kic = kib_ref[0, pl.ds(off, lc), :]
        acc = jnp.zeros((lc, tq), F32)
        for h in range(IDX_HEADS):
            s = lax.dot_general(kic, qi_ref[0, :, h * IDX_DIM:(h + 1) * IDX_DIM], NT_DIMS,
                                preferred_element_type=F32)
            acc = acc + wit_ref[h:h + 1, :] * jnp.maximum(s, 0.0)
        l_pos = off + row_iota
        visible = ((l_pos >> CHUNK_SHIFT) <= q_chunk) & (l_pos < l_valid)
        sc_ref[pl.ds(off, lc), :] = jnp.where(visible, acc, -jnp.inf)
        return carry

    lax.fori_loop(0, nk, score_chunk, 0)

    def count(pred):
        ways = 4 * SUBLANES

        def body(kc, part):
            off = pl.multiple_of(kc * lc, lc)
            hit = jnp.where(pred(sc_ref[pl.ds(off, lc), :]), 1.0, 0.0)
            return part + jnp.sum(hit.reshape(lc // ways, ways, tq), axis=0)
        part = lax.fori_loop(0, nk, body, jnp.zeros((ways, tq), F32))
        return jnp.sum(part, axis=0, keepdims=True)

    def bisect(_, carry):
        lo, hi = carry
        mid = (lo >> 1) + (hi >> 1) + (lo & hi & 1)
        cand = _key_to_float(mid)
        ok = count(lambda blk: blk >= cand) >= topk
        return jnp.where(ok, mid, lo), jnp.where(ok, hi, mid)

    lo, _ = lax.fori_loop(0, 32, bisect,
                          (jnp.full((1, tq), KEY_NEG_INF, I32), jnp.full((1, tq), KEY_POS_INF + 1, I32)))
    thr = _key_to_float(lo)
    n_above = count(lambda blk: blk > thr)
    n_ties = topk - n_above

    tri = jnp.where(lax.broadcasted_iota(I32, (lc, lc), 0) >= lax.broadcasted_iota(I32, (lc, lc), 1),
                    1.0, 0.0).astype(BF16)

    def fold8(x):
        return x.reshape(lc // SUBLANES, SUBLANES, tq)

    def logits_chunk(kc, carry):
        ties_before, m8 = carry
        off = pl.multiple_of(kc * lc, lc)
        blk = sc_ref[pl.ds(off, lc), :]
        tie = blk == thr
        tie_rank = jnp.dot(tri, jnp.where(tie, 1.0, 0.0).astype(BF16), preferred_element_type=F32) + ties_before
        bias = jnp.where(blk > thr, 0.0, jnp.where(tie, jnp.where(tie_rank <= n_ties, 0.0, NEG_BIG), NEG_BIG))
        bias = jnp.where(blk == -jnp.inf, NEG_BIG, bias)
        kc_b = kb_ref[0, pl.ds(off, lc), :]
        m_rows = []
        for h in range(N_HEADS):
            s = lax.dot_general(kc_b, q_ref[0, :, h * HEAD_DIM:(h + 1) * HEAD_DIM], NT_DIMS,
                                preferred_element_type=F32) + bias
            s_ref[h, pl.ds(off, lc), :] = s
            m_rows.append(jnp.maximum(m8[h * SUBLANES:(h + 1) * SUBLANES], jnp.max(fold8(s), axis=0)))
        return tie_rank[lc - 1:lc, :], jnp.concatenate(m_rows, axis=0)

    _, m8 = lax.fori_loop(0, nk, logits_chunk,
                          (jnp.zeros((1, tq), F32), jnp.full((N_HEADS * SUBLANES, tq), NEG_BIG, F32)))
    m_all = jnp.max(m8.reshape(N_HEADS, SUBLANES, tq), axis=1)

    for acc_ref in acc_refs:
        acc_ref[...] = jnp.zeros(acc_ref.shape, F32)

    def pv_chunk(kc, l8):
        off = pl.multiple_of(kc * lc, lc)
        vt_c = vt_ref[kc]
        l_rows = []
        for h in range(N_HEADS):
            p = jnp.exp(s_ref[h, pl.ds(off, lc), :] - m_all[h:h + 1, :])
            l_rows.append(l8[h * SUBLANES:(h + 1) * SUBLANES] + jnp.sum(fold8(p), axis=0))
            acc_refs[h][...] += jnp.dot(vt_c, p.astype(BF16), preferred_element_type=F32)
        return jnp.concatenate(l_rows, axis=0)

    l8 = lax.fori_loop(0, nk, pv_chunk, jnp.zeros((N_HEADS * SUBLANES, tq), F32))
    l_all = jnp.sum(l8.reshape(N_HEADS, SUBLANES, tq), axis=1)

    for h in range(N_HEADS):
        out_ref[h * HEAD_DIM:(h + 1) * HEAD_DIM, :] = acc_refs[h][...] / l_all[h:h + 1, :]
    o_ref[0] = out_ref[...].T.astype(BF16)


def _dsa(q, qi, wit, kb, kib, vt, *, tq, lc, causal, l_valid, q_pos0, topk):
    bsz, tq_tot, aw = q.shape
    l_tot = kb.shape[1]
    nq = tq_tot // tq
    nkc = l_tot // lc
    kern = functools.partial(_dsa_kernel, tq=tq, lc=lc, nk_static=nkc, causal=causal, l_valid=l_valid,
                             q_pos0=q_pos0, topk=topk)
    return pl.pallas_call(
        kern,
        grid=(bsz, nq),
        in_specs=[
            pl.BlockSpec((1, tq, aw), lambda b, i: (b, i, 0)),
            pl.BlockSpec((1, tq, aw), lambda b, i: (b, i, 0)),
            pl.BlockSpec((IDX_HEADS, tq), lambda b, i: (0, b * nq + i)),
            pl.BlockSpec((1, l_tot, HEAD_DIM), lambda b, i: (b, 0, 0)),
            pl.BlockSpec((1, l_tot, IDX_DIM), lambda b, i: (b, 0, 0)),
            pl.BlockSpec((nkc, HEAD_DIM, lc), lambda b, i: (b, 0, 0)),
        ],
        out_specs=pl.BlockSpec((1, tq, aw), lambda b, i: (b, i, 0)),
        out_shape=jax.ShapeDtypeStruct((bsz, tq_tot, aw), BF16),
        scratch_shapes=[
            pltpu.VMEM((l_tot, tq), F32),
            pltpu.VMEM((N_HEADS, l_tot, tq), F32),
            pltpu.VMEM((aw, tq), F32),
        ] + [pltpu.VMEM((HEAD_DIM, tq), F32) for _ in range(N_HEADS)],
        compiler_params=_params(("parallel", "parallel")),
        name="dsa",
    )(q, qi, wit, kb, kib, vt)


def _pool_kernel(u_ref, pre_ref, pw_ref, sc_ref, o_ref, ext_ref, *, t_len, tt, pos0):
    pad = pre_ref.shape[1]
    gw = pw_ref.shape[1]
    ext_ref[0:pad, :] = pre_ref[0]
    ext_ref[pad:pad + t_len, :] = u_ref[0]
    row = lax.broadcasted_iota(I32, (tt, gw), 0)
    for t in range(t_len // tt):
        r0 = t * tt
        for g, w in enumerate(POOL_WINDOWS):
            cols = slice(g * gw, (g + 1) * gw)
            cur = ext_ref[pad + r0:pad + r0 + tt, cols]
            wsum = cur
            for j in range(1, w):
                wsum = wsum + ext_ref[pad + r0 - j:pad + r0 - j + tt, cols]
            cnt = jnp.minimum(w, pos0 + r0 + 1 + row).astype(F32)
            diff = (wsum / cnt - cur).astype(BF16)
            y = jnp.dot(diff, pw_ref[g], preferred_element_type=F32)
            o_ref[0, r0:r0 + tt, cols] = (y * sc_ref[:, cols]).astype(BF16)


def _pool(u, prefix, pool_w_b, pool_scale, *, pos0):
    bsz, t_len, c = u.shape
    pad = prefix.shape[1]
    tt = min(t_len, 256)
    return pl.pallas_call(
        functools.partial(_pool_kernel, t_len=t_len, tt=tt, pos0=pos0),
        grid=(bsz,),
        in_specs=[pl.BlockSpec((1, t_len, c), lambda b: (b, 0, 0)),
                  pl.BlockSpec((1, pad, c), lambda b: (b, 0, 0)),
                  pl.BlockSpec(pool_w_b.shape, lambda b: (0, 0, 0)),
                  pl.BlockSpec((1, c), lambda b: (0, 0))],
        out_specs=pl.BlockSpec((1, t_len, c), lambda b: (b, 0, 0)),
        out_shape=jax.ShapeDtypeStruct((bsz, t_len, c), BF16),
        scratch_shapes=[pltpu.VMEM((pad + t_len, c), F32)],
        compiler_params=_params(("parallel",)),
        name="pool",
    )(u, prefix, pool_w_b, pool_scale)


ROW_ALIGN = 16


def _tail_kernel(x_ref, att_ref, pool_ref, g0_ref, b0_ref, woa_ref, wob_ref, g1_ref, b1_ref,
                 wrh_ref, wrl_ref, br_ref,
                 h1_ref, cpos_ref, gw_ref, seg_ref, len_ref, dst_ref, used_ref, carry_ref, *, tm):
    @pl.when(pl.program_id(0) == 0)
    def _():
        carry_ref[...] = jnp.zeros(carry_ref.shape, F32)

    h = _layer_norm(x_ref[...], g0_ref[...], b0_ref[...])
    mix = (jnp.dot(att_ref[...], woa_ref[...], preferred_element_type=F32)
           + jnp.dot(pool_ref[...], wob_ref[...], preferred_element_type=F32))
    h1 = _layer_norm(DEEPNORM_ALPHA * h + mix, g1_ref[...], b1_ref[...])
    h1_ref[...] = h1

    hh = h1.astype(BF16)
    hl = (h1 - hh.astype(F32)).astype(BF16)
    logits = (lax.dot_general(wrh_ref[...], hh, NT_DIMS, preferred_element_type=F32)
              + lax.dot_general(wrh_ref[...], hl, NT_DIMS, preferred_element_type=F32)
              + lax.dot_general(wrl_ref[...], hh, NT_DIMS, preferred_element_type=F32)
              + br_ref[...])

    e_iota = lax.broadcasted_iota(I32, (N_EXPERTS, tm), 0).astype(F32)
    work = logits
    vals, hots = [], []
    for k in range(TOP_K):
        m = jnp.max(work, axis=0, keepdims=True)
        idx = jnp.min(jnp.where(work == m, e_iota, float(N_EXPERTS)), axis=0, keepdims=True)
        hot = e_iota == idx
        vals.append(m)
        hots.append(jnp.where(hot, 1.0, 0.0))
        work = jnp.where(hot, -jnp.inf, work)
    exps = [jnp.exp(v - vals[0]) for v in vals]
    den = exps[0] + exps[1] + exps[2] + exps[3]
    for k in range(TOP_K):
        gw_ref[k:k + 1, :] = exps[k] / den

    hot_all = (hots[0] + hots[1] + hots[2] + hots[3]).astype(BF16)
    count = jnp.dot(hot_all, jnp.ones((tm, LANES), BF16), preferred_element_type=F32)
    chunk_len = jnp.ceil(count * (1.0 / ROW_ALIGN)) * ROW_ALIGN
    lower = jnp.where(lax.broadcasted_iota(I32, (N_EXPERTS, N_EXPERTS), 1)
                      < lax.broadcasted_iota(I32, (N_EXPERTS, N_EXPERTS), 0), 1.0, 0.0).astype(BF16)
    seg_base = jnp.dot(lower, chunk_len.astype(BF16), preferred_element_type=F32)
    before = jnp.where(lax.broadcasted_iota(I32, (tm, tm), 0) < lax.broadcasted_iota(I32, (tm, tm), 1),
                       1.0, 0.0).astype(BF16)
    slot = jnp.dot(hot_all, before, preferred_element_type=F32) + jnp.tile(seg_base, (1, tm // LANES))
    for k in range(TOP_K):
        cpos_ref[k:k + 1, :] = jnp.sum(hots[k] * slot, axis=0, keepdims=True).astype(I32)
    carry = carry_ref[...]
    seg_ref[0] = seg_base
    len_ref[0] = chunk_len
    dst_ref[0] = carry
    carry_ref[...] = carry + chunk_len
    used_ref[...] = carry + chunk_len


def _tail(x2d, att, pool, g0, b0, woa, wob, g1, b1, wrh, wrl, br, *, tm):
    n, d = x2d.shape
    aw = att.shape[1]
    nt = n // tm
    row = lambda i: (i, 0)
    const = lambda i: (0, 0)
    col = lambda i: (0, i)
    tab = lambda i: (i, 0, 0)
    tab_shape = jax.ShapeDtypeStruct((nt, N_EXPERTS, LANES), F32)
    tab_spec = pl.BlockSpec((1, N_EXPERTS, LANES), tab)
    return pl.pallas_call(
        functools.partial(_tail_kernel, tm=tm),
        grid=(nt,),
        in_specs=[pl.BlockSpec((tm, d), row), pl.BlockSpec((tm, aw), row), pl.BlockSpec((tm, d - aw), row),
                  pl.BlockSpec((1, d), const), pl.BlockSpec((1, d), const),
                  pl.BlockSpec(woa.shape, const), pl.BlockSpec(wob.shape, const),
                  pl.BlockSpec((1, d), const), pl.BlockSpec((1, d), const),
                  pl.BlockSpec(wrh.shape, const), pl.BlockSpec(wrl.shape, const), pl.BlockSpec(br.shape, const)],
        out_specs=(pl.BlockSpec((tm, d), row), pl.BlockSpec((TOP_K, tm), col), pl.BlockSpec((TOP_K, tm), col),
                   tab_spec, tab_spec, tab_spec, pl.BlockSpec((N_EXPERTS, LANES), const)),
        out_shape=(jax.ShapeDtypeStruct((n, d), F32), jax.ShapeDtypeStruct((TOP_K, n), I32),
                   jax.ShapeDtypeStruct((TOP_K, n), F32), tab_shape, tab_shape, tab_shape,
                   jax.ShapeDtypeStruct((N_EXPERTS, LANES), F32)),
        scratch_shapes=[pltpu.VMEM((N_EXPERTS, LANES), F32)],
        compiler_params=_params(("arbitrary",)),
        name="tail",
    )(x2d, att, pool, g0, b0, woa, wob, g1, b1, wrh, wrl, br)


G_BLOCK = 256


def _stage_rows(tm):
    raw = TOP_K * tm + (ROW_ALIGN - 1) * N_EXPERTS
    return -(-raw // G_BLOCK) * G_BLOCK


def _chunk_copies(seg_ref, len_ref, dst_ref, tile, max_len, make_copy, act):
    def per_expert(e, carry):
        idx = tile * N_EXPERTS + e
        seg, ln, dst = seg_ref[idx], len_ref[idx], dst_ref[idx]
        size = ROW_ALIGN
        while size <= max_len:
            done = ln & (-2 * size)

            @pl.when((ln & size) != 0)
            def _(size=size, done=done):
                act(make_copy(pl.multiple_of(seg + done, ROW_ALIGN), pl.multiple_of(dst + done, ROW_ALIGN), size))
            size *= 2
        return carry

    lax.fori_loop(0, N_EXPERTS, per_expert, 0)


def _by_staged_blocks(seg_ref, len_ref, tile, n_full, body):
    last = tile * N_EXPERTS + N_EXPERTS - 1
    total = seg_ref[last] + len_ref[last]
    pl.when(total <= (n_full - 1) * G_BLOCK)(lambda: body(n_full - 1))
    pl.when(total > (n_full - 1) * G_BLOCK)(lambda: body(n_full))


def _route_onehot(iota, pos_list, val_list):
    g = jnp.zeros(iota.shape, F32)
    for pos, val in zip(pos_list, val_list):
        g = jnp.where(iota == pos, val, g)
    return g.astype(BF16)


def _dispatch_kernel(seg_ref, len_ref, dst_ref, h1_ref, cpos_ref, xs_hbm, stage_ref, sem, *, tm):
    tile = pl.program_id(0)
    h1b = h1_ref[...].astype(BF16)
    pos_rows = [cpos_ref[k:k + 1, :] for k in range(TOP_K)]

    def stage_blocks(n_blk):
        for blk in range(n_blk):
            iota = blk * G_BLOCK + lax.broadcasted_iota(I32, (G_BLOCK, tm), 0)
            g = _route_onehot(iota, pos_rows, [1.0] * TOP_K)
            stage_ref[blk * G_BLOCK:(blk + 1) * G_BLOCK, :] = jnp.dot(
                g, h1b, preferred_element_type=F32).astype(BF16)

    _by_staged_blocks(seg_ref, len_ref, tile, stage_ref.shape[0] // G_BLOCK, stage_blocks)

    def make_copy(seg, dst, size):
        return pltpu.make_async_copy(stage_ref.at[pl.ds(seg, size), :], xs_hbm.at[pl.ds(dst, size), :], sem)

    _chunk_copies(seg_ref, len_ref, dst_ref, tile, tm, make_copy, lambda c: c.start())
    _chunk_copies(seg_ref, len_ref, dst_ref, tile, tm, make_copy, lambda c: c.wait())


def _dispatch(seg, ln, dst, h1, cpos, *, tm, p_rows):
    n, d = h1.shape
    return pl.pallas_call(
        functools.partial(_dispatch_kernel, tm=tm),
        grid_spec=pltpu.PrefetchScalarGridSpec(
            num_scalar_prefetch=3,
            grid=(n // tm,),
            in_specs=[pl.BlockSpec((tm, d), lambda i, *_: (i, 0)), pl.BlockSpec((TOP_K, tm), lambda i, *_: (0, i))],
            out_specs=pl.BlockSpec(memory_space=pl.ANY),
            scratch_shapes=[pltpu.VMEM((_stage_rows(tm), d), BF16), pltpu.SemaphoreType.DMA(())],
        ),
        out_shape=jax.ShapeDtypeStruct((p_rows, d), BF16),
        compiler_params=_params(("arbitrary",)),
        name="dispatch",
    )(seg, ln, dst, h1, cpos)


FF_CHUNK = 256


def _moe_kernel(te_ref, tv_ref, x_ref, w1_ref, b1_ref, w2_ref, b2_ref, y_ref, w1b_ref, w2b_ref, act_ref):
    i = pl.program_id(0)
    d_ff = w2_ref.shape[1]
    valid = tv_ref[i]

    @pl.when((i == 0) | (te_ref[i] != te_ref[jnp.maximum(i - 1, 0)]))
    def _():
        w1b_ref[...] = w1_ref[0].astype(BF16)
        w2b_ref[...] = w2_ref[0].astype(BF16)

    @pl.when(valid > 0)
    def _():
        rows = lax.broadcasted_iota(I32, x_ref.shape, 0)
        xb = jnp.where(rows < valid, x_ref[...].astype(F32), 0.0).astype(BF16)
        for j in range(d_ff // FF_CHUNK):
            gs = slice(j * FF_CHUNK, (j + 1) * FF_CHUNK)
            us = slice(d_ff + j * FF_CHUNK, d_ff + (j + 1) * FF_CHUNK)
            gate = jnp.dot(xb, w1b_ref[:, gs], preferred_element_type=F32) + b1_ref[0, :, gs]
            up = jnp.dot(xb, w1b_ref[:, us], preferred_element_type=F32) + b1_ref[0, :, us]
            gate = jnp.minimum(gate, SWIGLU_LIMIT)
            up = jnp.clip(up, -SWIGLU_LIMIT, SWIGLU_LIMIT)
            act_ref[:, gs] = ((up + 1.0) * (gate * jax.nn.sigmoid(SWIGLU_ALPHA * gate))).astype(BF16)
        y = jnp.dot(act_ref[...], w2b_ref[...], preferred_element_type=F32) + b2_ref[0]
        y_ref[...] = y.astype(BF16)

    @pl.when(valid <= 0)
    def _():
        y_ref[...] = jnp.zeros(y_ref.shape, BF16)


def _moe(tile_expert, tile_valid, xs, w1, b1, w2, b2, *, tmoe):
    p, d = xs.shape
    e, _, ff2 = w1.shape
    d_ff = ff2 // 2
    return pl.pallas_call(
        _moe_kernel,
        grid_spec=pltpu.PrefetchScalarGridSpec(
            num_scalar_prefetch=2,
            grid=(p // tmoe,),
            in_specs=[pl.BlockSpec((tmoe, d), lambda i, te, tv: (i, 0)),
                      pl.BlockSpec((1, d, ff2), lambda i, te, tv: (te[i], 0, 0)),
                      pl.BlockSpec((1, 1, ff2), lambda i, te, tv: (te[i], 0, 0)),
                      pl.BlockSpec((1, d_ff, d), lambda i, te, tv: (te[i], 0, 0)),
                      pl.BlockSpec((1, 1, d), lambda i, te, tv: (te[i], 0, 0))],
            out_specs=pl.BlockSpec((tmoe, d), lambda i, te, tv: (i, 0)),
            scratch_shapes=[pltpu.VMEM((d, ff2), BF16), pltpu.VMEM((d_ff, d), BF16), pltpu.VMEM((tmoe, d_ff), BF16)],
        ),
        out_shape=jax.ShapeDtypeStruct((p, d), BF16),
        compiler_params=_params(("arbitrary",)),
        name="moe",
    )(tile_expert, tile_valid, xs, w1, b1.reshape(e, 1, ff2), w2, b2.reshape(e, 1, d))


def _combine_kernel(seg_ref, len_ref, dst_ref, h1_ref, cpos_ref, gw_ref, g2_ref, b2_ref, ys_hbm, o_ref,
                    stage_ref, gate_ref, sem, *, tm):
    tile = pl.program_id(0)

    @pl.when(tile == 0)
    def _():
        stage_ref[...] = jnp.zeros(stage_ref.shape, BF16)

    def make_copy(seg, dst, size):
        return pltpu.make_async_copy(ys_hbm.at[pl.ds(dst, size), :], stage_ref.at[pl.ds(seg, size), :], sem)

    _chunk_copies(seg_ref, len_ref, dst_ref, tile, tm, make_copy, lambda c: c.start())
    cpos = cpos_ref[...]
    gw = gw_ref[...]
    pos_cols = [cpos[:, k:k + 1] for k in range(TOP_K)]
    gw_cols = [gw[:, k:k + 1] for k in range(TOP_K)]

    def finish(n_blk):
        for blk in range(n_blk):
            iota = blk * G_BLOCK + lax.broadcasted_iota(I32, (tm, G_BLOCK), 1)
            gate_ref[:, blk * G_BLOCK:(blk + 1) * G_BLOCK] = _route_onehot(iota, pos_cols, gw_cols)
        _chunk_copies(seg_ref, len_ref, dst_ref, tile, tm, make_copy, lambda c: c.wait())
        rows = n_blk * G_BLOCK
        m = jnp.dot(gate_ref[:, :rows], stage_ref[:rows, :], preferred_element_type=F32)
        o_ref[...] = _layer_norm(DEEPNORM_ALPHA * h1_ref[...] + m, g2_ref[...], b2_ref[...])

    _by_staged_blocks(seg_ref, len_ref, tile, stage_ref.shape[0] // G_BLOCK, finish)


def _combine(seg, ln, dst, h1, cpos_nt, gw_nt, g2, b2, ys, *, tm):
    n, d = h1.shape
    return pl.pallas_call(
        functools.partial(_combine_kernel, tm=tm),
        grid_spec=pltpu.PrefetchScalarGridSpec(
            num_scalar_prefetch=3,
            grid=(n // tm,),
            in_specs=[pl.BlockSpec((tm, d), lambda i, *_: (i, 0)),
                      pl.BlockSpec((tm, TOP_K), lambda i, *_: (i, 0)), pl.BlockSpec((tm, TOP_K), lambda i, *_: (i, 0)),
                      pl.BlockSpec((1, d), lambda i, *_: (0, 0)), pl.BlockSpec((1, d), lambda i, *_: (0, 0)),
                      pl.BlockSpec(memory_space=pl.ANY)],
            out_specs=pl.BlockSpec((tm, d), lambda i, *_: (i, 0)),
            scratch_shapes=[pltpu.VMEM((_stage_rows(tm), d), BF16), pltpu.VMEM((tm, _stage_rows(tm)), BF16),
                            pltpu.SemaphoreType.DMA(())],
        ),
        out_shape=jax.ShapeDtypeStruct((n, d), F32),
        compiler_params=_params(("arbitrary",)),
        name="combine",
    )(seg, ln, dst, h1, cpos_nt, gw_nt, g2, b2, ys)


def _block_tail(x2d, att, pool, wts, *, tm, tmoe):
    n, d = x2d.shape
    tm = min(tm, n)
    nt = n // tm
    h1, cpos, gw_t, seg_t, len_t, dst_t, used = _tail(x2d, att, pool, wts["g0"], wts["b0"], wts["woa"], wts["wob"],
                                                      wts["g1"], wts["b1"], wts["wrh"], wts["wrl"], wts["br"], tm=tm)
    used = used[:, 0].astype(I32)
    cap = ((used + tmoe - 1) // tmoe) * tmoe
    ends = jnp.cumsum(cap)
    offs = ends - cap
    n_tiles = -(-(TOP_K * n + (ROW_ALIGN - 1) * N_EXPERTS * nt) // tmoe) + N_EXPERTS
    tile_start = jnp.arange(n_tiles, dtype=I32) * tmoe
    tile_expert = jnp.minimum(jnp.sum((ends[None, :] <= tile_start[:, None]).astype(I32), axis=1), N_EXPERTS - 1)
    tile_valid = jnp.clip(offs[tile_expert] + used[tile_expert] - tile_start, 0, tmoe).astype(I32)
    tile_valid = jnp.where(tile_start < ends[-1], tile_valid, 0)
    seg = seg_t[:, :, 0].astype(I32).reshape(-1)
    ln = len_t[:, :, 0].astype(I32).reshape(-1)
    dst = (dst_t[:, :, 0].astype(I32) + offs[None, :]).reshape(-1)

    xs = _dispatch(seg, ln, dst, h1, cpos, tm=tm, p_rows=n_tiles * tmoe)
    ys = _moe(tile_expert, tile_valid, xs, wts["w1e"], wts["b1e"], wts["w2e"], wts["b2e"], tmoe=tmoe)
    return _combine(seg, ln, dst, h1, cpos.T, gw_t.T, wts["g2"], wts["b2"], ys, tm=tm)


def kernel(x_prompt, x_sample, cache_k, cache_v, cache_kidx, state_pool, ln0_g, ln0_b, w_in, w_o,
           pool_w, pool_scale, ln1_g, ln1_b, w_router, b_router, w1, b1, w2, b2, ln2_g, ln2_b):
    bp, s_len, d = x_prompt.shape
    bs, t_len, _ = x_sample.shape
    l_past = cache_k.shape[2]
    aw = N_HEADS * HEAD_DIM
    pw = d - aw
    lyr = 0
    lc = 256

    k_off = aw
    v_off = k_off + HEAD_DIM
    qi_off = v_off + HEAD_DIM
    ki_off = qi_off + IDX_HEADS * IDX_DIM
    wi_off = ki_off + IDX_DIM
    u_off = wi_off + IDX_HEADS
    win = w_in[lyr]
    wa = jnp.concatenate([win[:, 0:k_off], win[:, qi_off:ki_off], win[:, u_off:u_off + pw]], axis=1).astype(BF16)
    wb = jnp.concatenate([win[:, k_off:v_off], win[:, v_off:qi_off], win[:, ki_off:wi_off],
                          jnp.zeros((d, 64), F32)], axis=1).astype(BF16)
    wt = jnp.concatenate([win[:, v_off:qi_off].T, win[:, wi_off:u_off].T, jnp.zeros((8, d), F32)], axis=0).astype(BF16)

    g0 = ln0_g.reshape(1, d)
    b0 = ln0_b.reshape(1, d)
    wrt = w_router[lyr].T
    wrh = wrt.astype(BF16)
    wts = dict(
        g0=g0, b0=b0,
        woa=w_o[lyr][:aw].astype(BF16), wob=w_o[lyr][aw:].astype(BF16),
        g1=ln1_g[lyr].reshape(1, d), b1=ln1_b[lyr].reshape(1, d),
        wrh=wrh, wrl=(wrt - wrh.astype(F32)).astype(BF16), br=b_router[lyr].reshape(N_EXPERTS, 1),
        w1e=w1[lyr], b1e=b1[lyr], w2e=w2[lyr], b2e=b2[lyr],
        g2=ln2_g[lyr].reshape(1, d), b2=ln2_b[lyr].reshape(1, d),
    )
    pool_w_b = pool_w[lyr].astype(BF16)
    pool_sc = pool_scale[lyr].reshape(1, pw)

    xp = x_prompt.reshape(bp * s_len, d)
    q, qi, u, k, v, ki, kb, kib, vt, wit = _proj(xp, g0, b0, wa, wb, wt, tm=512, lc=lc)
    att_p = _dsa(q.reshape(bp, s_len, aw), qi.reshape(bp, s_len, aw), wit,
                 kb.reshape(bp, s_len, HEAD_DIM), kib.reshape(bp, s_len, IDX_DIM), vt,
                 tq=256, lc=lc, causal=True, l_valid=s_len, q_pos0=0, topk=min(TOPK_MAX, s_len // 4))
    u_p = u.reshape(bp, s_len, pw)
    pool_p = _pool(u_p, jnp.zeros((bp, POOL_PAST + 1, pw), F32), pool_w_b, pool_sc, pos0=0)
    y_p = _block_tail(xp, att_p.reshape(bp * s_len, aw), pool_p.reshape(bp * s_len, pw), wts,
                      tm=512, tmoe=512)

    xs = x_sample.reshape(bs * t_len, d)
    qs, qis, us, kn, vn, kin, _, _, _, wits = _proj(xs, g0, b0, wa, wb, wt, tm=512, lc=lc)
    l_all = l_past + t_len
    l_pad = -(-l_all // lc) * lc
    tq_s = LANES
    pad_keys = lambda a: jnp.pad(a, ((0, 0), (0, l_pad - l_all), (0, 0)))
    k_all = pad_keys(jnp.concatenate([cache_k[lyr], kn.reshape(bs, t_len, HEAD_DIM)], axis=1))
    v_all = pad_keys(jnp.concatenate([cache_v[lyr], vn.reshape(bs, t_len, HEAD_DIM)], axis=1))
    ki_all = pad_keys(jnp.concatenate([cache_kidx[lyr], kin.reshape(bs, t_len, IDX_DIM)], axis=1))
    vt_all = v_all.reshape(bs, l_pad // lc, lc, HEAD_DIM).transpose(0, 1, 3, 2).reshape(-1, HEAD_DIM, lc)
    pad_q = lambda a: jnp.pad(a.reshape(bs, t_len, aw), ((0, 0), (0, tq_s - t_len), (0, 0)))
    wit_s = jnp.pad(wits.reshape(IDX_HEADS, bs, t_len), ((0, 0), (0, 0), (0, tq_s - t_len))).reshape(IDX_HEADS, -1)
    att_s = _dsa(pad_q(qs), pad_q(qis), wit_s, k_all.astype(BF16), ki_all.astype(BF16), vt_all.astype(BF16),
                 tq=tq_s, lc=lc, causal=False, l_valid=l_all, q_pos0=l_past, topk=min(TOPK_MAX, l_all // 4))
    att_s = att_s[:, :t_len].reshape(bs * t_len, aw)
    us3 = us.reshape(bs, t_len, pw)
    prefix_s = jnp.concatenate([jnp.zeros((bs, 1, pw), F32), state_pool[lyr]], axis=1)
    pool_s = _pool(us3, prefix_s, pool_w_b, pool_sc, pos0=l_past)
    y_s = _block_tail(xs, att_s, pool_s.reshape(bs * t_len, pw), wts, tm=512, tmoe=128)

    pool_state_p = u_p[:, s_len - POOL_PAST:]
    pool_state_s = jnp.concatenate([state_pool[lyr], us3], axis=1)[:, -POOL_PAST:]
    return (y_p.reshape(bp, s_len, d), y_s.reshape(bs, t_len, d),
            k.reshape(1, bp, s_len, HEAD_DIM), v.reshape(1, bp, s_len, HEAD_DIM),
            ki.reshape(1, bp, s_len, IDX_DIM), pool_state_p[None],
            kn.reshape(1, bs, t_len, HEAD_DIM), vn.reshape(1, bs, t_len, HEAD_DIM),
            kin.reshape(1, bs, t_len, IDX_DIM), pool_state_s[None])
```

```python
import functools

import jax
import jax.numpy as jnp
from jax import lax
from jax.experimental import pallas as pl
from jax.experimental.pallas import tpu as pltpu

F32 = jnp.float32
BF16 = jnp.bfloat16
I32 = jnp.int32

CHUNK = 64
CHUNK_SHIFT = 6
assert 1 << CHUNK_SHIFT == CHUNK
N_HEADS = 8
HEAD_DIM = 64
IDX_HEADS = 8
IDX_DIM = 64
TOPK_MAX = 256
POOL_WINDOWS = (2, 4, 8, 16)
POOL_PAST = 15
N_EXPERTS = 32
TOP_K = 4
SWIGLU_LIMIT = 7.0
SWIGLU_ALPHA = 1.702
LN_EPS = 1e-5
DEPTH = 1
DEEPNORM_ALPHA = (2 * DEPTH) ** 0.25

LANES = 128
SUBLANES = 8
BF16_ROWS = 16
VMEM_LIMIT_BYTES = 56 * 1024 * 1024

NEG_BIG = -1e30
KEY_NEG_INF = -2139095041
KEY_POS_INF = 2139095040

NT_DIMS = (((1,), (1,)), ((), ()))


def _layer_norm(x, g, b):
    mu = jnp.mean(x, axis=-1, keepdims=True)
    xc = x - mu
    var = jnp.mean(xc * xc, axis=-1, keepdims=True)
    return xc * lax.rsqrt(var + LN_EPS) * g + b


def _params(sem):
    return pltpu.CompilerParams(dimension_semantics=sem, vmem_limit_bytes=VMEM_LIMIT_BYTES)


def _proj_kernel(x_ref, g_ref, b_ref, wa_ref, wb_ref, wt_ref,
                 q_ref, qi_ref, u_ref, k_ref, v_ref, ki_ref, kb_ref, kib_ref, vt_ref, wit_ref, *, lc):
    h = _layer_norm(x_ref[...], g_ref[...], b_ref[...])
    hb = h.astype(BF16)
    aw = N_HEADS * HEAD_DIM
    pa = jnp.dot(hb, wa_ref[...], preferred_element_type=F32)
    q_ref[...] = (pa[:, :aw] * (HEAD_DIM ** -0.5)).astype(BF16)
    qi_ref[...] = pa[:, aw:2 * aw].astype(BF16)
    u_ref[...] = pa[:, 2 * aw:]
    pb = jnp.dot(hb, wb_ref[...], preferred_element_type=F32)
    k = pb[:, 0:HEAD_DIM]
    v = pb[:, HEAD_DIM:2 * HEAD_DIM]
    ki = pb[:, 2 * HEAD_DIM:2 * HEAD_DIM + IDX_DIM]
    k_ref[...] = k
    v_ref[...] = v
    ki_ref[...] = ki
    kb_ref[...] = k.astype(BF16)
    kib_ref[...] = ki.astype(BF16)
    pt = lax.dot_general(wt_ref[...], hb, NT_DIMS, preferred_element_type=F32)
    for c in range(vt_ref.shape[0]):
        vt_ref[c] = pt[0:HEAD_DIM, c * lc:(c + 1) * lc].astype(BF16)
    wi = pt[HEAD_DIM:HEAD_DIM + IDX_HEADS, :]
    wit_ref[...] = (wi * (IDX_HEADS ** -0.5)) * (IDX_DIM ** -0.5)


def _proj(x2d, g, b, wa, wb, wt, *, tm, lc):
    n, d = x2d.shape
    tm = min(tm, n)
    aw = N_HEADS * HEAD_DIM
    uw = wa.shape[1] - 2 * aw
    row = lambda i: (i, 0)
    const = lambda i: (0, 0)
    out_shape = (
        jax.ShapeDtypeStruct((n, aw), BF16),
        jax.ShapeDtypeStruct((n, aw), BF16),
        jax.ShapeDtypeStruct((n, uw), F32),
        jax.ShapeDtypeStruct((n, HEAD_DIM), F32),
        jax.ShapeDtypeStruct((n, HEAD_DIM), F32),
        jax.ShapeDtypeStruct((n, IDX_DIM), F32),
        jax.ShapeDtypeStruct((n, HEAD_DIM), BF16),
        jax.ShapeDtypeStruct((n, IDX_DIM), BF16),
        jax.ShapeDtypeStruct((n // lc, HEAD_DIM, lc), BF16),
        jax.ShapeDtypeStruct((IDX_HEADS, n), F32),
    )
    out_specs = (
        pl.BlockSpec((tm, aw), row), pl.BlockSpec((tm, aw), row), pl.BlockSpec((tm, uw), row),
        pl.BlockSpec((tm, HEAD_DIM), row), pl.BlockSpec((tm, HEAD_DIM), row), pl.BlockSpec((tm, IDX_DIM), row),
        pl.BlockSpec((tm, HEAD_DIM), row), pl.BlockSpec((tm, IDX_DIM), row),
        pl.BlockSpec((tm // lc, HEAD_DIM, lc), lambda i: (i, 0, 0)),
        pl.BlockSpec((IDX_HEADS, tm), lambda i: (0, i)),
    )
    return pl.pallas_call(
        functools.partial(_proj_kernel, lc=lc),
        grid=(n // tm,),
        in_specs=[pl.BlockSpec((tm, d), row), pl.BlockSpec((1, d), const), pl.BlockSpec((1, d), const),
                  pl.BlockSpec(wa.shape, const), pl.BlockSpec(wb.shape, const), pl.BlockSpec(wt.shape, const)],
        out_specs=out_specs,
        out_shape=out_shape,
        compiler_params=_params(("parallel",)),
        name="proj",
    )(x2d, g, b, wa, wb, wt)


def _key_to_float(key):
    bits = jnp.where(key >= 0, key, key ^ jnp.int32(0x7FFFFFFF))
    return lax.bitcast_convert_type(bits, F32)


def _dsa_kernel(q_ref, qi_ref, wit_ref, kb_ref, kib_ref, vt_ref, o_ref,
                sc_ref, sch_ref, s_ref, out_ref, *acc_refs, tq, lc, nk_static, causal, l_valid, q_pos0, topk):
    qb = pl.program_id(1)
    nk = qb * (tq // lc) + (tq // lc) if causal else nk_static
    q_chunk = (q_pos0 + qb * tq + lax.broadcasted_iota(I32, (lc, tq), 1)) >> CHUNK_SHIFT
    row_iota = lax.broadcasted_iota(I32, (lc, tq), 0)

    def chunk_loop(body, init):
        if causal:
            return lax.fori_loop(0, nk, body, init)
        return lax.fori_loop(0, nk_static, body, init, unroll=True)

    def score_chunk(kc, carry):
        off = pl.multiple_of(kc * lc, lc)
        kic = kib_ref[0, pl.ds(off, lc), :]
        acc = jnp.zeros((lc, tq), F32)
        for h in range(IDX_HEADS):
            s = lax.dot_general(kic, qi_ref[0, :, h * IDX_DIM:(h + 1) * IDX_DIM], NT_DIMS,
                                preferred_element_type=F32)
            acc = acc + wit_ref[h:h + 1, :] * jnp.maximum(s, 0.0)
        l_pos = off + row_iota
        visible = ((l_pos >> CHUNK_SHIFT) <= q_chunk) & (l_pos < l_valid)
        score = jnp.where(visible, acc, -jnp.inf)
        sc_ref[pl.ds(off, lc), :] = score
        sch_ref[pl.ds(off, lc), :] = score.astype(BF16)
        return carry

    chunk_loop(score_chunk, 0)

    def count(pred):
        ways = 4 * SUBLANES

        def body(kc, part):
            off = pl.multiple_of(kc * lc, lc)
            hit = jnp.where(pred(sc_ref[pl.ds(off, lc), :]), 1.0, 0.0)
            return part + jnp.sum(hit.reshape(lc // ways, ways, tq), axis=0)
        return jnp.sum(chunk_loop(body, jnp.zeros((ways, tq), F32)), axis=0, keepdims=True)

    def count_coarse(cand):
        ways = 2 * BF16_ROWS
        assert sc_ref.shape[0] // ways <= 256

        def body(kc, part):
            off = pl.multiple_of(kc * lc, lc)
            hit = jnp.where(sch_ref[pl.ds(off, lc), :] >= cand, jnp.ones((), BF16), jnp.zeros((), BF16))
            hit = hit.reshape(lc // ways, ways, tq)
            terms = [hit[j] for j in range(lc // ways)]
            while len(terms) > 1:
                terms = [a + b for a, b in zip(terms[::2], terms[1::2])]
            return part + terms[0]
        part = chunk_loop(body, jnp.zeros((ways, tq), BF16))
        return jnp.sum(part.astype(F32), axis=0, keepdims=True)

    def bisect(count_ge, key_to_value, lo, hi, steps):
        def step(_, carry):
            lo, hi = carry
            mid = (lo >> 1) + (hi >> 1) + (lo & hi & 1)
            ok = count_ge(key_to_value(mid)) >= topk
            return jnp.where(ok, mid, lo), jnp.where(ok, hi, mid)
        return lax.fori_loop(0, steps, step, (lo, hi))[0]

    def coarse_key_to_f32_key(k16):
        return jnp.where(k16 >= 0, k16 << 16, (k16 << 16) | 0xFFFF)

    full = lambda v: jnp.full((1, tq), v, I32)
    k16 = bisect(count_coarse, lambda k: _key_to_float(coarse_key_to_f32_key(k)).astype(BF16),
                 full(KEY_NEG_INF >> 16), full((KEY_POS_INF >> 16) + 1), 16)
    kb = coarse_key_to_f32_key(k16)
    half = 1 << 15
    lo = bisect(lambda cand: count(lambda blk: blk >= cand), _key_to_float,
                jnp.maximum(kb, KEY_NEG_INF + half + 1) - (half + 1),
                jnp.minimum(kb, KEY_POS_INF - half) + (half + 1), 17)
    thr = _key_to_float(lo)
    n_above = count(lambda blk: blk > thr)
    n_ties = topk - n_above

    tri = jnp.where(lax.broadcasted_iota(I32, (lc, lc), 0) >= lax.broadcasted_iota(I32, (lc, lc), 1),
                    1.0, 0.0).astype(BF16)

    def fold8(x):
        return x.reshape(lc // SUBLANES, SUBLANES, tq)

    def logits_chunk(kc, carry):
        ties_before, m8 = carry
        off = pl.multiple_of(kc * lc, lc)
        blk = sc_ref[pl.ds(off, lc), :]
        tie = blk == thr
        tie_rank = jnp.dot(tri, jnp.where(tie, 1.0, 0.0).astype(BF16), preferred_element_type=F32) + ties_before
        bias = jnp.where(blk > thr, 0.0, jnp.where(tie, jnp.where(tie_rank <= n_ties, 0.0, NEG_BIG), NEG_BIG))
        bias = jnp.where(blk == -jnp.inf, NEG_BIG, bias)
        kc_b = kb_ref[0, pl.ds(off, lc), :]
        m_rows = []
        for h in range(N_HEADS):
            s = lax.dot_general(kc_b, q_ref[0, :, h * HEAD_DIM:(h + 1) * HEAD_DIM], NT_DIMS,
                                preferred_element_type=F32) + bias
            s_ref[h, pl.ds(off, lc), :] = s
            m_rows.append(jnp.maximum(m8[h * SUBLANES:(h + 1) * SUBLANES], jnp.max(fold8(s), axis=0)))
        return tie_rank[lc - 1:lc, :], jnp.concatenate(m_rows, axis=0)

    _, m8 = chunk_loop(logits_chunk, (jnp.zeros((1, tq), F32), jnp.full((N_HEADS * SUBLANES, tq), NEG_BIG, F32)))
    m_all = jnp.max(m8.reshape(N_HEADS, SUBLANES, tq), axis=1)

    for acc_ref in acc_refs:
        acc_ref[...] = jnp.zeros(acc_ref.shape, F32)

    def pv_chunk(kc, l8):
        off = pl.multiple_of(kc * lc, lc)
        vt_c = vt_ref[kc]
        l_rows = []
        for h in range(N_HEADS):
            p = jnp.exp(s_ref[h, pl.ds(off, lc), :] - m_all[h:h + 1, :])
            l_rows.append(l8[h * SUBLANES:(h + 1) * SUBLANES] + jnp.sum(fold8(p), axis=0))
            acc_refs[h][...] += jnp.dot(vt_c, p.astype(BF16), preferred_element_type=F32)
        return jnp.concatenate(l_rows, axis=0)

    l8 = chunk_loop(pv_chunk, jnp.zeros((N_HEADS * SUBLANES, tq), F32))
    l_all = jnp.sum(l8.reshape(N_HEADS, SUBLANES, tq), axis=1)

    for h in range(N_HEADS):
        out_ref[h * HEAD_DIM:(h + 1) * HEAD_DIM, :] = acc_refs[h][...] / l_all[h:h + 1, :]
    o_ref[0] = out_ref[...].T.astype(BF16)


def _dsa(q, qi, wit, kb, kib, vt, *, tq, lc, causal, l_valid, q_pos0, topk):
    bsz, tq_tot, aw = q.shape
    l_tot = kb.shape[1]
    nq = tq_tot // tq
    nkc = l_tot // lc
    kern = functools.partial(_dsa_kernel, tq=tq, lc=lc, nk_static=nkc, causal=causal, l_valid=l_valid,
                             q_pos0=q_pos0, topk=topk)
    return pl.pallas_call(
        kern,
        grid=(bsz, nq),
        in_specs=[
            pl.BlockSpec((1, tq, aw), lambda b, i: (b, i, 0)),
            pl.BlockSpec((1, tq, aw), lambda b, i: (b, i, 0)),
            pl.BlockSpec((IDX_HEADS, tq), lambda b, i: (0, b * nq + i)),
            pl.BlockSpec((1, l_tot, HEAD_DIM), lambda b, i: (b, 0, 0)),
            pl.BlockSpec((1, l_tot, IDX_DIM), lambda b, i: (b, 0, 0)),
            pl.BlockSpec((nkc, HEAD_DIM, lc), lambda b, i: (b, 0, 0)),
        ],
        out_specs=pl.BlockSpec((1, tq, aw), lambda b, i: (b, i, 0)),
        out_shape=jax.ShapeDtypeStruct((bsz, tq_tot, aw), BF16),
        scratch_shapes=[
            pltpu.VMEM((l_tot, tq), F32),
            pltpu.VMEM((l_tot, tq), BF16),
            pltpu.VMEM((N_HEADS, l_tot, tq), F32),
            pltpu.VMEM((aw, tq), F32),
        ] + [pltpu.VMEM((HEAD_DIM, tq), F32) for _ in range(N_HEADS)],
        compiler_params=_params(("parallel", "parallel")),
        name="dsa",
    )(q, qi, wit, kb, kib, vt)


def _pool_kernel(u_ref, pre_ref, pw_ref, sc_ref, o_ref, ext_ref, *, t_len, tt, pos0):
    pad = pre_ref.shape[1]
    gw = pw_ref.shape[1]
    ext_ref[0:pad, :] = pre_ref[0]
    ext_ref[pad:pad + t_len, :] = u_ref[0]
    row = lax.broadcasted_iota(I32, (tt, gw), 0)
    for t in range(t_len // tt):
        r0 = t * tt
        for g, w in enumerate(POOL_WINDOWS):
            cols = slice(g * gw, (g + 1) * gw)
            cur = ext_ref[pad + r0:pad + r0 + tt, cols]
            wsum = cur
            for j in range(1, w):
                wsum = wsum + ext_ref[pad + r0 - j:pad + r0 - j + tt, cols]
            cnt = jnp.minimum(w, pos0 + r0 + 1 + row).astype(F32)
            diff = (wsum / cnt - cur).astype(BF16)
            y = jnp.dot(diff, pw_ref[g], preferred_element_type=F32)
            o_ref[0, r0:r0 + tt, cols] = (y * sc_ref[:, cols]).astype(BF16)


def _pool(u, prefix, pool_w_b, pool_scale, *, pos0):
    bsz, t_len, c = u.shape
    pad = prefix.shape[1]
    tt = min(t_len, 256)
    return pl.pallas_call(
        functools.partial(_pool_kernel, t_len=t_len, tt=tt, pos0=pos0),
        grid=(bsz,),
        in_specs=[pl.BlockSpec((1, t_len, c), lambda b: (b, 0, 0)),
                  pl.BlockSpec((1, pad, c), lambda b: (b, 0, 0)),
                  pl.BlockSpec(pool_w_b.shape, lambda b: (0, 0, 0)),
                  pl.BlockSpec((1, c), lambda b: (0, 0))],
        out_specs=pl.BlockSpec((1, t_len, c), lambda b: (b, 0, 0)),
        out_shape=jax.ShapeDtypeStruct((bsz, t_len, c), BF16),
        scratch_shapes=[pltpu.VMEM((pad + t_len, c), F32)],
        compiler_params=_params(("parallel",)),
        name="pool",
    )(u, prefix, pool_w_b, pool_scale)


ROW_ALIGN = 16


def _tail_kernel(x_ref, att_ref, pool_ref, g0_ref, b0_ref, woa_ref, wob_ref, g1_ref, b1_ref,
                 wrh_ref, wrl_ref, br_ref,
                 h1_ref, cpos_ref, gw_ref, seg_ref, len_ref, dst_ref, used_ref, carry_ref, *, tm):
    @pl.when(pl.program_id(0) == 0)
    def _():
        carry_ref[...] = jnp.zeros(carry_ref.shape, F32)

    h = _layer_norm(x_ref[...], g0_ref[...], b0_ref[...])
    mix = (jnp.dot(att_ref[...], woa_ref[...], preferred_element_type=F32)
           + jnp.dot(pool_ref[...], wob_ref[...], preferred_element_type=F32))
    h1 = _layer_norm(DEEPNORM_ALPHA * h + mix, g1_ref[...], b1_ref[...])
    h1_ref[...] = h1

    hh = h1.astype(BF16)
    hl = (h1 - hh.astype(F32)).astype(BF16)
    logits = (lax.dot_general(wrh_ref[...], hh, NT_DIMS, preferred_element_type=F32)
              + lax.dot_general(wrh_ref[...], hl, NT_DIMS, preferred_element_type=F32)
              + lax.dot_general(wrl_ref[...], hh, NT_DIMS, preferred_element_type=F32)
              + br_ref[...])

    e_iota = lax.broadcasted_iota(I32, (N_EXPERTS, tm), 0).astype(F32)
    work = logits
    vals, hots = [], []
    for k in range(TOP_K):
        m = jnp.max(work, axis=0, keepdims=True)
        idx = jnp.min(jnp.where(work == m, e_iota, float(N_EXPERTS)), axis=0, keepdims=True)
        hot = e_iota == idx
        vals.append(m)
        hots.append(jnp.where(hot, 1.0, 0.0))
        work = jnp.where(hot, -jnp.inf, work)
    exps = [jnp.exp(v - vals[0]) for v in vals]
    den = exps[0] + exps[1] + exps[2] + exps[3]
    for k in range(TOP_K):
        gw_ref[k:k + 1, :] = exps[k] / den

    hot_all = (hots[0] + hots[1] + hots[2] + hots[3]).astype(BF16)
    count = jnp.dot(hot_all, jnp.ones((tm, LANES), BF16), preferred_element_type=F32)
    chunk_len = jnp.ceil(count * (1.0 / ROW_ALIGN)) * ROW_ALIGN
    lower = jnp.where(lax.broadcasted_iota(I32, (N_EXPERTS, N_EXPERTS), 1)
                      < lax.broadcasted_iota(I32, (N_EXPERTS, N_EXPERTS), 0), 1.0, 0.0).astype(BF16)
    seg_base = jnp.dot(lower, chunk_len.astype(BF16), preferred_element_type=F32)
    before = jnp.where(lax.broadcasted_iota(I32, (tm, tm), 0) < lax.broadcasted_iota(I32, (tm, tm), 1),
                       1.0, 0.0).astype(BF16)
    slot = jnp.dot(hot_all, before, preferred_element_type=F32) + jnp.tile(seg_base, (1, tm // LANES))
    for k in range(TOP_K):
        cpos_ref[k:k + 1, :] = jnp.sum(hots[k] * slot, axis=0, keepdims=True).astype(I32)
    carry = carry_ref[...]
    seg_ref[0] = seg_base
    len_ref[0] = chunk_len
    dst_ref[0] = carry
    carry_ref[...] = carry + chunk_len
    used_ref[...] = carry + chunk_len


def _tail(x2d, att, pool, g0, b0, woa, wob, g1, b1, wrh, wrl, br, *, tm):
    n, d = x2d.shape
    aw = att.shape[1]
    nt = n // tm
    row = lambda i: (i, 0)
    const = lambda i: (0, 0)
    col = lambda i: (0, i)
    tab = lambda i: (i, 0, 0)
    tab_shape = jax.ShapeDtypeStruct((nt, N_EXPERTS, LANES), F32)
    tab_spec = pl.BlockSpec((1, N_EXPERTS, LANES), tab)
    return pl.pallas_call(
        functools.partial(_tail_kernel, tm=tm),
        grid=(nt,),
        in_specs=[pl.BlockSpec((tm, d), row), pl.BlockSpec((tm, aw), row), pl.BlockSpec((tm, d - aw), row),
                  pl.BlockSpec((1, d), const), pl.BlockSpec((1, d), const),
                  pl.BlockSpec(woa.shape, const), pl.BlockSpec(wob.shape, const),
                  pl.BlockSpec((1, d), const), pl.BlockSpec((1, d), const),
                  pl.BlockSpec(wrh.shape, const), pl.BlockSpec(wrl.shape, const), pl.BlockSpec(br.shape, const)],
        out_specs=(pl.BlockSpec((tm, d), row), pl.BlockSpec((TOP_K, tm), col), pl.BlockSpec((TOP_K, tm), col),
                   tab_spec, tab_spec, tab_spec, pl.BlockSpec((N_EXPERTS, LANES), const)),
        out_shape=(jax.ShapeDtypeStruct((n, d), F32), jax.ShapeDtypeStruct((TOP_K, n), I32),
                   jax.ShapeDtypeStruct((TOP_K, n), F32), tab_shape, tab_shape, tab_shape,
                   jax.ShapeDtypeStruct((N_EXPERTS, LANES), F32)),
        scratch_shapes=[pltpu.VMEM((N_EXPERTS, LANES), F32)],
        compiler_params=_params(("arbitrary",)),
        name="tail",
    )(x2d, att, pool, g0, b0, woa, wob, g1, b1, wrh, wrl, br)


G_BLOCK = 256


def _stage_rows(tm):
    raw = TOP_K * tm + (ROW_ALIGN - 1) * N_EXPERTS
    return -(-raw // G_BLOCK) * G_BLOCK


def _chunk_copies(seg_ref, len_ref, dst_ref, tile, max_len, make_copy, act):
    def per_expert(e, carry):
        idx = tile * N_EXPERTS + e
        seg, ln, dst = seg_ref[idx], len_ref[idx], dst_ref[idx]
        size = ROW_ALIGN
        while size <= max_len:
            done = ln & (-2 * size)

            @pl.when((ln & size) != 0)
            def _(size=size, done=done):
                act(make_copy(pl.multiple_of(seg + done, ROW_ALIGN), pl.multiple_of(dst + done, ROW_ALIGN), size))
            size *= 2
        return carry

    lax.fori_loop(0, N_EXPERTS, per_expert, 0)


def _by_staged_blocks(seg_ref, len_ref, tile, n_full, body):
    last = tile * N_EXPERTS + N_EXPERTS - 1
    total = seg_ref[last] + len_ref[last]
    pl.when(total <= (n_full - 1) * G_BLOCK)(lambda: body(n_full - 1))
    pl.when(total > (n_full - 1) * G_BLOCK)(lambda: body(n_full))


def _route_onehot(iota, pos_list, val_list):
    g = jnp.zeros(iota.shape, F32)
    for pos, val in zip(pos_list, val_list):
        g = jnp.where(iota == pos, val, g)
    return g.astype(BF16)


def _dispatch_kernel(seg_ref, len_ref, dst_ref, h1_ref, cpos_ref, xs_hbm, stage_ref, sem, *, tm):
    tile = pl.program_id(0)
    h1b = h1_ref[...].astype(BF16)
    pos_rows = [cpos_ref[k:k + 1, :] for k in range(TOP_K)]

    def stage_blocks(n_blk):
        for blk in range(n_blk):
            iota = blk * G_BLOCK + lax.broadcasted_iota(I32, (G_BLOCK, tm), 0)
            g = _route_onehot(iota, pos_rows, [1.0] * TOP_K)
            stage_ref[blk * G_BLOCK:(blk + 1) * G_BLOCK, :] = jnp.dot(
                g, h1b, preferred_element_type=F32).astype(BF16)

    _by_staged_blocks(seg_ref, len_ref, tile, stage_ref.shape[0] // G_BLOCK, stage_blocks)

    def make_copy(seg, dst, size):
        return pltpu.make_async_copy(stage_ref.at[pl.ds(seg, size), :], xs_hbm.at[pl.ds(dst, size), :], sem)

    _chunk_copies(seg_ref, len_ref, dst_ref, tile, tm, make_copy, lambda c: c.start())
    _chunk_copies(seg_ref, len_ref, dst_ref, tile, tm, make_copy, lambda c: c.wait())


def _dispatch(seg, ln, dst, h1, cpos, *, tm, p_rows):
    n, d = h1.shape
    return pl.pallas_call(
        functools.partial(_dispatch_kernel, tm=tm),
        grid_spec=pltpu.PrefetchScalarGridSpec(
            num_scalar_prefetch=3,
            grid=(n // tm,),
            in_specs=[pl.BlockSpec((tm, d), lambda i, *_: (i, 0)), pl.BlockSpec((TOP_K, tm), lambda i, *_: (0, i))],
            out_specs=pl.BlockSpec(memory_space=pl.ANY),
            scratch_shapes=[pltpu.VMEM((_stage_rows(tm), d), BF16), pltpu.SemaphoreType.DMA(())],
        ),
        out_shape=jax.ShapeDtypeStruct((p_rows, d), BF16),
        compiler_params=_params(("arbitrary",)),
        name="dispatch",
    )(seg, ln, dst, h1, cpos)


FF_CHUNK = 256


def _moe_kernel(te_ref, tv_ref, x_ref, w1_ref, b1_ref, w2_ref, b2_ref, y_ref, w1b_ref, w2b_ref, act_ref):
    i = pl.program_id(0)
    d_ff = w2_ref.shape[1]
    valid = tv_ref[i]

    @pl.when((i == 0) | (te_ref[i] != te_ref[jnp.maximum(i - 1, 0)]))
    def _():
        w1b_ref[...] = w1_ref[0].astype(BF16)
        w2b_ref[...] = w2_ref[0].astype(BF16)

    @pl.when(valid > 0)
    def _():
        rows = lax.broadcasted_iota(I32, x_ref.shape, 0)
        xb = jnp.where(rows < valid, x_ref[...].astype(F32), 0.0).astype(BF16)
        for j in range(d_ff // FF_CHUNK):
            gs = slice(j * FF_CHUNK, (j + 1) * FF_CHUNK)
            us = slice(d_ff + j * FF_CHUNK, d_ff + (j + 1) * FF_CHUNK)
            gate = jnp.dot(xb, w1b_ref[:, gs], preferred_element_type=F32) + b1_ref[0, :, gs]
            up = jnp.dot(xb, w1b_ref[:, us], preferred_element_type=F32) + b1_ref[0, :, us]
            gate = jnp.minimum(gate, SWIGLU_LIMIT)
            up = jnp.clip(up, -SWIGLU_LIMIT, SWIGLU_LIMIT)
            act_ref[:, gs] = ((up + 1.0) * (gate * jax.nn.sigmoid(SWIGLU_ALPHA * gate))).astype(BF16)
        y = jnp.dot(act_ref[...], w2b_ref[...], preferred_element_type=F32) + b2_ref[0]
        y_ref[...] = y.astype(BF16)

    @pl.when(valid <= 0)
    def _():
        y_ref[...] = jnp.zeros(y_ref.shape, BF16)


def _moe(tile_expert, tile_valid, xs, w1, b1, w2, b2, *, tmoe):
    p, d = xs.shape
    e, _, ff2 = w1.shape
    d_ff = ff2 // 2
    return pl.pallas_call(
        _moe_kernel,
        grid_spec=pltpu.PrefetchScalarGridSpec(
            num_scalar_prefetch=2,
            grid=(p // tmoe,),
            in_specs=[pl.BlockSpec((tmoe, d), lambda i, te, tv: (i, 0)),
                      pl.BlockSpec((1, d, ff2), lambda i, te, tv: (te[i], 0, 0)),
                      pl.BlockSpec((1, 1, ff2), lambda i, te, tv: (te[i], 0, 0)),
                      pl.BlockSpec((1, d_ff, d), lambda i, te, tv: (te[i], 0, 0)),
                      pl.BlockSpec((1, 1, d), lambda i, te, tv: (te[i], 0, 0))],
            out_specs=pl.BlockSpec((tmoe, d), lambda i, te, tv: (i, 0)),
            scratch_shapes=[pltpu.VMEM((d, ff2), BF16), pltpu.VMEM((d_ff, d), BF16), pltpu.VMEM((tmoe, d_ff), BF16)],
        ),
        out_shape=jax.ShapeDtypeStruct((p, d), BF16),
        compiler_params=_params(("arbitrary",)),
        name="moe",
    )(tile_expert, tile_valid, xs, w1, b1.reshape(e, 1, ff2), w2, b2.reshape(e, 1, d))


def _combine_kernel(seg_ref, len_ref, dst_ref, h1_ref, cpos_ref, gw_ref, g2_ref, b2_ref, ys_hbm, o_ref,
                    stage_ref, gate_ref, sem, *, tm):
    tile = pl.program_id(0)

    @pl.when(tile == 0)
    def _():
        stage_ref[...] = jnp.zeros(stage_ref.shape, BF16)

    def make_copy(seg, dst, size):
        return pltpu.make_async_copy(ys_hbm.at[pl.ds(dst, size), :], stage_ref.at[pl.ds(seg, size), :], sem)

    _chunk_copies(seg_ref, len_ref, dst_ref, tile, tm, make_copy, lambda c: c.start())
    cpos = cpos_ref[...]
    gw = gw_ref[...]
    pos_cols = [cpos[:, k:k + 1] for k in range(TOP_K)]
    gw_cols = [gw[:, k:k + 1] for k in range(TOP_K)]

    def finish(n_blk):
        for blk in range(n_blk):
            iota = blk * G_BLOCK + lax.broadcasted_iota(I32, (tm, G_BLOCK), 1)
            gate_ref[:, blk * G_BLOCK:(blk + 1) * G_BLOCK] = _route_onehot(iota, pos_cols, gw_cols)
        _chunk_copies(seg_ref, len_ref, dst_ref, tile, tm, make_copy, lambda c: c.wait())
        rows = n_blk * G_BLOCK
        m = jnp.dot(gate_ref[:, :rows], stage_ref[:rows, :], preferred_element_type=F32)
        o_ref[...] = _layer_norm(DEEPNORM_ALPHA * h1_ref[...] + m, g2_ref[...], b2_ref[...])

    _by_staged_blocks(seg_ref, len_ref, tile, stage_ref.shape[0] // G_BLOCK, finish)


def _combine(seg, ln, dst, h1, cpos_nt, gw_nt, g2, b2, ys, *, tm):
    n, d = h1.shape
    return pl.pallas_call(
        functools.partial(_combine_kernel, tm=tm),
        grid_spec=pltpu.PrefetchScalarGridSpec(
            num_scalar_prefetch=3,
            grid=(n // tm,),
            in_specs=[pl.BlockSpec((tm, d), lambda i, *_: (i, 0)),
                      pl.BlockSpec((tm, TOP_K), lambda i, *_: (i, 0)), pl.BlockSpec((tm, TOP_K), lambda i, *_: (i, 0)),
                      pl.BlockSpec((1, d), lambda i, *_: (0, 0)), pl.BlockSpec((1, d), lambda i, *_: (0, 0)),
                      pl.BlockSpec(memory_space=pl.ANY)],
            out_specs=pl.BlockSpec((tm, d), lambda i, *_: (i, 0)),
            scratch_shapes=[pltpu.VMEM((_stage_rows(tm), d), BF16), pltpu.VMEM((tm, _stage_rows(tm)), BF16),
                            pltpu.SemaphoreType.DMA(())],
        ),
        out_shape=jax.ShapeDtypeStruct((n, d), F32),
        compiler_params=_params(("arbitrary",)),
        name="combine",
    )(seg, ln, dst, h1, cpos_nt, gw_nt, g2, b2, ys)


def _block_tail(x2d, att, pool, wts, *, tm, tmoe):
    n, d = x2d.shape
    tm = min(tm, n)
    nt = n // tm
    h1, cpos, gw_t, seg_t, len_t, dst_t, used = _tail(x2d, att, pool, wts["g0"], wts["b0"], wts["woa"], wts["wob"],
                                                      wts["g1"], wts["b1"], wts["wrh"], wts["wrl"], wts["br"], tm=tm)
    used = used[:, 0].astype(I32)
    cap = ((used + tmoe - 1) // tmoe) * tmoe
    ends = jnp.cumsum(cap)
    offs = ends - cap
    n_tiles = -(-(TOP_K * n + (ROW_ALIGN - 1) * N_EXPERTS * nt) // tmoe) + N_EXPERTS
    tile_start = jnp.arange(n_tiles, dtype=I32) * tmoe
    tile_expert = jnp.minimum(jnp.sum((ends[None, :] <= tile_start[:, None]).astype(I32), axis=1), N_EXPERTS - 1)
    tile_valid = jnp.clip(offs[tile_expert] + used[tile_expert] - tile_start, 0, tmoe).astype(I32)
    tile_valid = jnp.where(tile_start < ends[-1], tile_valid, 0)
    seg = seg_t[:, :, 0].astype(I32).reshape(-1)
    ln = len_t[:, :, 0].astype(I32).reshape(-1)
    dst = (dst_t[:, :, 0].astype(I32) + offs[None, :]).reshape(-1)

    xs = _dispatch(seg, ln, dst, h1, cpos, tm=tm, p_rows=n_tiles * tmoe)
    ys = _moe(tile_expert, tile_valid, xs, wts["w1e"], wts["b1e"], wts["w2e"], wts["b2e"], tmoe=tmoe)
    return _combine(seg, ln, dst, h1, cpos.T, gw_t.T, wts["g2"], wts["b2"], ys, tm=tm)


def kernel(x_prompt, x_sample, cache_k, cache_v, cache_kidx, state_pool, ln0_g, ln0_b, w_in, w_o,
           pool_w, pool_scale, ln1_g, ln1_b, w_router, b_router, w1, b1, w2, b2, ln2_g, ln2_b):
    bp, s_len, d = x_prompt.shape
    bs, t_len, _ = x_sample.shape
    l_past = cache_k.shape[2]
    aw = N_HEADS * HEAD_DIM
    pw = d - aw
    lyr = 0
    lc = 256

    k_off = aw
    v_off = k_off + HEAD_DIM
    qi_off = v_off + HEAD_DIM
    ki_off = qi_off + IDX_HEADS * IDX_DIM
    wi_off = ki_off + IDX_DIM
    u_off = wi_off + IDX_HEADS
    win = w_in[lyr]
    wa = jnp.concatenate([win[:, 0:k_off], win[:, qi_off:ki_off], win[:, u_off:u_off + pw]], axis=1).astype(BF16)
    wb = jnp.concatenate([win[:, k_off:v_off], win[:, v_off:qi_off], win[:, ki_off:wi_off],
                          jnp.zeros((d, 64), F32)], axis=1).astype(BF16)
    wt = jnp.concatenate([win[:, v_off:qi_off].T, win[:, wi_off:u_off].T, jnp.zeros((8, d), F32)], axis=0).astype(BF16)

    g0 = ln0_g.reshape(1, d)
    b0 = ln0_b.reshape(1, d)
    wrt = w_router[lyr].T
    wrh = wrt.astype(BF16)
    wts = dict(
        g0=g0, b0=b0,
        woa=w_o[lyr][:aw].astype(BF16), wob=w_o[lyr][aw:].astype(BF16),
        g1=ln1_g[lyr].reshape(1, d), b1=ln1_b[lyr].reshape(1, d),
        wrh=wrh, wrl=(wrt - wrh.astype(F32)).astype(BF16), br=b_router[lyr].reshape(N_EXPERTS, 1),
        w1e=w1[lyr], b1e=b1[lyr], w2e=w2[lyr], b2e=b2[lyr],
        g2=ln2_g[lyr].reshape(1, d), b2=ln2_b[lyr].reshape(1, d),
    )
    pool_w_b = pool_w[lyr].astype(BF16)
    pool_sc = pool_scale[lyr].reshape(1, pw)

    xp = x_prompt.reshape(bp * s_len, d)
    q, qi, u, k, v, ki, kb, kib, vt, wit = _proj(xp, g0, b0, wa, wb, wt, tm=512, lc=lc)
    att_p = _dsa(q.reshape(bp, s_len, aw), qi.reshape(bp, s_len, aw), wit,
                 kb.reshape(bp, s_len, HEAD_DIM), kib.reshape(bp, s_len, IDX_DIM), vt,
                 tq=256, lc=lc, causal=True, l_valid=s_len, q_pos0=0, topk=min(TOPK_MAX, s_len // 4))
    u_p = u.reshape(bp, s_len, pw)
    pool_p = _pool(u_p, jnp.zeros((bp, POOL_PAST + 1, pw), F32), pool_w_b, pool_sc, pos0=0)
    y_p = _block_tail(xp, att_p.reshape(bp * s_len, aw), pool_p.reshape(bp * s_len, pw), wts,
                      tm=512, tmoe=1024)

    xs = x_sample.reshape(bs * t_len, d)
    qs, qis, us, kn, vn, kin, _, _, _, wits = _proj(xs, g0, b0, wa, wb, wt, tm=512, lc=lc)
    l_all = l_past + t_len
    l_pad = -(-l_all // lc) * lc
    tq_s = LANES
    pad_keys = lambda a: jnp.pad(a, ((0, 0), (0, l_pad - l_all), (0, 0)))
    k_all = pad_keys(jnp.concatenate([cache_k[lyr], kn.reshape(bs, t_len, HEAD_DIM)], axis=1))
    v_all = pad_keys(jnp.concatenate([cache_v[lyr], vn.reshape(bs, t_len, HEAD_DIM)], axis=1))
    ki_all = pad_keys(jnp.concatenate([cache_kidx[lyr], kin.reshape(bs, t_len, IDX_DIM)], axis=1))
    vt_all = v_all.reshape(bs, l_pad // lc, lc, HEAD_DIM).transpose(0, 1, 3, 2).reshape(-1, HEAD_DIM, lc)
    pad_q = lambda a: jnp.pad(a.reshape(bs, t_len, aw), ((0, 0), (0, tq_s - t_len), (0, 0)))
    wit_s = jnp.pad(wits.reshape(IDX_HEADS, bs, t_len), ((0, 0), (0, 0), (0, tq_s - t_len))).reshape(IDX_HEADS, -1)
    att_s = _dsa(pad_q(qs), pad_q(qis), wit_s, k_all.astype(BF16), ki_all.astype(BF16), vt_all.astype(BF16),
                 tq=tq_s, lc=lc, causal=False, l_valid=l_all, q_pos0=l_past, topk=min(TOPK_MAX, l_all // 4))
    att_s = att_s[:, :t_len].reshape(bs * t_len, aw)
    us3 = us.reshape(bs, t_len, pw)
    prefix_s = jnp.concatenate([jnp.zeros((bs, 1, pw), F32), state_pool[lyr]], axis=1)
    pool_s = _pool(us3, prefix_s, pool_w_b, pool_sc, pos0=l_past)
    y_s = _block_tail(xs, att_s, pool_s.reshape(bs * t_len, pw), wts, tm=512, tmoe=128)

    pool_state_p = u_p[:, s_len - POOL_PAST:]
    pool_state_s = jnp.concatenate([state_pool[lyr], us3], axis=1)[:, -POOL_PAST:]
    return (y_p.reshape(bp, s_len, d), y_s.reshape(bs, t_len, d),
            k.reshape(1, bp, s_len, HEAD_DIM), v.reshape(1, bp, s_len, HEAD_DIM),
            ki.reshape(1, bp, s_len, IDX_DIM), pool_state_p[None],
            kn.reshape(1, bs, t_len, HEAD_DIM), vn.reshape(1, bs, t_len, HEAD_DIM),
            kin.reshape(1, bs, t_len, IDX_DIM), pool_state_s[None])
```

```python
import functools

import jax
import jax.numpy as jnp
from jax import lax
from jax.experimental import pallas as pl
from jax.experimental.pallas import tpu as pltpu

F32 = jnp.float32
BF16 = jnp.bfloat16
I32 = jnp.int32

CHUNK = 64
CHUNK_SHIFT = 6
assert 1 << CHUNK_SHIFT == CHUNK
N_HEADS = 8
HEAD_DIM = 64
IDX_HEADS = 8
IDX_DIM = 64
TOPK_MAX = 256
POOL_WINDOWS = (2, 4, 8, 16)
POOL_PAST = 15
N_EXPERTS = 32
TOP_K = 4
SWIGLU_LIMIT = 7.0
SWIGLU_ALPHA = 1.702
LN_EPS = 1e-5
DEPTH = 1
DEEPNORM_ALPHA = (2 * DEPTH) ** 0.25

LANES = 128
SUBLANES = 8
BF16_ROWS = 16
VMEM_LIMIT_BYTES = 56 * 1024 * 1024

NEG_BIG = -1e30
KEY_NEG_INF = -2139095041
KEY_POS_INF = 2139095040

NT_DIMS = (((1,), (1,)), ((), ()))


def _layer_norm(x, g, b):
    mu = jnp.mean(x, axis=-1, keepdims=True)
    xc = x - mu
    var = jnp.mean(xc * xc, axis=-1, keepdims=True)
    return xc * lax.rsqrt(var + LN_EPS) * g + b


def _params(sem):
    return pltpu.CompilerParams(dimension_semantics=sem, vmem_limit_bytes=VMEM_LIMIT_BYTES)


def _proj_kernel(x_ref, g_ref, b_ref, wa_ref, wb_ref, wt_ref,
                 q_ref, qi_ref, u_ref, k_ref, v_ref, ki_ref, kb_ref, kib_ref, vt_ref, wit_ref, *, lc):
    h = _layer_norm(x_ref[...], g_ref[...], b_ref[...])
    hb = h.astype(BF16)
    aw = N_HEADS * HEAD_DIM
    pa = jnp.dot(hb, wa_ref[...], preferred_element_type=F32)
    q_ref[...] = (pa[:, :aw] * (HEAD_DIM ** -0.5)).astype(BF16)
    qi_ref[...] = pa[:, aw:2 * aw].astype(BF16)
    u_ref[...] = pa[:, 2 * aw:]
    pb = jnp.dot(hb, wb_ref[...], preferred_element_type=F32)
    k = pb[:, 0:HEAD_DIM]
    v = pb[:, HEAD_DIM:2 * HEAD_DIM]
    ki = pb[:, 2 * HEAD_DIM:2 * HEAD_DIM + IDX_DIM]
    k_ref[...] = k
    v_ref[...] = v
    ki_ref[...] = ki
    kb_ref[...] = k.astype(BF16)
    kib_ref[...] = ki.astype(BF16)
    pt = lax.dot_general(wt_ref[...], hb, NT_DIMS, preferred_element_type=F32)
    for c in range(vt_ref.shape[0]):
        vt_ref[c] = pt[0:HEAD_DIM, c * lc:(c + 1) * lc].astype(BF16)
    wi = pt[HEAD_DIM:HEAD_DIM + IDX_HEADS, :]
    wit_ref[...] = (wi * (IDX_HEADS ** -0.5)) * (IDX_DIM ** -0.5)


def _proj(x2d, g, b, wa, wb, wt, *, tm, lc):
    n, d = x2d.shape
    tm = min(tm, n)
    aw = N_HEADS * HEAD_DIM
    uw = wa.shape[1] - 2 * aw
    row = lambda i: (i, 0)
    const = lambda i: (0, 0)
    out_shape = (
        jax.ShapeDtypeStruct((n, aw), BF16),
        jax.ShapeDtypeStruct((n, aw), BF16),
        jax.ShapeDtypeStruct((n, uw), F32),
        jax.ShapeDtypeStruct((n, HEAD_DIM), F32),
        jax.ShapeDtypeStruct((n, HEAD_DIM), F32),
        jax.ShapeDtypeStruct((n, IDX_DIM), F32),
        jax.ShapeDtypeStruct((n, HEAD_DIM), BF16),
        jax.ShapeDtypeStruct((n, IDX_DIM), BF16),
        jax.ShapeDtypeStruct((n // lc, HEAD_DIM, lc), BF16),
        jax.ShapeDtypeStruct((IDX_HEADS, n), F32),
    )
    out_specs = (
        pl.BlockSpec((tm, aw), row), pl.BlockSpec((tm, aw), row), pl.BlockSpec((tm, uw), row),
        pl.BlockSpec((tm, HEAD_DIM), row), pl.BlockSpec((tm, HEAD_DIM), row), pl.BlockSpec((tm, IDX_DIM), row),
        pl.BlockSpec((tm, HEAD_DIM), row), pl.BlockSpec((tm, IDX_DIM), row),
        pl.BlockSpec((tm // lc, HEAD_DIM, lc), lambda i: (i, 0, 0)),
        pl.BlockSpec((IDX_HEADS, tm), lambda i: (0, i)),
    )
    return pl.pallas_call(
        functools.partial(_proj_kernel, lc=lc),
        grid=(n // tm,),
        in_specs=[pl.BlockSpec((tm, d), row), pl.BlockSpec((1, d), const), pl.BlockSpec((1, d), const),
                  pl.BlockSpec(wa.shape, const), pl.BlockSpec(wb.shape, const), pl.BlockSpec(wt.shape, const)],
        out_specs=out_specs,
        out_shape=out_shape,
        compiler_params=_params(("parallel",)),
        name="proj",
    )(x2d, g, b, wa, wb, wt)


def _key_to_float(key):
    bits = jnp.where(key >= 0, key, key ^ jnp.int32(0x7FFFFFFF))
    return lax.bitcast_convert_type(bits, F32)


def _dsa_kernel(q_ref, qi_ref, wit_ref, kb_ref, kib_ref, vt_ref, o_ref,
                sc_ref, sch_ref, s_ref, out_ref, *acc_refs, tq, lc, nk_static, causal, l_valid, q_pos0, topk):
    qb = pl.program_id(1)
    nk = qb * (tq // lc) + (tq // lc) if causal else nk_static
    q_chunk = (q_pos0 + qb * tq + lax.broadcasted_iota(I32, (lc, tq), 1)) >> CHUNK_SHIFT
    row_iota = lax.broadcasted_iota(I32, (lc, tq), 0)

    def chunk_loop(body, init):
        if causal:
            return lax.fori_loop(0, nk, body, init)
        return lax.fori_loop(0, nk_static, body, init, unroll=True)

    def score_chunk(kc, carry):
        off = pl.multiple_of(kc * lc, lc)
        kic = kib_ref[0, pl.ds(off, lc), :]
        acc = jnp.zeros((lc, tq), F32)
        for h in range(IDX_HEADS):
            s = lax.dot_general(kic, qi_ref[0, :, h * IDX_DIM:(h + 1) * IDX_DIM], NT_DIMS,
                                preferred_element_type=F32)
            acc = acc + wit_ref[h:h + 1, :] * jnp.maximum(s, 0.0)
        l_pos = off + row_iota
        visible = ((l_pos >> CHUNK_SHIFT) <= q_chunk) & (l_pos < l_valid)
        score = jnp.where(visible, acc, -jnp.inf)
        sc_ref[pl.ds(off, lc), :] = score
        sch_ref[pl.ds(off, lc), :] = score.astype(BF16)
        return carry

    chunk_loop(score_chunk, 0)

    def count(pred):
        ways = 4 * SUBLANES

        def body(kc, part):
            off = pl.multiple_of(kc * lc, lc)
            hit = jnp.where(pred(sc_ref[pl.ds(off, lc), :]), 1.0, 0.0)
            return part + jnp.sum(hit.reshape(lc // ways, ways, tq), axis=0)
        return jnp.sum(chunk_loop(body, jnp.zeros((ways, tq), F32)), axis=0, keepdims=True)

    def count_coarse(cand):
        ways = 2 * BF16_ROWS
        assert sc_ref.shape[0] // ways <= 256

        def body(kc, part):
            off = pl.multiple_of(kc * lc, lc)
            hit = jnp.where(sch_ref[pl.ds(off, lc), :] >= cand, jnp.ones((), BF16), jnp.zeros((), BF16))
            hit = hit.reshape(lc // ways, ways, tq)
            terms = [hit[j] for j in range(lc // ways)]
            while len(terms) > 1:
                terms = [a + b for a, b in zip(terms[::2], terms[1::2])]
            return part + terms[0]
        part = chunk_loop(body, jnp.zeros((ways, tq), BF16))
        return jnp.sum(part.astype(F32), axis=0, keepdims=True)

    def bisect(count_ge, key_to_value, lo, hi, steps):
        def step(_, carry):
            lo, hi = carry
            mid = (lo >> 1) + (hi >> 1) + (lo & hi & 1)
            ok = count_ge(key_to_value(mid)) >= topk
            return jnp.where(ok, mid, lo), jnp.where(ok, hi, mid)
        return lax.fori_loop(0, steps, step, (lo, hi))[0]

    def coarse_key_to_f32_key(k16):
        return jnp.where(k16 >= 0, k16 << 16, (k16 << 16) | 0xFFFF)

    full = lambda v: jnp.full((1, tq), v, I32)
    k16 = bisect(count_coarse, lambda k: _key_to_float(coarse_key_to_f32_key(k)).astype(BF16),
                 full(KEY_NEG_INF >> 16), full((KEY_POS_INF >> 16) + 1), 16)
    kb = coarse_key_to_f32_key(k16)
    half = 1 << 15
    lo = bisect(lambda cand: count(lambda blk: blk >= cand), _key_to_float,
                jnp.maximum(kb, KEY_NEG_INF + half + 1) - (half + 1),
                jnp.minimum(kb, KEY_POS_INF - half) + (half + 1), 17)
    thr = _key_to_float(lo)
    n_above = count(lambda blk: blk > thr)
    n_ties = topk - n_above

    tri = jnp.where(lax.broadcasted_iota(I32, (lc, lc), 0) >= lax.broadcasted_iota(I32, (lc, lc), 1),
                    1.0, 0.0).astype(BF16)

    def fold8(x):
        return x.reshape(lc // SUBLANES, SUBLANES, tq)

    def logits_chunk(kc, carry):
        ties_before, m8 = carry
        off = pl.multiple_of(kc * lc, lc)
        blk = sc_ref[pl.ds(off, lc), :]
        tie = blk == thr
        tie_rank = jnp.dot(tri, jnp.where(tie, 1.0, 0.0).astype(BF16), preferred_element_type=F32) + ties_before
        bias = jnp.where(blk > thr, 0.0, jnp.where(tie, jnp.where(tie_rank <= n_ties, 0.0, NEG_BIG), NEG_BIG))
        bias = jnp.where(blk == -jnp.inf, NEG_BIG, bias)
        kc_b = kb_ref[0, pl.ds(off, lc), :]
        m_rows = []
        for h in range(N_HEADS):
            s = lax.dot_general(kc_b, q_ref[0, :, h * HEAD_DIM:(h + 1) * HEAD_DIM], NT_DIMS,
                                preferred_element_type=F32) + bias
            s_ref[h, pl.ds(off, lc), :] = s
            m_rows.append(jnp.maximum(m8[h * SUBLANES:(h + 1) * SUBLANES], jnp.max(fold8(s), axis=0)))
        return tie_rank[lc - 1:lc, :], jnp.concatenate(m_rows, axis=0)

    _, m8 = chunk_loop(logits_chunk, (jnp.zeros((1, tq), F32), jnp.full((N_HEADS * SUBLANES, tq), NEG_BIG, F32)))
    m_all = jnp.max(m8.reshape(N_HEADS, SUBLANES, tq), axis=1)

    for acc_ref in acc_refs:
        acc_ref[...] = jnp.zeros(acc_ref.shape, F32)

    def pv_chunk(kc, l8):
        off = pl.multiple_of(kc * lc, lc)
        vt_c = vt_ref[kc]
        l_rows = []
        for h in range(N_HEADS):
            p = jnp.exp(s_ref[h, pl.ds(off, lc), :] - m_all[h:h + 1, :])
            l_rows.append(l8[h * SUBLANES:(h + 1) * SUBLANES] + jnp.sum(fold8(p), axis=0))
            acc_refs[h][...] += jnp.dot(vt_c, p.astype(BF16), preferred_element_type=F32)
        return jnp.concatenate(l_rows, axis=0)

    l8 = chunk_loop(pv_chunk, jnp.zeros((N_HEADS * SUBLANES, tq), F32))
    l_all = jnp.sum(l8.reshape(N_HEADS, SUBLANES, tq), axis=1)

    for h in range(N_HEADS):
        out_ref[h * HEAD_DIM:(h + 1) * HEAD_DIM, :] = acc_refs[h][...] / l_all[h:h + 1, :]
    o_ref[0] = out_ref[...].T.astype(BF16)


def _dsa(q, qi, wit, kb, kib, vt, *, tq, lc, causal, l_valid, q_pos0, topk):
    bsz, tq_tot, aw = q.shape
    l_tot = kb.shape[1]
    nq = tq_tot // tq
    nkc = l_tot // lc
    kern = functools.partial(_dsa_kernel, tq=tq, lc=lc, nk_static=nkc, causal=causal, l_valid=l_valid,
                             q_pos0=q_pos0, topk=topk)
    return pl.pallas_call(
        kern,
        grid=(bsz, nq),
        in_specs=[
            pl.BlockSpec((1, tq, aw), lambda b, i: (b, i, 0)),
            pl.BlockSpec((1, tq, aw), lambda b, i: (b, i, 0)),
            pl.BlockSpec((IDX_HEADS, tq), lambda b, i: (0, b * nq + i)),
            pl.BlockSpec((1, l_tot, HEAD_DIM), lambda b, i: (b, 0, 0)),
            pl.BlockSpec((1, l_tot, IDX_DIM), lambda b, i: (b, 0, 0)),
            pl.BlockSpec((nkc, HEAD_DIM, lc), lambda b, i: (b, 0, 0)),
        ],
        out_specs=pl.BlockSpec((1, tq, aw), lambda b, i: (b, i, 0)),
        out_shape=jax.ShapeDtypeStruct((bsz, tq_tot, aw), BF16),
        scratch_shapes=[
            pltpu.VMEM((l_tot, tq), F32),
            pltpu.VMEM((l_tot, tq), BF16),
            pltpu.VMEM((N_HEADS, l_tot, tq), F32),
            pltpu.VMEM((aw, tq), F32),
        ] + [pltpu.VMEM((HEAD_DIM, tq), F32) for _ in range(N_HEADS)],
        compiler_params=_params(("parallel", "parallel")),
        name="dsa",
    )(q, qi, wit, kb, kib, vt)


def _pool_kernel(u_ref, pre_ref, pw_ref, sc_ref, o_ref, ext_ref, *, t_len, tt, pos0):
    pad = pre_ref.shape[1]
    gw = pw_ref.shape[1]
    ext_ref[0:pad, :] = pre_ref[0]
    ext_ref[pad:pad + t_len, :] = u_ref[0]
    row = lax.broadcasted_iota(I32, (tt, gw), 0)
    for t in range(t_len // tt):
        r0 = t * tt
        for g, w in enumerate(POOL_WINDOWS):
            cols = slice(g * gw, (g + 1) * gw)
            cur = ext_ref[pad + r0:pad + r0 + tt, cols]
            wsum = cur
            for j in range(1, w):
                wsum = wsum + ext_ref[pad + r0 - j:pad + r0 - j + tt, cols]
            cnt = jnp.minimum(w, pos0 + r0 + 1 + row).astype(F32)
            diff = (wsum / cnt - cur).astype(BF16)
            y = jnp.dot(diff, pw_ref[g], preferred_element_type=F32)
            o_ref[0, r0:r0 + tt, cols] = (y * sc_ref[:, cols]).astype(BF16)


def _pool(u, prefix, pool_w_b, pool_scale, *, pos0):
    bsz, t_len, c = u.shape
    pad = prefix.shape[1]
    tt = min(t_len, 256)
    return pl.pallas_call(
        functools.partial(_pool_kernel, t_len=t_len, tt=tt, pos0=pos0),
        grid=(bsz,),
        in_specs=[pl.BlockSpec((1, t_len, c), lambda b: (b, 0, 0)),
                  pl.BlockSpec((1, pad, c), lambda b: (b, 0, 0)),
                  pl.BlockSpec(pool_w_b.shape, lambda b: (0, 0, 0)),
                  pl.BlockSpec((1, c), lambda b: (0, 0))],
        out_specs=pl.BlockSpec((1, t_len, c), lambda b: (b, 0, 0)),
        out_shape=jax.ShapeDtypeStruct((bsz, t_len, c), BF16),
        scratch_shapes=[pltpu.VMEM((pad + t_len, c), F32)],
        compiler_params=_params(("parallel",)),
        name="pool",
    )(u, prefix, pool_w_b, pool_scale)


ROW_ALIGN = 16


def _tail_kernel(x_ref, att_ref, pool_ref, g0_ref, b0_ref, woa_ref, wob_ref, g1_ref, b1_ref,
                 wrh_ref, wrl_ref, br_ref,
                 h1_ref, cpos_ref, gw_ref, seg_ref, len_ref, dst_ref, used_ref, carry_ref, *, tm):
    @pl.when(pl.program_id(0) == 0)
    def _():
        carry_ref[...] = jnp.zeros(carry_ref.shape, F32)

    h = _layer_norm(x_ref[...], g0_ref[...], b0_ref[...])
    mix = (jnp.dot(att_ref[...], woa_ref[...], preferred_element_type=F32)
           + jnp.dot(pool_ref[...], wob_ref[...], preferred_element_type=F32))
    h1 = _layer_norm(DEEPNORM_ALPHA * h + mix, g1_ref[...], b1_ref[...])
    h1_ref[...] = h1

    hh = h1.astype(BF16)
    hl = (h1 - hh.astype(F32)).astype(BF16)
    logits = (lax.dot_general(wrh_ref[...], hh, NT_DIMS, preferred_element_type=F32)
              + lax.dot_general(wrh_ref[...], hl, NT_DIMS, preferred_element_type=F32)
              + lax.dot_general(wrl_ref[...], hh, NT_DIMS, preferred_element_type=F32)
              + br_ref[...])

    e_iota = lax.broadcasted_iota(I32, (N_EXPERTS, tm), 0).astype(F32)
    work = logits
    vals, hots = [], []
    for k in range(TOP_K):
        m = jnp.max(work, axis=0, keepdims=True)
        idx = jnp.min(jnp.where(work == m, e_iota, float(N_EXPERTS)), axis=0, keepdims=True)
        hot = e_iota == idx
        vals.append(m)
        hots.append(jnp.where(hot, 1.0, 0.0))
        work = jnp.where(hot, -jnp.inf, work)
    exps = [jnp.exp(v - vals[0]) for v in vals]
    den = exps[0] + exps[1] + exps[2] + exps[3]
    for k in range(TOP_K):
        gw_ref[k:k + 1, :] = exps[k] / den

    hot_all = (hots[0] + hots[1] + hots[2] + hots[3]).astype(BF16)
    count = jnp.dot(hot_all, jnp.ones((tm, LANES), BF16), preferred_element_type=F32)
    chunk_len = jnp.ceil(count * (1.0 / ROW_ALIGN)) * ROW_ALIGN
    lower = jnp.where(lax.broadcasted_iota(I32, (N_EXPERTS, N_EXPERTS), 1)
                      < lax.broadcasted_iota(I32, (N_EXPERTS, N_EXPERTS), 0), 1.0, 0.0).astype(BF16)
    seg_base = jnp.dot(lower, chunk_len.astype(BF16), preferred_element_type=F32)
    before = jnp.where(lax.broadcasted_iota(I32, (tm, tm), 0) < lax.broadcasted_iota(I32, (tm, tm), 1),
                       1.0, 0.0).astype(BF16)
    slot = jnp.dot(hot_all, before, preferred_element_type=F32) + jnp.tile(seg_base, (1, tm // LANES))
    for k in range(TOP_K):
        cpos_ref[k:k + 1, :] = jnp.sum(hots[k] * slot, axis=0, keepdims=True).astype(I32)
    carry = carry_ref[...]
    seg_ref[0] = seg_base
    len_ref[0] = chunk_len
    dst_ref[0] = carry
    carry_ref[...] = carry + chunk_len
    used_ref[...] = carry + chunk_len


def _tail(x2d, att, pool, g0, b0, woa, wob, g1, b1, wrh, wrl, br, *, tm):
    n, d = x2d.shape
    aw = att.shape[1]
    nt = n // tm
    row = lambda i: (i, 0)
    const = lambda i: (0, 0)
    col = lambda i: (0, i)
    tab = lambda i: (i, 0, 0)
    tab_shape = jax.ShapeDtypeStruct((nt, N_EXPERTS, LANES), F32)
    tab_spec = pl.BlockSpec((1, N_EXPERTS, LANES), tab)
    return pl.pallas_call(
        functools.partial(_tail_kernel, tm=tm),
        grid=(nt,),
        in_specs=[pl.BlockSpec((tm, d), row), pl.BlockSpec((tm, aw), row), pl.BlockSpec((tm, d - aw), row),
                  pl.BlockSpec((1, d), const), pl.BlockSpec((1, d), const),
                  pl.BlockSpec(woa.shape, const), pl.BlockSpec(wob.shape, const),
                  pl.BlockSpec((1, d), const), pl.BlockSpec((1, d), const),
                  pl.BlockSpec(wrh.shape, const), pl.BlockSpec(wrl.shape, const), pl.BlockSpec(br.shape, const)],
        out_specs=(pl.BlockSpec((tm, d), row), pl.BlockSpec((TOP_K, tm), col), pl.BlockSpec((TOP_K, tm), col),
                   tab_spec, tab_spec, tab_spec, pl.BlockSpec((N_EXPERTS, LANES), const)),
        out_shape=(jax.ShapeDtypeStruct((n, d), F32), jax.ShapeDtypeStruct((TOP_K, n), I32),
                   jax.ShapeDtypeStruct((TOP_K, n), F32), tab_shape, tab_shape, tab_shape,
                   jax.ShapeDtypeStruct((N_EXPERTS, LANES), F32)),
        scratch_shapes=[pltpu.VMEM((N_EXPERTS, LANES), F32)],
        compiler_params=_params(("arbitrary",)),
        name="tail",
    )(x2d, att, pool, g0, b0, woa, wob, g1, b1, wrh, wrl, br)


G_BLOCK = 256


def _stage_rows(tm):
    raw = TOP_K * tm + (ROW_ALIGN - 1) * N_EXPERTS
    return -(-raw // G_BLOCK) * G_BLOCK


def _chunk_copies(seg_ref, len_ref, dst_ref, tile, max_len, make_copy, act):
    def per_expert(e, carry):
        idx = tile * N_EXPERTS + e
        seg, ln, dst = seg_ref[idx], len_ref[idx], dst_ref[idx]
        size = ROW_ALIGN
        while size <= max_len:
            done = ln & (-2 * size)

            @pl.when((ln & size) != 0)
            def _(size=size, done=done):
                act(make_copy(pl.multiple_of(seg + done, ROW_ALIGN), pl.multiple_of(dst + done, ROW_ALIGN), size))
            size *= 2
        return carry

    lax.fori_loop(0, N_EXPERTS, per_expert, 0)


def _by_staged_blocks(seg_ref, len_ref, tile, n_full, body):
    last = tile * N_EXPERTS + N_EXPERTS - 1
    total = seg_ref[last] + len_ref[last]
    pl.when(total <= (n_full - 1) * G_BLOCK)(lambda: body(n_full - 1))
    pl.when(total > (n_full - 1) * G_BLOCK)(lambda: body(n_full))


def _route_onehot(iota, pos_list, val_list):
    g = jnp.zeros(iota.shape, F32)
    for pos, val in zip(pos_list, val_list):
        g = jnp.where(iota == pos, val, g)
    return g.astype(BF16)


def _dispatch_kernel(seg_ref, len_ref, dst_ref, h1_ref, cpos_ref, xs_hbm, stage_ref, sem, *, tm):
    tile = pl.program_id(0)
    h1b = h1_ref[...].astype(BF16)
    pos_rows = [cpos_ref[k:k + 1, :] for k in range(TOP_K)]

    def stage_blocks(n_blk):
        for blk in range(n_blk):
            iota = blk * G_BLOCK + lax.broadcasted_iota(I32, (G_BLOCK, tm), 0)
            g = _route_onehot(iota, pos_rows, [1.0] * TOP_K)
            stage_ref[blk * G_BLOCK:(blk + 1) * G_BLOCK, :] = jnp.dot(
                g, h1b, preferred_element_type=F32).astype(BF16)

    _by_staged_blocks(seg_ref, len_ref, tile, stage_ref.shape[0] // G_BLOCK, stage_blocks)

    def make_copy(seg, dst, size):
        return pltpu.make_async_copy(stage_ref.at[pl.ds(seg, size), :], xs_hbm.at[pl.ds(dst, size), :], sem)

    _chunk_copies(seg_ref, len_ref, dst_ref, tile, tm, make_copy, lambda c: c.start())
    _chunk_copies(seg_ref, len_ref, dst_ref, tile, tm, make_copy, lambda c: c.wait())


def _dispatch(seg, ln, dst, h1, cpos, *, tm, p_rows):
    n, d = h1.shape
    return pl.pallas_call(
        functools.partial(_dispatch_kernel, tm=tm),
        grid_spec=pltpu.PrefetchScalarGridSpec(
            num_scalar_prefetch=3,
            grid=(n // tm,),
            in_specs=[pl.BlockSpec((tm, d), lambda i, *_: (i, 0)), pl.BlockSpec((TOP_K, tm), lambda i, *_: (0, i))],
            out_specs=pl.BlockSpec(memory_space=pl.ANY),
            scratch_shapes=[pltpu.VMEM((_stage_rows(tm), d), BF16), pltpu.SemaphoreType.DMA(())],
        ),
        out_shape=jax.ShapeDtypeStruct((p_rows, d), BF16),
        compiler_params=_params(("arbitrary",)),
        name="dispatch",
    )(seg, ln, dst, h1, cpos)


FF_CHUNK = 256


def _moe_kernel(te_ref, tv_ref, x_ref, w1_ref, b1_ref, w2_ref, b2_ref, y_ref, w1b_ref, w2b_ref, act_ref):
    i = pl.program_id(0)
    d_ff = w2_ref.shape[1]
    valid = tv_ref[i]

    @pl.when((i == 0) | (te_ref[i] != te_ref[jnp.maximum(i - 1, 0)]))
    def _():
        w1b_ref[...] = w1_ref[0].astype(BF16)
        w2b_ref[...] = w2_ref[0].astype(BF16)

    @pl.when(valid > 0)
    def _():
        rows = lax.broadcasted_iota(I32, x_ref.shape, 0)
        xb = jnp.where(rows < valid, x_ref[...].astype(F32), 0.0).astype(BF16)
        for j in range(d_ff // FF_CHUNK):
            gs = slice(j * FF_CHUNK, (j + 1) * FF_CHUNK)
            us = slice(d_ff + j * FF_CHUNK, d_ff + (j + 1) * FF_CHUNK)
            gate = jnp.dot(xb, w1b_ref[:, gs], preferred_element_type=F32) + b1_ref[0, :, gs]
            up = jnp.dot(xb, w1b_ref[:, us], preferred_element_type=F32) + b1_ref[0, :, us]
            gate = jnp.minimum(gate, SWIGLU_LIMIT)
            up = jnp.clip(up, -SWIGLU_LIMIT, SWIGLU_LIMIT)
            act_ref[:, gs] = ((up + 1.0) * (gate * jax.nn.sigmoid(SWIGLU_ALPHA * gate))).astype(BF16)
        y = jnp.dot(act_ref[...], w2b_ref[...], preferred_element_type=F32) + b2_ref[0]
        y_ref[...] = y.astype(BF16)

    @pl.when(valid <= 0)
    def _():
        y_ref[...] = jnp.zeros(y_ref.shape, BF16)


def _moe(tile_expert, tile_valid, xs, w1, b1, w2, b2, *, tmoe):
    p, d = xs.shape
    e, _, ff2 = w1.shape
    d_ff = ff2 // 2
    return pl.pallas_call(
        _moe_kernel,
        grid_spec=pltpu.PrefetchScalarGridSpec(
            num_scalar_prefetch=2,
            grid=(p // tmoe,),
            in_specs=[pl.BlockSpec((tmoe, d), lambda i, te, tv: (i, 0)),
                      pl.BlockSpec((1, d, ff2), lambda i, te, tv: (te[i], 0, 0)),
                      pl.BlockSpec((1, 1, ff2), lambda i, te, tv: (te[i], 0, 0)),
                      pl.BlockSpec((1, d_ff, d), lambda i, te, tv: (te[i], 0, 0)),
                      pl.BlockSpec((1, 1, d), lambda i, te, tv: (te[i], 0, 0))],
            out_specs=pl.BlockSpec((tmoe, d), lambda i, te, tv: (i, 0)),
            scratch_shapes=[pltpu.VMEM((d, ff2), BF16), pltpu.VMEM((d_ff, d), BF16), pltpu.VMEM((tmoe, d_ff), BF16)],
        ),
        out_shape=jax.ShapeDtypeStruct((p, d), BF16),
        compiler_params=_params(("arbitrary",)),
        name="moe",
    )(tile_expert, tile_valid, xs, w1, b1.reshape(e, 1, ff2), w2, b2.reshape(e, 1, d))


def _combine_kernel(seg_ref, len_ref, dst_ref, h1_ref, cpos_ref, gw_ref, g2_ref, b2_ref, ys_hbm, o_ref,
                    stage_ref, gate_ref, sem, *, tm):
    tile = pl.program_id(0)

    @pl.when(tile == 0)
    def _():
        stage_ref[...] = jnp.zeros(stage_ref.shape, BF16)

    def make_copy(seg, dst, size):
        return pltpu.make_async_copy(ys_hbm.at[pl.ds(dst, size), :], stage_ref.at[pl.ds(seg, size), :], sem)

    _chunk_copies(seg_ref, len_ref, dst_ref, tile, tm, make_copy, lambda c: c.start())
    cpos = cpos_ref[...]
    gw = gw_ref[...]
    pos_cols = [cpos[:, k:k + 1] for k in range(TOP_K)]
    gw_cols = [gw[:, k:k + 1] for k in range(TOP_K)]

    def finish(n_blk):
        for blk in range(n_blk):
            iota = blk * G_BLOCK + lax.broadcasted_iota(I32, (tm, G_BLOCK), 1)
            gate_ref[:, blk * G_BLOCK:(blk + 1) * G_BLOCK] = _route_onehot(iota, pos_cols, gw_cols)
        _chunk_copies(seg_ref, len_ref, dst_ref, tile, tm, make_copy, lambda c: c.wait())
        rows = n_blk * G_BLOCK
        m = jnp.dot(gate_ref[:, :rows], stage_ref[:rows, :], preferred_element_type=F32)
        o_ref[...] = _layer_norm(DEEPNORM_ALPHA * h1_ref[...] + m, g2_ref[...], b2_ref[...])

    _by_staged_blocks(seg_ref, len_ref, tile, stage_ref.shape[0] // G_BLOCK, finish)


def _combine(seg, ln, dst, h1, cpos_nt, gw_nt, g2, b2, ys, *, tm):
    n, d = h1.shape
    return pl.pallas_call(
        functools.partial(_combine_kernel, tm=tm),
        grid_spec=pltpu.PrefetchScalarGridSpec(
            num_scalar_prefetch=3,
            grid=(n // tm,),
            in_specs=[pl.BlockSpec((tm, d), lambda i, *_: (i, 0)),
                      pl.BlockSpec((tm, TOP_K), lambda i, *_: (i, 0)), pl.BlockSpec((tm, TOP_K), lambda i, *_: (i, 0)),
                      pl.BlockSpec((1, d), lambda i, *_: (0, 0)), pl.BlockSpec((1, d), lambda i, *_: (0, 0)),
                      pl.BlockSpec(memory_space=pl.ANY)],
            out_specs=pl.BlockSpec((tm, d), lambda i, *_: (i, 0)),
            scratch_shapes=[pltpu.VMEM((_stage_rows(tm), d), BF16), pltpu.VMEM((tm, _stage_rows(tm)), BF16),
                            pltpu.SemaphoreType.DMA(())],
        ),
        out_shape=jax.ShapeDtypeStruct((n, d), F32),
        compiler_params=_params(("arbitrary",)),
        name="combine",
    )(seg, ln, dst, h1, cpos_nt, gw_nt, g2, b2, ys)


def _block_tail(x2d, att, pool, wts, *, tm, tmoe):
    n, d = x2d.shape
    tm = min(tm, n)
    nt = n // tm
    h1, cpos, gw_t, seg_t, len_t, dst_t, used = _tail(x2d, att, pool, wts["g0"], wts["b0"], wts["woa"], wts["wob"],
                                                      wts["g1"], wts["b1"], wts["wrh"], wts["wrl"], wts["br"], tm=tm)
    used = used[:, 0].astype(I32)
    cap = ((used + tmoe - 1) // tmoe) * tmoe
    ends = jnp.cumsum(cap)
    offs = ends - cap
    n_tiles = -(-(TOP_K * n + (ROW_ALIGN - 1) * N_EXPERTS * nt) // tmoe) + N_EXPERTS
    tile_start = jnp.arange(n_tiles, dtype=I32) * tmoe
    tile_expert = jnp.minimum(jnp.sum((ends[None, :] <= tile_start[:, None]).astype(I32), axis=1), N_EXPERTS - 1)
    tile_valid = jnp.clip(offs[tile_expert] + used[tile_expert] - tile_start, 0, tmoe).astype(I32)
    tile_valid = jnp.where(tile_start < ends[-1], tile_valid, 0)
    seg = seg_t[:, :, 0].astype(I32).reshape(-1)
    ln = len_t[:, :, 0].astype(I32).reshape(-1)
    dst = (dst_t[:, :, 0].astype(I32) + offs[None, :]).reshape(-1)

    xs = _dispatch(seg, ln, dst, h1, cpos, tm=tm, p_rows=n_tiles * tmoe)
    ys = _moe(tile_expert, tile_valid, xs, wts["w1e"], wts["b1e"], wts["w2e"], wts["b2e"], tmoe=tmoe)
    return _combine(seg, ln, dst, h1, cpos.T, gw_t.T, wts["g2"], wts["b2"], ys, tm=tm)


def kernel(x_prompt, x_sample, cache_k, cache_v, cache_kidx, state_pool, ln0_g, ln0_b, w_in, w_o,
           pool_w, pool_scale, ln1_g, ln1_b, w_router, b_router, w1, b1, w2, b2, ln2_g, ln2_b):
    bp, s_len, d = x_prompt.shape
    bs, t_len, _ = x_sample.shape
    l_past = cache_k.shape[2]
    aw = N_HEADS * HEAD_DIM
    pw = d - aw
    lyr = 0
    lc = 256

    k_off = aw
    v_off = k_off + HEAD_DIM
    qi_off = v_off + HEAD_DIM
    ki_off = qi_off + IDX_HEADS * IDX_DIM
    wi_off = ki_off + IDX_DIM
    u_off = wi_off + IDX_HEADS
    win = w_in[lyr]
    wa = jnp.concatenate([win[:, 0:k_off], win[:, qi_off:ki_off], win[:, u_off:u_off + pw]], axis=1).astype(BF16)
    wb = jnp.concatenate([win[:, k_off:v_off], win[:, v_off:qi_off], win[:, ki_off:wi_off],
                          jnp.zeros((d, 64), F32)], axis=1).astype(BF16)
    wt = jnp.concatenate([win[:, v_off:qi_off].T, win[:, wi_off:u_off].T, jnp.zeros((8, d), F32)], axis=0).astype(BF16)

    g0 = ln0_g.reshape(1, d)
    b0 = ln0_b.reshape(1, d)
    wrt = w_router[lyr].T
    wrh = wrt.astype(BF16)
    wts = dict(
        g0=g0, b0=b0,
        woa=w_o[lyr][:aw].astype(BF16), wob=w_o[lyr][aw:].astype(BF16),
        g1=ln1_g[lyr].reshape(1, d), b1=ln1_b[lyr].reshape(1, d),
        wrh=wrh, wrl=(wrt - wrh.astype(F32)).astype(BF16), br=b_router[lyr].reshape(N_EXPERTS, 1),
        w1e=w1[lyr], b1e=b1[lyr], w2e=w2[lyr], b2e=b2[lyr],
        g2=ln2_g[lyr].reshape(1, d), b2=ln2_b[lyr].reshape(1, d),
    )
    pool_w_b = pool_w[lyr].astype(BF16)
    pool_sc = pool_scale[lyr].reshape(1, pw)

    xp = x_prompt.reshape(bp * s_len, d)
    q, qi, u, k, v, ki, kb, kib, vt, wit = _proj(xp, g0, b0, wa, wb, wt, tm=512, lc=lc)
    att_p = _dsa(q.reshape(bp, s_len, aw), qi.reshape(bp, s_len, aw), wit,
                 kb.reshape(bp, s_len, HEAD_DIM), kib.reshape(bp, s_len, IDX_DIM), vt,
                 tq=256, lc=lc, causal=True, l_valid=s_len, q_pos0=0, topk=min(TOPK_MAX, s_len // 4))
    u_p = u.reshape(bp, s_len, pw)
    pool_p = _pool(u_p, jnp.zeros((bp, POOL_PAST + 1, pw), F32), pool_w_b, pool_sc, pos0=0)
    y_p = _block_tail(xp, att_p.reshape(bp * s_len, aw), pool_p.reshape(bp * s_len, pw), wts,
                      tm=256, tmoe=1024)

    xs = x_sample.reshape(bs * t_len, d)
    qs, qis, us, kn, vn, kin, _, _, _, wits = _proj(xs, g0, b0, wa, wb, wt, tm=512, lc=lc)
    l_all = l_past + t_len
    l_pad = -(-l_all // lc) * lc
    tq_s = LANES
    pad_keys = lambda a: jnp.pad(a, ((0, 0), (0, l_pad - l_all), (0, 0)))
    k_all = pad_keys(jnp.concatenate([cache_k[lyr], kn.reshape(bs, t_len, HEAD_DIM)], axis=1))
    v_all = pad_keys(jnp.concatenate([cache_v[lyr], vn.reshape(bs, t_len, HEAD_DIM)], axis=1))
    ki_all = pad_keys(jnp.concatenate([cache_kidx[lyr], kin.reshape(bs, t_len, IDX_DIM)], axis=1))
    vt_all = v_all.reshape(bs, l_pad // lc, lc, HEAD_DIM).transpose(0, 1, 3, 2).reshape(-1, HEAD_DIM, lc)
    pad_q = lambda a: jnp.pad(a.reshape(bs, t_len, aw), ((0, 0), (0, tq_s - t_len), (0, 0)))
    wit_s = jnp.pad(wits.reshape(IDX_HEADS, bs, t_len), ((0, 0), (0, 0), (0, tq_s - t_len))).reshape(IDX_HEADS, -1)
    att_s = _dsa(pad_q(qs), pad_q(qis), wit_s, k_all.astype(BF16), ki_all.astype(BF16), vt_all.astype(BF16),
                 tq=tq_s, lc=lc, causal=False, l_valid=l_all, q_pos0=l_past, topk=min(TOPK_MAX, l_all // 4))
    att_s = att_s[:, :t_len].reshape(bs * t_len, aw)
    us3 = us.reshape(bs, t_len, pw)
    prefix_s = jnp.concatenate([jnp.zeros((bs, 1, pw), F32), state_pool[lyr]], axis=1)
    pool_s = _pool(us3, prefix_s, pool_w_b, pool_sc, pos0=l_past)
    y_s = _block_tail(xs, att_s, pool_s.reshape(bs * t_len, pw), wts, tm=256, tmoe=128)

    pool_state_p = u_p[:, s_len - POOL_PAST:]
    pool_state_s = jnp.concatenate([state_pool[lyr], us3], axis=1)[:, -POOL_PAST:]
    return (y_p.reshape(bp, s_len, d), y_s.reshape(bs, t_len, d),
            k.reshape(1, bp, s_len, HEAD_DIM), v.reshape(1, bp, s_len, HEAD_DIM),
            ki.reshape(1, bp, s_len, IDX_DIM), pool_state_p[None],
            kn.reshape(1, bs, t_len, HEAD_DIM), vn.reshape(1, bs, t_len, HEAD_DIM),
            kin.reshape(1, bs, t_len, IDX_DIM), pool_state_s[None])
```

```python
import functools

import jax
import jax.numpy as jnp
from jax import lax
from jax.experimental import pallas as pl
from jax.experimental.pallas import tpu as pltpu

F32 = jnp.float32
BF16 = jnp.bfloat16
I32 = jnp.int32

CHUNK = 64
CHUNK_SHIFT = 6
assert 1 << CHUNK_SHIFT == CHUNK
N_HEADS = 8
HEAD_DIM = 64
IDX_HEADS = 8
IDX_DIM = 64
TOPK_MAX = 256
POOL_WINDOWS = (2, 4, 8, 16)
POOL_PAST = 15
N_EXPERTS = 32
TOP_K = 4
SWIGLU_LIMIT = 7.0
SWIGLU_ALPHA = 1.702
LN_EPS = 1e-5
DEPTH = 1
DEEPNORM_ALPHA = (2 * DEPTH) ** 0.25

LANES = 128
SUBLANES = 8
BF16_ROWS = 16
VMEM_LIMIT_BYTES = 56 * 1024 * 1024

NEG_BIG = -1e30
KEY_NEG_INF = -2139095041
KEY_POS_INF = 2139095040

NT_DIMS = (((1,), (1,)), ((), ()))


def _layer_norm(x, g, b):
    mu = jnp.mean(x, axis=-1, keepdims=True)
    xc = x - mu
    var = jnp.mean(xc * xc, axis=-1, keepdims=True)
    return xc * lax.rsqrt(var + LN_EPS) * g + b


def _params(sem):
    return pltpu.CompilerParams(dimension_semantics=sem, vmem_limit_bytes=VMEM_LIMIT_BYTES)


def _proj_kernel(x_ref, g_ref, b_ref, wa_ref, wb_ref, wt_ref,
                 q_ref, qi_ref, u_ref, k_ref, v_ref, ki_ref, kb_ref, kib_ref, vt_ref, wit_ref, *, lc):
    h = _layer_norm(x_ref[...], g_ref[...], b_ref[...])
    hb = h.astype(BF16)
    aw = N_HEADS * HEAD_DIM
    pa = jnp.dot(hb, wa_ref[...], preferred_element_type=F32)
    q_ref[...] = (pa[:, :aw] * (HEAD_DIM ** -0.5)).astype(BF16)
    qi_ref[...] = pa[:, aw:2 * aw].astype(BF16)
    u_ref[...] = pa[:, 2 * aw:]
    pb = jnp.dot(hb, wb_ref[...], preferred_element_type=F32)
    k = pb[:, 0:HEAD_DIM]
    v = pb[:, HEAD_DIM:2 * HEAD_DIM]
    ki = pb[:, 2 * HEAD_DIM:2 * HEAD_DIM + IDX_DIM]
    k_ref[...] = k
    v_ref[...] = v
    ki_ref[...] = ki
    kb_ref[...] = k.astype(BF16)
    kib_ref[...] = ki.astype(BF16)
    pt = lax.dot_general(wt_ref[...], hb, NT_DIMS, preferred_element_type=F32)
    for c in range(vt_ref.shape[0]):
        vt_ref[c] = pt[0:HEAD_DIM, c * lc:(c + 1) * lc].astype(BF16)
    wi = pt[HEAD_DIM:HEAD_DIM + IDX_HEADS, :]
    wit_ref[...] = (wi * (IDX_HEADS ** -0.5)) * (IDX_DIM ** -0.5)


def _proj(x2d, g, b, wa, wb, wt, *, tm, lc):
    n, d = x2d.shape
    tm = min(tm, n)
    aw = N_HEADS * HEAD_DIM
    uw = wa.shape[1] - 2 * aw
    row = lambda i: (i, 0)
    const = lambda i: (0, 0)
    out_shape = (
        jax.ShapeDtypeStruct((n, aw), BF16),
        jax.ShapeDtypeStruct((n, aw), BF16),
        jax.ShapeDtypeStruct((n, uw), F32),
        jax.ShapeDtypeStruct((n, HEAD_DIM), F32),
        jax.ShapeDtypeStruct((n, HEAD_DIM), F32),
        jax.ShapeDtypeStruct((n, IDX_DIM), F32),
        jax.ShapeDtypeStruct((n, HEAD_DIM), BF16),
        jax.ShapeDtypeStruct((n, IDX_DIM), BF16),
        jax.ShapeDtypeStruct((n // lc, HEAD_DIM, lc), BF16),
        jax.ShapeDtypeStruct((IDX_HEADS, n), F32),
    )
    out_specs = (
        pl.BlockSpec((tm, aw), row), pl.BlockSpec((tm, aw), row), pl.BlockSpec((tm, uw), row),
        pl.BlockSpec((tm, HEAD_DIM), row), pl.BlockSpec((tm, HEAD_DIM), row), pl.BlockSpec((tm, IDX_DIM), row),
        pl.BlockSpec((tm, HEAD_DIM), row), pl.BlockSpec((tm, IDX_DIM), row),
        pl.BlockSpec((tm // lc, HEAD_DIM, lc), lambda i: (i, 0, 0)),
        pl.BlockSpec((IDX_HEADS, tm), lambda i: (0, i)),
    )
    return pl.pallas_call(
        functools.partial(_proj_kernel, lc=lc),
        grid=(n // tm,),
        in_specs=[pl.BlockSpec((tm, d), row), pl.BlockSpec((1, d), const), pl.BlockSpec((1, d), const),
                  pl.BlockSpec(wa.shape, const), pl.BlockSpec(wb.shape, const), pl.BlockSpec(wt.shape, const)],
        out_specs=out_specs,
        out_shape=out_shape,
        compiler_params=_params(("parallel",)),
        name="proj",
    )(x2d, g, b, wa, wb, wt)


def _key_to_float(key):
    bits = jnp.where(key >= 0, key, key ^ jnp.int32(0x7FFFFFFF))
    return lax.bitcast_convert_type(bits, F32)


def _dsa_kernel(q_ref, qi_ref, wit_ref, kb_ref, kib_ref, vt_ref, o_ref,
                sc_ref, sch_ref, s_ref, out_ref, *acc_refs, tq, lc, nk_static, causal, l_valid, q_pos0, topk):
    qb = pl.program_id(1)
    nk = qb * (tq // lc) + (tq // lc) if causal else nk_static
    q_chunk = (q_pos0 + qb * tq + lax.broadcasted_iota(I32, (lc, tq), 1)) >> CHUNK_SHIFT
    row_iota = lax.broadcasted_iota(I32, (lc, tq), 0)

    def chunk_loop(body, init):
        if causal:
            return lax.fori_loop(0, nk, body, init)
        return lax.fori_loop(0, nk_static, body, init, unroll=True)

    def score_chunk(kc, carry):
        off = pl.multiple_of(kc * lc, lc)
        kic = kib_ref[0, pl.ds(off, lc), :]
        acc = jnp.zeros((lc, tq), F32)
        for h in range(IDX_HEADS):
            s = lax.dot_general(kic, qi_ref[0, :, h * IDX_DIM:(h + 1) * IDX_DIM], NT_DIMS,
                                preferred_element_type=F32)
            acc = acc + wit_ref[h:h + 1, :] * jnp.maximum(s, 0.0)
        l_pos = off + row_iota
        visible = ((l_pos >> CHUNK_SHIFT) <= q_chunk) & (l_pos < l_valid)
        score = jnp.where(visible, acc, -jnp.inf)
        sc_ref[pl.ds(off, lc), :] = score
        sch_ref[pl.ds(off, lc), :] = score.astype(BF16)
        return carry

    chunk_loop(score_chunk, 0)

    def count(pred):
        ways = 4 * SUBLANES

        def body(kc, part):
            off = pl.multiple_of(kc * lc, lc)
            hit = jnp.where(pred(sc_ref[pl.ds(off, lc), :]), 1.0, 0.0)
            return part + jnp.sum(hit.reshape(lc // ways, ways, tq), axis=0)
        return jnp.sum(chunk_loop(body, jnp.zeros((ways, tq), F32)), axis=0, keepdims=True)

    def count_coarse(cand):
        ways = 2 * BF16_ROWS
        assert sc_ref.shape[0] // ways <= 256

        def body(kc, part):
            off = pl.multiple_of(kc * lc, lc)
            hit = jnp.where(sch_ref[pl.ds(off, lc), :] >= cand, jnp.ones((), BF16), jnp.zeros((), BF16))
            hit = hit.reshape(lc // ways, ways, tq)
            terms = [hit[j] for j in range(lc // ways)]
            while len(terms) > 1:
                terms = [a + b for a, b in zip(terms[::2], terms[1::2])]
            return part + terms[0]
        part = chunk_loop(body, jnp.zeros((ways, tq), BF16))
        return jnp.sum(part.astype(F32), axis=0, keepdims=True)

    def bisect(count_ge, key_to_value, lo, hi, steps):
        def step(_, carry):
            lo, hi = carry
            mid = (lo >> 1) + (hi >> 1) + (lo & hi & 1)
            ok = count_ge(key_to_value(mid)) >= topk
            return jnp.where(ok, mid, lo), jnp.where(ok, hi, mid)
        return lax.fori_loop(0, steps, step, (lo, hi))[0]

    def coarse_key_to_f32_key(k16):
        return jnp.where(k16 >= 0, k16 << 16, (k16 << 16) | 0xFFFF)

    full = lambda v: jnp.full((1, tq), v, I32)
    k16 = bisect(count_coarse, lambda k: _key_to_float(coarse_key_to_f32_key(k)).astype(BF16),
                 full(KEY_NEG_INF >> 16), full((KEY_POS_INF >> 16) + 1), 16)
    kb = coarse_key_to_f32_key(k16)
    half = 1 << 15
    lo = bisect(lambda cand: count(lambda blk: blk >= cand), _key_to_float,
                jnp.maximum(kb, KEY_NEG_INF + half + 1) - (half + 1),
                jnp.minimum(kb, KEY_POS_INF - half) + (half + 1), 17)
    thr = _key_to_float(lo)
    n_above = count(lambda blk: blk > thr)
    n_ties = topk - n_above

    tri = jnp.where(lax.broadcasted_iota(I32, (lc, lc), 0) >= lax.broadcasted_iota(I32, (lc, lc), 1),
                    1.0, 0.0).astype(BF16)

    def fold8(x):
        return x.reshape(lc // SUBLANES, SUBLANES, tq)

    def logits_chunk(kc, carry):
        ties_before, m8 = carry
        off = pl.multiple_of(kc * lc, lc)
        blk = sc_ref[pl.ds(off, lc), :]
        tie = blk == thr
        tie_rank = jnp.dot(tri, jnp.where(tie, 1.0, 0.0).astype(BF16), preferred_element_type=F32) + ties_before
        bias = jnp.where(blk > thr, 0.0, jnp.where(tie, jnp.where(tie_rank <= n_ties, 0.0, NEG_BIG), NEG_BIG))
        bias = jnp.where(blk == -jnp.inf, NEG_BIG, bias)
        kc_b = kb_ref[0, pl.ds(off, lc), :]
        m_rows = []
        for h in range(N_HEADS):
            s = lax.dot_general(kc_b, q_ref[0, :, h * HEAD_DIM:(h + 1) * HEAD_DIM], NT_DIMS,
                                preferred_element_type=F32) + bias
            s_ref[h, pl.ds(off, lc), :] = s
            m_rows.append(jnp.maximum(m8[h * SUBLANES:(h + 1) * SUBLANES], jnp.max(fold8(s), axis=0)))
        return tie_rank[lc - 1:lc, :], jnp.concatenate(m_rows, axis=0)

    _, m8 = chunk_loop(logits_chunk, (jnp.zeros((1, tq), F32), jnp.full((N_HEADS * SUBLANES, tq), NEG_BIG, F32)))
    m_all = jnp.max(m8.reshape(N_HEADS, SUBLANES, tq), axis=1)

    for acc_ref in acc_refs:
        acc_ref[...] = jnp.zeros(acc_ref.shape, F32)

    def pv_chunk(kc, l8):
        off = pl.multiple_of(kc * lc, lc)
        vt_c = vt_ref[kc]
        l_rows = []
        for h in range(N_HEADS):
            p = jnp.exp(s_ref[h, pl.ds(off, lc), :] - m_all[h:h + 1, :])
            l_rows.append(l8[h * SUBLANES:(h + 1) * SUBLANES] + jnp.sum(fold8(p), axis=0))
            acc_refs[h][...] += jnp.dot(vt_c, p.astype(BF16), preferred_element_type=F32)
        return jnp.concatenate(l_rows, axis=0)

    l8 = chunk_loop(pv_chunk, jnp.zeros((N_HEADS * SUBLANES, tq), F32))
    l_all = jnp.sum(l8.reshape(N_HEADS, SUBLANES, tq), axis=1)

    for h in range(N_HEADS):
        out_ref[h * HEAD_DIM:(h + 1) * HEAD_DIM, :] = acc_refs[h][...] / l_all[h:h + 1, :]
    o_ref[0] = out_ref[...].T.astype(BF16)


def _dsa(q, qi, wit, kb, kib, vt, *, tq, lc, causal, l_valid, q_pos0, topk):
    bsz, tq_tot, aw = q.shape
    l_tot = kb.shape[1]
    nq = tq_tot // tq
    nkc = l_tot // lc
    kern = functools.partial(_dsa_kernel, tq=tq, lc=lc, nk_static=nkc, causal=causal, l_valid=l_valid,
                             q_pos0=q_pos0, topk=topk)
    return pl.pallas_call(
        kern,
        grid=(bsz, nq),
        in_specs=[
            pl.BlockSpec((1, tq, aw), lambda b, i: (b, i, 0)),
            pl.BlockSpec((1, tq, aw), lambda b, i: (b, i, 0)),
            pl.BlockSpec((IDX_HEADS, tq), lambda b, i: (0, b * nq + i)),
            pl.BlockSpec((1, l_tot, HEAD_DIM), lambda b, i: (b, 0, 0)),
            pl.BlockSpec((1, l_tot, IDX_DIM), lambda b, i: (b, 0, 0)),
            pl.BlockSpec((nkc, HEAD_DIM, lc), lambda b, i: (b, 0, 0)),
        ],
        out_specs=pl.BlockSpec((1, tq, aw), lambda b, i: (b, i, 0)),
        out_shape=jax.ShapeDtypeStruct((bsz, tq_tot, aw), BF16),
        scratch_shapes=[
            pltpu.VMEM((l_tot, tq), F32),
            pltpu.VMEM((l_tot, tq), BF16),
            pltpu.VMEM((N_HEADS, l_tot, tq), F32),
            pltpu.VMEM((aw, tq), F32),
        ] + [pltpu.VMEM((HEAD_DIM, tq), F32) for _ in range(N_HEADS)],
        compiler_params=_params(("parallel", "parallel")),
        name="dsa",
    )(q, qi, wit, kb, kib, vt)


def _pool_kernel(u_ref, pre_ref, pw_ref, sc_ref, o_ref, ext_ref, *, t_len, tt, pos0):
    pad = pre_ref.shape[1]
    gw = pw_ref.shape[1]
    ext_ref[0:pad, :] = pre_ref[0]
    ext_ref[pad:pad + t_len, :] = u_ref[0]
    row = lax.broadcasted_iota(I32, (tt, gw), 0)
    for t in range(t_len // tt):
        r0 = t * tt
        for g, w in enumerate(POOL_WINDOWS):
            cols = slice(g * gw, (g + 1) * gw)
            cur = ext_ref[pad + r0:pad + r0 + tt, cols]
            wsum = cur
            for j in range(1, w):
                wsum = wsum + ext_ref[pad + r0 - j:pad + r0 - j + tt, cols]
            cnt = jnp.minimum(w, pos0 + r0 + 1 + row).astype(F32)
            diff = (wsum / cnt - cur).astype(BF16)
            y = jnp.dot(diff, pw_ref[g], preferred_element_type=F32)
            o_ref[0, r0:r0 + tt, cols] = (y * sc_ref[:, cols]).astype(BF16)


def _pool(u, prefix, pool_w_b, pool_scale, *, pos0):
    bsz, t_len, c = u.shape
    pad = prefix.shape[1]
    tt = min(t_len, 256)
    return pl.pallas_call(
        functools.partial(_pool_kernel, t_len=t_len, tt=tt, pos0=pos0),
        grid=(bsz,),
        in_specs=[pl.BlockSpec((1, t_len, c), lambda b: (b, 0, 0)),
                  pl.BlockSpec((1, pad, c), lambda b: (b, 0, 0)),
                  pl.BlockSpec(pool_w_b.shape, lambda b: (0, 0, 0)),
                  pl.BlockSpec((1, c), lambda b: (0, 0))],
        out_specs=pl.BlockSpec((1, t_len, c), lambda b: (b, 0, 0)),
        out_shape=jax.ShapeDtypeStruct((bsz, t_len, c), BF16),
        scratch_shapes=[pltpu.VMEM((pad + t_len, c), F32)],
        compiler_params=_params(("parallel",)),
        name="pool",
    )(u, prefix, pool_w_b, pool_scale)


ROW_ALIGN = 16


def _tail_kernel(x_ref, att_ref, pool_ref, g0_ref, b0_ref, woa_ref, wob_ref, g1_ref, b1_ref,
                 wrh_ref, wrl_ref, br_ref,
                 h1_ref, cpos_ref, gw_ref, seg_ref, len_ref, dst_ref, used_ref, carry_ref, *, tm):
    @pl.when(pl.program_id(0) == 0)
    def _():
        carry_ref[...] = jnp.zeros(carry_ref.shape, F32)

    h = _layer_norm(x_ref[...], g0_ref[...], b0_ref[...])
    mix = (jnp.dot(att_ref[...], woa_ref[...], preferred_element_type=F32)
           + jnp.dot(pool_ref[...], wob_ref[...], preferred_element_type=F32))
    h1 = _layer_norm(DEEPNORM_ALPHA * h + mix, g1_ref[...], b1_ref[...])
    h1_ref[...] = h1

    hh = h1.astype(BF16)
    hl = (h1 - hh.astype(F32)).astype(BF16)
    logits = (lax.dot_general(wrh_ref[...], hh, NT_DIMS, preferred_element_type=F32)
              + lax.dot_general(wrh_ref[...], hl, NT_DIMS, preferred_element_type=F32)
              + lax.dot_general(wrl_ref[...], hh, NT_DIMS, preferred_element_type=F32)
              + br_ref[...])

    e_iota = lax.broadcasted_iota(I32, (N_EXPERTS, tm), 0).astype(F32)
    work = logits
    vals, hots = [], []
    for k in range(TOP_K):
        m = jnp.max(work, axis=0, keepdims=True)
        idx = jnp.min(jnp.where(work == m, e_iota, float(N_EXPERTS)), axis=0, keepdims=True)
        hot = e_iota == idx
        vals.append(m)
        hots.append(jnp.where(hot, 1.0, 0.0))
        work = jnp.where(hot, -jnp.inf, work)
    exps = [jnp.exp(v - vals[0]) for v in vals]
    den = exps[0] + exps[1] + exps[2] + exps[3]
    for k in range(TOP_K):
        gw_ref[k:k + 1, :] = exps[k] / den

    hot_all = (hots[0] + hots[1] + hots[2] + hots[3]).astype(BF16)
    count = jnp.dot(hot_all, jnp.ones((tm, LANES), BF16), preferred_element_type=F32)
    chunk_len = jnp.ceil(count * (1.0 / ROW_ALIGN)) * ROW_ALIGN
    lower = jnp.where(lax.broadcasted_iota(I32, (N_EXPERTS, N_EXPERTS), 1)
                      < lax.broadcasted_iota(I32, (N_EXPERTS, N_EXPERTS), 0), 1.0, 0.0).astype(BF16)
    seg_base = jnp.dot(lower, chunk_len.astype(BF16), preferred_element_type=F32)
    before = jnp.where(lax.broadcasted_iota(I32, (tm, tm), 0) < lax.broadcasted_iota(I32, (tm, tm), 1),
                       1.0, 0.0).astype(BF16)
    slot = jnp.dot(hot_all, before, preferred_element_type=F32) + jnp.tile(seg_base, (1, tm // LANES))
    for k in range(TOP_K):
        cpos_ref[k:k + 1, :] = jnp.sum(hots[k] * slot, axis=0, keepdims=True).astype(I32)
    carry = carry_ref[...]
    seg_ref[0] = seg_base
    len_ref[0] = chunk_len
    dst_ref[0] = carry
    carry_ref[...] = carry + chunk_len
    used_ref[...] = carry + chunk_len


def _tail(x2d, att, pool, g0, b0, woa, wob, g1, b1, wrh, wrl, br, *, tm):
    n, d = x2d.shape
    aw = att.shape[1]
    nt = n // tm
    row = lambda i: (i, 0)
    const = lambda i: (0, 0)
    col = lambda i: (0, i)
    tab = lambda i: (i, 0, 0)
    tab_shape = jax.ShapeDtypeStruct((nt, N_EXPERTS, LANES), F32)
    tab_spec = pl.BlockSpec((1, N_EXPERTS, LANES), tab)
    return pl.pallas_call(
        functools.partial(_tail_kernel, tm=tm),
        grid=(nt,),
        in_specs=[pl.BlockSpec((tm, d), row), pl.BlockSpec((tm, aw), row), pl.BlockSpec((tm, d - aw), row),
                  pl.BlockSpec((1, d), const), pl.BlockSpec((1, d), const),
                  pl.BlockSpec(woa.shape, const), pl.BlockSpec(wob.shape, const),
                  pl.BlockSpec((1, d), const), pl.BlockSpec((1, d), const),
                  pl.BlockSpec(wrh.shape, const), pl.BlockSpec(wrl.shape, const), pl.BlockSpec(br.shape, const)],
        out_specs=(pl.BlockSpec((tm, d), row), pl.BlockSpec((TOP_K, tm), col), pl.BlockSpec((TOP_K, tm), col),
                   tab_spec, tab_spec, tab_spec, pl.BlockSpec((N_EXPERTS, LANES), const)),
        out_shape=(jax.ShapeDtypeStruct((n, d), F32), jax.ShapeDtypeStruct((TOP_K, n), I32),
                   jax.ShapeDtypeStruct((TOP_K, n), F32), tab_shape, tab_shape, tab_shape,
                   jax.ShapeDtypeStruct((N_EXPERTS, LANES), F32)),
        scratch_shapes=[pltpu.VMEM((N_EXPERTS, LANES), F32)],
        compiler_params=_params(("arbitrary",)),
        name="tail",
    )(x2d, att, pool, g0, b0, woa, wob, g1, b1, wrh, wrl, br)


G_BLOCK = 256


def _stage_rows(tm):
    raw = TOP_K * tm + (ROW_ALIGN - 1) * N_EXPERTS
    return -(-raw // G_BLOCK) * G_BLOCK


def _chunk_copies(seg_ref, len_ref, dst_ref, tile, make_copy, act):
    def per_expert(e, carry):
        idx = tile * N_EXPERTS + e
        seg, ln, dst = seg_ref[idx], len_ref[idx], dst_ref[idx]

        @pl.when(ln > 0)
        def _():
            act(make_copy(pl.multiple_of(seg, ROW_ALIGN), pl.multiple_of(dst, ROW_ALIGN),
                          pl.multiple_of(ln, ROW_ALIGN)))
        return carry

    lax.fori_loop(0, N_EXPERTS, per_expert, 0)


def _by_staged_blocks(seg_ref, len_ref, tile, n_full, body):
    last = tile * N_EXPERTS + N_EXPERTS - 1
    total = seg_ref[last] + len_ref[last]
    pl.when(total <= (n_full - 1) * G_BLOCK)(lambda: body(n_full - 1))
    pl.when(total > (n_full - 1) * G_BLOCK)(lambda: body(n_full))


def _route_onehot(iota, pos_list, val_list):
    g = jnp.zeros(iota.shape, F32)
    for pos, val in zip(pos_list, val_list):
        g = jnp.where(iota == pos, val, g)
    return g.astype(BF16)


def _dispatch_kernel(seg_ref, len_ref, dst_ref, h1_ref, cpos_ref, xs_hbm, stage_ref, sem, *, tm):
    tile = pl.program_id(0)
    slot = tile % 2
    h1b = h1_ref[...].astype(BF16)
    pos_rows = [cpos_ref[k:k + 1, :] for k in range(TOP_K)]

    def stage_blocks(n_blk):
        for blk in range(n_blk):
            iota = blk * G_BLOCK + lax.broadcasted_iota(I32, (G_BLOCK, tm), 0)
            g = _route_onehot(iota, pos_rows, [1.0] * TOP_K)
            stage_ref[slot, blk * G_BLOCK:(blk + 1) * G_BLOCK, :] = jnp.dot(
                g, h1b, preferred_element_type=F32).astype(BF16)

    _by_staged_blocks(seg_ref, len_ref, tile, stage_ref.shape[1] // G_BLOCK, stage_blocks)

    def copy_from(buf):
        def make_copy(seg, dst, size):
            return pltpu.make_async_copy(stage_ref.at[buf, pl.ds(seg, size), :], xs_hbm.at[pl.ds(dst, size), :],
                                         sem.at[buf])
        return make_copy

    _chunk_copies(seg_ref, len_ref, dst_ref, tile, copy_from(slot), lambda c: c.start())

    @pl.when(tile > 0)
    def _():
        _chunk_copies(seg_ref, len_ref, dst_ref, tile - 1, copy_from(1 - slot), lambda c: c.wait())

    @pl.when(tile == pl.num_programs(0) - 1)
    def _():
        _chunk_copies(seg_ref, len_ref, dst_ref, tile, copy_from(slot), lambda c: c.wait())


def _dispatch(seg, ln, dst, h1, cpos, *, tm, p_rows):
    n, d = h1.shape
    return pl.pallas_call(
        functools.partial(_dispatch_kernel, tm=tm),
        grid_spec=pltpu.PrefetchScalarGridSpec(
            num_scalar_prefetch=3,
            grid=(n // tm,),
            in_specs=[pl.BlockSpec((tm, d), lambda i, *_: (i, 0)), pl.BlockSpec((TOP_K, tm), lambda i, *_: (0, i))],
            out_specs=pl.BlockSpec(memory_space=pl.ANY),
            scratch_shapes=[pltpu.VMEM((2, _stage_rows(tm), d), BF16), pltpu.SemaphoreType.DMA((2,))],
        ),
        out_shape=jax.ShapeDtypeStruct((p_rows, d), BF16),
        compiler_params=_params(("arbitrary",)),
        name="dispatch",
    )(seg, ln, dst, h1, cpos)


FF_CHUNK = 256


def _moe_kernel(te_ref, tv_ref, x_ref, w1_ref, b1_ref, w2_ref, b2_ref, y_ref, w1b_ref, w2b_ref, act_ref):
    i = pl.program_id(0)
    d_ff = w2_ref.shape[1]
    valid = tv_ref[i]

    @pl.when((i == 0) | (te_ref[i] != te_ref[jnp.maximum(i - 1, 0)]))
    def _():
        w1b_ref[...] = w1_ref[0].astype(BF16)
        w2b_ref[...] = w2_ref[0].astype(BF16)

    @pl.when(valid > 0)
    def _():
        rows = lax.broadcasted_iota(I32, x_ref.shape, 0)
        xb = jnp.where(rows < valid, x_ref[...].astype(F32), 0.0).astype(BF16)
        for j in range(d_ff // FF_CHUNK):
            gs = slice(j * FF_CHUNK, (j + 1) * FF_CHUNK)
            us = slice(d_ff + j * FF_CHUNK, d_ff + (j + 1) * FF_CHUNK)
            gate = jnp.dot(xb, w1b_ref[:, gs], preferred_element_type=F32) + b1_ref[0, :, gs]
            up = jnp.dot(xb, w1b_ref[:, us], preferred_element_type=F32) + b1_ref[0, :, us]
            gate = jnp.minimum(gate, SWIGLU_LIMIT)
            up = jnp.clip(up, -SWIGLU_LIMIT, SWIGLU_LIMIT)
            act_ref[:, gs] = ((up + 1.0) * (gate * jax.nn.sigmoid(SWIGLU_ALPHA * gate))).astype(BF16)
        y = jnp.dot(act_ref[...], w2b_ref[...], preferred_element_type=F32) + b2_ref[0]
        y_ref[...] = y.astype(BF16)

    @pl.when(valid <= 0)
    def _():
        y_ref[...] = jnp.zeros(y_ref.shape, BF16)


def _moe(tile_expert, tile_valid, xs, w1, b1, w2, b2, *, tmoe):
    p, d = xs.shape
    e, _, ff2 = w1.shape
    d_ff = ff2 // 2
    return pl.pallas_call(
        _moe_kernel,
        grid_spec=pltpu.PrefetchScalarGridSpec(
            num_scalar_prefetch=2,
            grid=(p // tmoe,),
            in_specs=[pl.BlockSpec((tmoe, d), lambda i, te, tv: (i, 0)),
                      pl.BlockSpec((1, d, ff2), lambda i, te, tv: (te[i], 0, 0)),
                      pl.BlockSpec((1, 1, ff2), lambda i, te, tv: (te[i], 0, 0)),
                      pl.BlockSpec((1, d_ff, d), lambda i, te, tv: (te[i], 0, 0)),
                      pl.BlockSpec((1, 1, d), lambda i, te, tv: (te[i], 0, 0))],
            out_specs=pl.BlockSpec((tmoe, d), lambda i, te, tv: (i, 0)),
            scratch_shapes=[pltpu.VMEM((d, ff2), BF16), pltpu.VMEM((d_ff, d), BF16), pltpu.VMEM((tmoe, d_ff), BF16)],
        ),
        out_shape=jax.ShapeDtypeStruct((p, d), BF16),
        compiler_params=_params(("arbitrary",)),
        name="moe",
    )(tile_expert, tile_valid, xs, w1, b1.reshape(e, 1, ff2), w2, b2.reshape(e, 1, d))


def _combine_kernel(seg_ref, len_ref, dst_ref, h1_ref, cpos_ref, gw_ref, g2_ref, b2_ref, ys_hbm, o_ref,
                    stage_ref, gate_ref, sem, *, tm):
    tile = pl.program_id(0)
    slot = tile % 2

    def copy_into(buf):
        def make_copy(seg, dst, size):
            return pltpu.make_async_copy(ys_hbm.at[pl.ds(dst, size), :], stage_ref.at[buf, pl.ds(seg, size), :],
                                         sem.at[buf])
        return make_copy

    @pl.when(tile == 0)
    def _():
        stage_ref[...] = jnp.zeros(stage_ref.shape, BF16)
        _chunk_copies(seg_ref, len_ref, dst_ref, tile, copy_into(slot), lambda c: c.start())

    @pl.when(tile + 1 < pl.num_programs(0))
    def _():
        _chunk_copies(seg_ref, len_ref, dst_ref, tile + 1, copy_into(1 - slot), lambda c: c.start())

    cpos = cpos_ref[...]
    gw = gw_ref[...]
    pos_cols = [cpos[:, k:k + 1] for k in range(TOP_K)]
    gw_cols = [gw[:, k:k + 1] for k in range(TOP_K)]

    def finish(n_blk):
        for blk in range(n_blk):
            iota = blk * G_BLOCK + lax.broadcasted_iota(I32, (tm, G_BLOCK), 1)
            gate_ref[:, blk * G_BLOCK:(blk + 1) * G_BLOCK] = _route_onehot(iota, pos_cols, gw_cols)
        _chunk_copies(seg_ref, len_ref, dst_ref, tile, copy_into(slot), lambda c: c.wait())
        rows = n_blk * G_BLOCK
        m = jnp.dot(gate_ref[:, :rows], stage_ref[slot, :rows, :], preferred_element_type=F32)
        o_ref[...] = _layer_norm(DEEPNORM_ALPHA * h1_ref[...] + m, g2_ref[...], b2_ref[...])

    _by_staged_blocks(seg_ref, len_ref, tile, stage_ref.shape[1] // G_BLOCK, finish)


def _combine(seg, ln, dst, h1, cpos_nt, gw_nt, g2, b2, ys, *, tm):
    n, d = h1.shape
    return pl.pallas_call(
        functools.partial(_combine_kernel, tm=tm),
        grid_spec=pltpu.PrefetchScalarGridSpec(
            num_scalar_prefetch=3,
            grid=(n // tm,),
            in_specs=[pl.BlockSpec((tm, d), lambda i, *_: (i, 0)),
                      pl.BlockSpec((tm, TOP_K), lambda i, *_: (i, 0)), pl.BlockSpec((tm, TOP_K), lambda i, *_: (i, 0)),
                      pl.BlockSpec((1, d), lambda i, *_: (0, 0)), pl.BlockSpec((1, d), lambda i, *_: (0, 0)),
                      pl.BlockSpec(memory_space=pl.ANY)],
            out_specs=pl.BlockSpec((tm, d), lambda i, *_: (i, 0)),
            scratch_shapes=[pltpu.VMEM((2, _stage_rows(tm), d), BF16), pltpu.VMEM((tm, _stage_rows(tm)), BF16),
                            pltpu.SemaphoreType.DMA((2,))],
        ),
        out_shape=jax.ShapeDtypeStruct((n, d), F32),
        compiler_params=_params(("arbitrary",)),
        name="combine",
    )(seg, ln, dst, h1, cpos_nt, gw_nt, g2, b2, ys)


def _block_tail(x2d, att, pool, wts, *, tm, tmoe):
    n, d = x2d.shape
    tm = min(tm, n)
    nt = n // tm
    h1, cpos, gw_t, seg_t, len_t, dst_t, used = _tail(x2d, att, pool, wts["g0"], wts["b0"], wts["woa"], wts["wob"],
                                                      wts["g1"], wts["b1"], wts["wrh"], wts["wrl"], wts["br"], tm=tm)
    used = used[:, 0].astype(I32)
    cap = ((used + tmoe - 1) // tmoe) * tmoe
    ends = jnp.cumsum(cap)
    offs = ends - cap
    n_tiles = -(-(TOP_K * n + (ROW_ALIGN - 1) * N_EXPERTS * nt) // tmoe) + N_EXPERTS
    tile_start = jnp.arange(n_tiles, dtype=I32) * tmoe
    tile_expert = jnp.minimum(jnp.sum((ends[None, :] <= tile_start[:, None]).astype(I32), axis=1), N_EXPERTS - 1)
    tile_valid = jnp.clip(offs[tile_expert] + used[tile_expert] - tile_start, 0, tmoe).astype(I32)
    tile_valid = jnp.where(tile_start < ends[-1], tile_valid, 0)
    seg = seg_t[:, :, 0].astype(I32).reshape(-1)
    ln = len_t[:, :, 0].astype(I32).reshape(-1)
    dst = (dst_t[:, :, 0].astype(I32) + offs[None, :]).reshape(-1)

    xs = _dispatch(seg, ln, dst, h1, cpos, tm=tm, p_rows=n_tiles * tmoe)
    ys = _moe(tile_expert, tile_valid, xs, wts["w1e"], wts["b1e"], wts["w2e"], wts["b2e"], tmoe=tmoe)
    return _combine(seg, ln, dst, h1, cpos.T, gw_t.T, wts["g2"], wts["b2"], ys, tm=tm)


def kernel(x_prompt, x_sample, cache_k, cache_v, cache_kidx, state_pool, ln0_g, ln0_b, w_in, w_o,
           pool_w, pool_scale, ln1_g, ln1_b, w_router, b_router, w1, b1, w2, b2, ln2_g, ln2_b):
    bp, s_len, d = x_prompt.shape
    bs, t_len, _ = x_sample.shape
    l_past = cache_k.shape[2]
    aw = N_HEADS * HEAD_DIM
    pw = d - aw
    lyr = 0
    lc = 256

    k_off = aw
    v_off = k_off + HEAD_DIM
    qi_off = v_off + HEAD_DIM
    ki_off = qi_off + IDX_HEADS * IDX_DIM
    wi_off = ki_off + IDX_DIM
    u_off = wi_off + IDX_HEADS
    win = w_in[lyr]
    wa = jnp.concatenate([win[:, 0:k_off], win[:, qi_off:ki_off], win[:, u_off:u_off + pw]], axis=1).astype(BF16)
    wb = jnp.concatenate([win[:, k_off:v_off], win[:, v_off:qi_off], win[:, ki_off:wi_off],
                          jnp.zeros((d, 64), F32)], axis=1).astype(BF16)
    wt = jnp.concatenate([win[:, v_off:qi_off].T, win[:, wi_off:u_off].T, jnp.zeros((8, d), F32)], axis=0).astype(BF16)

    g0 = ln0_g.reshape(1, d)
    b0 = ln0_b.reshape(1, d)
    wrt = w_router[lyr].T
    wrh = wrt.astype(BF16)
    wts = dict(
        g0=g0, b0=b0,
        woa=w_o[lyr][:aw].astype(BF16), wob=w_o[lyr][aw:].astype(BF16),
        g1=ln1_g[lyr].reshape(1, d), b1=ln1_b[lyr].reshape(1, d),
        wrh=wrh, wrl=(wrt - wrh.astype(F32)).astype(BF16), br=b_router[lyr].reshape(N_EXPERTS, 1),
        w1e=w1[lyr], b1e=b1[lyr], w2e=w2[lyr], b2e=b2[lyr],
        g2=ln2_g[lyr].reshape(1, d), b2=ln2_b[lyr].reshape(1, d),
    )
    pool_w_b = pool_w[lyr].astype(BF16)
    pool_sc = pool_scale[lyr].reshape(1, pw)

    xp = x_prompt.reshape(bp * s_len, d)
    q, qi, u, k, v, ki, kb, kib, vt, wit = _proj(xp, g0, b0, wa, wb, wt, tm=512, lc=lc)
    att_p = _dsa(q.reshape(bp, s_len, aw), qi.reshape(bp, s_len, aw), wit,
                 kb.reshape(bp, s_len, HEAD_DIM), kib.reshape(bp, s_len, IDX_DIM), vt,
                 tq=256, lc=lc, causal=True, l_valid=s_len, q_pos0=0, topk=min(TOPK_MAX, s_len // 4))
    u_p = u.reshape(bp, s_len, pw)
    pool_p = _pool(u_p, jnp.zeros((bp, POOL_PAST + 1, pw), F32), pool_w_b, pool_sc, pos0=0)
    y_p = _block_tail(xp, att_p.reshape(bp * s_len, aw), pool_p.reshape(bp * s_len, pw), wts,
                      tm=512, tmoe=1024)

    xs = x_sample.reshape(bs * t_len, d)
    qs, qis, us, kn, vn, kin, _, _, _, wits = _proj(xs, g0, b0, wa, wb, wt, tm=512, lc=lc)
    l_all = l_past + t_len
    l_pad = -(-l_all // lc) * lc
    tq_s = LANES
    pad_keys = lambda a: jnp.pad(a, ((0, 0), (0, l_pad - l_all), (0, 0)))
    k_all = pad_keys(jnp.concatenate([cache_k[lyr], kn.reshape(bs, t_len, HEAD_DIM)], axis=1))
    v_all = pad_keys(jnp.concatenate([cache_v[lyr], vn.reshape(bs, t_len, HEAD_DIM)], axis=1))
    ki_all = pad_keys(jnp.concatenate([cache_kidx[lyr], kin.reshape(bs, t_len, IDX_DIM)], axis=1))
    vt_all = v_all.reshape(bs, l_pad // lc, lc, HEAD_DIM).transpose(0, 1, 3, 2).reshape(-1, HEAD_DIM, lc)
    pad_q = lambda a: jnp.pad(a.reshape(bs, t_len, aw), ((0, 0), (0, tq_s - t_len), (0, 0)))
    wit_s = jnp.pad(wits.reshape(IDX_HEADS, bs, t_len), ((0, 0), (0, 0), (0, tq_s - t_len))).reshape(IDX_HEADS, -1)
    att_s = _dsa(pad_q(qs), pad_q(qis), wit_s, k_all.astype(BF16), ki_all.astype(BF16), vt_all.astype(BF16),
                 tq=tq_s, lc=lc, causal=False, l_valid=l_all, q_pos0=l_past, topk=min(TOPK_MAX, l_all // 4))
    att_s = att_s[:, :t_len].reshape(bs * t_len, aw)
    us3 = us.reshape(bs, t_len, pw)
    prefix_s = jnp.concatenate([jnp.zeros((bs, 1, pw), F32), state_pool[lyr]], axis=1)
    pool_s = _pool(us3, prefix_s, pool_w_b, pool_sc, pos0=l_past)
    y_s = _block_tail(xs, att_s, pool_s.reshape(bs * t_len, pw), wts, tm=512, tmoe=128)

    pool_state_p = u_p[:, s_len - POOL_PAST:]
    pool_state_s = jnp.concatenate([state_pool[lyr], us3], axis=1)[:, -POOL_PAST:]
    return (y_p.reshape(bp, s_len, d), y_s.reshape(bs, t_len, d),
            k.reshape(1, bp, s_len, HEAD_DIM), v.reshape(1, bp, s_len, HEAD_DIM),
            ki.reshape(1, bp, s_len, IDX_DIM), pool_state_p[None],
            kn.reshape(1, bs, t_len, HEAD_DIM), vn.reshape(1, bs, t_len, HEAD_DIM),
            kin.reshape(1, bs, t_len, IDX_DIM), pool_state_s[None])
```

```python
import functools

import jax
import jax.numpy as jnp
from jax import lax
from jax.experimental import pallas as pl
from jax.experimental.pallas import tpu as pltpu

F32 = jnp.float32
BF16 = jnp.bfloat16
I32 = jnp.int32

CHUNK = 64
CHUNK_SHIFT = 6
assert 1 << CHUNK_SHIFT == CHUNK
N_HEADS = 8
HEAD_DIM = 64
IDX_HEADS = 8
IDX_DIM = 64
TOPK_MAX = 256
POOL_WINDOWS = (2, 4, 8, 16)
POOL_PAST = 15
N_EXPERTS = 32
TOP_K = 4
SWIGLU_LIMIT = 7.0
SWIGLU_ALPHA = 1.702
LN_EPS = 1e-5
DEPTH = 1
DEEPNORM_ALPHA = (2 * DEPTH) ** 0.25
LOG2_E = 1.4426950408889634

LANES = 128
SUBLANES = 8
BF16_ROWS = 16
VMEM_LIMIT_BYTES = 56 * 1024 * 1024

NEG_BIG = -1e30
KEY_NEG_INF = -2139095041
KEY_POS_INF = 2139095040

NT_DIMS = (((1,), (1,)), ((), ()))


def _layer_norm(x, g, b):
    mu = jnp.mean(x, axis=-1, keepdims=True)
    xc = x - mu
    var = jnp.mean(xc * xc, axis=-1, keepdims=True)
    return xc * lax.rsqrt(var + LN_EPS) * g + b


def _params(sem):
    return pltpu.CompilerParams(dimension_semantics=sem, vmem_limit_bytes=VMEM_LIMIT_BYTES)


VT_ROWS = HEAD_DIM + BF16_ROWS


def _denominator_rows(width):
    return jnp.where(lax.broadcasted_iota(I32, (VT_ROWS - HEAD_DIM, width), 0) == 0, 1.0, 0.0)


def _proj_kernel(x_ref, g_ref, b_ref, wa_ref, wb_ref, wt_ref,
                 q_ref, qi_ref, u_ref, k_ref, v_ref, ki_ref, kb_ref, kib_ref, vt_ref, wit_ref, *, lc):
    h = _layer_norm(x_ref[...], g_ref[...], b_ref[...])
    hb = h.astype(BF16)
    aw = N_HEADS * HEAD_DIM
    pa = jnp.dot(hb, wa_ref[...], preferred_element_type=F32)
    q_ref[...] = (pa[:, :aw] * (HEAD_DIM ** -0.5 * LOG2_E)).astype(BF16)
    qi_ref[...] = pa[:, aw:2 * aw].astype(BF16)
    u_ref[...] = pa[:, 2 * aw:]
    pb = jnp.dot(hb, wb_ref[...], preferred_element_type=F32)
    k = pb[:, 0:HEAD_DIM]
    v = pb[:, HEAD_DIM:2 * HEAD_DIM]
    ki = pb[:, 2 * HEAD_DIM:2 * HEAD_DIM + IDX_DIM]
    k_ref[...] = k
    v_ref[...] = v
    ki_ref[...] = ki
    kb_ref[...] = k.astype(BF16)
    kib_ref[...] = ki.astype(BF16)
    pt = lax.dot_general(wt_ref[...], hb, NT_DIMS, preferred_element_type=F32)
    ones_rows = _denominator_rows(lc)
    for c in range(vt_ref.shape[0]):
        vt_ref[c] = jnp.concatenate([pt[0:HEAD_DIM, c * lc:(c + 1) * lc], ones_rows], axis=0).astype(BF16)
    wi = pt[HEAD_DIM:HEAD_DIM + IDX_HEADS, :]
    wit_ref[...] = (wi * (IDX_HEADS ** -0.5)) * (IDX_DIM ** -0.5)


def _proj(x2d, g, b, wa, wb, wt, *, tm, lc):
    n, d = x2d.shape
    tm = min(tm, n)
    aw = N_HEADS * HEAD_DIM
    uw = wa.shape[1] - 2 * aw
    row = lambda i: (i, 0)
    const = lambda i: (0, 0)
    out_shape = (
        jax.ShapeDtypeStruct((n, aw), BF16),
        jax.ShapeDtypeStruct((n, aw), BF16),
        jax.ShapeDtypeStruct((n, uw), F32),
        jax.ShapeDtypeStruct((n, HEAD_DIM), F32),
        jax.ShapeDtypeStruct((n, HEAD_DIM), F32),
        jax.ShapeDtypeStruct((n, IDX_DIM), F32),
        jax.ShapeDtypeStruct((n, HEAD_DIM), BF16),
        jax.ShapeDtypeStruct((n, IDX_DIM), BF16),
        jax.ShapeDtypeStruct((n // lc, VT_ROWS, lc), BF16),
        jax.ShapeDtypeStruct((IDX_HEADS, n), F32),
    )
    out_specs = (
        pl.BlockSpec((tm, aw), row), pl.BlockSpec((tm, aw), row), pl.BlockSpec((tm, uw), row),
        pl.BlockSpec((tm, HEAD_DIM), row), pl.BlockSpec((tm, HEAD_DIM), row), pl.BlockSpec((tm, IDX_DIM), row),
        pl.BlockSpec((tm, HEAD_DIM), row), pl.BlockSpec((tm, IDX_DIM), row),
        pl.BlockSpec((tm // lc, VT_ROWS, lc), lambda i: (i, 0, 0)),
        pl.BlockSpec((IDX_HEADS, tm), lambda i: (0, i)),
    )
    return pl.pallas_call(
        functools.partial(_proj_kernel, lc=lc),
        grid=(n // tm,),
        in_specs=[pl.BlockSpec((tm, d), row), pl.BlockSpec((1, d), const), pl.BlockSpec((1, d), const),
                  pl.BlockSpec(wa.shape, const), pl.BlockSpec(wb.shape, const), pl.BlockSpec(wt.shape, const)],
        out_specs=out_specs,
        out_shape=out_shape,
        compiler_params=_params(("parallel",)),
        name="proj",
    )(x2d, g, b, wa, wb, wt)


def _key_to_float(key):
    bits = jnp.where(key >= 0, key, key ^ jnp.int32(0x7FFFFFFF))
    return lax.bitcast_convert_type(bits, F32)


def _dsa_kernel(q_ref, qi_ref, wit_ref, kb_ref, kib_ref, vt_ref, o_ref,
                sc_ref, sch_ref, s_ref, out_ref, *acc_refs, tq, lc, nk_static, causal, l_valid, q_pos0, topk):
    qb = pl.program_id(1)
    nk = qb * (tq // lc) + (tq // lc) if causal else nk_static
    q_chunk = (q_pos0 + qb * tq + lax.broadcasted_iota(I32, (1, tq), 1)) >> CHUNK_SHIFT

    def chunk_loop(body, init):
        if causal:
            return lax.fori_loop(0, nk, body, init)
        return lax.fori_loop(0, nk_static, body, init, unroll=True)

    def wide_chunk_loop(body, init):
        pair = lambda i, c: body(pl.multiple_of(i * 2 * lc, 2 * lc), 2 * lc, c)
        if not causal:
            carry = lax.fori_loop(0, nk_static // 2, pair, init, unroll=True)
            return body((nk_static - 1) * lc, lc, carry) if nk_static % 2 else carry
        carry = lax.fori_loop(0, nk // 2, pair, init)
        return lax.cond(nk % 2 == 1, lambda c: body(pl.multiple_of((nk - 1) * lc, lc), lc, c), lambda c: c, carry)

    def score_span(off, rows, carry):
        kic = kib_ref[0, pl.ds(off, rows), :]
        acc = jnp.zeros((rows, tq), F32)
        for h in range(IDX_HEADS):
            s = lax.dot_general(kic, qi_ref[0, :, h * IDX_DIM:(h + 1) * IDX_DIM], NT_DIMS,
                                preferred_element_type=F32)
            acc = acc + wit_ref[h:h + 1, :] * jnp.maximum(s, 0.0)
        l_pos = off + lax.broadcasted_iota(I32, (rows, tq), 0)
        visible = ((l_pos >> CHUNK_SHIFT) <= q_chunk) & (l_pos < l_valid)
        score = jnp.where(visible, acc, -jnp.inf)
        sc_ref[pl.ds(off, rows), :] = score
        sch_ref[pl.ds(off, rows), :] = score.astype(BF16)
        return carry

    wide_chunk_loop(score_span, 0)

    def count(pred):
        ways = 4 * SUBLANES

        def body(kc, part):
            off = pl.multiple_of(kc * lc, lc)
            hit = jnp.where(pred(sc_ref[pl.ds(off, lc), :]), 1.0, 0.0)
            return part + jnp.sum(hit.reshape(lc // ways, ways, tq), axis=0)
        return jnp.sum(chunk_loop(body, jnp.zeros((ways, tq), F32)), axis=0, keepdims=True)

    def count_coarse(cand):
        ways = 2 * BF16_ROWS
        assert sc_ref.shape[0] // ways <= 256

        def body(kc, part):
            off = pl.multiple_of(kc * lc, lc)
            hit = jnp.where(sch_ref[pl.ds(off, lc), :] >= cand, jnp.ones((), BF16), jnp.zeros((), BF16))
            hit = hit.reshape(lc // ways, ways, tq)
            terms = [hit[j] for j in range(lc // ways)]
            while len(terms) > 1:
                terms = [a + b for a, b in zip(terms[::2], terms[1::2])]
            return part + terms[0]
        part = chunk_loop(body, jnp.zeros((ways, tq), BF16))
        return jnp.sum(part.astype(F32), axis=0, keepdims=True)

    def bisect(count_ge, key_to_value, lo, hi, steps):
        def step(_, carry):
            lo, hi = carry
            mid = (lo >> 1) + (hi >> 1) + (lo & hi & 1)
            ok = count_ge(key_to_value(mid)) >= topk
            return jnp.where(ok, mid, lo), jnp.where(ok, hi, mid)
        return lax.fori_loop(0, steps, step, (lo, hi))[0]

    def coarse_key_to_f32_key(k16):
        return jnp.where(k16 >= 0, k16 << 16, (k16 << 16) | 0xFFFF)

    full = lambda v: jnp.full((1, tq), v, I32)
    k16 = bisect(count_coarse, lambda k: _key_to_float(coarse_key_to_f32_key(k)).astype(BF16),
                 full(KEY_NEG_INF >> 16), full((KEY_POS_INF >> 16) + 1), 16)
    kb = coarse_key_to_f32_key(k16)
    half = 1 << 15
    lo = bisect(lambda cand: count(lambda blk: blk >= cand), _key_to_float,
                jnp.maximum(kb, KEY_NEG_INF + half + 1) - (half + 1),
                jnp.minimum(kb, KEY_POS_INF - half) + (half + 1), 17)
    thr = _key_to_float(lo)
    n_above = count(lambda blk: blk > thr)
    n_ties = topk - n_above

    tri = jnp.where(lax.broadcasted_iota(I32, (lc, lc), 0) >= lax.broadcasted_iota(I32, (lc, lc), 1),
                    1.0, 0.0).astype(BF16)

    def logits_span(off, rows, carry):
        ties_before, m8 = carry
        biases = []
        for r0 in range(0, rows, lc):
            blk = sc_ref[pl.ds(off + r0, lc), :]
            tie = blk == thr
            tie_rank = (jnp.dot(tri, jnp.where(tie, 1.0, 0.0).astype(BF16), preferred_element_type=F32)
                        + ties_before)
            bias = jnp.where(blk > thr, 0.0, jnp.where(tie, jnp.where(tie_rank <= n_ties, 0.0, NEG_BIG), NEG_BIG))
            biases.append(jnp.where(blk == -jnp.inf, NEG_BIG, bias))
            ties_before = tie_rank[lc - 1:lc, :]
        bias = jnp.concatenate(biases, axis=0)
        kc_b = kb_ref[0, pl.ds(off, rows), :]
        m_rows = []
        for h in range(N_HEADS):
            s = lax.dot_general(kc_b, q_ref[0, :, h * HEAD_DIM:(h + 1) * HEAD_DIM], NT_DIMS,
                                preferred_element_type=F32) + bias
            s_ref[h, pl.ds(off, rows), :] = s
            s8 = jnp.max(s.reshape(rows // SUBLANES, SUBLANES, tq), axis=0)
            m_rows.append(jnp.maximum(m8[h * SUBLANES:(h + 1) * SUBLANES], s8))
        return ties_before, jnp.concatenate(m_rows, axis=0)

    _, m8 = wide_chunk_loop(logits_span,
                            (jnp.zeros((1, tq), F32), jnp.full((N_HEADS * SUBLANES, tq), NEG_BIG, F32)))
    m_all = jnp.max(m8.reshape(N_HEADS, SUBLANES, tq), axis=1)

    for acc_ref in acc_refs:
        acc_ref[...] = jnp.zeros(acc_ref.shape, F32)

    def pv_chunk(kc, carry):
        off = pl.multiple_of(kc * lc, lc)
        vt_c = vt_ref[kc]
        for h in range(N_HEADS):
            p = jnp.exp2(s_ref[h, pl.ds(off, lc), :] - m_all[h:h + 1, :])
            acc_refs[h][...] += jnp.dot(vt_c, p.astype(BF16), preferred_element_type=F32)
        return carry

    chunk_loop(pv_chunk, 0)

    for h in range(N_HEADS):
        acc = acc_refs[h][...]
        out_ref[h * HEAD_DIM:(h + 1) * HEAD_DIM, :] = acc[0:HEAD_DIM] / acc[HEAD_DIM:HEAD_DIM + 1]
    o_ref[0] = out_ref[...].T.astype(BF16)


def _dsa(q, qi, wit, kb, kib, vt, *, tq, lc, causal, l_valid, q_pos0, topk):
    bsz, tq_tot, aw = q.shape
    l_tot = kb.shape[1]
    nq = tq_tot // tq
    nkc = l_tot // lc
    kern = functools.partial(_dsa_kernel, tq=tq, lc=lc, nk_static=nkc, causal=causal, l_valid=l_valid,
                             q_pos0=q_pos0, topk=topk)
    return pl.pallas_call(
        kern,
        grid=(bsz, nq),
        in_specs=[
            pl.BlockSpec((1, tq, aw), lambda b, i: (b, i, 0)),
            pl.BlockSpec((1, tq, aw), lambda b, i: (b, i, 0)),
            pl.BlockSpec((IDX_HEADS, tq), lambda b, i: (0, b * nq + i)),
            pl.BlockSpec((1, l_tot, HEAD_DIM), lambda b, i: (b, 0, 0)),
            pl.BlockSpec((1, l_tot, IDX_DIM), lambda b, i: (b, 0, 0)),
            pl.BlockSpec((nkc, VT_ROWS, lc), lambda b, i: (b, 0, 0)),
        ],
        out_specs=pl.BlockSpec((1, tq, aw), lambda b, i: (b, i, 0)),
        out_shape=jax.ShapeDtypeStruct((bsz, tq_tot, aw), BF16),
        scratch_shapes=[
            pltpu.VMEM((l_tot, tq), F32),
            pltpu.VMEM((l_tot, tq), BF16),
            pltpu.VMEM((N_HEADS, l_tot, tq), F32),
            pltpu.VMEM((aw, tq), F32),
        ] + [pltpu.VMEM((VT_ROWS, tq), F32) for _ in range(N_HEADS)],
        compiler_params=_params(("parallel", "parallel")),
        name="dsa",
    )(q, qi, wit, kb, kib, vt)


def _pool_kernel(u_ref, pre_ref, pw_ref, sc_ref, o_ref, ext_ref, *, t_len, tt, pos0):
    pad = pre_ref.shape[1]
    gw = pw_ref.shape[1]
    ext_ref[0:pad, :] = pre_ref[0]
    ext_ref[pad:pad + t_len, :] = u_ref[0]
    row = lax.broadcasted_iota(I32, (tt, gw), 0)
    for t in range(t_len // tt):
        r0 = t * tt
        for g, w in enumerate(POOL_WINDOWS):
            cols = slice(g * gw, (g + 1) * gw)
            cur = ext_ref[pad + r0:pad + r0 + tt, cols]
            wsum = cur
            for j in range(1, w):
                wsum = wsum + ext_ref[pad + r0 - j:pad + r0 - j + tt, cols]
            cnt = jnp.minimum(w, pos0 + r0 + 1 + row).astype(F32)
            diff = (wsum / cnt - cur).astype(BF16)
            y = jnp.dot(diff, pw_ref[g], preferred_element_type=F32)
            o_ref[0, r0:r0 + tt, cols] = (y * sc_ref[:, cols]).astype(BF16)


def _pool(u, prefix, pool_w_b, pool_scale, *, pos0):
    bsz, t_len, c = u.shape
    pad = prefix.shape[1]
    tt = min(t_len, 256)
    return pl.pallas_call(
        functools.partial(_pool_kernel, t_len=t_len, tt=tt, pos0=pos0),
        grid=(bsz,),
        in_specs=[pl.BlockSpec((1, t_len, c), lambda b: (b, 0, 0)),
                  pl.BlockSpec((1, pad, c), lambda b: (b, 0, 0)),
                  pl.BlockSpec(pool_w_b.shape, lambda b: (0, 0, 0)),
                  pl.BlockSpec((1, c), lambda b: (0, 0))],
        out_specs=pl.BlockSpec((1, t_len, c), lambda b: (b, 0, 0)),
        out_shape=jax.ShapeDtypeStruct((bsz, t_len, c), BF16),
        scratch_shapes=[pltpu.VMEM((pad + t_len, c), F32)],
        compiler_params=_params(("parallel",)),
        name="pool",
    )(u, prefix, pool_w_b, pool_scale)


ROW_ALIGN = 16


def _tail_kernel(x_ref, att_ref, pool_ref, g0_ref, b0_ref, woa_ref, wob_ref, g1_ref, b1_ref,
                 wrh_ref, wrl_ref, br_ref,
                 carry0_ref, h1_ref, cpos_ref, gw_ref, seg_ref, len_ref, dst_ref, used_ref, carry_ref, *, tm, sub):
    @pl.when(pl.program_id(0) == 0)
    def _():
        carry_ref[...] = carry0_ref[...]

    chunk_lens = []
    for s in range(sub):
        rows = slice(s * tm, (s + 1) * tm)
        chunk_lens.append(_tail_tile(x_ref[rows, :], att_ref[rows, :], pool_ref[rows, :], g0_ref, b0_ref, woa_ref,
                                     wob_ref, g1_ref, b1_ref, wrh_ref, wrl_ref, br_ref,
                                     h1_ref.at[rows, :], cpos_ref.at[:, rows], gw_ref.at[:, rows], seg_ref.at[s],
                                     len_ref.at[s], tm=tm))
    carry = carry_ref[...]
    for s in range(sub):
        dst_ref[s] = carry
        carry = carry + chunk_lens[s]
    carry_ref[...] = carry
    used_ref[...] = carry


def _tail_tile(x, att, pool, g0_ref, b0_ref, woa_ref, wob_ref, g1_ref, b1_ref, wrh_ref, wrl_ref, br_ref,
               h1_ref, cpos_ref, gw_ref, seg_ref, len_ref, *, tm):
    h = _layer_norm(x, g0_ref[...], b0_ref[...])
    mix = (jnp.dot(att, woa_ref[...], preferred_element_type=F32)
           + jnp.dot(pool, wob_ref[...], preferred_element_type=F32))
    h1 = _layer_norm(DEEPNORM_ALPHA * h + mix, g1_ref[...], b1_ref[...])
    h1_ref[...] = h1

    hh = h1.astype(BF16)
    hl = (h1 - hh.astype(F32)).astype(BF16)
    logits = (lax.dot_general(wrh_ref[...], hh, NT_DIMS, preferred_element_type=F32)
              + lax.dot_general(wrh_ref[...], hl, NT_DIMS, preferred_element_type=F32)
              + lax.dot_general(wrl_ref[...], hh, NT_DIMS, preferred_element_type=F32)
              + br_ref[...])

    e_iota = lax.broadcasted_iota(I32, (N_EXPERTS, tm), 0).astype(F32)
    work = logits
    vals, hots = [], []
    for k in range(TOP_K):
        m = jnp.max(work, axis=0, keepdims=True)
        idx = jnp.min(jnp.where(work == m, e_iota, float(N_EXPERTS)), axis=0, keepdims=True)
        hot = e_iota == idx
        vals.append(m)
        hots.append(jnp.where(hot, 1.0, 0.0))
        work = jnp.where(hot, -jnp.inf, work)
    exps = [jnp.exp(v - vals[0]) for v in vals]
    den = exps[0] + exps[1] + exps[2] + exps[3]
    for k in range(TOP_K):
        gw_ref[k:k + 1, :] = exps[k] / den

    hot_all = (hots[0] + hots[1] + hots[2] + hots[3]).astype(BF16)
    count = jnp.dot(hot_all, jnp.ones((tm, LANES), BF16), preferred_element_type=F32)
    chunk_len = jnp.ceil(count * (1.0 / ROW_ALIGN)) * ROW_ALIGN
    lower = jnp.where(lax.broadcasted_iota(I32, (N_EXPERTS, N_EXPERTS), 1)
                      < lax.broadcasted_iota(I32, (N_EXPERTS, N_EXPERTS), 0), 1.0, 0.0).astype(BF16)
    seg_base = jnp.dot(lower, chunk_len.astype(BF16), preferred_element_type=F32)
    before = jnp.where(lax.broadcasted_iota(I32, (tm, tm), 0) < lax.broadcasted_iota(I32, (tm, tm), 1),
                       1.0, 0.0).astype(BF16)
    slot = jnp.dot(hot_all, before, preferred_element_type=F32) + jnp.tile(seg_base, (1, tm // LANES))
    for k in range(TOP_K):
        cpos_ref[k:k + 1, :] = jnp.sum(hots[k] * slot, axis=0, keepdims=True).astype(I32)
    seg_ref[...] = seg_base
    len_ref[...] = chunk_len
    return chunk_len


def _tail(x2d, att, pool, g0, b0, woa, wob, g1, b1, wrh, wrl, br, carry0, *, tm):
    n, d = x2d.shape
    aw = att.shape[1]
    nt = n // tm
    sub = 2 if nt % 2 == 0 else 1
    ts = sub * tm
    row = lambda i: (i, 0)
    const = lambda i: (0, 0)
    col = lambda i: (0, i)
    tab = lambda i: (i, 0, 0)
    tab_shape = jax.ShapeDtypeStruct((nt, N_EXPERTS, LANES), F32)
    tab_spec = pl.BlockSpec((sub, N_EXPERTS, LANES), tab)
    return pl.pallas_call(
        functools.partial(_tail_kernel, tm=tm, sub=sub),
        grid=(nt // sub,),
        in_specs=[pl.BlockSpec((ts, d), row), pl.BlockSpec((ts, aw), row), pl.BlockSpec((ts, d - aw), row),
                  pl.BlockSpec((1, d), const), pl.BlockSpec((1, d), const),
                  pl.BlockSpec(woa.shape, const), pl.BlockSpec(wob.shape, const),
                  pl.BlockSpec((1, d), const), pl.BlockSpec((1, d), const),
                  pl.BlockSpec(wrh.shape, const), pl.BlockSpec(wrl.shape, const), pl.BlockSpec(br.shape, const),
                  pl.BlockSpec((N_EXPERTS, LANES), const)],
        out_specs=(pl.BlockSpec((ts, d), row), pl.BlockSpec((TOP_K, ts), col), pl.BlockSpec((TOP_K, ts), col),
                   tab_spec, tab_spec, tab_spec, pl.BlockSpec((N_EXPERTS, LANES), const)),
        out_shape=(jax.ShapeDtypeStruct((n, d), F32), jax.ShapeDtypeStruct((TOP_K, n), I32),
                   jax.ShapeDtypeStruct((TOP_K, n), F32), tab_shape, tab_shape, tab_shape,
                   jax.ShapeDtypeStruct((N_EXPERTS, LANES), F32)),
        scratch_shapes=[pltpu.VMEM((N_EXPERTS, LANES), F32)],
        compiler_params=_params(("arbitrary",)),
        name="tail",
    )(x2d, att, pool, g0, b0, woa, wob, g1, b1, wrh, wrl, br, carry0)


G_BLOCK = 256


def _stage_rows(tm):
    raw = TOP_K * tm + (ROW_ALIGN - 1) * N_EXPERTS
    return -(-raw // G_BLOCK) * G_BLOCK


def _chunk_copies(seg_ref, len_ref, dst_ref, tile, make_copy, act):
    def per_expert(e, carry):
        idx = tile * N_EXPERTS + e
        seg, ln, dst = seg_ref[idx], len_ref[idx], dst_ref[idx]

        @pl.when(ln > 0)
        def _():
            act(make_copy(pl.multiple_of(seg, ROW_ALIGN), pl.multiple_of(dst, ROW_ALIGN),
                          pl.multiple_of(ln, ROW_ALIGN)))
        return carry

    lax.fori_loop(0, N_EXPERTS, per_expert, 0)


def _by_staged_blocks(seg_ref, len_ref, tile, n_full, body):
    last = tile * N_EXPERTS + N_EXPERTS - 1
    total = seg_ref[last] + len_ref[last]
    pl.when(total <= (n_full - 1) * G_BLOCK)(lambda: body(n_full - 1))
    pl.when(total > (n_full - 1) * G_BLOCK)(lambda: body(n_full))


def _route_onehot(iota, pos_list, val_list):
    g = jnp.zeros(iota.shape, F32)
    for pos, val in zip(pos_list, val_list):
        g = jnp.where(iota == pos, val, g)
    return g.astype(BF16)


def _dispatch_kernel(seg_ref, len_ref, dst_ref, h1_ref, cpos_ref, *rest, tm):
    xs_hbm, stage_ref, sem = rest[-3:]
    tile = pl.program_id(0)
    slot = tile % 2
    h1b = h1_ref[...].astype(BF16)
    pos_rows = [cpos_ref[k:k + 1, :] for k in range(TOP_K)]

    def stage_blocks(n_blk):
        for blk in range(n_blk):
            iota = blk * G_BLOCK + lax.broadcasted_iota(I32, (G_BLOCK, tm), 0)
            g = _route_onehot(iota, pos_rows, [1.0] * TOP_K)
            stage_ref[slot, blk * G_BLOCK:(blk + 1) * G_BLOCK, :] = jnp.dot(
                g, h1b, preferred_element_type=F32).astype(BF16)

    _by_staged_blocks(seg_ref, len_ref, tile, stage_ref.shape[1] // G_BLOCK, stage_blocks)

    def copy_from(buf):
        def make_copy(seg, dst, size):
            return pltpu.make_async_copy(stage_ref.at[buf, pl.ds(seg, size), :], xs_hbm.at[pl.ds(dst, size), :],
                                         sem.at[buf])
        return make_copy

    _chunk_copies(seg_ref, len_ref, dst_ref, tile, copy_from(slot), lambda c: c.start())

    @pl.when(tile > 0)
    def _():
        _chunk_copies(seg_ref, len_ref, dst_ref, tile - 1, copy_from(1 - slot), lambda c: c.wait())

    @pl.when(tile == pl.num_programs(0) - 1)
    def _():
        _chunk_copies(seg_ref, len_ref, dst_ref, tile, copy_from(slot), lambda c: c.wait())


def _dispatch(seg, ln, dst, h1, cpos, xs_prev, *, tm, p_rows):
    n, d = h1.shape
    in_specs = [pl.BlockSpec((tm, d), lambda i, *_: (i, 0)), pl.BlockSpec((TOP_K, tm), lambda i, *_: (0, i))]
    operands = [seg, ln, dst, h1, cpos]
    aliases = {}
    if xs_prev is not None:
        in_specs.append(pl.BlockSpec(memory_space=pl.ANY))
        aliases = {len(operands): 0}
        operands.append(xs_prev)
    return pl.pallas_call(
        functools.partial(_dispatch_kernel, tm=tm),
        grid_spec=pltpu.PrefetchScalarGridSpec(
            num_scalar_prefetch=3,
            grid=(n // tm,),
            in_specs=in_specs,
            out_specs=pl.BlockSpec(memory_space=pl.ANY),
            scratch_shapes=[pltpu.VMEM((2, _stage_rows(tm), d), BF16), pltpu.SemaphoreType.DMA((2,))],
        ),
        out_shape=jax.ShapeDtypeStruct((p_rows, d), BF16),
        input_output_aliases=aliases,
        compiler_params=_params(("arbitrary",)),
        name="dispatch",
    )(*operands)


FF_CHUNK = 256


def _moe_kernel(te_ref, tv_ref, x_ref, w1_ref, b1_ref, w2_ref, b2_ref, y_ref, w1b_ref, w2b_ref, act_ref):
    i = pl.program_id(0)
    d_ff = w2_ref.shape[1]
    valid = tv_ref[i]

    @pl.when((i == 0) | (te_ref[i] != te_ref[jnp.maximum(i - 1, 0)]))
    def _():
        w1b_ref[...] = w1_ref[0].astype(BF16)
        w2b_ref[...] = w2_ref[0].astype(BF16)

    @pl.when(valid > 0)
    def _():
        rows = lax.broadcasted_iota(I32, x_ref.shape, 0)
        xb = jnp.where(rows < valid, x_ref[...].astype(F32), 0.0).astype(BF16)
        for j in range(d_ff // FF_CHUNK):
            gs = slice(j * FF_CHUNK, (j + 1) * FF_CHUNK)
            us = slice(d_ff + j * FF_CHUNK, d_ff + (j + 1) * FF_CHUNK)
            gate = jnp.dot(xb, w1b_ref[:, gs], preferred_element_type=F32) + b1_ref[0, :, gs]
            up = jnp.dot(xb, w1b_ref[:, us], preferred_element_type=F32) + b1_ref[0, :, us]
            gate = jnp.minimum(gate, SWIGLU_LIMIT)
            up = jnp.clip(up, -SWIGLU_LIMIT, SWIGLU_LIMIT)
            act_ref[:, gs] = ((up + 1.0) * (gate * jax.nn.sigmoid(SWIGLU_ALPHA * gate))).astype(BF16)
        y = jnp.dot(act_ref[...], w2b_ref[...], preferred_element_type=F32) + b2_ref[0]
        y_ref[...] = y.astype(BF16)

    @pl.when(valid <= 0)
    def _():
        y_ref[...] = jnp.zeros(y_ref.shape, BF16)


def _moe(tile_expert, tile_valid, xs, w1, b1, w2, b2, *, tmoe):
    p, d = xs.shape
    e, _, ff2 = w1.shape
    d_ff = ff2 // 2
    return pl.pallas_call(
        _moe_kernel,
        grid_spec=pltpu.PrefetchScalarGridSpec(
            num_scalar_prefetch=2,
            grid=(p // tmoe,),
            in_specs=[pl.BlockSpec((tmoe, d), lambda i, te, tv: (i, 0)),
                      pl.BlockSpec((1, d, ff2), lambda i, te, tv: (te[i], 0, 0)),
                      pl.BlockSpec((1, 1, ff2), lambda i, te, tv: (te[i], 0, 0)),
                      pl.BlockSpec((1, d_ff, d), lambda i, te, tv: (te[i], 0, 0)),
                      pl.BlockSpec((1, 1, d), lambda i, te, tv: (te[i], 0, 0))],
            out_specs=pl.BlockSpec((tmoe, d), lambda i, te, tv: (i, 0)),
            scratch_shapes=[pltpu.VMEM((d, ff2), BF16), pltpu.VMEM((d_ff, d), BF16), pltpu.VMEM((tmoe, d_ff), BF16)],
        ),
        out_shape=jax.ShapeDtypeStruct((p, d), BF16),
        compiler_params=_params(("arbitrary",)),
        name="moe",
    )(tile_expert, tile_valid, xs, w1, b1.reshape(e, 1, ff2), w2, b2.reshape(e, 1, d))


def _combine_kernel(seg_ref, len_ref, dst_ref, h1_ref, cpos_ref, gw_ref, g2_ref, b2_ref, ys_hbm, o_ref,
                    stage_ref, gate_ref, sem, *, tm):
    tile = pl.program_id(0)
    slot = tile % 2

    def copy_into(buf):
        def make_copy(seg, dst, size):
            return pltpu.make_async_copy(ys_hbm.at[pl.ds(dst, size), :], stage_ref.at[buf, pl.ds(seg, size), :],
                                         sem.at[buf])
        return make_copy

    @pl.when(tile == 0)
    def _():
        stage_ref[...] = jnp.zeros(stage_ref.shape, BF16)
        _chunk_copies(seg_ref, len_ref, dst_ref, tile, copy_into(slot), lambda c: c.start())

    @pl.when(tile + 1 < pl.num_programs(0))
    def _():
        _chunk_copies(seg_ref, len_ref, dst_ref, tile + 1, copy_into(1 - slot), lambda c: c.start())

    cpos = cpos_ref[...]
    gw = gw_ref[...]
    pos_cols = [cpos[:, k:k + 1] for k in range(TOP_K)]
    gw_cols = [gw[:, k:k + 1] for k in range(TOP_K)]

    def finish(n_blk):
        for blk in range(n_blk):
            iota = blk * G_BLOCK + lax.broadcasted_iota(I32, (tm, G_BLOCK), 1)
            gate_ref[:, blk * G_BLOCK:(blk + 1) * G_BLOCK] = _route_onehot(iota, pos_cols, gw_cols)
        _chunk_copies(seg_ref, len_ref, dst_ref, tile, copy_into(slot), lambda c: c.wait())
        rows = n_blk * G_BLOCK
        m = jnp.dot(gate_ref[:, :rows], stage_ref[slot, :rows, :], preferred_element_type=F32)
        o_ref[...] = _layer_norm(DEEPNORM_ALPHA * h1_ref[...] + m, g2_ref[...], b2_ref[...])

    _by_staged_blocks(seg_ref, len_ref, tile, stage_ref.shape[1] // G_BLOCK, finish)


def _combine(seg, ln, dst, h1, cpos_nt, gw_nt, g2, b2, ys, *, tm):
    n, d = h1.shape
    return pl.pallas_call(
        functools.partial(_combine_kernel, tm=tm),
        grid_spec=pltpu.PrefetchScalarGridSpec(
            num_scalar_prefetch=3,
            grid=(n // tm,),
            in_specs=[pl.BlockSpec((tm, d), lambda i, *_: (i, 0)),
                      pl.BlockSpec((tm, TOP_K), lambda i, *_: (i, 0)), pl.BlockSpec((tm, TOP_K), lambda i, *_: (i, 0)),
                      pl.BlockSpec((1, d), lambda i, *_: (0, 0)), pl.BlockSpec((1, d), lambda i, *_: (0, 0)),
                      pl.BlockSpec(memory_space=pl.ANY)],
            out_specs=pl.BlockSpec((tm, d), lambda i, *_: (i, 0)),
            scratch_shapes=[pltpu.VMEM((2, _stage_rows(tm), d), BF16), pltpu.VMEM((tm, _stage_rows(tm)), BF16),
                            pltpu.SemaphoreType.DMA((2,))],
        ),
        out_shape=jax.ShapeDtypeStruct((n, d), F32),
        compiler_params=_params(("arbitrary",)),
        name="combine",
    )(seg, ln, dst, h1, cpos_nt, gw_nt, g2, b2, ys)


def _block_tails(groups, wts, *, tm, tmoe):
    carry = jnp.zeros((N_EXPERTS, LANES), F32)
    plans, max_rows = [], 0
    for x2d, att, pool in groups:
        n = x2d.shape[0]
        tm_g = min(tm, n)
        h1, cpos, gw_t, seg_t, len_t, dst_t, carry = _tail(
            x2d, att, pool, wts["g0"], wts["b0"], wts["woa"], wts["wob"], wts["g1"], wts["b1"],
            wts["wrh"], wts["wrl"], wts["br"], carry, tm=tm_g)
        plans.append((tm_g, h1, cpos, gw_t, seg_t, len_t, dst_t))
        max_rows += TOP_K * n + (ROW_ALIGN - 1) * N_EXPERTS * (n // tm_g)
    used = carry[:, 0].astype(I32)
    cap = ((used + tmoe - 1) // tmoe) * tmoe
    ends = jnp.cumsum(cap)
    offs = ends - cap
    n_tiles = -(-max_rows // tmoe) + N_EXPERTS
    tile_start = jnp.arange(n_tiles, dtype=I32) * tmoe
    tile_expert = jnp.minimum(jnp.sum((ends[None, :] <= tile_start[:, None]).astype(I32), axis=1), N_EXPERTS - 1)
    tile_valid = jnp.clip(offs[tile_expert] + used[tile_expert] - tile_start, 0, tmoe).astype(I32)
    tile_valid = jnp.where(tile_start < ends[-1], tile_valid, 0)

    xs, tables = None, []
    for tm_g, h1, cpos, gw_t, seg_t, len_t, dst_t in plans:
        seg = seg_t[:, :, 0].astype(I32).reshape(-1)
        ln = len_t[:, :, 0].astype(I32).reshape(-1)
        dst = (dst_t[:, :, 0].astype(I32) + offs[None, :]).reshape(-1)
        tables.append((seg, ln, dst))
        xs = _dispatch(seg, ln, dst, h1, cpos, xs, tm=tm_g, p_rows=n_tiles * tmoe)
    ys = _moe(tile_expert, tile_valid, xs, wts["w1e"], wts["b1e"], wts["w2e"], wts["b2e"], tmoe=tmoe)
    return [_combine(seg, ln, dst, h1, cpos.T, gw_t.T, wts["g2"], wts["b2"], ys, tm=tm_g)
            for (seg, ln, dst), (tm_g, h1, cpos, gw_t, _, _, _) in zip(tables, plans)]


def kernel(x_prompt, x_sample, cache_k, cache_v, cache_kidx, state_pool, ln0_g, ln0_b, w_in, w_o,
           pool_w, pool_scale, ln1_g, ln1_b, w_router, b_router, w1, b1, w2, b2, ln2_g, ln2_b):
    bp, s_len, d = x_prompt.shape
    bs, t_len, _ = x_sample.shape
    l_past = cache_k.shape[2]
    aw = N_HEADS * HEAD_DIM
    pw = d - aw
    lyr = 0
    lc = 256

    k_off = aw
    v_off = k_off + HEAD_DIM
    qi_off = v_off + HEAD_DIM
    ki_off = qi_off + IDX_HEADS * IDX_DIM
    wi_off = ki_off + IDX_DIM
    u_off = wi_off + IDX_HEADS
    win = w_in[lyr]
    wa = jnp.concatenate([win[:, 0:k_off], win[:, qi_off:ki_off], win[:, u_off:u_off + pw]], axis=1).astype(BF16)
    wb = jnp.concatenate([win[:, k_off:v_off], win[:, v_off:qi_off], win[:, ki_off:wi_off],
                          jnp.zeros((d, 64), F32)], axis=1).astype(BF16)
    wt = jnp.concatenate([win[:, v_off:qi_off].T, win[:, wi_off:u_off].T, jnp.zeros((8, d), F32)], axis=0).astype(BF16)

    g0 = ln0_g.reshape(1, d)
    b0 = ln0_b.reshape(1, d)
    wrt = w_router[lyr].T
    wrh = wrt.astype(BF16)
    wts = dict(
        g0=g0, b0=b0,
        woa=w_o[lyr][:aw].astype(BF16), wob=w_o[lyr][aw:].astype(BF16),
        g1=ln1_g[lyr].reshape(1, d), b1=ln1_b[lyr].reshape(1, d),
        wrh=wrh, wrl=(wrt - wrh.astype(F32)).astype(BF16), br=b_router[lyr].reshape(N_EXPERTS, 1),
        w1e=w1[lyr], b1e=b1[lyr], w2e=w2[lyr], b2e=b2[lyr],
        g2=ln2_g[lyr].reshape(1, d), b2=ln2_b[lyr].reshape(1, d),
    )
    pool_w_b = pool_w[lyr].astype(BF16)
    pool_sc = pool_scale[lyr].reshape(1, pw)

    xp = x_prompt.reshape(bp * s_len, d)
    q, qi, u, k, v, ki, kb, kib, vt, wit = _proj(xp, g0, b0, wa, wb, wt, tm=512, lc=lc)
    att_p = _dsa(q.reshape(bp, s_len, aw), qi.reshape(bp, s_len, aw), wit,
                 kb.reshape(bp, s_len, HEAD_DIM), kib.reshape(bp, s_len, IDX_DIM), vt,
                 tq=256, lc=lc, causal=True, l_valid=s_len, q_pos0=0, topk=min(TOPK_MAX, s_len // 4))
    u_p = u.reshape(bp, s_len, pw)
    pool_p = _pool(u_p, jnp.zeros((bp, POOL_PAST + 1, pw), F32), pool_w_b, pool_sc, pos0=0)

    xs = x_sample.reshape(bs * t_len, d)
    qs, qis, us, kn, vn, kin, _, _, _, wits = _proj(xs, g0, b0, wa, wb, wt, tm=512, lc=lc)
    l_all = l_past + t_len
    l_pad = -(-l_all // lc) * lc
    tq_s = LANES
    pad_keys = lambda a: jnp.pad(a, ((0, 0), (0, l_pad - l_all), (0, 0)))
    k_all = pad_keys(jnp.concatenate([cache_k[lyr], kn.reshape(bs, t_len, HEAD_DIM)], axis=1))
    v_all = pad_keys(jnp.concatenate([cache_v[lyr], vn.reshape(bs, t_len, HEAD_DIM)], axis=1))
    ki_all = pad_keys(jnp.concatenate([cache_kidx[lyr], kin.reshape(bs, t_len, IDX_DIM)], axis=1))
    vt_all = v_all.reshape(bs, l_pad // lc, lc, HEAD_DIM).transpose(0, 1, 3, 2).reshape(-1, HEAD_DIM, lc)
    vt_all = jnp.concatenate([vt_all, jnp.broadcast_to(_denominator_rows(lc), (vt_all.shape[0], VT_ROWS - HEAD_DIM, lc))],
                             axis=1)
    pad_q = lambda a: jnp.pad(a.reshape(bs, t_len, aw), ((0, 0), (0, tq_s - t_len), (0, 0)))
    wit_s = jnp.pad(wits.reshape(IDX_HEADS, bs, t_len), ((0, 0), (0, 0), (0, tq_s - t_len))).reshape(IDX_HEADS, -1)
    att_s = _dsa(pad_q(qs), pad_q(qis), wit_s, k_all.astype(BF16), ki_all.astype(BF16), vt_all.astype(BF16),
                 tq=tq_s, lc=lc, causal=False, l_valid=l_all, q_pos0=l_past, topk=min(TOPK_MAX, l_all // 4))
    att_s = att_s[:, :t_len].reshape(bs * t_len, aw)
    us3 = us.reshape(bs, t_len, pw)
    prefix_s = jnp.concatenate([jnp.zeros((bs, 1, pw), F32), state_pool[lyr]], axis=1)
    pool_s = _pool(us3, prefix_s, pool_w_b, pool_sc, pos0=l_past)

    y_p, y_s = _block_tails([(xp, att_p.reshape(bp * s_len, aw), pool_p.reshape(bp * s_len, pw)),
                             (xs, att_s, pool_s.reshape(bs * t_len, pw))], wts, tm=512, tmoe=1024)

    pool_state_p = u_p[:, s_len - POOL_PAST:]
    pool_state_s = jnp.concatenate([state_pool[lyr], us3], axis=1)[:, -POOL_PAST:]
    return (y_p.reshape(bp, s_len, d), y_s.reshape(bs, t_len, d),
            k.reshape(1, bp, s_len, HEAD_DIM), v.reshape(1, bp, s_len, HEAD_DIM),
            ki.reshape(1, bp, s_len, IDX_DIM), pool_state_p[None],
            kn.reshape(1, bs, t_len, HEAD_DIM), vn.reshape(1, bs, t_len, HEAD_DIM),
            kin.reshape(1, bs, t_len, IDX_DIM), pool_state_s[None])
```

```python
import functools

import jax
import jax.numpy as jnp
from jax import lax
from jax.experimental import pallas as pl
from jax.experimental.pallas import tpu as pltpu

F32 = jnp.float32
BF16 = jnp.bfloat16
I32 = jnp.int32

CHUNK = 64
CHUNK_SHIFT = 6
assert 1 << CHUNK_SHIFT == CHUNK
N_HEADS = 8
HEAD_DIM = 64
IDX_HEADS = 8
IDX_DIM = 64
TOPK_MAX = 256
POOL_WINDOWS = (2, 4, 8, 16)
POOL_PAST = 15
N_EXPERTS = 32
TOP_K = 4
SWIGLU_LIMIT = 7.0
SWIGLU_ALPHA = 1.702
LN_EPS = 1e-5
DEPTH = 1
DEEPNORM_ALPHA = (2 * DEPTH) ** 0.25
LOG2_E = 1.4426950408889634

LANES = 128
SUBLANES = 8
BF16_ROWS = 16
VMEM_LIMIT_BYTES = 56 * 1024 * 1024

NEG_BIG = -1e30
KEY_NEG_INF = -2139095041
KEY_POS_INF = 2139095040

NT_DIMS = (((1,), (1,)), ((), ()))


def _layer_norm(x, g, b):
    mu = jnp.mean(x, axis=-1, keepdims=True)
    xc = x - mu
    var = jnp.mean(xc * xc, axis=-1, keepdims=True)
    return xc * lax.rsqrt(var + LN_EPS) * g + b


def _params(sem):
    return pltpu.CompilerParams(dimension_semantics=sem, vmem_limit_bytes=VMEM_LIMIT_BYTES)


VT_ROWS = HEAD_DIM + BF16_ROWS


def _denominator_rows(width):
    return jnp.where(lax.broadcasted_iota(I32, (VT_ROWS - HEAD_DIM, width), 0) == 0, 1.0, 0.0)


def _proj_kernel(x_ref, g_ref, b_ref, wa_ref, wb_ref, wt_ref,
                 q_ref, qi_ref, u_ref, k_ref, v_ref, ki_ref, kb_ref, kib_ref, vt_ref, wit_ref, *, lc):
    h = _layer_norm(x_ref[...], g_ref[...], b_ref[...])
    hb = h.astype(BF16)
    aw = N_HEADS * HEAD_DIM
    pa = jnp.dot(hb, wa_ref[...], preferred_element_type=F32)
    q_ref[...] = (pa[:, :aw] * (HEAD_DIM ** -0.5 * LOG2_E)).astype(BF16)
    qi_ref[...] = pa[:, aw:2 * aw].astype(BF16)
    u_ref[...] = pa[:, 2 * aw:]
    pb = jnp.dot(hb, wb_ref[...], preferred_element_type=F32)
    k = pb[:, 0:HEAD_DIM]
    v = pb[:, HEAD_DIM:2 * HEAD_DIM]
    ki = pb[:, 2 * HEAD_DIM:2 * HEAD_DIM + IDX_DIM]
    k_ref[...] = k
    v_ref[...] = v
    ki_ref[...] = ki
    kb_ref[...] = k.astype(BF16)
    kib_ref[...] = ki.astype(BF16)
    pt = lax.dot_general(wt_ref[...], hb, NT_DIMS, preferred_element_type=F32)
    ones_rows = _denominator_rows(lc)
    for c in range(vt_ref.shape[0]):
        vt_ref[c] = jnp.concatenate([pt[0:HEAD_DIM, c * lc:(c + 1) * lc], ones_rows], axis=0).astype(BF16)
    wi = pt[HEAD_DIM:HEAD_DIM + IDX_HEADS, :]
    wit_ref[...] = (wi * (IDX_HEADS ** -0.5)) * (IDX_DIM ** -0.5)


def _proj(x2d, g, b, wa, wb, wt, *, tm, lc):
    n, d = x2d.shape
    tm = min(tm, n)
    aw = N_HEADS * HEAD_DIM
    uw = wa.shape[1] - 2 * aw
    row = lambda i: (i, 0)
    const = lambda i: (0, 0)
    out_shape = (
        jax.ShapeDtypeStruct((n, aw), BF16),
        jax.ShapeDtypeStruct((n, aw), BF16),
        jax.ShapeDtypeStruct((n, uw), F32),
        jax.ShapeDtypeStruct((n, HEAD_DIM), F32),
        jax.ShapeDtypeStruct((n, HEAD_DIM), F32),
        jax.ShapeDtypeStruct((n, IDX_DIM), F32),
        jax.ShapeDtypeStruct((n, HEAD_DIM), BF16),
        jax.ShapeDtypeStruct((n, IDX_DIM), BF16),
        jax.ShapeDtypeStruct((n // lc, VT_ROWS, lc), BF16),
        jax.ShapeDtypeStruct((IDX_HEADS, n), F32),
    )
    out_specs = (
        pl.BlockSpec((tm, aw), row), pl.BlockSpec((tm, aw), row), pl.BlockSpec((tm, uw), row),
        pl.BlockSpec((tm, HEAD_DIM), row), pl.BlockSpec((tm, HEAD_DIM), row), pl.BlockSpec((tm, IDX_DIM), row),
        pl.BlockSpec((tm, HEAD_DIM), row), pl.BlockSpec((tm, IDX_DIM), row),
        pl.BlockSpec((tm // lc, VT_ROWS, lc), lambda i: (i, 0, 0)),
        pl.BlockSpec((IDX_HEADS, tm), lambda i: (0, i)),
    )
    return pl.pallas_call(
        functools.partial(_proj_kernel, lc=lc),
        grid=(n // tm,),
        in_specs=[pl.BlockSpec((tm, d), row), pl.BlockSpec((1, d), const), pl.BlockSpec((1, d), const),
                  pl.BlockSpec(wa.shape, const), pl.BlockSpec(wb.shape, const), pl.BlockSpec(wt.shape, const)],
        out_specs=out_specs,
        out_shape=out_shape,
        compiler_params=_params(("parallel",)),
        name="proj",
    )(x2d, g, b, wa, wb, wt)


def _key_to_float(key):
    bits = jnp.where(key >= 0, key, key ^ jnp.int32(0x7FFFFFFF))
    return lax.bitcast_convert_type(bits, F32)


def _dsa_kernel(q_ref, qi_ref, wit_ref, kb_ref, kib_ref, vt_ref, o_ref,
                sc_ref, sch_ref, s_ref, out_ref, *acc_refs, tq, lc, nk_static, causal, l_valid, q_pos0, topk):
    qb = pl.program_id(1)
    nk = qb * (tq // lc) + (tq // lc) if causal else nk_static
    q_chunk = (q_pos0 + qb * tq + lax.broadcasted_iota(I32, (1, tq), 1)) >> CHUNK_SHIFT

    def chunk_loop(body, init):
        if causal:
            return lax.fori_loop(0, nk, body, init)
        return lax.fori_loop(0, nk_static, body, init, unroll=True)

    def wide_chunk_loop(body, init):
        pair = lambda i, c: body(pl.multiple_of(i * 2 * lc, 2 * lc), 2 * lc, c)
        if not causal:
            carry = lax.fori_loop(0, nk_static // 2, pair, init, unroll=True)
            return body((nk_static - 1) * lc, lc, carry) if nk_static % 2 else carry
        carry = lax.fori_loop(0, nk // 2, pair, init)
        return lax.cond(nk % 2 == 1, lambda c: body(pl.multiple_of((nk - 1) * lc, lc), lc, c), lambda c: c, carry)

    def score_span(off, rows, carry):
        kic = kib_ref[0, pl.ds(off, rows), :]
        acc = jnp.zeros((rows, tq), F32)
        for h in range(IDX_HEADS):
            s = lax.dot_general(kic, qi_ref[0, :, h * IDX_DIM:(h + 1) * IDX_DIM], NT_DIMS,
                                preferred_element_type=F32)
            acc = acc + wit_ref[h:h + 1, :] * jnp.maximum(s, 0.0)
        l_pos = off + lax.broadcasted_iota(I32, (rows, tq), 0)
        visible = ((l_pos >> CHUNK_SHIFT) <= q_chunk) & (l_pos < l_valid)
        score = jnp.where(visible, acc, -jnp.inf)
        sc_ref[pl.ds(off, rows), :] = score
        sch_ref[pl.ds(off, rows), :] = score.astype(BF16)
        return carry

    wide_chunk_loop(score_span, 0)

    def count(pred):
        ways = 4 * SUBLANES

        def body(kc, part):
            off = pl.multiple_of(kc * lc, lc)
            hit = jnp.where(pred(sc_ref[pl.ds(off, lc), :]), 1.0, 0.0)
            return part + jnp.sum(hit.reshape(lc // ways, ways, tq), axis=0)
        return jnp.sum(chunk_loop(body, jnp.zeros((ways, tq), F32)), axis=0, keepdims=True)

    def count_coarse(cand):
        ways = 2 * BF16_ROWS
        assert sc_ref.shape[0] // ways <= 256

        def body(kc, part):
            off = pl.multiple_of(kc * lc, lc)
            hit = jnp.where(sch_ref[pl.ds(off, lc), :] >= cand, jnp.ones((), BF16), jnp.zeros((), BF16))
            hit = hit.reshape(lc // ways, ways, tq)
            terms = [hit[j] for j in range(lc // ways)]
            while len(terms) > 1:
                terms = [a + b for a, b in zip(terms[::2], terms[1::2])]
            return part + terms[0]
        part = chunk_loop(body, jnp.zeros((ways, tq), BF16))
        return jnp.sum(part.astype(F32), axis=0, keepdims=True)

    def bisect(count_ge, key_to_value, lo, hi, steps):
        def step(_, carry):
            lo, hi = carry
            mid = (lo >> 1) + (hi >> 1) + (lo & hi & 1)
            ok = count_ge(key_to_value(mid)) >= topk
            return jnp.where(ok, mid, lo), jnp.where(ok, hi, mid)
        return lax.fori_loop(0, steps, step, (lo, hi))[0]

    def coarse_key_to_f32_key(k16):
        return jnp.where(k16 >= 0, k16 << 16, (k16 << 16) | 0xFFFF)

    full = lambda v: jnp.full((1, tq), v, I32)
    k16 = bisect(count_coarse, lambda k: _key_to_float(coarse_key_to_f32_key(k)).astype(BF16),
                 full(KEY_NEG_INF >> 16), full((KEY_POS_INF >> 16) + 1), 16)
    kb = coarse_key_to_f32_key(k16)
    span = 1 << 16
    lo = bisect(lambda cand: count(lambda blk: blk >= cand), _key_to_float,
                jnp.maximum(kb, KEY_NEG_INF + span) - span, jnp.minimum(kb, KEY_POS_INF + 1 - span) + span, 17)
    thr = _key_to_float(lo)
    n_above = count(lambda blk: blk > thr)
    n_ties = topk - n_above

    tri = jnp.where(lax.broadcasted_iota(I32, (lc, lc), 0) >= lax.broadcasted_iota(I32, (lc, lc), 1),
                    1.0, 0.0).astype(BF16)

    def logits_span(off, rows, carry):
        ties_before, m8 = carry
        biases = []
        for r0 in range(0, rows, lc):
            blk = sc_ref[pl.ds(off + r0, lc), :]
            tie = blk == thr
            tie_rank = (jnp.dot(tri, jnp.where(tie, 1.0, 0.0).astype(BF16), preferred_element_type=F32)
                        + ties_before)
            bias = jnp.where(blk > thr, 0.0, jnp.where(tie, jnp.where(tie_rank <= n_ties, 0.0, NEG_BIG), NEG_BIG))
            biases.append(jnp.where(blk == -jnp.inf, NEG_BIG, bias))
            ties_before = tie_rank[lc - 1:lc, :]
        bias = jnp.concatenate(biases, axis=0)
        kc_b = kb_ref[0, pl.ds(off, rows), :]
        m_rows = []
        for h in range(N_HEADS):
            s = lax.dot_general(kc_b, q_ref[0, :, h * HEAD_DIM:(h + 1) * HEAD_DIM], NT_DIMS,
                                preferred_element_type=F32) + bias
            s_ref[h, pl.ds(off, rows), :] = s
            s8 = jnp.max(s.reshape(rows // SUBLANES, SUBLANES, tq), axis=0)
            m_rows.append(jnp.maximum(m8[h * SUBLANES:(h + 1) * SUBLANES], s8))
        return ties_before, jnp.concatenate(m_rows, axis=0)

    _, m8 = wide_chunk_loop(logits_span,
                            (jnp.zeros((1, tq), F32), jnp.full((N_HEADS * SUBLANES, tq), NEG_BIG, F32)))
    m_all = jnp.max(m8.reshape(N_HEADS, SUBLANES, tq), axis=1)

    for acc_ref in acc_refs:
        acc_ref[...] = jnp.zeros(acc_ref.shape, F32)

    def pv_chunk(kc, carry):
        off = pl.multiple_of(kc * lc, lc)
        vt_c = vt_ref[kc]
        for h in range(N_HEADS):
            p = jnp.exp2(s_ref[h, pl.ds(off, lc), :] - m_all[h:h + 1, :])
            acc_refs[h][...] += jnp.dot(vt_c, p.astype(BF16), preferred_element_type=F32)
        return carry

    chunk_loop(pv_chunk, 0)

    for h in range(N_HEADS):
        acc = acc_refs[h][...]
        out_ref[h * HEAD_DIM:(h + 1) * HEAD_DIM, :] = acc[0:HEAD_DIM] / acc[HEAD_DIM:HEAD_DIM + 1]
    o_ref[0] = out_ref[...].T.astype(BF16)


def _dsa(q, qi, wit, kb, kib, vt, *, tq, lc, causal, l_valid, q_pos0, topk):
    bsz, tq_tot, aw = q.shape
    l_tot = kb.shape[1]
    nq = tq_tot // tq
    nkc = l_tot // lc
    kern = functools.partial(_dsa_kernel, tq=tq, lc=lc, nk_static=nkc, causal=causal, l_valid=l_valid,
                             q_pos0=q_pos0, topk=topk)
    return pl.pallas_call(
        kern,
        grid=(bsz, nq),
        in_specs=[
            pl.BlockSpec((1, tq, aw), lambda b, i: (b, i, 0)),
            pl.BlockSpec((1, tq, aw), lambda b, i: (b, i, 0)),
            pl.BlockSpec((IDX_HEADS, tq), lambda b, i: (0, b * nq + i)),
            pl.BlockSpec((1, l_tot, HEAD_DIM), lambda b, i: (b, 0, 0)),
            pl.BlockSpec((1, l_tot, IDX_DIM), lambda b, i: (b, 0, 0)),
            pl.BlockSpec((nkc, VT_ROWS, lc), lambda b, i: (b, 0, 0)),
        ],
        out_specs=pl.BlockSpec((1, tq, aw), lambda b, i: (b, i, 0)),
        out_shape=jax.ShapeDtypeStruct((bsz, tq_tot, aw), BF16),
        scratch_shapes=[
            pltpu.VMEM((l_tot, tq), F32),
            pltpu.VMEM((l_tot, tq), BF16),
            pltpu.VMEM((N_HEADS, l_tot, tq), F32),
            pltpu.VMEM((aw, tq), F32),
        ] + [pltpu.VMEM((VT_ROWS, tq), F32) for _ in range(N_HEADS)],
        compiler_params=_params(("parallel", "parallel")),
        name="dsa",
    )(q, qi, wit, kb, kib, vt)


def _pool_kernel(u_ref, pre_ref, pw_ref, sc_ref, o_ref, ext_ref, *, t_len, tt, pos0):
    pad = pre_ref.shape[1]
    gw = pw_ref.shape[1]
    ext_ref[0:pad, :] = pre_ref[0]
    ext_ref[pad:pad + t_len, :] = u_ref[0]
    row = lax.broadcasted_iota(I32, (tt, gw), 0)
    for t in range(t_len // tt):
        r0 = t * tt
        for g, w in enumerate(POOL_WINDOWS):
            cols = slice(g * gw, (g + 1) * gw)
            cur = ext_ref[pad + r0:pad + r0 + tt, cols]
            wsum = cur
            for j in range(1, w):
                wsum = wsum + ext_ref[pad + r0 - j:pad + r0 - j + tt, cols]
            cnt = jnp.minimum(w, pos0 + r0 + 1 + row).astype(F32)
            diff = (wsum / cnt - cur).astype(BF16)
            y = jnp.dot(diff, pw_ref[g], preferred_element_type=F32)
            o_ref[0, r0:r0 + tt, cols] = (y * sc_ref[:, cols]).astype(BF16)


def _pool(u, prefix, pool_w_b, pool_scale, *, pos0):
    bsz, t_len, c = u.shape
    pad = prefix.shape[1]
    tt = min(t_len, 256)
    return pl.pallas_call(
        functools.partial(_pool_kernel, t_len=t_len, tt=tt, pos0=pos0),
        grid=(bsz,),
        in_specs=[pl.BlockSpec((1, t_len, c), lambda b: (b, 0, 0)),
                  pl.BlockSpec((1, pad, c), lambda b: (b, 0, 0)),
                  pl.BlockSpec(pool_w_b.shape, lambda b: (0, 0, 0)),
                  pl.BlockSpec((1, c), lambda b: (0, 0))],
        out_specs=pl.BlockSpec((1, t_len, c), lambda b: (b, 0, 0)),
        out_shape=jax.ShapeDtypeStruct((bsz, t_len, c), BF16),
        scratch_shapes=[pltpu.VMEM((pad + t_len, c), F32)],
        compiler_params=_params(("parallel",)),
        name="pool",
    )(u, prefix, pool_w_b, pool_scale)


ROW_ALIGN = 16


def _tail_kernel(x_ref, att_ref, pool_ref, g0_ref, b0_ref, woa_ref, wob_ref, g1_ref, b1_ref,
                 wrh_ref, wrl_ref, br_ref,
                 carry0_ref, h1_ref, cpos_ref, gw_ref, seg_ref, len_ref, dst_ref, used_ref, carry_ref, *, tm, sub):
    @pl.when(pl.program_id(0) == 0)
    def _():
        carry_ref[...] = carry0_ref[...]

    chunk_lens = []
    for s in range(sub):
        rows = slice(s * tm, (s + 1) * tm)
        chunk_lens.append(_tail_tile(x_ref[rows, :], att_ref[rows, :], pool_ref[rows, :], g0_ref, b0_ref, woa_ref,
                                     wob_ref, g1_ref, b1_ref, wrh_ref, wrl_ref, br_ref,
                                     h1_ref.at[rows, :], cpos_ref.at[:, rows], gw_ref.at[:, rows], seg_ref.at[s],
                                     len_ref.at[s], tm=tm))
    carry = carry_ref[...]
    for s in range(sub):
        dst_ref[s] = carry
        carry = carry + chunk_lens[s]
    carry_ref[...] = carry
    used_ref[...] = carry


def _tail_tile(x, att, pool, g0_ref, b0_ref, woa_ref, wob_ref, g1_ref, b1_ref, wrh_ref, wrl_ref, br_ref,
               h1_ref, cpos_ref, gw_ref, seg_ref, len_ref, *, tm):
    h = _layer_norm(x, g0_ref[...], b0_ref[...])
    mix = (jnp.dot(att, woa_ref[...], preferred_element_type=F32)
           + jnp.dot(pool, wob_ref[...], preferred_element_type=F32))
    h1 = _layer_norm(DEEPNORM_ALPHA * h + mix, g1_ref[...], b1_ref[...])
    h1_ref[...] = h1

    hh = h1.astype(BF16)
    hl = (h1 - hh.astype(F32)).astype(BF16)
    logits = (lax.dot_general(wrh_ref[...], hh, NT_DIMS, preferred_element_type=F32)
              + lax.dot_general(wrh_ref[...], hl, NT_DIMS, preferred_element_type=F32)
              + lax.dot_general(wrl_ref[...], hh, NT_DIMS, preferred_element_type=F32)
              + br_ref[...])

    e_iota = lax.broadcasted_iota(I32, (N_EXPERTS, tm), 0).astype(F32)
    work = logits
    vals, hots = [], []
    for k in range(TOP_K):
        m = jnp.max(work, axis=0, keepdims=True)
        idx = jnp.min(jnp.where(work == m, e_iota, float(N_EXPERTS)), axis=0, keepdims=True)
        hot = e_iota == idx
        vals.append(m)
        hots.append(jnp.where(hot, 1.0, 0.0))
        work = jnp.where(hot, -jnp.inf, work)
    exps = [jnp.exp(v - vals[0]) for v in vals]
    den = exps[0] + exps[1] + exps[2] + exps[3]
    for k in range(TOP_K):
        gw_ref[k:k + 1, :] = exps[k] / den

    hot_all = (hots[0] + hots[1] + hots[2] + hots[3]).astype(BF16)
    count = jnp.dot(hot_all, jnp.ones((tm, LANES), BF16), preferred_element_type=F32)
    chunk_len = jnp.ceil(count * (1.0 / ROW_ALIGN)) * ROW_ALIGN
    lower = jnp.where(lax.broadcasted_iota(I32, (N_EXPERTS, N_EXPERTS), 1)
                      < lax.broadcasted_iota(I32, (N_EXPERTS, N_EXPERTS), 0), 1.0, 0.0).astype(BF16)
    seg_base = jnp.dot(lower, chunk_len.astype(BF16), preferred_element_type=F32)
    before = jnp.where(lax.broadcasted_iota(I32, (tm, tm), 0) < lax.broadcasted_iota(I32, (tm, tm), 1),
                       1.0, 0.0).astype(BF16)
    slot = jnp.dot(hot_all, before, preferred_element_type=F32) + jnp.tile(seg_base, (1, tm // LANES))
    for k in range(TOP_K):
        cpos_ref[k:k + 1, :] = jnp.sum(hots[k] * slot, axis=0, keepdims=True).astype(I32)
    seg_ref[...] = seg_base
    len_ref[...] = chunk_len
    return chunk_len


def _tail(x2d, att, pool, g0, b0, woa, wob, g1, b1, wrh, wrl, br, carry0, *, tm):
    n, d = x2d.shape
    aw = att.shape[1]
    nt = n // tm
    sub = 2 if nt % 2 == 0 else 1
    ts = sub * tm
    row = lambda i: (i, 0)
    const = lambda i: (0, 0)
    col = lambda i: (0, i)
    tab = lambda i: (i, 0, 0)
    tab_shape = jax.ShapeDtypeStruct((nt, N_EXPERTS, LANES), F32)
    tab_spec = pl.BlockSpec((sub, N_EXPERTS, LANES), tab)
    return pl.pallas_call(
        functools.partial(_tail_kernel, tm=tm, sub=sub),
        grid=(nt // sub,),
        in_specs=[pl.BlockSpec((ts, d), row), pl.BlockSpec((ts, aw), row), pl.BlockSpec((ts, d - aw), row),
                  pl.BlockSpec((1, d), const), pl.BlockSpec((1, d), const),
                  pl.BlockSpec(woa.shape, const), pl.BlockSpec(wob.shape, const),
                  pl.BlockSpec((1, d), const), pl.BlockSpec((1, d), const),
                  pl.BlockSpec(wrh.shape, const), pl.BlockSpec(wrl.shape, const), pl.BlockSpec(br.shape, const),
                  pl.BlockSpec((N_EXPERTS, LANES), const)],
        out_specs=(pl.BlockSpec((ts, d), row), pl.BlockSpec((TOP_K, ts), col), pl.BlockSpec((TOP_K, ts), col),
                   tab_spec, tab_spec, tab_spec, pl.BlockSpec((N_EXPERTS, LANES), const)),
        out_shape=(jax.ShapeDtypeStruct((n, d), F32), jax.ShapeDtypeStruct((TOP_K, n), I32),
                   jax.ShapeDtypeStruct((TOP_K, n), F32), tab_shape, tab_shape, tab_shape,
                   jax.ShapeDtypeStruct((N_EXPERTS, LANES), F32)),
        scratch_shapes=[pltpu.VMEM((N_EXPERTS, LANES), F32)],
        compiler_params=_params(("arbitrary",)),
        name="tail",
    )(x2d, att, pool, g0, b0, woa, wob, g1, b1, wrh, wrl, br, carry0)


G_BLOCK = 256


def _stage_rows(tm):
    raw = TOP_K * tm + (ROW_ALIGN - 1) * N_EXPERTS
    return -(-raw // G_BLOCK) * G_BLOCK


def _chunk_copies(seg_ref, len_ref, dst_ref, tile, make_copy, act):
    def per_expert(e, carry):
        idx = tile * N_EXPERTS + e
        seg, ln, dst = seg_ref[idx], len_ref[idx], dst_ref[idx]

        @pl.when(ln > 0)
        def _():
            act(make_copy(pl.multiple_of(seg, ROW_ALIGN), pl.multiple_of(dst, ROW_ALIGN),
                          pl.multiple_of(ln, ROW_ALIGN)))
        return carry

    lax.fori_loop(0, N_EXPERTS, per_expert, 0)


def _by_staged_blocks(seg_ref, len_ref, tile, n_full, body):
    last = tile * N_EXPERTS + N_EXPERTS - 1
    total = seg_ref[last] + len_ref[last]
    pl.when(total <= (n_full - 1) * G_BLOCK)(lambda: body(n_full - 1))
    pl.when(total > (n_full - 1) * G_BLOCK)(lambda: body(n_full))


def _route_onehot(iota, pos_list, val_list):
    g = jnp.zeros(iota.shape, F32)
    for pos, val in zip(pos_list, val_list):
        g = jnp.where(iota == pos, val, g)
    return g.astype(BF16)


def _dispatch_kernel(seg_ref, len_ref, dst_ref, h1_ref, cpos_ref, *rest, tm):
    xs_hbm, stage_ref, sem = rest[-3:]
    tile = pl.program_id(0)
    slot = tile % 2
    h1b = h1_ref[...].astype(BF16)
    pos_rows = [cpos_ref[k:k + 1, :] for k in range(TOP_K)]

    def stage_blocks(n_blk):
        for blk in range(n_blk):
            iota = blk * G_BLOCK + lax.broadcasted_iota(I32, (G_BLOCK, tm), 0)
            g = _route_onehot(iota, pos_rows, [1.0] * TOP_K)
            stage_ref[slot, blk * G_BLOCK:(blk + 1) * G_BLOCK, :] = jnp.dot(
                g, h1b, preferred_element_type=F32).astype(BF16)

    _by_staged_blocks(seg_ref, len_ref, tile, stage_ref.shape[1] // G_BLOCK, stage_blocks)

    def copy_from(buf):
        def make_copy(seg, dst, size):
            return pltpu.make_async_copy(stage_ref.at[buf, pl.ds(seg, size), :], xs_hbm.at[pl.ds(dst, size), :],
                                         sem.at[buf])
        return make_copy

    _chunk_copies(seg_ref, len_ref, dst_ref, tile, copy_from(slot), lambda c: c.start())

    @pl.when(tile > 0)
    def _():
        _chunk_copies(seg_ref, len_ref, dst_ref, tile - 1, copy_from(1 - slot), lambda c: c.wait())

    @pl.when(tile == pl.num_programs(0) - 1)
    def _():
        _chunk_copies(seg_ref, len_ref, dst_ref, tile, copy_from(slot), lambda c: c.wait())


def _dispatch(seg, ln, dst, h1, cpos, xs_prev, *, tm, p_rows):
    n, d = h1.shape
    in_specs = [pl.BlockSpec((tm, d), lambda i, *_: (i, 0)), pl.BlockSpec((TOP_K, tm), lambda i, *_: (0, i))]
    operands = [seg, ln, dst, h1, cpos]
    aliases = {}
    if xs_prev is not None:
        in_specs.append(pl.BlockSpec(memory_space=pl.ANY))
        aliases = {len(operands): 0}
        operands.append(xs_prev)
    return pl.pallas_call(
        functools.partial(_dispatch_kernel, tm=tm),
        grid_spec=pltpu.PrefetchScalarGridSpec(
            num_scalar_prefetch=3,
            grid=(n // tm,),
            in_specs=in_specs,
            out_specs=pl.BlockSpec(memory_space=pl.ANY),
            scratch_shapes=[pltpu.VMEM((2, _stage_rows(tm), d), BF16), pltpu.SemaphoreType.DMA((2,))],
        ),
        out_shape=jax.ShapeDtypeStruct((p_rows, d), BF16),
        input_output_aliases=aliases,
        compiler_params=_params(("arbitrary",)),
        name="dispatch",
    )(*operands)


FF_CHUNK = 256


def _moe_kernel(te_ref, tv_ref, x_ref, w1_ref, b1_ref, w2_ref, b2_ref, y_ref, w1b_ref, w2b_ref, act_ref):
    i = pl.program_id(0)
    d_ff = w2_ref.shape[1]
    valid = tv_ref[i]

    @pl.when((i == 0) | (te_ref[i] != te_ref[jnp.maximum(i - 1, 0)]))
    def _():
        w1b_ref[...] = w1_ref[0].astype(BF16)
        w2b_ref[...] = w2_ref[0].astype(BF16)

    @pl.when(valid > 0)
    def _():
        rows = lax.broadcasted_iota(I32, x_ref.shape, 0)
        xb = jnp.where(rows < valid, x_ref[...].astype(F32), 0.0).astype(BF16)
        for j in range(d_ff // FF_CHUNK):
            gs = slice(j * FF_CHUNK, (j + 1) * FF_CHUNK)
            us = slice(d_ff + j * FF_CHUNK, d_ff + (j + 1) * FF_CHUNK)
            gate = jnp.dot(xb, w1b_ref[:, gs], preferred_element_type=F32) + b1_ref[0, :, gs]
            up = jnp.dot(xb, w1b_ref[:, us], preferred_element_type=F32) + b1_ref[0, :, us]
            gate = jnp.minimum(gate, SWIGLU_LIMIT)
            up = jnp.clip(up, -SWIGLU_LIMIT, SWIGLU_LIMIT)
            act_ref[:, gs] = ((up + 1.0) * (gate * jax.nn.sigmoid(SWIGLU_ALPHA * gate))).astype(BF16)
        y = jnp.dot(act_ref[...], w2b_ref[...], preferred_element_type=F32) + b2_ref[0]
        y_ref[...] = y.astype(BF16)

    @pl.when(valid <= 0)
    def _():
        y_ref[...] = jnp.zeros(y_ref.shape, BF16)


def _moe(tile_expert, tile_valid, xs, w1, b1, w2, b2, *, tmoe):
    p, d = xs.shape
    e, _, ff2 = w1.shape
    d_ff = ff2 // 2
    return pl.pallas_call(
        _moe_kernel,
        grid_spec=pltpu.PrefetchScalarGridSpec(
            num_scalar_prefetch=2,
            grid=(p // tmoe,),
            in_specs=[pl.BlockSpec((tmoe, d), lambda i, te, tv: (i, 0)),
                      pl.BlockSpec((1, d, ff2), lambda i, te, tv: (te[i], 0, 0)),
                      pl.BlockSpec((1, 1, ff2), lambda i, te, tv: (te[i], 0, 0)),
                      pl.BlockSpec((1, d_ff, d), lambda i, te, tv: (te[i], 0, 0)),
                      pl.BlockSpec((1, 1, d), lambda i, te, tv: (te[i], 0, 0))],
            out_specs=pl.BlockSpec((tmoe, d), lambda i, te, tv: (i, 0)),
            scratch_shapes=[pltpu.VMEM((d, ff2), BF16), pltpu.VMEM((d_ff, d), BF16), pltpu.VMEM((tmoe, d_ff), BF16)],
        ),
        out_shape=jax.ShapeDtypeStruct((p, d), BF16),
        compiler_params=_params(("arbitrary",)),
        name="moe",
    )(tile_expert, tile_valid, xs, w1, b1.reshape(e, 1, ff2), w2, b2.reshape(e, 1, d))


def _combine_kernel(seg_ref, len_ref, dst_ref, h1_ref, cpos_ref, gw_ref, g2_ref, b2_ref, ys_hbm, o_ref,
                    stage_ref, gate_ref, sem, *, tm):
    tile = pl.program_id(0)
    slot = tile % 2

    def copy_into(buf):
        def make_copy(seg, dst, size):
            return pltpu.make_async_copy(ys_hbm.at[pl.ds(dst, size), :], stage_ref.at[buf, pl.ds(seg, size), :],
                                         sem.at[buf])
        return make_copy

    @pl.when(tile == 0)
    def _():
        stage_ref[...] = jnp.zeros(stage_ref.shape, BF16)
        _chunk_copies(seg_ref, len_ref, dst_ref, tile, copy_into(slot), lambda c: c.start())

    @pl.when(tile + 1 < pl.num_programs(0))
    def _():
        _chunk_copies(seg_ref, len_ref, dst_ref, tile + 1, copy_into(1 - slot), lambda c: c.start())

    cpos = cpos_ref[...]
    gw = gw_ref[...]
    pos_cols = [cpos[:, k:k + 1] for k in range(TOP_K)]
    gw_cols = [gw[:, k:k + 1] for k in range(TOP_K)]

    def finish(n_blk):
        for blk in range(n_blk):
            iota = blk * G_BLOCK + lax.broadcasted_iota(I32, (tm, G_BLOCK), 1)
            gate_ref[:, blk * G_BLOCK:(blk + 1) * G_BLOCK] = _route_onehot(iota, pos_cols, gw_cols)
        _chunk_copies(seg_ref, len_ref, dst_ref, tile, copy_into(slot), lambda c: c.wait())
        rows = n_blk * G_BLOCK
        m = jnp.dot(gate_ref[:, :rows], stage_ref[slot, :rows, :], preferred_element_type=F32)
        o_ref[...] = _layer_norm(DEEPNORM_ALPHA * h1_ref[...] + m, g2_ref[...], b2_ref[...])

    _by_staged_blocks(seg_ref, len_ref, tile, stage_ref.shape[1] // G_BLOCK, finish)


def _combine(seg, ln, dst, h1, cpos_nt, gw_nt, g2, b2, ys, *, tm):
    n, d = h1.shape
    return pl.pallas_call(
        functools.partial(_combine_kernel, tm=tm),
        grid_spec=pltpu.PrefetchScalarGridSpec(
            num_scalar_prefetch=3,
            grid=(n // tm,),
            in_specs=[pl.BlockSpec((tm, d), lambda i, *_: (i, 0)),
                      pl.BlockSpec((tm, TOP_K), lambda i, *_: (i, 0)), pl.BlockSpec((tm, TOP_K), lambda i, *_: (i, 0)),
                      pl.BlockSpec((1, d), lambda i, *_: (0, 0)), pl.BlockSpec((1, d), lambda i, *_: (0, 0)),
                      pl.BlockSpec(memory_space=pl.ANY)],
            out_specs=pl.BlockSpec((tm, d), lambda i, *_: (i, 0)),
            scratch_shapes=[pltpu.VMEM((2, _stage_rows(tm), d), BF16), pltpu.VMEM((tm, _stage_rows(tm)), BF16),
                            pltpu.SemaphoreType.DMA((2,))],
        ),
        out_shape=jax.ShapeDtypeStruct((n, d), F32),
        compiler_params=_params(("arbitrary",)),
        name="combine",
    )(seg, ln, dst, h1, cpos_nt, gw_nt, g2, b2, ys)


def _block_tails(groups, wts, *, tm, tmoe):
    carry = jnp.zeros((N_EXPERTS, LANES), F32)
    plans, max_rows = [], 0
    for x2d, att, pool in groups:
        n = x2d.shape[0]
        tm_g = min(tm, n)
        h1, cpos, gw_t, seg_t, len_t, dst_t, carry = _tail(
            x2d, att, pool, wts["g0"], wts["b0"], wts["woa"], wts["wob"], wts["g1"], wts["b1"],
            wts["wrh"], wts["wrl"], wts["br"], carry, tm=tm_g)
        plans.append((tm_g, h1, cpos, gw_t, seg_t, len_t, dst_t))
        max_rows += TOP_K * n + (ROW_ALIGN - 1) * N_EXPERTS * (n // tm_g)
    used = carry[:, 0].astype(I32)
    cap = ((used + tmoe - 1) // tmoe) * tmoe
    ends = jnp.cumsum(cap)
    offs = ends - cap
    n_tiles = -(-max_rows // tmoe) + N_EXPERTS
    tile_start = jnp.arange(n_tiles, dtype=I32) * tmoe
    tile_expert = jnp.minimum(jnp.sum((ends[None, :] <= tile_start[:, None]).astype(I32), axis=1), N_EXPERTS - 1)
    tile_valid = jnp.clip(offs[tile_expert] + used[tile_expert] - tile_start, 0, tmoe).astype(I32)
    tile_valid = jnp.where(tile_start < ends[-1], tile_valid, 0)

    xs, tables = None, []
    for tm_g, h1, cpos, gw_t, seg_t, len_t, dst_t in plans:
        seg = seg_t[:, :, 0].astype(I32).reshape(-1)
        ln = len_t[:, :, 0].astype(I32).reshape(-1)
        dst = (dst_t[:, :, 0].astype(I32) + offs[None, :]).reshape(-1)
        tables.append((seg, ln, dst))
        xs = _dispatch(seg, ln, dst, h1, cpos, xs, tm=tm_g, p_rows=n_tiles * tmoe)
    ys = _moe(tile_expert, tile_valid, xs, wts["w1e"], wts["b1e"], wts["w2e"], wts["b2e"], tmoe=tmoe)
    return [_combine(seg, ln, dst, h1, cpos.T, gw_t.T, wts["g2"], wts["b2"], ys, tm=tm_g)
            for (seg, ln, dst), (tm_g, h1, cpos, gw_t, _, _, _) in zip(tables, plans)]


def kernel(x_prompt, x_sample, cache_k, cache_v, cache_kidx, state_pool, ln0_g, ln0_b, w_in, w_o,
           pool_w, pool_scale, ln1_g, ln1_b, w_router, b_router, w1, b1, w2, b2, ln2_g, ln2_b):
    bp, s_len, d = x_prompt.shape
    bs, t_len, _ = x_sample.shape
    l_past = cache_k.shape[2]
    aw = N_HEADS * HEAD_DIM
    pw = d - aw
    lyr = 0
    lc = 256

    k_off = aw
    v_off = k_off + HEAD_DIM
    qi_off = v_off + HEAD_DIM
    ki_off = qi_off + IDX_HEADS * IDX_DIM
    wi_off = ki_off + IDX_DIM
    u_off = wi_off + IDX_HEADS
    win = w_in[lyr]
    wa = jnp.concatenate([win[:, 0:k_off], win[:, qi_off:ki_off], win[:, u_off:u_off + pw]], axis=1).astype(BF16)
    wb = jnp.concatenate([win[:, k_off:v_off], win[:, v_off:qi_off], win[:, ki_off:wi_off],
                          jnp.zeros((d, 64), F32)], axis=1).astype(BF16)
    wt = jnp.concatenate([win[:, v_off:qi_off].T, win[:, wi_off:u_off].T, jnp.zeros((8, d), F32)], axis=0).astype(BF16)

    g0 = ln0_g.reshape(1, d)
    b0 = ln0_b.reshape(1, d)
    wrt = w_router[lyr].T
    wrh = wrt.astype(BF16)
    wts = dict(
        g0=g0, b0=b0,
        woa=w_o[lyr][:aw].astype(BF16), wob=w_o[lyr][aw:].astype(BF16),
        g1=ln1_g[lyr].reshape(1, d), b1=ln1_b[lyr].reshape(1, d),
        wrh=wrh, wrl=(wrt - wrh.astype(F32)).astype(BF16), br=b_router[lyr].reshape(N_EXPERTS, 1),
        w1e=w1[lyr], b1e=b1[lyr], w2e=w2[lyr], b2e=b2[lyr],
        g2=ln2_g[lyr].reshape(1, d), b2=ln2_b[lyr].reshape(1, d),
    )
    pool_w_b = pool_w[lyr].astype(BF16)
    pool_sc = pool_scale[lyr].reshape(1, pw)

    xp = x_prompt.reshape(bp * s_len, d)
    q, qi, u, k, v, ki, kb, kib, vt, wit = _proj(xp, g0, b0, wa, wb, wt, tm=512, lc=lc)
    att_p = _dsa(q.reshape(bp, s_len, aw), qi.reshape(bp, s_len, aw), wit,
                 kb.reshape(bp, s_len, HEAD_DIM), kib.reshape(bp, s_len, IDX_DIM), vt,
                 tq=256, lc=lc, causal=True, l_valid=s_len, q_pos0=0, topk=min(TOPK_MAX, s_len // 4))
    u_p = u.reshape(bp, s_len, pw)
    pool_p = _pool(u_p, jnp.zeros((bp, POOL_PAST + 1, pw), F32), pool_w_b, pool_sc, pos0=0)

    xs = x_sample.reshape(bs * t_len, d)
    qs, qis, us, kn, vn, kin, _, _, _, wits = _proj(xs, g0, b0, wa, wb, wt, tm=512, lc=lc)
    l_all = l_past + t_len
    l_pad = -(-l_all // lc) * lc
    tq_s = LANES
    pad_keys = lambda a: jnp.pad(a, ((0, 0), (0, l_pad - l_all), (0, 0)))
    k_all = pad_keys(jnp.concatenate([cache_k[lyr], kn.reshape(bs, t_len, HEAD_DIM)], axis=1))
    v_all = pad_keys(jnp.concatenate([cache_v[lyr], vn.reshape(bs, t_len, HEAD_DIM)], axis=1))
    ki_all = pad_keys(jnp.concatenate([cache_kidx[lyr], kin.reshape(bs, t_len, IDX_DIM)], axis=1))
    vt_all = v_all.reshape(bs, l_pad // lc, lc, HEAD_DIM).transpose(0, 1, 3, 2).reshape(-1, HEAD_DIM, lc)
    vt_all = jnp.concatenate([vt_all, jnp.broadcast_to(_denominator_rows(lc), (vt_all.shape[0], VT_ROWS - HEAD_DIM, lc))],
                             axis=1)
    pad_q = lambda a: jnp.pad(a.reshape(bs, t_len, aw), ((0, 0), (0, tq_s - t_len), (0, 0)))
    wit_s = jnp.pad(wits.reshape(IDX_HEADS, bs, t_len), ((0, 0), (0, 0), (0, tq_s - t_len))).reshape(IDX_HEADS, -1)
    att_s = _dsa(pad_q(qs), pad_q(qis), wit_s, k_all.astype(BF16), ki_all.astype(BF16), vt_all.astype(BF16),
                 tq=tq_s, lc=lc, causal=False, l_valid=l_all, q_pos0=l_past, topk=min(TOPK_MAX, l_all // 4))
    att_s = att_s[:, :t_len].reshape(bs * t_len, aw)
    us3 = us.reshape(bs, t_len, pw)
    prefix_s = jnp.concatenate([jnp.zeros((bs, 1, pw), F32), state_pool[lyr]], axis=1)
    pool_s = _pool(us3, prefix_s, pool_w_b, pool_sc, pos0=l_past)

    y_p, y_s = _block_tails([(xp, att_p.reshape(bp * s_len, aw), pool_p.reshape(bp * s_len, pw)),
                             (xs, att_s, pool_s.reshape(bs * t_len, pw))], wts, tm=512, tmoe=1024)

    pool_state_p = u_p[:, s_len - POOL_PAST:]
    pool_state_s = jnp.concatenate([state_pool[lyr], us3], axis=1)[:, -POOL_PAST:]
    return (y_p.reshape(bp, s_len, d), y_s.reshape(bs, t_len, d),
            k.reshape(1, bp, s_len, HEAD_DIM), v.reshape(1, bp, s_len, HEAD_DIM),
            ki.reshape(1, bp, s_len, IDX_DIM), pool_state_p[None],
            kn.reshape(1, bs, t_len, HEAD_DIM), vn.reshape(1, bs, t_len, HEAD_DIM),
            kin.reshape(1, bs, t_len, IDX_DIM), pool_state_s[None])
```

```python
import functools

import jax
import jax.numpy as jnp
from jax import lax
from jax.experimental import pallas as pl
from jax.experimental.pallas import tpu as pltpu

F32 = jnp.float32
BF16 = jnp.bfloat16
I32 = jnp.int32

CHUNK = 64
CHUNK_SHIFT = 6
assert 1 << CHUNK_SHIFT == CHUNK
N_HEADS = 8
HEAD_DIM = 64
IDX_HEADS = 8
IDX_DIM = 64
TOPK_MAX = 256
POOL_WINDOWS = (2, 4, 8, 16)
POOL_PAST = 15
N_EXPERTS = 32
TOP_K = 4
SWIGLU_LIMIT = 7.0
SWIGLU_ALPHA = 1.702
LN_EPS = 1e-5
DEPTH = 1
DEEPNORM_ALPHA = (2 * DEPTH) ** 0.25
LOG2_E = 1.4426950408889634

LANES = 128
SUBLANES = 8
BF16_ROWS = 16
VMEM_LIMIT_BYTES = 56 * 1024 * 1024

NEG_BIG = -1e30
KEY_NEG_INF = -2139095041
KEY_POS_INF = 2139095040

NT_DIMS = (((1,), (1,)), ((), ()))


def _layer_norm(x, g, b):
    mu = jnp.mean(x, axis=-1, keepdims=True)
    xc = x - mu
    var = jnp.mean(xc * xc, axis=-1, keepdims=True)
    return xc * lax.rsqrt(var + LN_EPS) * g + b


def _params(sem):
    return pltpu.CompilerParams(dimension_semantics=sem, vmem_limit_bytes=VMEM_LIMIT_BYTES)


VT_ROWS = HEAD_DIM + BF16_ROWS


def _denominator_rows(width):
    return jnp.where(lax.broadcasted_iota(I32, (VT_ROWS - HEAD_DIM, width), 0) == 0, 1.0, 0.0)


def _proj_kernel(x_ref, g_ref, b_ref, wa_ref, wb_ref,
                 q_ref, qi_ref, u_ref, k_ref, v_ref, ki_ref, kb_ref, kib_ref, vt_ref, wit_ref, *, lc):
    h = _layer_norm(x_ref[...], g_ref[...], b_ref[...])
    hb = h.astype(BF16)
    aw = N_HEADS * HEAD_DIM
    pa = jnp.dot(hb, wa_ref[...], preferred_element_type=F32)
    q_ref[...] = (pa[:, :aw] * (HEAD_DIM ** -0.5 * LOG2_E)).astype(BF16)
    qi_ref[...] = pa[:, aw:2 * aw].astype(BF16)
    u_ref[...] = pa[:, 2 * aw:]
    pb = jnp.dot(hb, wb_ref[...], preferred_element_type=F32)
    k = pb[:, 0:HEAD_DIM]
    v = pb[:, HEAD_DIM:2 * HEAD_DIM]
    ki = pb[:, 2 * HEAD_DIM:2 * HEAD_DIM + IDX_DIM]
    k_ref[...] = k
    v_ref[...] = v
    ki_ref[...] = ki
    kb_ref[...] = k.astype(BF16)
    kib_ref[...] = ki.astype(BF16)
    pt = pb.T
    ones_rows = _denominator_rows(lc)
    for c in range(vt_ref.shape[0]):
        vt_ref[c] = jnp.concatenate([pt[HEAD_DIM:2 * HEAD_DIM, c * lc:(c + 1) * lc], ones_rows],
                                    axis=0).astype(BF16)
    wi = pt[3 * HEAD_DIM:3 * HEAD_DIM + IDX_HEADS, :]
    wit_ref[...] = (wi * (IDX_HEADS ** -0.5)) * (IDX_DIM ** -0.5)


def _proj(x2d, g, b, wa, wb, *, tm, lc):
    n, d = x2d.shape
    tm = min(tm, n)
    aw = N_HEADS * HEAD_DIM
    uw = wa.shape[1] - 2 * aw
    row = lambda i: (i, 0)
    const = lambda i: (0, 0)
    out_shape = (
        jax.ShapeDtypeStruct((n, aw), BF16),
        jax.ShapeDtypeStruct((n, aw), BF16),
        jax.ShapeDtypeStruct((n, uw), F32),
        jax.ShapeDtypeStruct((n, HEAD_DIM), F32),
        jax.ShapeDtypeStruct((n, HEAD_DIM), F32),
        jax.ShapeDtypeStruct((n, IDX_DIM), F32),
        jax.ShapeDtypeStruct((n, HEAD_DIM), BF16),
        jax.ShapeDtypeStruct((n, IDX_DIM), BF16),
        jax.ShapeDtypeStruct((n // lc, VT_ROWS, lc), BF16),
        jax.ShapeDtypeStruct((IDX_HEADS, n), F32),
    )
    out_specs = (
        pl.BlockSpec((tm, aw), row), pl.BlockSpec((tm, aw), row), pl.BlockSpec((tm, uw), row),
        pl.BlockSpec((tm, HEAD_DIM), row), pl.BlockSpec((tm, HEAD_DIM), row), pl.BlockSpec((tm, IDX_DIM), row),
        pl.BlockSpec((tm, HEAD_DIM), row), pl.BlockSpec((tm, IDX_DIM), row),
        pl.BlockSpec((tm // lc, VT_ROWS, lc), lambda i: (i, 0, 0)),
        pl.BlockSpec((IDX_HEADS, tm), lambda i: (0, i)),
    )
    return pl.pallas_call(
        functools.partial(_proj_kernel, lc=lc),
        grid=(n // tm,),
        in_specs=[pl.BlockSpec((tm, d), row), pl.BlockSpec((1, d), const), pl.BlockSpec((1, d), const),
                  pl.BlockSpec(wa.shape, const), pl.BlockSpec(wb.shape, const)],
        out_specs=out_specs,
        out_shape=out_shape,
        compiler_params=_params(("parallel",)),
        name="proj",
    )(x2d, g, b, wa, wb)


def _key_to_float(key):
    bits = jnp.where(key >= 0, key, key ^ jnp.int32(0x7FFFFFFF))
    return lax.bitcast_convert_type(bits, F32)


def _dsa_kernel(q_ref, qi_ref, wit_ref, kb_ref, kib_ref, vt_ref, o_ref,
                sc_ref, sch_ref, s_ref, out_ref, *acc_refs, tq, lc, nk_static, causal, l_valid, q_pos0, topk):
    qb = pl.program_id(1)
    nk = qb * (tq // lc) + (tq // lc) if causal else nk_static
    q_chunk = (q_pos0 + qb * tq + lax.broadcasted_iota(I32, (1, tq), 1)) >> CHUNK_SHIFT

    def chunk_loop(body, init):
        if causal:
            return lax.fori_loop(0, nk, body, init)
        return lax.fori_loop(0, nk_static, body, init, unroll=True)

    def wide_chunk_loop(body, init):
        pair = lambda i, c: body(pl.multiple_of(i * 2 * lc, 2 * lc), 2 * lc, c)
        if not causal:
            carry = lax.fori_loop(0, nk_static // 2, pair, init, unroll=True)
            return body((nk_static - 1) * lc, lc, carry) if nk_static % 2 else carry
        carry = lax.fori_loop(0, nk // 2, pair, init)
        return lax.cond(nk % 2 == 1, lambda c: body(pl.multiple_of((nk - 1) * lc, lc), lc, c), lambda c: c, carry)

    def score_span(off, rows, carry):
        kic = kib_ref[0, pl.ds(off, rows), :]
        acc = jnp.zeros((rows, tq), F32)
        for h in range(IDX_HEADS):
            s = lax.dot_general(kic, qi_ref[0, :, h * IDX_DIM:(h + 1) * IDX_DIM], NT_DIMS,
                                preferred_element_type=F32)
            acc = acc + wit_ref[h:h + 1, :] * jnp.maximum(s, 0.0)
        l_pos = off + lax.broadcasted_iota(I32, (rows, tq), 0)
        visible = ((l_pos >> CHUNK_SHIFT) <= q_chunk) & (l_pos < l_valid)
        score = jnp.where(visible, acc, -jnp.inf)
        sc_ref[pl.ds(off, rows), :] = score
        sch_ref[pl.ds(off, rows), :] = score.astype(BF16)
        return carry

    wide_chunk_loop(score_span, 0)

    def count(pred):
        ways = 4 * SUBLANES

        def body(kc, part):
            off = pl.multiple_of(kc * lc, lc)
            hit = jnp.where(pred(sc_ref[pl.ds(off, lc), :]), 1.0, 0.0)
            return part + jnp.sum(hit.reshape(lc // ways, ways, tq), axis=0)
        return jnp.sum(chunk_loop(body, jnp.zeros((ways, tq), F32)), axis=0, keepdims=True)

    def count_coarse(cand):
        ways = 2 * BF16_ROWS
        assert sc_ref.shape[0] // ways <= 256

        def body(kc, part):
            off = pl.multiple_of(kc * lc, lc)
            hit = jnp.where(sch_ref[pl.ds(off, lc), :] >= cand, jnp.ones((), BF16), jnp.zeros((), BF16))
            hit = hit.reshape(lc // ways, ways, tq)
            terms = [hit[j] for j in range(lc // ways)]
            while len(terms) > 1:
                terms = [a + b for a, b in zip(terms[::2], terms[1::2])]
            return part + terms[0]
        part = chunk_loop(body, jnp.zeros((ways, tq), BF16))
        return jnp.sum(part.astype(F32), axis=0, keepdims=True)

    def bisect(count_ge, key_to_value, lo, hi, steps):
        def step(_, carry):
            lo, hi = carry
            mid = (lo >> 1) + (hi >> 1) + (lo & hi & 1)
            ok = count_ge(key_to_value(mid)) >= topk
            return jnp.where(ok, mid, lo), jnp.where(ok, hi, mid)
        return lax.fori_loop(0, steps, step, (lo, hi))[0]

    def coarse_key_to_f32_key(k16):
        return jnp.where(k16 >= 0, k16 << 16, (k16 << 16) | 0xFFFF)

    full = lambda v: jnp.full((1, tq), v, I32)
    k16 = bisect(count_coarse, lambda k: _key_to_float(coarse_key_to_f32_key(k)).astype(BF16),
                 full(KEY_NEG_INF >> 16), full((KEY_POS_INF >> 16) + 1), 16)
    kb = coarse_key_to_f32_key(k16)
    span = 1 << 16
    lo = bisect(lambda cand: count(lambda blk: blk >= cand), _key_to_float,
                jnp.maximum(kb, KEY_NEG_INF + span) - span, jnp.minimum(kb, KEY_POS_INF + 1 - span) + span, 17)
    thr = _key_to_float(lo)
    n_above = count(lambda blk: blk > thr)
    n_ties = topk - n_above

    tri = jnp.where(lax.broadcasted_iota(I32, (lc, lc), 0) >= lax.broadcasted_iota(I32, (lc, lc), 1),
                    1.0, 0.0).astype(BF16)

    def logits_span(off, rows, carry):
        ties_before, m8 = carry
        biases = []
        for r0 in range(0, rows, lc):
            blk = sc_ref[pl.ds(off + r0, lc), :]
            tie = blk == thr
            tie_rank = (jnp.dot(tri, jnp.where(tie, 1.0, 0.0).astype(BF16), preferred_element_type=F32)
                        + ties_before)
            bias = jnp.where(blk > thr, 0.0, jnp.where(tie, jnp.where(tie_rank <= n_ties, 0.0, NEG_BIG), NEG_BIG))
            biases.append(jnp.where(blk == -jnp.inf, NEG_BIG, bias))
            ties_before = tie_rank[lc - 1:lc, :]
        bias = jnp.concatenate(biases, axis=0)
        kc_b = kb_ref[0, pl.ds(off, rows), :]
        m_rows = []
        for h in range(N_HEADS):
            s = lax.dot_general(kc_b, q_ref[0, :, h * HEAD_DIM:(h + 1) * HEAD_DIM], NT_DIMS,
                                preferred_element_type=F32) + bias
            s_ref[h, pl.ds(off, rows), :] = s
            s8 = jnp.max(s.reshape(rows // SUBLANES, SUBLANES, tq), axis=0)
            m_rows.append(jnp.maximum(m8[h * SUBLANES:(h + 1) * SUBLANES], s8))
        return ties_before, jnp.concatenate(m_rows, axis=0)

    _, m8 = wide_chunk_loop(logits_span,
                            (jnp.zeros((1, tq), F32), jnp.full((N_HEADS * SUBLANES, tq), NEG_BIG, F32)))
    m_all = jnp.max(m8.reshape(N_HEADS, SUBLANES, tq), axis=1)

    for acc_ref in acc_refs:
        acc_ref[...] = jnp.zeros(acc_ref.shape, F32)

    def pv_chunk(kc, carry):
        off = pl.multiple_of(kc * lc, lc)
        vt_c = vt_ref[kc]
        for h in range(N_HEADS):
            p = jnp.exp2(s_ref[h, pl.ds(off, lc), :] - m_all[h:h + 1, :])
            acc_refs[h][...] += jnp.dot(vt_c, p.astype(BF16), preferred_element_type=F32)
        return carry

    chunk_loop(pv_chunk, 0)

    for h in range(N_HEADS):
        acc = acc_refs[h][...]
        out_ref[h * HEAD_DIM:(h + 1) * HEAD_DIM, :] = acc[0:HEAD_DIM] / acc[HEAD_DIM:HEAD_DIM + 1]
    o_ref[0] = out_ref[...].T.astype(BF16)


def _dsa(q, qi, wit, kb, kib, vt, *, tq, lc, causal, l_valid, q_pos0, topk):
    bsz, tq_tot, aw = q.shape
    l_tot = kb.shape[1]
    nq = tq_tot // tq
    nkc = l_tot // lc
    kern = functools.partial(_dsa_kernel, tq=tq, lc=lc, nk_static=nkc, causal=causal, l_valid=l_valid,
                             q_pos0=q_pos0, topk=topk)
    return pl.pallas_call(
        kern,
        grid=(bsz, nq),
        in_specs=[
            pl.BlockSpec((1, tq, aw), lambda b, i: (b, i, 0)),
            pl.BlockSpec((1, tq, aw), lambda b, i: (b, i, 0)),
            pl.BlockSpec((IDX_HEADS, tq), lambda b, i: (0, b * nq + i)),
            pl.BlockSpec((1, l_tot, HEAD_DIM), lambda b, i: (b, 0, 0)),
            pl.BlockSpec((1, l_tot, IDX_DIM), lambda b, i: (b, 0, 0)),
            pl.BlockSpec((nkc, VT_ROWS, lc), lambda b, i: (b, 0, 0)),
        ],
        out_specs=pl.BlockSpec((1, tq, aw), lambda b, i: (b, i, 0)),
        out_shape=jax.ShapeDtypeStruct((bsz, tq_tot, aw), BF16),
        scratch_shapes=[
            pltpu.VMEM((l_tot, tq), F32),
            pltpu.VMEM((l_tot, tq), BF16),
            pltpu.VMEM((N_HEADS, l_tot, tq), F32),
            pltpu.VMEM((aw, tq), F32),
        ] + [pltpu.VMEM((VT_ROWS, tq), F32) for _ in range(N_HEADS)],
        compiler_params=_params(("parallel", "parallel")),
        name="dsa",
    )(q, qi, wit, kb, kib, vt)


def _pool_kernel(u_ref, pre_ref, pw_ref, sc_ref, o_ref, ext_ref, *, t_len, tt, pos0):
    pad = pre_ref.shape[1]
    gw = pw_ref.shape[1]
    ext_ref[0:pad, :] = pre_ref[0]
    ext_ref[pad:pad + t_len, :] = u_ref[0]
    row = lax.broadcasted_iota(I32, (tt, gw), 0)
    for t in range(t_len // tt):
        r0 = t * tt
        for g, w in enumerate(POOL_WINDOWS):
            cols = slice(g * gw, (g + 1) * gw)
            cur = ext_ref[pad + r0:pad + r0 + tt, cols]
            wsum = cur
            for j in range(1, w):
                wsum = wsum + ext_ref[pad + r0 - j:pad + r0 - j + tt, cols]
            cnt = jnp.minimum(w, pos0 + r0 + 1 + row).astype(F32)
            diff = (wsum / cnt - cur).astype(BF16)
            y = jnp.dot(diff, pw_ref[g], preferred_element_type=F32)
            o_ref[0, r0:r0 + tt, cols] = (y * sc_ref[:, cols]).astype(BF16)


def _pool(u, prefix, pool_w_b, pool_scale, *, pos0):
    bsz, t_len, c = u.shape
    pad = prefix.shape[1]
    tt = min(t_len, 256)
    return pl.pallas_call(
        functools.partial(_pool_kernel, t_len=t_len, tt=tt, pos0=pos0),
        grid=(bsz,),
        in_specs=[pl.BlockSpec((1, t_len, c), lambda b: (b, 0, 0)),
                  pl.BlockSpec((1, pad, c), lambda b: (b, 0, 0)),
                  pl.BlockSpec(pool_w_b.shape, lambda b: (0, 0, 0)),
                  pl.BlockSpec((1, c), lambda b: (0, 0))],
        out_specs=pl.BlockSpec((1, t_len, c), lambda b: (b, 0, 0)),
        out_shape=jax.ShapeDtypeStruct((bsz, t_len, c), BF16),
        scratch_shapes=[pltpu.VMEM((pad + t_len, c), F32)],
        compiler_params=_params(("parallel",)),
        name="pool",
    )(u, prefix, pool_w_b, pool_scale)


ROW_ALIGN = 16


def _tail_kernel(x_ref, att_ref, pool_ref, g0_ref, b0_ref, woa_ref, wob_ref, g1_ref, b1_ref,
                 wrh_ref, wrl_ref, br_ref,
                 carry0_ref, h1_ref, cpos_ref, gw_ref, seg_ref, len_ref, dst_ref, used_ref, carry_ref, *, tm, sub):
    @pl.when(pl.program_id(0) == 0)
    def _():
        carry_ref[...] = carry0_ref[...]

    chunk_lens = []
    for s in range(sub):
        rows = slice(s * tm, (s + 1) * tm)
        chunk_lens.append(_tail_tile(x_ref[rows, :], att_ref[rows, :], pool_ref[rows, :], g0_ref, b0_ref, woa_ref,
                                     wob_ref, g1_ref, b1_ref, wrh_ref, wrl_ref, br_ref,
                                     h1_ref.at[rows, :], cpos_ref.at[:, rows], gw_ref.at[:, rows], seg_ref.at[s],
                                     len_ref.at[s], tm=tm))
    carry = carry_ref[...]
    for s in range(sub):
        dst_ref[s] = carry
        carry = carry + chunk_lens[s]
    carry_ref[...] = carry
    used_ref[...] = carry


def _tail_tile(x, att, pool, g0_ref, b0_ref, woa_ref, wob_ref, g1_ref, b1_ref, wrh_ref, wrl_ref, br_ref,
               h1_ref, cpos_ref, gw_ref, seg_ref, len_ref, *, tm):
    h = _layer_norm(x, g0_ref[...], b0_ref[...])
    mix = (jnp.dot(att, woa_ref[...], preferred_element_type=F32)
           + jnp.dot(pool, wob_ref[...], preferred_element_type=F32))
    h1 = _layer_norm(DEEPNORM_ALPHA * h + mix, g1_ref[...], b1_ref[...])
    h1_ref[...] = h1

    hh = h1.astype(BF16)
    hl = (h1 - hh.astype(F32)).astype(BF16)
    logits = (lax.dot_general(wrh_ref[...], hh, NT_DIMS, preferred_element_type=F32)
              + lax.dot_general(wrh_ref[...], hl, NT_DIMS, preferred_element_type=F32)
              + lax.dot_general(wrl_ref[...], hh, NT_DIMS, preferred_element_type=F32)
              + br_ref[...])

    e_iota = lax.broadcasted_iota(I32, (N_EXPERTS, tm), 0).astype(F32)
    work = logits
    vals, hots = [], []
    for k in range(TOP_K):
        m = jnp.max(work, axis=0, keepdims=True)
        idx = jnp.min(jnp.where(work == m, e_iota, float(N_EXPERTS)), axis=0, keepdims=True)
        hot = e_iota == idx
        vals.append(m)
        hots.append(jnp.where(hot, 1.0, 0.0))
        work = jnp.where(hot, -jnp.inf, work)
    exps = [jnp.exp(v - vals[0]) for v in vals]
    den = exps[0] + exps[1] + exps[2] + exps[3]
    for k in range(TOP_K):
        gw_ref[k:k + 1, :] = exps[k] / den

    hot_all = (hots[0] + hots[1] + hots[2] + hots[3]).astype(BF16)
    count = jnp.dot(hot_all, jnp.ones((tm, LANES), BF16), preferred_element_type=F32)
    chunk_len = jnp.ceil(count * (1.0 / ROW_ALIGN)) * ROW_ALIGN
    lower = jnp.where(lax.broadcasted_iota(I32, (N_EXPERTS, N_EXPERTS), 1)
                      < lax.broadcasted_iota(I32, (N_EXPERTS, N_EXPERTS), 0), 1.0, 0.0).astype(BF16)
    seg_base = jnp.dot(lower, chunk_len.astype(BF16), preferred_element_type=F32)
    before = jnp.where(lax.broadcasted_iota(I32, (tm, tm), 0) < lax.broadcasted_iota(I32, (tm, tm), 1),
                       1.0, 0.0).astype(BF16)
    slot = jnp.dot(hot_all, before, preferred_element_type=F32) + jnp.tile(seg_base, (1, tm // LANES))
    for k in range(TOP_K):
        cpos_ref[k:k + 1, :] = jnp.sum(hots[k] * slot, axis=0, keepdims=True).astype(I32)
    seg_ref[...] = seg_base
    len_ref[...] = chunk_len
    return chunk_len


def _tail(x2d, att, pool, g0, b0, woa, wob, g1, b1, wrh, wrl, br, carry0, *, tm):
    n, d = x2d.shape
    aw = att.shape[1]
    nt = n // tm
    sub = 2 if nt % 2 == 0 else 1
    ts = sub * tm
    row = lambda i: (i, 0)
    const = lambda i: (0, 0)
    col = lambda i: (0, i)
    tab = lambda i: (i, 0, 0)
    tab_shape = jax.ShapeDtypeStruct((nt, N_EXPERTS, LANES), F32)
    tab_spec = pl.BlockSpec((sub, N_EXPERTS, LANES), tab)
    return pl.pallas_call(
        functools.partial(_tail_kernel, tm=tm, sub=sub),
        grid=(nt // sub,),
        in_specs=[pl.BlockSpec((ts, d), row), pl.BlockSpec((ts, aw), row), pl.BlockSpec((ts, d - aw), row),
                  pl.BlockSpec((1, d), const), pl.BlockSpec((1, d), const),
                  pl.BlockSpec(woa.shape, const), pl.BlockSpec(wob.shape, const),
                  pl.BlockSpec((1, d), const), pl.BlockSpec((1, d), const),
                  pl.BlockSpec(wrh.shape, const), pl.BlockSpec(wrl.shape, const), pl.BlockSpec(br.shape, const),
                  pl.BlockSpec((N_EXPERTS, LANES), const)],
        out_specs=(pl.BlockSpec((ts, d), row), pl.BlockSpec((TOP_K, ts), col), pl.BlockSpec((TOP_K, ts), col),
                   tab_spec, tab_spec, tab_spec, pl.BlockSpec((N_EXPERTS, LANES), const)),
        out_shape=(jax.ShapeDtypeStruct((n, d), F32), jax.ShapeDtypeStruct((TOP_K, n), I32),
                   jax.ShapeDtypeStruct((TOP_K, n), F32), tab_shape, tab_shape, tab_shape,
                   jax.ShapeDtypeStruct((N_EXPERTS, LANES), F32)),
        scratch_shapes=[pltpu.VMEM((N_EXPERTS, LANES), F32)],
        compiler_params=_params(("arbitrary",)),
        name="tail",
    )(x2d, att, pool, g0, b0, woa, wob, g1, b1, wrh, wrl, br, carry0)


G_BLOCK = 256


def _stage_rows(tm):
    raw = TOP_K * tm + (ROW_ALIGN - 1) * N_EXPERTS
    return -(-raw // G_BLOCK) * G_BLOCK


def _chunk_copies(seg_ref, len_ref, dst_ref, tile, make_copy, act):
    def per_expert(e, carry):
        idx = tile * N_EXPERTS + e
        seg, ln, dst = seg_ref[idx], len_ref[idx], dst_ref[idx]

        @pl.when(ln > 0)
        def _():
            act(make_copy(pl.multiple_of(seg, ROW_ALIGN), pl.multiple_of(dst, ROW_ALIGN),
                          pl.multiple_of(ln, ROW_ALIGN)))
        return carry

    lax.fori_loop(0, N_EXPERTS, per_expert, 0)


def _by_staged_blocks(seg_ref, len_ref, tile, n_full, body):
    last = tile * N_EXPERTS + N_EXPERTS - 1
    total = seg_ref[last] + len_ref[last]
    pl.when(total <= (n_full - 1) * G_BLOCK)(lambda: body(n_full - 1))
    pl.when(total > (n_full - 1) * G_BLOCK)(lambda: body(n_full))


def _route_onehot(iota, pos_list, val_list):
    g = jnp.zeros(iota.shape, F32)
    for pos, val in zip(pos_list, val_list):
        g = jnp.where(iota == pos, val, g)
    return g.astype(BF16)


def _dispatch_kernel(seg_ref, len_ref, dst_ref, h1_ref, cpos_ref, *rest, tm):
    xs_hbm, stage_ref, sem = rest[-3:]
    tile = pl.program_id(0)
    slot = tile % 2
    h1b = h1_ref[...].astype(BF16)
    pos_rows = [cpos_ref[k:k + 1, :] for k in range(TOP_K)]

    def stage_blocks(n_blk):
        for blk in range(n_blk):
            iota = blk * G_BLOCK + lax.broadcasted_iota(I32, (G_BLOCK, tm), 0)
            g = _route_onehot(iota, pos_rows, [1.0] * TOP_K)
            stage_ref[slot, blk * G_BLOCK:(blk + 1) * G_BLOCK, :] = jnp.dot(
                g, h1b, preferred_element_type=F32).astype(BF16)

    _by_staged_blocks(seg_ref, len_ref, tile, stage_ref.shape[1] // G_BLOCK, stage_blocks)

    def copy_from(buf):
        def make_copy(seg, dst, size):
            return pltpu.make_async_copy(stage_ref.at[buf, pl.ds(seg, size), :], xs_hbm.at[pl.ds(dst, size), :],
                                         sem.at[buf])
        return make_copy

    _chunk_copies(seg_ref, len_ref, dst_ref, tile, copy_from(slot), lambda c: c.start())

    @pl.when(tile > 0)
    def _():
        _chunk_copies(seg_ref, len_ref, dst_ref, tile - 1, copy_from(1 - slot), lambda c: c.wait())

    @pl.when(tile == pl.num_programs(0) - 1)
    def _():
        _chunk_copies(seg_ref, len_ref, dst_ref, tile, copy_from(slot), lambda c: c.wait())


def _dispatch(seg, ln, dst, h1, cpos, xs_prev, *, tm, p_rows):
    n, d = h1.shape
    in_specs = [pl.BlockSpec((tm, d), lambda i, *_: (i, 0)), pl.BlockSpec((TOP_K, tm), lambda i, *_: (0, i))]
    operands = [seg, ln, dst, h1, cpos]
    aliases = {}
    if xs_prev is not None:
        in_specs.append(pl.BlockSpec(memory_space=pl.ANY))
        aliases = {len(operands): 0}
        operands.append(xs_prev)
    return pl.pallas_call(
        functools.partial(_dispatch_kernel, tm=tm),
        grid_spec=pltpu.PrefetchScalarGridSpec(
            num_scalar_prefetch=3,
            grid=(n // tm,),
            in_specs=in_specs,
            out_specs=pl.BlockSpec(memory_space=pl.ANY),
            scratch_shapes=[pltpu.VMEM((2, _stage_rows(tm), d), BF16), pltpu.SemaphoreType.DMA((2,))],
        ),
        out_shape=jax.ShapeDtypeStruct((p_rows, d), BF16),
        input_output_aliases=aliases,
        compiler_params=_params(("arbitrary",)),
        name="dispatch",
    )(*operands)


FF_CHUNK = 256


def _moe_kernel(te_ref, tv_ref, x_ref, w1_ref, b1_ref, w2_ref, b2_ref, y_ref, w1b_ref, w2b_ref, act_ref):
    i = pl.program_id(0)
    d_ff = w2_ref.shape[1]
    valid = tv_ref[i]

    @pl.when((i == 0) | (te_ref[i] != te_ref[jnp.maximum(i - 1, 0)]))
    def _():
        w1b_ref[...] = w1_ref[0].astype(BF16)
        w2b_ref[...] = w2_ref[0].astype(BF16)

    tmoe, d = x_ref.shape

    def ffn(n_rows):
        rows = lax.broadcasted_iota(I32, (n_rows, d), 0)
        xb = jnp.where(rows < valid, x_ref[0:n_rows, :].astype(F32), 0.0).astype(BF16)
        for j in range(d_ff // FF_CHUNK):
            gs = slice(j * FF_CHUNK, (j + 1) * FF_CHUNK)
            us = slice(d_ff + j * FF_CHUNK, d_ff + (j + 1) * FF_CHUNK)
            gate = jnp.dot(xb, w1b_ref[:, gs], preferred_element_type=F32) + b1_ref[0, :, gs]
            up = jnp.dot(xb, w1b_ref[:, us], preferred_element_type=F32) + b1_ref[0, :, us]
            gate = jnp.minimum(gate, SWIGLU_LIMIT)
            up = jnp.clip(up, -SWIGLU_LIMIT, SWIGLU_LIMIT)
            act_ref[0:n_rows, gs] = ((up + 1.0) * (gate * jax.nn.sigmoid(SWIGLU_ALPHA * gate))).astype(BF16)
        y = jnp.dot(act_ref[0:n_rows, :], w2b_ref[...], preferred_element_type=F32) + b2_ref[0]
        y_ref[0:n_rows, :] = y.astype(BF16)
        if n_rows < tmoe:
            y_ref[n_rows:, :] = jnp.zeros((tmoe - n_rows, d), BF16)

    half = tmoe // 2
    pl.when(valid > half)(lambda: ffn(tmoe))
    pl.when((valid > 0) & (valid <= half))(lambda: ffn(half))

    @pl.when(valid <= 0)
    def _():
        y_ref[...] = jnp.zeros(y_ref.shape, BF16)


def _moe(tile_expert, tile_valid, xs, w1, b1, w2, b2, *, tmoe):
    p, d = xs.shape
    e, _, ff2 = w1.shape
    d_ff = ff2 // 2
    return pl.pallas_call(
        _moe_kernel,
        grid_spec=pltpu.PrefetchScalarGridSpec(
            num_scalar_prefetch=2,
            grid=(p // tmoe,),
            in_specs=[pl.BlockSpec((tmoe, d), lambda i, te, tv: (i, 0)),
                      pl.BlockSpec((1, d, ff2), lambda i, te, tv: (te[i], 0, 0)),
                      pl.BlockSpec((1, 1, ff2), lambda i, te, tv: (te[i], 0, 0)),
                      pl.BlockSpec((1, d_ff, d), lambda i, te, tv: (te[i], 0, 0)),
                      pl.BlockSpec((1, 1, d), lambda i, te, tv: (te[i], 0, 0))],
            out_specs=pl.BlockSpec((tmoe, d), lambda i, te, tv: (i, 0)),
            scratch_shapes=[pltpu.VMEM((d, ff2), BF16), pltpu.VMEM((d_ff, d), BF16), pltpu.VMEM((tmoe, d_ff), BF16)],
        ),
        out_shape=jax.ShapeDtypeStruct((p, d), BF16),
        compiler_params=_params(("arbitrary",)),
        name="moe",
    )(tile_expert, tile_valid, xs, w1, b1.reshape(e, 1, ff2), w2, b2.reshape(e, 1, d))


def _combine_kernel(seg_ref, len_ref, dst_ref, h1_ref, cpos_ref, gw_ref, g2_ref, b2_ref, ys_hbm, o_ref,
                    stage_ref, gate_ref, sem, *, tm):
    tile = pl.program_id(0)
    slot = tile % 2

    def copy_into(buf):
        def make_copy(seg, dst, size):
            return pltpu.make_async_copy(ys_hbm.at[pl.ds(dst, size), :], stage_ref.at[buf, pl.ds(seg, size), :],
                                         sem.at[buf])
        return make_copy

    @pl.when(tile == 0)
    def _():
        stage_ref[...] = jnp.zeros(stage_ref.shape, BF16)
        _chunk_copies(seg_ref, len_ref, dst_ref, tile, copy_into(slot), lambda c: c.start())

    @pl.when(tile + 1 < pl.num_programs(0))
    def _():
        _chunk_copies(seg_ref, len_ref, dst_ref, tile + 1, copy_into(1 - slot), lambda c: c.start())

    cpos = cpos_ref[...]
    gw = gw_ref[...]
    pos_cols = [cpos[:, k:k + 1] for k in range(TOP_K)]
    gw_cols = [gw[:, k:k + 1] for k in range(TOP_K)]

    def finish(n_blk):
        _chunk_copies(seg_ref, len_ref, dst_ref, tile, copy_into(slot), lambda c: c.wait())
        m = None
        for blk in range(n_blk):
            iota = blk * G_BLOCK + lax.broadcasted_iota(I32, (tm, G_BLOCK), 1)
            gate = _route_onehot(iota, pos_cols, gw_cols)
            part = jnp.dot(gate, stage_ref[slot, blk * G_BLOCK:(blk + 1) * G_BLOCK, :], preferred_element_type=F32)
            m = part if m is None else part + m
        o_ref[...] = _layer_norm(DEEPNORM_ALPHA * h1_ref[...] + m, g2_ref[...], b2_ref[...])

    _by_staged_blocks(seg_ref, len_ref, tile, stage_ref.shape[1] // G_BLOCK, finish)


def _combine(seg, ln, dst, h1, cpos_nt, gw_nt, g2, b2, ys, *, tm):
    n, d = h1.shape
    return pl.pallas_call(
        functools.partial(_combine_kernel, tm=tm),
        grid_spec=pltpu.PrefetchScalarGridSpec(
            num_scalar_prefetch=3,
            grid=(n // tm,),
            in_specs=[pl.BlockSpec((tm, d), lambda i, *_: (i, 0)),
                      pl.BlockSpec((tm, TOP_K), lambda i, *_: (i, 0)), pl.BlockSpec((tm, TOP_K), lambda i, *_: (i, 0)),
                      pl.BlockSpec((1, d), lambda i, *_: (0, 0)), pl.BlockSpec((1, d), lambda i, *_: (0, 0)),
                      pl.BlockSpec(memory_space=pl.ANY)],
            out_specs=pl.BlockSpec((tm, d), lambda i, *_: (i, 0)),
            scratch_shapes=[pltpu.VMEM((2, _stage_rows(tm), d), BF16), pltpu.VMEM((tm, _stage_rows(tm)), BF16),
                            pltpu.SemaphoreType.DMA((2,))],
        ),
        out_shape=jax.ShapeDtypeStruct((n, d), F32),
        compiler_params=_params(("arbitrary",)),
        name="combine",
    )(seg, ln, dst, h1, cpos_nt, gw_nt, g2, b2, ys)


def _block_tails(groups, wts, *, tm, tmoe):
    carry = jnp.zeros((N_EXPERTS, LANES), F32)
    plans, max_rows = [], 0
    for x2d, att, pool in groups:
        n = x2d.shape[0]
        tm_g = min(tm, n)
        h1, cpos, gw_t, seg_t, len_t, dst_t, carry = _tail(
            x2d, att, pool, wts["g0"], wts["b0"], wts["woa"], wts["wob"], wts["g1"], wts["b1"],
            wts["wrh"], wts["wrl"], wts["br"], carry, tm=tm_g)
        plans.append((tm_g, h1, cpos, gw_t, seg_t, len_t, dst_t))
        max_rows += TOP_K * n + (ROW_ALIGN - 1) * N_EXPERTS * (n // tm_g)
    used = carry[:, 0].astype(I32)
    cap = ((used + tmoe - 1) // tmoe) * tmoe
    ends = jnp.cumsum(cap)
    offs = ends - cap
    n_tiles = -(-max_rows // tmoe) + N_EXPERTS
    tile_start = jnp.arange(n_tiles, dtype=I32) * tmoe
    tile_expert = jnp.minimum(jnp.sum((ends[None, :] <= tile_start[:, None]).astype(I32), axis=1), N_EXPERTS - 1)
    tile_valid = jnp.clip(offs[tile_expert] + used[tile_expert] - tile_start, 0, tmoe).astype(I32)
    tile_valid = jnp.where(tile_start < ends[-1], tile_valid, 0)

    xs, tables = None, []
    for tm_g, h1, cpos, gw_t, seg_t, len_t, dst_t in plans:
        seg = seg_t[:, :, 0].astype(I32).reshape(-1)
        ln = len_t[:, :, 0].astype(I32).reshape(-1)
        dst = (dst_t[:, :, 0].astype(I32) + offs[None, :]).reshape(-1)
        tables.append((seg, ln, dst))
        xs = _dispatch(seg, ln, dst, h1, cpos, xs, tm=tm_g, p_rows=n_tiles * tmoe)
    ys = _moe(tile_expert, tile_valid, xs, wts["w1e"], wts["b1e"], wts["w2e"], wts["b2e"], tmoe=tmoe)
    return [_combine(seg, ln, dst, h1, cpos.T, gw_t.T, wts["g2"], wts["b2"], ys, tm=tm_g)
            for (seg, ln, dst), (tm_g, h1, cpos, gw_t, _, _, _) in zip(tables, plans)]


def kernel(x_prompt, x_sample, cache_k, cache_v, cache_kidx, state_pool, ln0_g, ln0_b, w_in, w_o,
           pool_w, pool_scale, ln1_g, ln1_b, w_router, b_router, w1, b1, w2, b2, ln2_g, ln2_b):
    bp, s_len, d = x_prompt.shape
    bs, t_len, _ = x_sample.shape
    l_past = cache_k.shape[2]
    aw = N_HEADS * HEAD_DIM
    pw = d - aw
    lyr = 0
    lc = 256

    k_off = aw
    v_off = k_off + HEAD_DIM
    qi_off = v_off + HEAD_DIM
    ki_off = qi_off + IDX_HEADS * IDX_DIM
    wi_off = ki_off + IDX_DIM
    u_off = wi_off + IDX_HEADS
    win = w_in[lyr]
    wa = jnp.concatenate([win[:, 0:k_off], win[:, qi_off:ki_off], win[:, u_off:u_off + pw]], axis=1).astype(BF16)
    wb = jnp.concatenate([win[:, k_off:v_off], win[:, v_off:qi_off], win[:, ki_off:wi_off], win[:, wi_off:u_off],
                          jnp.zeros((d, HEAD_DIM - IDX_HEADS), F32)], axis=1).astype(BF16)

    g0 = ln0_g.reshape(1, d)
    b0 = ln0_b.reshape(1, d)
    wrt = w_router[lyr].T
    wrh = wrt.astype(BF16)
    wts = dict(
        g0=g0, b0=b0,
        woa=w_o[lyr][:aw].astype(BF16), wob=w_o[lyr][aw:].astype(BF16),
        g1=ln1_g[lyr].reshape(1, d), b1=ln1_b[lyr].reshape(1, d),
        wrh=wrh, wrl=(wrt - wrh.astype(F32)).astype(BF16), br=b_router[lyr].reshape(N_EXPERTS, 1),
        w1e=w1[lyr], b1e=b1[lyr], w2e=w2[lyr], b2e=b2[lyr],
        g2=ln2_g[lyr].reshape(1, d), b2=ln2_b[lyr].reshape(1, d),
    )
    pool_w_b = pool_w[lyr].astype(BF16)
    pool_sc = pool_scale[lyr].reshape(1, pw)

    xp = x_prompt.reshape(bp * s_len, d)
    q, qi, u, k, v, ki, kb, kib, vt, wit = _proj(xp, g0, b0, wa, wb, tm=512, lc=lc)
    att_p = _dsa(q.reshape(bp, s_len, aw), qi.reshape(bp, s_len, aw), wit,
                 kb.reshape(bp, s_len, HEAD_DIM), kib.reshape(bp, s_len, IDX_DIM), vt,
                 tq=256, lc=lc, causal=True, l_valid=s_len, q_pos0=0, topk=min(TOPK_MAX, s_len // 4))
    u_p = u.reshape(bp, s_len, pw)
    pool_p = _pool(u_p, jnp.zeros((bp, POOL_PAST + 1, pw), F32), pool_w_b, pool_sc, pos0=0)

    xs = x_sample.reshape(bs * t_len, d)
    qs, qis, us, kn, vn, kin, _, _, _, wits = _proj(xs, g0, b0, wa, wb, tm=512, lc=lc)
    l_all = l_past + t_len
    l_pad = -(-l_all // lc) * lc
    tq_s = LANES
    pad_keys = lambda a: jnp.pad(a, ((0, 0), (0, l_pad - l_all), (0, 0)))
    k_all = pad_keys(jnp.concatenate([cache_k[lyr], kn.reshape(bs, t_len, HEAD_DIM)], axis=1))
    v_all = pad_keys(jnp.concatenate([cache_v[lyr], vn.reshape(bs, t_len, HEAD_DIM)], axis=1))
    ki_all = pad_keys(jnp.concatenate([cache_kidx[lyr], kin.reshape(bs, t_len, IDX_DIM)], axis=1))
    vt_all = v_all.reshape(bs, l_pad // lc, lc, HEAD_DIM).transpose(0, 1, 3, 2).reshape(-1, HEAD_DIM, lc)
    vt_all = jnp.concatenate([vt_all, jnp.broadcast_to(_denominator_rows(lc), (vt_all.shape[0], VT_ROWS - HEAD_DIM, lc))],
                             axis=1)
    pad_q = lambda a: jnp.pad(a.reshape(bs, t_len, aw), ((0, 0), (0, tq_s - t_len), (0, 0)))
    wit_s = jnp.pad(wits.reshape(IDX_HEADS, bs, t_len), ((0, 0), (0, 0), (0, tq_s - t_len))).reshape(IDX_HEADS, -1)
    att_s = _dsa(pad_q(qs), pad_q(qis), wit_s, k_all.astype(BF16), ki_all.astype(BF16), vt_all.astype(BF16),
                 tq=tq_s, lc=lc, causal=False, l_valid=l_all, q_pos0=l_past, topk=min(TOPK_MAX, l_all // 4))
    att_s = att_s[:, :t_len].reshape(bs * t_len, aw)
    us3 = us.reshape(bs, t_len, pw)
    prefix_s = jnp.concatenate([jnp.zeros((bs, 1, pw), F32), state_pool[lyr]], axis=1)
    pool_s = _pool(us3, prefix_s, pool_w_b, pool_sc, pos0=l_past)

    y_p, y_s = _block_tails([(xp, att_p.reshape(bp * s_len, aw), pool_p.reshape(bp * s_len, pw)),
                             (xs, att_s, pool_s.reshape(bs * t_len, pw))], wts, tm=512, tmoe=1024)

    pool_state_p = u_p[:, s_len - POOL_PAST:]
    pool_state_s = jnp.concatenate([state_pool[lyr], us3], axis=1)[:, -POOL_PAST:]
    return (y_p.reshape(bp, s_len, d), y_s.reshape(bs, t_len, d),
            k.reshape(1, bp, s_len, HEAD_DIM), v.reshape(1, bp, s_len, HEAD_DIM),
            ki.reshape(1, bp, s_len, IDX_DIM), pool_state_p[None],
            kn.reshape(1, bs, t_len, HEAD_DIM), vn.reshape(1, bs, t_len, HEAD_DIM),
            kin.reshape(1, bs, t_len, IDX_DIM), pool_state_s[None])
```

```python
import functools

import jax
import jax.numpy as jnp
from jax import lax
from jax.experimental import pallas as pl
from jax.experimental.pallas import tpu as pltpu

F32 = jnp.float32
BF16 = jnp.bfloat16
I32 = jnp.int32

CHUNK = 64
CHUNK_SHIFT = 6
assert 1 << CHUNK_SHIFT == CHUNK
N_HEADS = 8
HEAD_DIM = 64
IDX_HEADS = 8
IDX_DIM = 64
TOPK_MAX = 256
POOL_WINDOWS = (2, 4, 8, 16)
POOL_PAST = 15
N_EXPERTS = 32
TOP_K = 4
SWIGLU_LIMIT = 7.0
SWIGLU_ALPHA = 1.702
LN_EPS = 1e-5
DEPTH = 1
DEEPNORM_ALPHA = (2 * DEPTH) ** 0.25
LOG2_E = 1.4426950408889634

LANES = 128
SUBLANES = 8
BF16_ROWS = 16
VMEM_LIMIT_BYTES = 56 * 1024 * 1024

NEG_BIG = -1e30
KEY_NEG_INF = -2139095041
KEY_POS_INF = 2139095040

NT_DIMS = (((1,), (1,)), ((), ()))


def _layer_norm(x, g, b):
    mu = jnp.mean(x, axis=-1, keepdims=True)
    xc = x - mu
    var = jnp.mean(xc * xc, axis=-1, keepdims=True)
    return xc * lax.rsqrt(var + LN_EPS) * g + b


def _params(sem):
    return pltpu.CompilerParams(dimension_semantics=sem, vmem_limit_bytes=VMEM_LIMIT_BYTES)


VT_ROWS = HEAD_DIM + BF16_ROWS


def _denominator_rows(width):
    return jnp.where(lax.broadcasted_iota(I32, (VT_ROWS - HEAD_DIM, width), 0) == 0, 1.0, 0.0)


def _proj_kernel(x_ref, g_ref, b_ref, wa_ref, wb_ref,
                 q_ref, qi_ref, u_ref, k_ref, v_ref, ki_ref, kb_ref, kib_ref, vt_ref, wit_ref, *, lc):
    h = _layer_norm(x_ref[...], g_ref[...], b_ref[...])
    hb = h.astype(BF16)
    aw = N_HEADS * HEAD_DIM
    pa = jnp.dot(hb, wa_ref[...], preferred_element_type=F32)
    q_ref[...] = (pa[:, :aw] * (HEAD_DIM ** -0.5 * LOG2_E)).astype(BF16)
    qi_ref[...] = pa[:, aw:2 * aw].astype(BF16)
    u_ref[...] = pa[:, 2 * aw:]
    pb = jnp.dot(hb, wb_ref[...], preferred_element_type=F32)
    k = pb[:, 0:HEAD_DIM]
    v = pb[:, HEAD_DIM:2 * HEAD_DIM]
    ki = pb[:, 2 * HEAD_DIM:2 * HEAD_DIM + IDX_DIM]
    k_ref[...] = k
    v_ref[...] = v
    ki_ref[...] = ki
    kb_ref[...] = k.astype(BF16)
    kib_ref[...] = ki.astype(BF16)
    pt = pb.T
    ones_rows = _denominator_rows(lc)
    for c in range(vt_ref.shape[0]):
        vt_ref[c] = jnp.concatenate([pt[HEAD_DIM:2 * HEAD_DIM, c * lc:(c + 1) * lc], ones_rows],
                                    axis=0).astype(BF16)
    wi = pt[3 * HEAD_DIM:3 * HEAD_DIM + IDX_HEADS, :]
    wit_ref[...] = (wi * (IDX_HEADS ** -0.5)) * (IDX_DIM ** -0.5)


def _proj(x2d, g, b, wa, wb, *, tm, lc):
    n, d = x2d.shape
    tm = min(tm, n)
    aw = N_HEADS * HEAD_DIM
    uw = wa.shape[1] - 2 * aw
    row = lambda i: (i, 0)
    const = lambda i: (0, 0)
    out_shape = (
        jax.ShapeDtypeStruct((n, aw), BF16),
        jax.ShapeDtypeStruct((n, aw), BF16),
        jax.ShapeDtypeStruct((n, uw), F32),
        jax.ShapeDtypeStruct((n, HEAD_DIM), F32),
        jax.ShapeDtypeStruct((n, HEAD_DIM), F32),
        jax.ShapeDtypeStruct((n, IDX_DIM), F32),
        jax.ShapeDtypeStruct((n, HEAD_DIM), BF16),
        jax.ShapeDtypeStruct((n, IDX_DIM), BF16),
        jax.ShapeDtypeStruct((n // lc, VT_ROWS, lc), BF16),
        jax.ShapeDtypeStruct((IDX_HEADS, n), F32),
    )
    out_specs = (
        pl.BlockSpec((tm, aw), row), pl.BlockSpec((tm, aw), row), pl.BlockSpec((tm, uw), row),
        pl.BlockSpec((tm, HEAD_DIM), row), pl.BlockSpec((tm, HEAD_DIM), row), pl.BlockSpec((tm, IDX_DIM), row),
        pl.BlockSpec((tm, HEAD_DIM), row), pl.BlockSpec((tm, IDX_DIM), row),
        pl.BlockSpec((tm // lc, VT_ROWS, lc), lambda i: (i, 0, 0)),
        pl.BlockSpec((IDX_HEADS, tm), lambda i: (0, i)),
    )
    return pl.pallas_call(
        functools.partial(_proj_kernel, lc=lc),
        grid=(n // tm,),
        in_specs=[pl.BlockSpec((tm, d), row), pl.BlockSpec((1, d), const), pl.BlockSpec((1, d), const),
                  pl.BlockSpec(wa.shape, const), pl.BlockSpec(wb.shape, const)],
        out_specs=out_specs,
        out_shape=out_shape,
        compiler_params=_params(("parallel",)),
        name="proj",
    )(x2d, g, b, wa, wb)


def _key_to_float(key):
    bits = jnp.where(key >= 0, key, key ^ jnp.int32(0x7FFFFFFF))
    return lax.bitcast_convert_type(bits, F32)


def _dsa_kernel(q_ref, qi_ref, wit_ref, kb_ref, kib_ref, vt_ref, o_ref,
                sc_ref, sch_ref, s_ref, out_ref, *acc_refs, tq, lc, nk_static, causal, l_valid, q_pos0, topk):
    qb = pl.program_id(1)
    nk = qb * (tq // lc) + (tq // lc) if causal else nk_static
    q_chunk = (q_pos0 + qb * tq + lax.broadcasted_iota(I32, (1, tq), 1)) >> CHUNK_SHIFT

    def chunk_loop(body, init):
        if causal:
            return lax.fori_loop(0, nk, body, init)
        return lax.fori_loop(0, nk_static, body, init, unroll=True)

    def wide_chunk_loop(body, init):
        pair = lambda i, c: body(pl.multiple_of(i * 2 * lc, 2 * lc), 2 * lc, c)
        if not causal:
            carry = lax.fori_loop(0, nk_static // 2, pair, init, unroll=True)
            return body((nk_static - 1) * lc, lc, carry) if nk_static % 2 else carry
        carry = lax.fori_loop(0, nk // 2, pair, init)
        return lax.cond(nk % 2 == 1, lambda c: body(pl.multiple_of((nk - 1) * lc, lc), lc, c), lambda c: c, carry)

    def score_span(off, rows, carry):
        kic = kib_ref[0, pl.ds(off, rows), :]
        acc = jnp.zeros((rows, tq), F32)
        for h in range(IDX_HEADS):
            s = lax.dot_general(kic, qi_ref[0, :, h * IDX_DIM:(h + 1) * IDX_DIM], NT_DIMS,
                                preferred_element_type=F32)
            acc = acc + wit_ref[h:h + 1, :] * jnp.maximum(s, 0.0)
        l_pos = off + lax.broadcasted_iota(I32, (rows, tq), 0)
        visible = ((l_pos >> CHUNK_SHIFT) <= q_chunk) & (l_pos < l_valid)
        score = jnp.where(visible, acc, -jnp.inf)
        sc_ref[pl.ds(off, rows), :] = score
        sch_ref[pl.ds(off, rows), :] = score.astype(BF16)
        return carry

    wide_chunk_loop(score_span, 0)

    def count(pred):
        ways = 4 * SUBLANES

        def body(kc, part):
            off = pl.multiple_of(kc * lc, lc)
            hit = jnp.where(pred(sc_ref[pl.ds(off, lc), :]), 1.0, 0.0)
            return part + jnp.sum(hit.reshape(lc // ways, ways, tq), axis=0)
        return jnp.sum(chunk_loop(body, jnp.zeros((ways, tq), F32)), axis=0, keepdims=True)

    def count_coarse(cand):
        ways = 2 * BF16_ROWS
        assert sc_ref.shape[0] // ways <= 256

        def body(kc, part):
            off = pl.multiple_of(kc * lc, lc)
            hit = jnp.where(sch_ref[pl.ds(off, lc), :] >= cand, jnp.ones((), BF16), jnp.zeros((), BF16))
            hit = hit.reshape(lc // ways, ways, tq)
            terms = [hit[j] for j in range(lc // ways)]
            while len(terms) > 1:
                terms = [a + b for a, b in zip(terms[::2], terms[1::2])]
            return part + terms[0]
        part = chunk_loop(body, jnp.zeros((ways, tq), BF16))
        return jnp.sum(part.astype(F32), axis=0, keepdims=True)

    def bisect(count_ge, key_to_value, lo, hi, steps):
        def step(_, carry):
            lo, hi = carry
            mid = (lo >> 1) + (hi >> 1) + (lo & hi & 1)
            ok = count_ge(key_to_value(mid)) >= topk
            return jnp.where(ok, mid, lo), jnp.where(ok, hi, mid)
        return lax.fori_loop(0, steps, step, (lo, hi))[0]

    def coarse_key_to_f32_key(k16):
        return jnp.where(k16 >= 0, k16 << 16, (k16 << 16) | 0xFFFF)

    full = lambda v: jnp.full((1, tq), v, I32)
    k16 = bisect(count_coarse, lambda k: _key_to_float(coarse_key_to_f32_key(k)).astype(BF16),
                 full(KEY_NEG_INF >> 16), full((KEY_POS_INF >> 16) + 1), 16)
    kb = coarse_key_to_f32_key(k16)
    span = 1 << 16
    lo = bisect(lambda cand: count(lambda blk: blk >= cand), _key_to_float,
                jnp.maximum(kb, KEY_NEG_INF + span) - span, jnp.minimum(kb, KEY_POS_INF + 1 - span) + span, 17)
    thr = _key_to_float(lo)
    n_above = count(lambda blk: blk > thr)
    n_ties = topk - n_above

    tri = jnp.where(lax.broadcasted_iota(I32, (lc, lc), 0) >= lax.broadcasted_iota(I32, (lc, lc), 1),
                    1.0, 0.0).astype(BF16)

    def logits_span(off, rows, carry):
        ties_before, m8 = carry
        biases = []
        for r0 in range(0, rows, lc):
            blk = sc_ref[pl.ds(off + r0, lc), :]
            tie = blk == thr
            tie_rank = (jnp.dot(tri, jnp.where(tie, 1.0, 0.0).astype(BF16), preferred_element_type=F32)
                        + ties_before)
            bias = jnp.where(blk > thr, 0.0, jnp.where(tie, jnp.where(tie_rank <= n_ties, 0.0, NEG_BIG), NEG_BIG))
            biases.append(jnp.where(blk == -jnp.inf, NEG_BIG, bias))
            ties_before = tie_rank[lc - 1:lc, :]
        bias = jnp.concatenate(biases, axis=0)
        kc_b = kb_ref[0, pl.ds(off, rows), :]
        m_rows = []
        for h in range(N_HEADS):
            s = lax.dot_general(kc_b, q_ref[0, :, h * HEAD_DIM:(h + 1) * HEAD_DIM], NT_DIMS,
                                preferred_element_type=F32) + bias
            s_ref[h, pl.ds(off, rows), :] = s
            s8 = jnp.max(s.reshape(rows // SUBLANES, SUBLANES, tq), axis=0)
            m_rows.append(jnp.maximum(m8[h * SUBLANES:(h + 1) * SUBLANES], s8))
        return ties_before, jnp.concatenate(m_rows, axis=0)

    _, m8 = wide_chunk_loop(logits_span,
                            (jnp.zeros((1, tq), F32), jnp.full((N_HEADS * SUBLANES, tq), NEG_BIG, F32)))
    m_all = jnp.max(m8.reshape(N_HEADS, SUBLANES, tq), axis=1)

    for acc_ref in acc_refs:
        acc_ref[...] = jnp.zeros(acc_ref.shape, F32)

    def pv_chunk(kc, carry):
        off = pl.multiple_of(kc * lc, lc)
        vt_c = vt_ref[kc]
        for h in range(N_HEADS):
            p = jnp.exp2(s_ref[h, pl.ds(off, lc), :] - m_all[h:h + 1, :])
            acc_refs[h][...] += jnp.dot(vt_c, p.astype(BF16), preferred_element_type=F32)
        return carry

    chunk_loop(pv_chunk, 0)

    for h in range(N_HEADS):
        acc = acc_refs[h][...]
        out_ref[h * HEAD_DIM:(h + 1) * HEAD_DIM, :] = acc[0:HEAD_DIM] / acc[HEAD_DIM:HEAD_DIM + 1]
    o_ref[0] = out_ref[...].T.astype(BF16)


def _dsa(q, qi, wit, kb, kib, vt, *, tq, lc, causal, l_valid, q_pos0, topk):
    bsz, tq_tot, aw = q.shape
    l_tot = kb.shape[1]
    nq = tq_tot // tq
    nkc = l_tot // lc
    kern = functools.partial(_dsa_kernel, tq=tq, lc=lc, nk_static=nkc, causal=causal, l_valid=l_valid,
                             q_pos0=q_pos0, topk=topk)
    return pl.pallas_call(
        kern,
        grid=(bsz, nq),
        in_specs=[
            pl.BlockSpec((1, tq, aw), lambda b, i: (b, i, 0)),
            pl.BlockSpec((1, tq, aw), lambda b, i: (b, i, 0)),
            pl.BlockSpec((IDX_HEADS, tq), lambda b, i: (0, b * nq + i)),
            pl.BlockSpec((1, l_tot, HEAD_DIM), lambda b, i: (b, 0, 0)),
            pl.BlockSpec((1, l_tot, IDX_DIM), lambda b, i: (b, 0, 0)),
            pl.BlockSpec((nkc, VT_ROWS, lc), lambda b, i: (b, 0, 0)),
        ],
        out_specs=pl.BlockSpec((1, tq, aw), lambda b, i: (b, i, 0)),
        out_shape=jax.ShapeDtypeStruct((bsz, tq_tot, aw), BF16),
        scratch_shapes=[
            pltpu.VMEM((l_tot, tq), F32),
            pltpu.VMEM((l_tot, tq), BF16),
            pltpu.VMEM((N_HEADS, l_tot, tq), F32),
            pltpu.VMEM((aw, tq), F32),
        ] + [pltpu.VMEM((VT_ROWS, tq), F32) for _ in range(N_HEADS)],
        compiler_params=_params(("parallel", "parallel")),
        name="dsa",
    )(q, qi, wit, kb, kib, vt)


def _pool_kernel(u_ref, pre_ref, pw_ref, sc_ref, o_ref, ext_ref, *, t_len, tt, pos0):
    pad = pre_ref.shape[1]
    gw = pw_ref.shape[1]
    ext_ref[0:pad, :] = pre_ref[0]
    ext_ref[pad:pad + t_len, :] = u_ref[0]
    row = lax.broadcasted_iota(I32, (tt, gw), 0)
    for t in range(t_len // tt):
        r0 = t * tt
        for g, w in enumerate(POOL_WINDOWS):
            cols = slice(g * gw, (g + 1) * gw)
            cur = ext_ref[pad + r0:pad + r0 + tt, cols]
            wsum = cur
            for j in range(1, w):
                wsum = wsum + ext_ref[pad + r0 - j:pad + r0 - j + tt, cols]
            cnt = jnp.minimum(w, pos0 + r0 + 1 + row).astype(F32)
            diff = (wsum / cnt - cur).astype(BF16)
            y = jnp.dot(diff, pw_ref[g], preferred_element_type=F32)
            o_ref[0, r0:r0 + tt, cols] = (y * sc_ref[:, cols]).astype(BF16)


def _pool(u, prefix, pool_w_b, pool_scale, *, pos0):
    bsz, t_len, c = u.shape
    pad = prefix.shape[1]
    tt = min(t_len, 256)
    return pl.pallas_call(
        functools.partial(_pool_kernel, t_len=t_len, tt=tt, pos0=pos0),
        grid=(bsz,),
        in_specs=[pl.BlockSpec((1, t_len, c), lambda b: (b, 0, 0)),
                  pl.BlockSpec((1, pad, c), lambda b: (b, 0, 0)),
                  pl.BlockSpec(pool_w_b.shape, lambda b: (0, 0, 0)),
                  pl.BlockSpec((1, c), lambda b: (0, 0))],
        out_specs=pl.BlockSpec((1, t_len, c), lambda b: (b, 0, 0)),
        out_shape=jax.ShapeDtypeStruct((bsz, t_len, c), BF16),
        scratch_shapes=[pltpu.VMEM((pad + t_len, c), F32)],
        compiler_params=_params(("parallel",)),
        name="pool",
    )(u, prefix, pool_w_b, pool_scale)


ROW_ALIGN = 16


def _tail_kernel(x_ref, att_ref, pool_ref, g0_ref, b0_ref, woa_ref, wob_ref, g1_ref, b1_ref,
                 wrh_ref, wrl_ref, br_ref,
                 carry0_ref, h1_ref, cpos_ref, gw_ref, seg_ref, len_ref, dst_ref, used_ref, carry_ref, *, tm, sub):
    @pl.when(pl.program_id(0) == 0)
    def _():
        carry_ref[...] = carry0_ref[...]

    chunk_lens = []
    for s in range(sub):
        rows = slice(s * tm, (s + 1) * tm)
        chunk_lens.append(_tail_tile(x_ref[rows, :], att_ref[rows, :], pool_ref[rows, :], g0_ref, b0_ref, woa_ref,
                                     wob_ref, g1_ref, b1_ref, wrh_ref, wrl_ref, br_ref,
                                     h1_ref.at[rows, :], cpos_ref.at[:, rows], gw_ref.at[:, rows], seg_ref.at[s],
                                     len_ref.at[s], tm=tm))
    carry = carry_ref[...]
    for s in range(sub):
        dst_ref[s] = carry
        carry = carry + chunk_lens[s]
    carry_ref[...] = carry
    used_ref[...] = carry


def _tail_tile(x, att, pool, g0_ref, b0_ref, woa_ref, wob_ref, g1_ref, b1_ref, wrh_ref, wrl_ref, br_ref,
               h1_ref, cpos_ref, gw_ref, seg_ref, len_ref, *, tm):
    h = _layer_norm(x, g0_ref[...], b0_ref[...])
    mix = (jnp.dot(att, woa_ref[...], preferred_element_type=F32)
           + jnp.dot(pool, wob_ref[...], preferred_element_type=F32))
    h1 = _layer_norm(DEEPNORM_ALPHA * h + mix, g1_ref[...], b1_ref[...])
    h1_ref[...] = h1

    hh = h1.astype(BF16)
    hl = (h1 - hh.astype(F32)).astype(BF16)
    logits = (lax.dot_general(wrh_ref[...], hh, NT_DIMS, preferred_element_type=F32)
              + lax.dot_general(wrh_ref[...], hl, NT_DIMS, preferred_element_type=F32)
              + lax.dot_general(wrl_ref[...], hh, NT_DIMS, preferred_element_type=F32)
              + br_ref[...])

    e_iota = lax.broadcasted_iota(I32, (N_EXPERTS, tm), 0).astype(F32)
    work = logits
    vals, hots = [], []
    for k in range(TOP_K):
        m = jnp.max(work, axis=0, keepdims=True)
        idx = jnp.min(jnp.where(work == m, e_iota, float(N_EXPERTS)), axis=0, keepdims=True)
        hot = e_iota == idx
        vals.append(m)
        hots.append(jnp.where(hot, 1.0, 0.0))
        work = jnp.where(hot, -jnp.inf, work)
    exps = [jnp.exp(v - vals[0]) for v in vals]
    den = exps[0] + exps[1] + exps[2] + exps[3]
    for k in range(TOP_K):
        gw_ref[k:k + 1, :] = exps[k] / den

    hot_all = (hots[0] + hots[1] + hots[2] + hots[3]).astype(BF16)
    count = jnp.dot(hot_all, jnp.ones((tm, LANES), BF16), preferred_element_type=F32)
    chunk_len = jnp.ceil(count * (1.0 / ROW_ALIGN)) * ROW_ALIGN
    lower = jnp.where(lax.broadcasted_iota(I32, (N_EXPERTS, N_EXPERTS), 1)
                      < lax.broadcasted_iota(I32, (N_EXPERTS, N_EXPERTS), 0), 1.0, 0.0).astype(BF16)
    seg_base = jnp.dot(lower, chunk_len.astype(BF16), preferred_element_type=F32)
    before = jnp.where(lax.broadcasted_iota(I32, (tm, tm), 0) < lax.broadcasted_iota(I32, (tm, tm), 1),
                       1.0, 0.0).astype(BF16)
    slot = jnp.dot(hot_all, before, preferred_element_type=F32) + jnp.tile(seg_base, (1, tm // LANES))
    for k in range(TOP_K):
        cpos_ref[k:k + 1, :] = jnp.sum(hots[k] * slot, axis=0, keepdims=True).astype(I32)
    seg_ref[...] = seg_base
    len_ref[...] = chunk_len
    return chunk_len


def _tail(x2d, att, pool, g0, b0, woa, wob, g1, b1, wrh, wrl, br, carry0, *, tm):
    n, d = x2d.shape
    aw = att.shape[1]
    nt = n // tm
    sub = 2 if nt % 2 == 0 else 1
    ts = sub * tm
    row = lambda i: (i, 0)
    const = lambda i: (0, 0)
    col = lambda i: (0, i)
    tab = lambda i: (i, 0, 0)
    tab_shape = jax.ShapeDtypeStruct((nt, N_EXPERTS, LANES), F32)
    tab_spec = pl.BlockSpec((sub, N_EXPERTS, LANES), tab)
    return pl.pallas_call(
        functools.partial(_tail_kernel, tm=tm, sub=sub),
        grid=(nt // sub,),
        in_specs=[pl.BlockSpec((ts, d), row), pl.BlockSpec((ts, aw), row), pl.BlockSpec((ts, d - aw), row),
                  pl.BlockSpec((1, d), const), pl.BlockSpec((1, d), const),
                  pl.BlockSpec(woa.shape, const), pl.BlockSpec(wob.shape, const),
                  pl.BlockSpec((1, d), const), pl.BlockSpec((1, d), const),
                  pl.BlockSpec(wrh.shape, const), pl.BlockSpec(wrl.shape, const), pl.BlockSpec(br.shape, const),
                  pl.BlockSpec((N_EXPERTS, LANES), const)],
        out_specs=(pl.BlockSpec((ts, d), row), pl.BlockSpec((TOP_K, ts), col), pl.BlockSpec((TOP_K, ts), col),
                   tab_spec, tab_spec, tab_spec, pl.BlockSpec((N_EXPERTS, LANES), const)),
        out_shape=(jax.ShapeDtypeStruct((n, d), F32), jax.ShapeDtypeStruct((TOP_K, n), I32),
                   jax.ShapeDtypeStruct((TOP_K, n), F32), tab_shape, tab_shape, tab_shape,
                   jax.ShapeDtypeStruct((N_EXPERTS, LANES), F32)),
        scratch_shapes=[pltpu.VMEM((N_EXPERTS, LANES), F32)],
        compiler_params=_params(("arbitrary",)),
        name="tail",
    )(x2d, att, pool, g0, b0, woa, wob, g1, b1, wrh, wrl, br, carry0)


G_BLOCK = 256


def _stage_rows(tm):
    raw = TOP_K * tm + (ROW_ALIGN - 1) * N_EXPERTS
    return -(-raw // G_BLOCK) * G_BLOCK


def _chunk_copies(seg_ref, len_ref, dst_ref, tile, make_copy, act):
    def per_expert(e, carry):
        idx = tile * N_EXPERTS + e
        seg, ln, dst = seg_ref[idx], len_ref[idx], dst_ref[idx]

        @pl.when(ln > 0)
        def _():
            act(make_copy(pl.multiple_of(seg, ROW_ALIGN), pl.multiple_of(dst, ROW_ALIGN),
                          pl.multiple_of(ln, ROW_ALIGN)))
        return carry

    lax.fori_loop(0, N_EXPERTS, per_expert, 0)


def _by_staged_blocks(seg_ref, len_ref, tile, n_full, body):
    last = tile * N_EXPERTS + N_EXPERTS - 1
    total = seg_ref[last] + len_ref[last]
    pl.when(total <= (n_full - 1) * G_BLOCK)(lambda: body(n_full - 1))
    pl.when(total > (n_full - 1) * G_BLOCK)(lambda: body(n_full))


def _route_onehot(iota, pos_list, val_list):
    g = jnp.zeros(iota.shape, F32)
    for pos, val in zip(pos_list, val_list):
        g = jnp.where(iota == pos, val, g)
    return g.astype(BF16)


def _dispatch_kernel(seg_ref, len_ref, dst_ref, h1_ref, cpos_ref, *rest, tm):
    xs_hbm, stage_ref, sem = rest[-3:]
    tile = pl.program_id(0)
    slot = tile % 2
    h1b = h1_ref[...].astype(BF16)
    pos_rows = [cpos_ref[k:k + 1, :] for k in range(TOP_K)]

    def stage_blocks(n_blk):
        for blk in range(n_blk):
            iota = blk * G_BLOCK + lax.broadcasted_iota(I32, (G_BLOCK, tm), 0)
            g = _route_onehot(iota, pos_rows, [1.0] * TOP_K)
            stage_ref[slot, blk * G_BLOCK:(blk + 1) * G_BLOCK, :] = jnp.dot(
                g, h1b, preferred_element_type=F32).astype(BF16)

    _by_staged_blocks(seg_ref, len_ref, tile, stage_ref.shape[1] // G_BLOCK, stage_blocks)

    def copy_from(buf):
        def make_copy(seg, dst, size):
            return pltpu.make_async_copy(stage_ref.at[buf, pl.ds(seg, size), :], xs_hbm.at[pl.ds(dst, size), :],
                                         sem.at[buf])
        return make_copy

    _chunk_copies(seg_ref, len_ref, dst_ref, tile, copy_from(slot), lambda c: c.start())

    @pl.when(tile > 0)
    def _():
        _chunk_copies(seg_ref, len_ref, dst_ref, tile - 1, copy_from(1 - slot), lambda c: c.wait())

    @pl.when(tile == pl.num_programs(0) - 1)
    def _():
        _chunk_copies(seg_ref, len_ref, dst_ref, tile, copy_from(slot), lambda c: c.wait())


def _dispatch(seg, ln, dst, h1, cpos, xs_prev, *, tm, p_rows):
    n, d = h1.shape
    in_specs = [pl.BlockSpec((tm, d), lambda i, *_: (i, 0)), pl.BlockSpec((TOP_K, tm), lambda i, *_: (0, i))]
    operands = [seg, ln, dst, h1, cpos]
    aliases = {}
    if xs_prev is not None:
        in_specs.append(pl.BlockSpec(memory_space=pl.ANY))
        aliases = {len(operands): 0}
        operands.append(xs_prev)
    return pl.pallas_call(
        functools.partial(_dispatch_kernel, tm=tm),
        grid_spec=pltpu.PrefetchScalarGridSpec(
            num_scalar_prefetch=3,
            grid=(n // tm,),
            in_specs=in_specs,
            out_specs=pl.BlockSpec(memory_space=pl.ANY),
            scratch_shapes=[pltpu.VMEM((2, _stage_rows(tm), d), BF16), pltpu.SemaphoreType.DMA((2,))],
        ),
        out_shape=jax.ShapeDtypeStruct((p_rows, d), BF16),
        input_output_aliases=aliases,
        compiler_params=_params(("arbitrary",)),
        name="dispatch",
    )(*operands)


FF_CHUNK = 256


def _moe_kernel(te_ref, tv_ref, x_ref, w1_ref, b1_ref, w2_ref, b2_ref, y_ref, w1b_ref, w2b_ref, act_ref):
    i = pl.program_id(0)
    d_ff = w2_ref.shape[1]
    valid = tv_ref[i]

    @pl.when((i == 0) | (te_ref[i] != te_ref[jnp.maximum(i - 1, 0)]))
    def _():
        w1b_ref[...] = w1_ref[0].astype(BF16)
        w2b_ref[...] = w2_ref[0].astype(BF16)

    @pl.when(valid > 0)
    def _():
        rows = lax.broadcasted_iota(I32, x_ref.shape, 0)
        xb = jnp.where(rows < valid, x_ref[...].astype(F32), 0.0).astype(BF16)
        for j in range(d_ff // FF_CHUNK):
            gs = slice(j * FF_CHUNK, (j + 1) * FF_CHUNK)
            us = slice(d_ff + j * FF_CHUNK, d_ff + (j + 1) * FF_CHUNK)
            gate = jnp.dot(xb, w1b_ref[:, gs], preferred_element_type=F32) + b1_ref[0, :, gs]
            up = jnp.dot(xb, w1b_ref[:, us], preferred_element_type=F32) + b1_ref[0, :, us]
            gate = jnp.minimum(gate, SWIGLU_LIMIT)
            up = jnp.clip(up, -SWIGLU_LIMIT, SWIGLU_LIMIT)
            act_ref[:, gs] = ((up + 1.0) * (gate * jax.nn.sigmoid(SWIGLU_ALPHA * gate))).astype(BF16)
        y = jnp.dot(act_ref[...], w2b_ref[...], preferred_element_type=F32) + b2_ref[0]
        y_ref[...] = y.astype(BF16)

    @pl.when(valid <= 0)
    def _():
        y_ref[...] = jnp.zeros(y_ref.shape, BF16)


def _moe(tile_expert, tile_valid, xs, w1, b1, w2, b2, *, tmoe):
    p, d = xs.shape
    e, _, ff2 = w1.shape
    d_ff = ff2 // 2
    return pl.pallas_call(
        _moe_kernel,
        grid_spec=pltpu.PrefetchScalarGridSpec(
            num_scalar_prefetch=2,
            grid=(p // tmoe,),
            in_specs=[pl.BlockSpec((tmoe, d), lambda i, te, tv: (i, 0)),
                      pl.BlockSpec((1, d, ff2), lambda i, te, tv: (te[i], 0, 0)),
                      pl.BlockSpec((1, 1, ff2), lambda i, te, tv: (te[i], 0, 0)),
                      pl.BlockSpec((1, d_ff, d), lambda i, te, tv: (te[i], 0, 0)),
                      pl.BlockSpec((1, 1, d), lambda i, te, tv: (te[i], 0, 0))],
            out_specs=pl.BlockSpec((tmoe, d), lambda i, te, tv: (i, 0)),
            scratch_shapes=[pltpu.VMEM((d, ff2), BF16), pltpu.VMEM((d_ff, d), BF16), pltpu.VMEM((tmoe, d_ff), BF16)],
        ),
        out_shape=jax.ShapeDtypeStruct((p, d), BF16),
        compiler_params=_params(("arbitrary",)),
        name="moe",
    )(tile_expert, tile_valid, xs, w1, b1.reshape(e, 1, ff2), w2, b2.reshape(e, 1, d))


def _combine_kernel(seg_ref, len_ref, dst_ref, h1_ref, cpos_ref, gw_ref, g2_ref, b2_ref, ys_hbm, o_ref,
                    stage_ref, gate_ref, sem, *, tm):
    tile = pl.program_id(0)
    slot = tile % 2

    def copy_into(buf):
        def make_copy(seg, dst, size):
            return pltpu.make_async_copy(ys_hbm.at[pl.ds(dst, size), :], stage_ref.at[buf, pl.ds(seg, size), :],
                                         sem.at[buf])
        return make_copy

    @pl.when(tile == 0)
    def _():
        stage_ref[...] = jnp.zeros(stage_ref.shape, BF16)
        _chunk_copies(seg_ref, len_ref, dst_ref, tile, copy_into(slot), lambda c: c.start())

    @pl.when(tile + 1 < pl.num_programs(0))
    def _():
        _chunk_copies(seg_ref, len_ref, dst_ref, tile + 1, copy_into(1 - slot), lambda c: c.start())

    cpos = cpos_ref[...]
    gw = gw_ref[...]
    pos_cols = [cpos[:, k:k + 1] for k in range(TOP_K)]
    gw_cols = [gw[:, k:k + 1] for k in range(TOP_K)]

    def finish(n_blk):
        _chunk_copies(seg_ref, len_ref, dst_ref, tile, copy_into(slot), lambda c: c.wait())
        m = None
        for blk in range(n_blk):
            iota = blk * G_BLOCK + lax.broadcasted_iota(I32, (tm, G_BLOCK), 1)
            gate = _route_onehot(iota, pos_cols, gw_cols)
            part = jnp.dot(gate, stage_ref[slot, blk * G_BLOCK:(blk + 1) * G_BLOCK, :], preferred_element_type=F32)
            m = part if m is None else part + m
        o_ref[...] = _layer_norm(DEEPNORM_ALPHA * h1_ref[...] + m, g2_ref[...], b2_ref[...])

    _by_staged_blocks(seg_ref, len_ref, tile, stage_ref.shape[1] // G_BLOCK, finish)


def _combine(seg, ln, dst, h1, cpos_nt, gw_nt, g2, b2, ys, *, tm):
    n, d = h1.shape
    return pl.pallas_call(
        functools.partial(_combine_kernel, tm=tm),
        grid_spec=pltpu.PrefetchScalarGridSpec(
            num_scalar_prefetch=3,
            grid=(n // tm,),
            in_specs=[pl.BlockSpec((tm, d), lambda i, *_: (i, 0)),
                      pl.BlockSpec((tm, TOP_K), lambda i, *_: (i, 0)), pl.BlockSpec((tm, TOP_K), lambda i, *_: (i, 0)),
                      pl.BlockSpec((1, d), lambda i, *_: (0, 0)), pl.BlockSpec((1, d), lambda i, *_: (0, 0)),
                      pl.BlockSpec(memory_space=pl.ANY)],
            out_specs=pl.BlockSpec((tm, d), lambda i, *_: (i, 0)),
            scratch_shapes=[pltpu.VMEM((2, _stage_rows(tm), d), BF16), pltpu.VMEM((tm, _stage_rows(tm)), BF16),
                            pltpu.SemaphoreType.DMA((2,))],
        ),
        out_shape=jax.ShapeDtypeStruct((n, d), F32),
        compiler_params=_params(("arbitrary",)),
        name="combine",
    )(seg, ln, dst, h1, cpos_nt, gw_nt, g2, b2, ys)


def _block_tails(groups, wts, *, tm, tmoe):
    carry = jnp.zeros((N_EXPERTS, LANES), F32)
    plans, max_rows = [], 0
    for x2d, att, pool in groups:
        n = x2d.shape[0]
        tm_g = min(tm, n)
        h1, cpos, gw_t, seg_t, len_t, dst_t, carry = _tail(
            x2d, att, pool, wts["g0"], wts["b0"], wts["woa"], wts["wob"], wts["g1"], wts["b1"],
            wts["wrh"], wts["wrl"], wts["br"], carry, tm=tm_g)
        plans.append((tm_g, h1, cpos, gw_t, seg_t, len_t, dst_t))
        max_rows += TOP_K * n + (ROW_ALIGN - 1) * N_EXPERTS * (n // tm_g)
    used = carry[:, 0].astype(I32)
    cap = ((used + tmoe - 1) // tmoe) * tmoe
    ends = jnp.cumsum(cap)
    offs = ends - cap
    n_tiles = -(-max_rows // tmoe) + N_EXPERTS
    tile_start = jnp.arange(n_tiles, dtype=I32) * tmoe
    tile_expert = jnp.minimum(jnp.sum((ends[None, :] <= tile_start[:, None]).astype(I32), axis=1), N_EXPERTS - 1)
    tile_valid = jnp.clip(offs[tile_expert] + used[tile_expert] - tile_start, 0, tmoe).astype(I32)
    tile_valid = jnp.where(tile_start < ends[-1], tile_valid, 0)

    xs, tables = None, []
    for tm_g, h1, cpos, gw_t, seg_t, len_t, dst_t in plans:
        seg = seg_t[:, :, 0].astype(I32).reshape(-1)
        ln = len_t[:, :, 0].astype(I32).reshape(-1)
        dst = (dst_t[:, :, 0].astype(I32) + offs[None, :]).reshape(-1)
        tables.append((seg, ln, dst))
        xs = _dispatch(seg, ln, dst, h1, cpos, xs, tm=tm_g, p_rows=n_tiles * tmoe)
    ys = _moe(tile_expert, tile_valid, xs, wts["w1e"], wts["b1e"], wts["w2e"], wts["b2e"], tmoe=tmoe)
    return [_combine(seg, ln, dst, h1, cpos.T, gw_t.T, wts["g2"], wts["b2"], ys, tm=tm_g)
            for (seg, ln, dst), (tm_g, h1, cpos, gw_t, _, _, _) in zip(tables, plans)]


def kernel(x_prompt, x_sample, cache_k, cache_v, cache_kidx, state_pool, ln0_g, ln0_b, w_in, w_o,
           pool_w, pool_scale, ln1_g, ln1_b, w_router, b_router, w1, b1, w2, b2, ln2_g, ln2_b):
    bp, s_len, d = x_prompt.shape
    bs, t_len, _ = x_sample.shape
    l_past = cache_k.shape[2]
    aw = N_HEADS * HEAD_DIM
    pw = d - aw
    lyr = 0
    lc = 256

    k_off = aw
    v_off = k_off + HEAD_DIM
    qi_off = v_off + HEAD_DIM
    ki_off = qi_off + IDX_HEADS * IDX_DIM
    wi_off = ki_off + IDX_DIM
    u_off = wi_off + IDX_HEADS
    win = w_in[lyr]
    wa = jnp.concatenate([win[:, 0:k_off], win[:, qi_off:ki_off], win[:, u_off:u_off + pw]], axis=1).astype(BF16)
    wb = jnp.concatenate([win[:, k_off:v_off], win[:, v_off:qi_off], win[:, ki_off:wi_off], win[:, wi_off:u_off],
                          jnp.zeros((d, HEAD_DIM - IDX_HEADS), F32)], axis=1).astype(BF16)

    g0 = ln0_g.reshape(1, d)
    b0 = ln0_b.reshape(1, d)
    wrt = w_router[lyr].T
    wrh = wrt.astype(BF16)
    wts = dict(
        g0=g0, b0=b0,
        woa=w_o[lyr][:aw].astype(BF16), wob=w_o[lyr][aw:].astype(BF16),
        g1=ln1_g[lyr].reshape(1, d), b1=ln1_b[lyr].reshape(1, d),
        wrh=wrh, wrl=(wrt - wrh.astype(F32)).astype(BF16), br=b_router[lyr].reshape(N_EXPERTS, 1),
        w1e=w1[lyr], b1e=b1[lyr], w2e=w2[lyr], b2e=b2[lyr],
        g2=ln2_g[lyr].reshape(1, d), b2=ln2_b[lyr].reshape(1, d),
    )
    pool_w_b = pool_w[lyr].astype(BF16)
    pool_sc = pool_scale[lyr].reshape(1, pw)

    xp = x_prompt.reshape(bp * s_len, d)
    q, qi, u, k, v, ki, kb, kib, vt, wit = _proj(xp, g0, b0, wa, wb, tm=512, lc=lc)
    att_p = _dsa(q.reshape(bp, s_len, aw), qi.reshape(bp, s_len, aw), wit,
                 kb.reshape(bp, s_len, HEAD_DIM), kib.reshape(bp, s_len, IDX_DIM), vt,
                 tq=256, lc=lc, causal=True, l_valid=s_len, q_pos0=0, topk=min(TOPK_MAX, s_len // 4))
    u_p = u.reshape(bp, s_len, pw)
    pool_p = _pool(u_p, jnp.zeros((bp, POOL_PAST + 1, pw), F32), pool_w_b, pool_sc, pos0=0)

    xs = x_sample.reshape(bs * t_len, d)
    qs, qis, us, kn, vn, kin, _, _, _, wits = _proj(xs, g0, b0, wa, wb, tm=512, lc=lc)
    l_all = l_past + t_len
    l_pad = -(-l_all // lc) * lc
    tq_s = LANES
    pad_keys = lambda a: jnp.pad(a, ((0, 0), (0, l_pad - l_all), (0, 0)))
    k_all = pad_keys(jnp.concatenate([cache_k[lyr], kn.reshape(bs, t_len, HEAD_DIM)], axis=1))
    v_all = pad_keys(jnp.concatenate([cache_v[lyr], vn.reshape(bs, t_len, HEAD_DIM)], axis=1))
    ki_all = pad_keys(jnp.concatenate([cache_kidx[lyr], kin.reshape(bs, t_len, IDX_DIM)], axis=1))
    vt_all = v_all.reshape(bs, l_pad // lc, lc, HEAD_DIM).transpose(0, 1, 3, 2).reshape(-1, HEAD_DIM, lc)
    vt_all = jnp.concatenate([vt_all, jnp.broadcast_to(_denominator_rows(lc), (vt_all.shape[0], VT_ROWS - HEAD_DIM, lc))],
                             axis=1)
    pad_q = lambda a: jnp.pad(a.reshape(bs, t_len, aw), ((0, 0), (0, tq_s - t_len), (0, 0)))
    wit_s = jnp.pad(wits.reshape(IDX_HEADS, bs, t_len), ((0, 0), (0, 0), (0, tq_s - t_len))).reshape(IDX_HEADS, -1)
    att_s = _dsa(pad_q(qs), pad_q(qis), wit_s, k_all.astype(BF16), ki_all.astype(BF16), vt_all.astype(BF16),
                 tq=tq_s, lc=lc, causal=False, l_valid=l_all, q_pos0=l_past, topk=min(TOPK_MAX, l_all // 4))
    att_s = att_s[:, :t_len].reshape(bs * t_len, aw)
    us3 = us.reshape(bs, t_len, pw)
    prefix_s = jnp.concatenate([jnp.zeros((bs, 1, pw), F32), state_pool[lyr]], axis=1)
    pool_s = _pool(us3, prefix_s, pool_w_b, pool_sc, pos0=l_past)

    y_p, y_s = _block_tails([(xp, att_p.reshape(bp * s_len, aw), pool_p.reshape(bp * s_len, pw)),
                             (xs, att_s, pool_s.reshape(bs * t_len, pw))], wts, tm=512, tmoe=1024)

    pool_state_p = u_p[:, s_len - POOL_PAST:]
    pool_state_s = jnp.concatenate([state_pool[lyr], us3], axis=1)[:, -POOL_PAST:]
    return (y_p.reshape(bp, s_len, d), y_s.reshape(bs, t_len, d),
            k.reshape(1, bp, s_len, HEAD_DIM), v.reshape(1, bp, s_len, HEAD_DIM),
            ki.reshape(1, bp, s_len, IDX_DIM), pool_state_p[None],
            kn.reshape(1, bs, t_len, HEAD_DIM), vn.reshape(1, bs, t_len, HEAD_DIM),
            kin.reshape(1, bs, t_len, IDX_DIM), pool_state_s[None])
```

```python
import functools

import jax
import jax.numpy as jnp
from jax import lax
from jax.experimental import pallas as pl
from jax.experimental.pallas import tpu as pltpu

F32 = jnp.float32
BF16 = jnp.bfloat16
I32 = jnp.int32

CHUNK = 64
CHUNK_SHIFT = 6
assert 1 << CHUNK_SHIFT == CHUNK
N_HEADS = 8
HEAD_DIM = 64
IDX_HEADS = 8
IDX_DIM = 64
TOPK_MAX = 256
POOL_WINDOWS = (2, 4, 8, 16)
POOL_PAST = 15
N_EXPERTS = 32
TOP_K = 4
SWIGLU_LIMIT = 7.0
SWIGLU_ALPHA = 1.702
LN_EPS = 1e-5
DEPTH = 1
DEEPNORM_ALPHA = (2 * DEPTH) ** 0.25
LOG2_E = 1.4426950408889634

LANES = 128
SUBLANES = 8
BF16_ROWS = 16
VMEM_LIMIT_BYTES = 56 * 1024 * 1024

NEG_BIG = -1e30
KEY_NEG_INF = -2139095041
KEY_POS_INF = 2139095040

NT_DIMS = (((1,), (1,)), ((), ()))


def _layer_norm(x, g, b):
    mu = jnp.mean(x, axis=-1, keepdims=True)
    xc = x - mu
    var = jnp.mean(xc * xc, axis=-1, keepdims=True)
    return xc * lax.rsqrt(var + LN_EPS) * g + b


def _params(sem):
    return pltpu.CompilerParams(dimension_semantics=sem, vmem_limit_bytes=VMEM_LIMIT_BYTES)


VT_ROWS = HEAD_DIM + BF16_ROWS


def _denominator_rows(width):
    return jnp.where(lax.broadcasted_iota(I32, (VT_ROWS - HEAD_DIM, width), 0) == 0, 1.0, 0.0)


def _proj_kernel(x_ref, g_ref, b_ref, wa_ref, wb_ref,
                 q_ref, qi_ref, u_ref, k_ref, v_ref, ki_ref, kb_ref, kib_ref, vt_ref, wit_ref, *, lc):
    h = _layer_norm(x_ref[...], g_ref[...], b_ref[...])
    hb = h.astype(BF16)
    aw = N_HEADS * HEAD_DIM
    pa = jnp.dot(hb, wa_ref[...], preferred_element_type=F32)
    q_ref[...] = (pa[:, :aw] * (HEAD_DIM ** -0.5 * LOG2_E)).astype(BF16)
    qi_ref[...] = pa[:, aw:2 * aw].astype(BF16)
    u_ref[...] = pa[:, 2 * aw:]
    pb = jnp.dot(hb, wb_ref[...], preferred_element_type=F32)
    k = pb[:, 0:HEAD_DIM]
    v = pb[:, HEAD_DIM:2 * HEAD_DIM]
    ki = pb[:, 2 * HEAD_DIM:2 * HEAD_DIM + IDX_DIM]
    k_ref[...] = k
    v_ref[...] = v
    ki_ref[...] = ki
    kb_ref[...] = k.astype(BF16)
    kib_ref[...] = ki.astype(BF16)
    pt = pb.T
    ones_rows = _denominator_rows(lc)
    for c in range(vt_ref.shape[0]):
        vt_ref[c] = jnp.concatenate([pt[HEAD_DIM:2 * HEAD_DIM, c * lc:(c + 1) * lc], ones_rows],
                                    axis=0).astype(BF16)
    wi = pt[3 * HEAD_DIM:3 * HEAD_DIM + IDX_HEADS, :]
    wit_ref[...] = (wi * (IDX_HEADS ** -0.5)) * (IDX_DIM ** -0.5)


def _proj(x2d, g, b, wa, wb, *, tm, lc):
    n, d = x2d.shape
    tm = min(tm, n)
    aw = N_HEADS * HEAD_DIM
    uw = wa.shape[1] - 2 * aw
    row = lambda i: (i, 0)
    const = lambda i: (0, 0)
    out_shape = (
        jax.ShapeDtypeStruct((n, aw), BF16),
        jax.ShapeDtypeStruct((n, aw), BF16),
        jax.ShapeDtypeStruct((n, uw), F32),
        jax.ShapeDtypeStruct((n, HEAD_DIM), F32),
        jax.ShapeDtypeStruct((n, HEAD_DIM), F32),
        jax.ShapeDtypeStruct((n, IDX_DIM), F32),
        jax.ShapeDtypeStruct((n, HEAD_DIM), BF16),
        jax.ShapeDtypeStruct((n, IDX_DIM), BF16),
        jax.ShapeDtypeStruct((n // lc, VT_ROWS, lc), BF16),
        jax.ShapeDtypeStruct((IDX_HEADS, n), F32),
    )
    out_specs = (
        pl.BlockSpec((tm, aw), row), pl.BlockSpec((tm, aw), row), pl.BlockSpec((tm, uw), row),
        pl.BlockSpec((tm, HEAD_DIM), row), pl.BlockSpec((tm, HEAD_DIM), row), pl.BlockSpec((tm, IDX_DIM), row),
        pl.BlockSpec((tm, HEAD_DIM), row), pl.BlockSpec((tm, IDX_DIM), row),
        pl.BlockSpec((tm // lc, VT_ROWS, lc), lambda i: (i, 0, 0)),
        pl.BlockSpec((IDX_HEADS, tm), lambda i: (0, i)),
    )
    return pl.pallas_call(
        functools.partial(_proj_kernel, lc=lc),
        grid=(n // tm,),
        in_specs=[pl.BlockSpec((tm, d), row), pl.BlockSpec((1, d), const), pl.BlockSpec((1, d), const),
                  pl.BlockSpec(wa.shape, const), pl.BlockSpec(wb.shape, const)],
        out_specs=out_specs,
        out_shape=out_shape,
        compiler_params=_params(("parallel",)),
        name="proj",
    )(x2d, g, b, wa, wb)


def _key_to_float(key):
    bits = jnp.where(key >= 0, key, key ^ jnp.int32(0x7FFFFFFF))
    return lax.bitcast_convert_type(bits, F32)


def _dsa_kernel(q_ref, qi_ref, wit_ref, kb_ref, kib_ref, vt_ref, o_ref,
                sc_ref, sch_ref, s_ref, out_ref, *acc_refs, tq, lc, nk_static, causal, l_valid, q_pos0, topk):
    qb = pl.program_id(1)
    nk = qb * (tq // lc) + (tq // lc) if causal else nk_static
    q_chunk = (q_pos0 + qb * tq + lax.broadcasted_iota(I32, (1, tq), 1)) >> CHUNK_SHIFT

    def chunk_loop(body, init):
        if causal:
            return lax.fori_loop(0, nk, body, init)
        return lax.fori_loop(0, nk_static, body, init, unroll=True)

    def wide_chunk_loop(body, init):
        quad = lambda i, c: body(pl.multiple_of(i * 4 * lc, 4 * lc), 4 * lc, c)
        if not causal:
            carry = lax.fori_loop(0, nk_static // 4, quad, init, unroll=True)
            done = nk_static // 4 * 4
            for span in (2, 1):
                if (nk_static - done) >= span:
                    carry = body(done * lc, span * lc, carry)
                    done += span
            return carry
        carry = lax.fori_loop(0, nk // 4, quad, init)
        done = (nk // 4) * 4
        carry = lax.cond((nk & 2) != 0, lambda c: body(pl.multiple_of(done * lc, 2 * lc), 2 * lc, c), lambda c: c,
                         carry)
        done = done + (nk & 2)
        return lax.cond((nk & 1) != 0, lambda c: body(pl.multiple_of(done * lc, lc), lc, c), lambda c: c, carry)

    def score_span(off, rows, carry):
        kic = kib_ref[0, pl.ds(off, rows), :]
        acc = jnp.zeros((rows, tq), F32)
        for h in range(IDX_HEADS):
            s = lax.dot_general(kic, qi_ref[0, :, h * IDX_DIM:(h + 1) * IDX_DIM], NT_DIMS,
                                preferred_element_type=F32)
            acc = acc + wit_ref[h:h + 1, :] * jnp.maximum(s, 0.0)
        l_pos = off + lax.broadcasted_iota(I32, (rows, tq), 0)
        visible = ((l_pos >> CHUNK_SHIFT) <= q_chunk) & (l_pos < l_valid)
        score = jnp.where(visible, acc, -jnp.inf)
        sc_ref[pl.ds(off, rows), :] = score
        sch_ref[pl.ds(off, rows), :] = score.astype(BF16)
        return carry

    wide_chunk_loop(score_span, 0)

    def count(pred):
        ways = 4 * SUBLANES

        def body(kc, part):
            off = pl.multiple_of(kc * lc, lc)
            hit = jnp.where(pred(sc_ref[pl.ds(off, lc), :]), 1.0, 0.0)
            return part + jnp.sum(hit.reshape(lc // ways, ways, tq), axis=0)
        return jnp.sum(chunk_loop(body, jnp.zeros((ways, tq), F32)), axis=0, keepdims=True)

    def count_coarse(cand):
        ways = 2 * BF16_ROWS
        assert sc_ref.shape[0] // ways <= 256

        def body(kc, part):
            off = pl.multiple_of(kc * lc, lc)
            hit = jnp.where(sch_ref[pl.ds(off, lc), :] >= cand, jnp.ones((), BF16), jnp.zeros((), BF16))
            hit = hit.reshape(lc // ways, ways, tq)
            terms = [hit[j] for j in range(lc // ways)]
            while len(terms) > 1:
                terms = [a + b for a, b in zip(terms[::2], terms[1::2])]
            return part + terms[0]
        part = chunk_loop(body, jnp.zeros((ways, tq), BF16))
        return jnp.sum(part.astype(F32), axis=0, keepdims=True)

    def bisect(count_ge, key_to_value, lo, hi, steps):
        def step(_, carry):
            lo, hi = carry
            mid = (lo >> 1) + (hi >> 1) + (lo & hi & 1)
            ok = count_ge(key_to_value(mid)) >= topk
            return jnp.where(ok, mid, lo), jnp.where(ok, hi, mid)
        return lax.fori_loop(0, steps, step, (lo, hi))[0]

    def coarse_key_to_f32_key(k16):
        return jnp.where(k16 >= 0, k16 << 16, (k16 << 16) | 0xFFFF)

    full = lambda v: jnp.full((1, tq), v, I32)
    k16 = bisect(count_coarse, lambda k: _key_to_float(coarse_key_to_f32_key(k)).astype(BF16),
                 full(KEY_NEG_INF >> 16), full((KEY_POS_INF >> 16) + 1), 16)
    kb = coarse_key_to_f32_key(k16)
    span = 1 << 16
    lo = bisect(lambda cand: count(lambda blk: blk >= cand), _key_to_float,
                jnp.maximum(kb, KEY_NEG_INF + span) - span, jnp.minimum(kb, KEY_POS_INF + 1 - span) + span, 17)
    thr = _key_to_float(lo)
    n_above = count(lambda blk: blk > thr)
    n_ties = topk - n_above

    tri = jnp.where(lax.broadcasted_iota(I32, (lc, lc), 0) >= lax.broadcasted_iota(I32, (lc, lc), 1),
                    1.0, 0.0).astype(BF16)

    def logits_span(off, rows, carry):
        ties_before, m8 = carry
        biases = []
        for r0 in range(0, rows, lc):
            blk = sc_ref[pl.ds(off + r0, lc), :]
            tie = blk == thr
            tie_rank = (jnp.dot(tri, jnp.where(tie, 1.0, 0.0).astype(BF16), preferred_element_type=F32)
                        + ties_before)
            bias = jnp.where(blk > thr, 0.0, jnp.where(tie, jnp.where(tie_rank <= n_ties, 0.0, NEG_BIG), NEG_BIG))
            biases.append(jnp.where(blk == -jnp.inf, NEG_BIG, bias))
            ties_before = tie_rank[lc - 1:lc, :]
        bias = jnp.concatenate(biases, axis=0)
        kc_b = kb_ref[0, pl.ds(off, rows), :]
        m_rows = []
        for h in range(N_HEADS):
            s = lax.dot_general(kc_b, q_ref[0, :, h * HEAD_DIM:(h + 1) * HEAD_DIM], NT_DIMS,
                                preferred_element_type=F32) + bias
            s_ref[h, pl.ds(off, rows), :] = s
            s8 = jnp.max(s.reshape(rows // SUBLANES, SUBLANES, tq), axis=0)
            m_rows.append(jnp.maximum(m8[h * SUBLANES:(h + 1) * SUBLANES], s8))
        return ties_before, jnp.concatenate(m_rows, axis=0)

    _, m8 = wide_chunk_loop(logits_span,
                            (jnp.zeros((1, tq), F32), jnp.full((N_HEADS * SUBLANES, tq), NEG_BIG, F32)))
    m_all = jnp.max(m8.reshape(N_HEADS, SUBLANES, tq), axis=1)

    for acc_ref in acc_refs:
        acc_ref[...] = jnp.zeros(acc_ref.shape, F32)

    def pv_chunk(kc, carry):
        off = pl.multiple_of(kc * lc, lc)
        vt_c = vt_ref[kc]
        for h in range(N_HEADS):
            p = jnp.exp2(s_ref[h, pl.ds(off, lc), :] - m_all[h:h + 1, :])
            acc_refs[h][...] += jnp.dot(vt_c, p.astype(BF16), preferred_element_type=F32)
        return carry

    chunk_loop(pv_chunk, 0)

    for h in range(N_HEADS):
        acc = acc_refs[h][...]
        out_ref[h * HEAD_DIM:(h + 1) * HEAD_DIM, :] = acc[0:HEAD_DIM] / acc[HEAD_DIM:HEAD_DIM + 1]
    o_ref[0] = out_ref[...].T.astype(BF16)


def _dsa(q, qi, wit, kb, kib, vt, *, tq, lc, causal, l_valid, q_pos0, topk):
    bsz, tq_tot, aw = q.shape
    l_tot = kb.shape[1]
    nq = tq_tot // tq
    nkc = l_tot // lc
    kern = functools.partial(_dsa_kernel, tq=tq, lc=lc, nk_static=nkc, causal=causal, l_valid=l_valid,
                             q_pos0=q_pos0, topk=topk)
    return pl.pallas_call(
        kern,
        grid=(bsz, nq),
        in_specs=[
            pl.BlockSpec((1, tq, aw), lambda b, i: (b, i, 0)),
            pl.BlockSpec((1, tq, aw), lambda b, i: (b, i, 0)),
            pl.BlockSpec((IDX_HEADS, tq), lambda b, i: (0, b * nq + i)),
            pl.BlockSpec((1, l_tot, HEAD_DIM), lambda b, i: (b, 0, 0)),
            pl.BlockSpec((1, l_tot, IDX_DIM), lambda b, i: (b, 0, 0)),
            pl.BlockSpec((nkc, VT_ROWS, lc), lambda b, i: (b, 0, 0)),
        ],
        out_specs=pl.BlockSpec((1, tq, aw), lambda b, i: (b, i, 0)),
        out_shape=jax.ShapeDtypeStruct((bsz, tq_tot, aw), BF16),
        scratch_shapes=[
            pltpu.VMEM((l_tot, tq), F32),
            pltpu.VMEM((l_tot, tq), BF16),
            pltpu.VMEM((N_HEADS, l_tot, tq), F32),
            pltpu.VMEM((aw, tq), F32),
        ] + [pltpu.VMEM((VT_ROWS, tq), F32) for _ in range(N_HEADS)],
        compiler_params=_params(("parallel", "parallel")),
        name="dsa",
    )(q, qi, wit, kb, kib, vt)


def _pool_kernel(u_ref, pre_ref, pw_ref, sc_ref, o_ref, ext_ref, *, t_len, tt, pos0):
    pad = pre_ref.shape[1]
    gw = pw_ref.shape[1]
    ext_ref[0:pad, :] = pre_ref[0]
    ext_ref[pad:pad + t_len, :] = u_ref[0]
    row = lax.broadcasted_iota(I32, (tt, gw), 0)
    for t in range(t_len // tt):
        r0 = t * tt
        for g, w in enumerate(POOL_WINDOWS):
            cols = slice(g * gw, (g + 1) * gw)
            cur = ext_ref[pad + r0:pad + r0 + tt, cols]
            wsum = cur
            for j in range(1, w):
                wsum = wsum + ext_ref[pad + r0 - j:pad + r0 - j + tt, cols]
            cnt = jnp.minimum(w, pos0 + r0 + 1 + row).astype(F32)
            diff = (wsum / cnt - cur).astype(BF16)
            y = jnp.dot(diff, pw_ref[g], preferred_element_type=F32)
            o_ref[0, r0:r0 + tt, cols] = (y * sc_ref[:, cols]).astype(BF16)


def _pool(u, prefix, pool_w_b, pool_scale, *, pos0):
    bsz, t_len, c = u.shape
    pad = prefix.shape[1]
    tt = min(t_len, 256)
    return pl.pallas_call(
        functools.partial(_pool_kernel, t_len=t_len, tt=tt, pos0=pos0),
        grid=(bsz,),
        in_specs=[pl.BlockSpec((1, t_len, c), lambda b: (b, 0, 0)),
                  pl.BlockSpec((1, pad, c), lambda b: (b, 0, 0)),
                  pl.BlockSpec(pool_w_b.shape, lambda b: (0, 0, 0)),
                  pl.BlockSpec((1, c), lambda b: (0, 0))],
        out_specs=pl.BlockSpec((1, t_len, c), lambda b: (b, 0, 0)),
        out_shape=jax.ShapeDtypeStruct((bsz, t_len, c), BF16),
        scratch_shapes=[pltpu.VMEM((pad + t_len, c), F32)],
        compiler_params=_params(("parallel",)),
        name="pool",
    )(u, prefix, pool_w_b, pool_scale)


ROW_ALIGN = 16


def _tail_kernel(x_ref, att_ref, pool_ref, g0_ref, b0_ref, woa_ref, wob_ref, g1_ref, b1_ref,
                 wrh_ref, wrl_ref, br_ref,
                 carry0_ref, h1_ref, cpos_ref, gw_ref, seg_ref, len_ref, dst_ref, used_ref, carry_ref, *, tm, sub):
    @pl.when(pl.program_id(0) == 0)
    def _():
        carry_ref[...] = carry0_ref[...]

    chunk_lens = []
    for s in range(sub):
        rows = slice(s * tm, (s + 1) * tm)
        chunk_lens.append(_tail_tile(x_ref[rows, :], att_ref[rows, :], pool_ref[rows, :], g0_ref, b0_ref, woa_ref,
                                     wob_ref, g1_ref, b1_ref, wrh_ref, wrl_ref, br_ref,
                                     h1_ref.at[rows, :], cpos_ref.at[:, rows], gw_ref.at[:, rows], seg_ref.at[s],
                                     len_ref.at[s], tm=tm))
    carry = carry_ref[...]
    for s in range(sub):
        dst_ref[s] = carry
        carry = carry + chunk_lens[s]
    carry_ref[...] = carry
    used_ref[...] = carry


def _tail_tile(x, att, pool, g0_ref, b0_ref, woa_ref, wob_ref, g1_ref, b1_ref, wrh_ref, wrl_ref, br_ref,
               h1_ref, cpos_ref, gw_ref, seg_ref, len_ref, *, tm):
    h = _layer_norm(x, g0_ref[...], b0_ref[...])
    mix = (jnp.dot(att, woa_ref[...], preferred_element_type=F32)
           + jnp.dot(pool, wob_ref[...], preferred_element_type=F32))
    h1 = _layer_norm(DEEPNORM_ALPHA * h + mix, g1_ref[...], b1_ref[...])
    h1_ref[...] = h1

    hh = h1.astype(BF16)
    hl = (h1 - hh.astype(F32)).astype(BF16)
    logits = (lax.dot_general(wrh_ref[...], hh, NT_DIMS, preferred_element_type=F32)
              + lax.dot_general(wrh_ref[...], hl, NT_DIMS, preferred_element_type=F32)
              + lax.dot_general(wrl_ref[...], hh, NT_DIMS, preferred_element_type=F32)
              + br_ref[...])

    e_iota = lax.broadcasted_iota(I32, (N_EXPERTS, tm), 0).astype(F32)
    work = logits
    vals, hots = [], []
    for k in range(TOP_K):
        m = jnp.max(work, axis=0, keepdims=True)
        idx = jnp.min(jnp.where(work == m, e_iota, float(N_EXPERTS)), axis=0, keepdims=True)
        hot = e_iota == idx
        vals.append(m)
        hots.append(jnp.where(hot, 1.0, 0.0))
        work = jnp.where(hot, -jnp.inf, work)
    exps = [jnp.exp(v - vals[0]) for v in vals]
    den = exps[0] + exps[1] + exps[2] + exps[3]
    for k in range(TOP_K):
        gw_ref[k:k + 1, :] = exps[k] / den

    hot_all = (hots[0] + hots[1] + hots[2] + hots[3]).astype(BF16)
    count = jnp.dot(hot_all, jnp.ones((tm, LANES), BF16), preferred_element_type=F32)
    chunk_len = jnp.ceil(count * (1.0 / ROW_ALIGN)) * ROW_ALIGN
    lower = jnp.where(lax.broadcasted_iota(I32, (N_EXPERTS, N_EXPERTS), 1)
                      < lax.broadcasted_iota(I32, (N_EXPERTS, N_EXPERTS), 0), 1.0, 0.0).astype(BF16)
    seg_base = jnp.dot(lower, chunk_len.astype(BF16), preferred_element_type=F32)
    before = jnp.where(lax.broadcasted_iota(I32, (tm, tm), 0) < lax.broadcasted_iota(I32, (tm, tm), 1),
                       1.0, 0.0).astype(BF16)
    slot = jnp.dot(hot_all, before, preferred_element_type=F32) + jnp.tile(seg_base, (1, tm // LANES))
    for k in range(TOP_K):
        cpos_ref[k:k + 1, :] = jnp.sum(hots[k] * slot, axis=0, keepdims=True).astype(I32)
    seg_ref[...] = seg_base
    len_ref[...] = chunk_len
    return chunk_len


def _tail(x2d, att, pool, g0, b0, woa, wob, g1, b1, wrh, wrl, br, carry0, *, tm):
    n, d = x2d.shape
    aw = att.shape[1]
    nt = n // tm
    sub = 2 if nt % 2 == 0 else 1
    ts = sub * tm
    row = lambda i: (i, 0)
    const = lambda i: (0, 0)
    col = lambda i: (0, i)
    tab = lambda i: (i, 0, 0)
    tab_shape = jax.ShapeDtypeStruct((nt, N_EXPERTS, LANES), F32)
    tab_spec = pl.BlockSpec((sub, N_EXPERTS, LANES), tab)
    return pl.pallas_call(
        functools.partial(_tail_kernel, tm=tm, sub=sub),
        grid=(nt // sub,),
        in_specs=[pl.BlockSpec((ts, d), row), pl.BlockSpec((ts, aw), row), pl.BlockSpec((ts, d - aw), row),
                  pl.BlockSpec((1, d), const), pl.BlockSpec((1, d), const),
                  pl.BlockSpec(woa.shape, const), pl.BlockSpec(wob.shape, const),
                  pl.BlockSpec((1, d), const), pl.BlockSpec((1, d), const),
                  pl.BlockSpec(wrh.shape, const), pl.BlockSpec(wrl.shape, const), pl.BlockSpec(br.shape, const),
                  pl.BlockSpec((N_EXPERTS, LANES), const)],
        out_specs=(pl.BlockSpec((ts, d), row), pl.BlockSpec((TOP_K, ts), col), pl.BlockSpec((TOP_K, ts), col),
                   tab_spec, tab_spec, tab_spec, pl.BlockSpec((N_EXPERTS, LANES), const)),
        out_shape=(jax.ShapeDtypeStruct((n, d), F32), jax.ShapeDtypeStruct((TOP_K, n), I32),
                   jax.ShapeDtypeStruct((TOP_K, n), F32), tab_shape, tab_shape, tab_shape,
                   jax.ShapeDtypeStruct((N_EXPERTS, LANES), F32)),
        scratch_shapes=[pltpu.VMEM((N_EXPERTS, LANES), F32)],
        compiler_params=_params(("arbitrary",)),
        name="tail",
    )(x2d, att, pool, g0, b0, woa, wob, g1, b1, wrh, wrl, br, carry0)


G_BLOCK = 256


def _stage_rows(tm):
    raw = TOP_K * tm + (ROW_ALIGN - 1) * N_EXPERTS
    return -(-raw // G_BLOCK) * G_BLOCK


def _chunk_copies(seg_ref, len_ref, dst_ref, tile, make_copy, act):
    def per_expert(e, carry):
        idx = tile * N_EXPERTS + e
        seg, ln, dst = seg_ref[idx], len_ref[idx], dst_ref[idx]

        @pl.when(ln > 0)
        def _():
            act(make_copy(pl.multiple_of(seg, ROW_ALIGN), pl.multiple_of(dst, ROW_ALIGN),
                          pl.multiple_of(ln, ROW_ALIGN)))
        return carry

    lax.fori_loop(0, N_EXPERTS, per_expert, 0)


def _by_staged_blocks(seg_ref, len_ref, tile, n_full, body):
    last = tile * N_EXPERTS + N_EXPERTS - 1
    total = seg_ref[last] + len_ref[last]
    pl.when(total <= (n_full - 1) * G_BLOCK)(lambda: body(n_full - 1))
    pl.when(total > (n_full - 1) * G_BLOCK)(lambda: body(n_full))


def _route_onehot(iota, pos_list, val_list):
    g = jnp.zeros(iota.shape, F32)
    for pos, val in zip(pos_list, val_list):
        g = jnp.where(iota == pos, val, g)
    return g.astype(BF16)


def _dispatch_kernel(seg_ref, len_ref, dst_ref, h1_ref, cpos_ref, *rest, tm):
    xs_hbm, stage_ref, sem = rest[-3:]
    tile = pl.program_id(0)
    slot = tile % 2
    h1b = h1_ref[...].astype(BF16)
    pos_rows = [cpos_ref[k:k + 1, :] for k in range(TOP_K)]

    def stage_blocks(n_blk):
        for blk in range(n_blk):
            iota = blk * G_BLOCK + lax.broadcasted_iota(I32, (G_BLOCK, tm), 0)
            g = _route_onehot(iota, pos_rows, [1.0] * TOP_K)
            stage_ref[slot, blk * G_BLOCK:(blk + 1) * G_BLOCK, :] = jnp.dot(
                g, h1b, preferred_element_type=F32).astype(BF16)

    _by_staged_blocks(seg_ref, len_ref, tile, stage_ref.shape[1] // G_BLOCK, stage_blocks)

    def copy_from(buf):
        def make_copy(seg, dst, size):
            return pltpu.make_async_copy(stage_ref.at[buf, pl.ds(seg, size), :], xs_hbm.at[pl.ds(dst, size), :],
                                         sem.at[buf])
        return make_copy

    _chunk_copies(seg_ref, len_ref, dst_ref, tile, copy_from(slot), lambda c: c.start())

    @pl.when(tile > 0)
    def _():
        _chunk_copies(seg_ref, len_ref, dst_ref, tile - 1, copy_from(1 - slot), lambda c: c.wait())

    @pl.when(tile == pl.num_programs(0) - 1)
    def _():
        _chunk_copies(seg_ref, len_ref, dst_ref, tile, copy_from(slot), lambda c: c.wait())


def _dispatch(seg, ln, dst, h1, cpos, xs_prev, *, tm, p_rows):
    n, d = h1.shape
    in_specs = [pl.BlockSpec((tm, d), lambda i, *_: (i, 0)), pl.BlockSpec((TOP_K, tm), lambda i, *_: (0, i))]
    operands = [seg, ln, dst, h1, cpos]
    aliases = {}
    if xs_prev is not None:
        in_specs.append(pl.BlockSpec(memory_space=pl.ANY))
        aliases = {len(operands): 0}
        operands.append(xs_prev)
    return pl.pallas_call(
        functools.partial(_dispatch_kernel, tm=tm),
        grid_spec=pltpu.PrefetchScalarGridSpec(
            num_scalar_prefetch=3,
            grid=(n // tm,),
            in_specs=in_specs,
            out_specs=pl.BlockSpec(memory_space=pl.ANY),
            scratch_shapes=[pltpu.VMEM((2, _stage_rows(tm), d), BF16), pltpu.SemaphoreType.DMA((2,))],
        ),
        out_shape=jax.ShapeDtypeStruct((p_rows, d), BF16),
        input_output_aliases=aliases,
        compiler_params=_params(("arbitrary",)),
        name="dispatch",
    )(*operands)


FF_CHUNK = 256


def _moe_kernel(te_ref, tv_ref, x_ref, w1_ref, b1_ref, w2_ref, b2_ref, y_ref, w1b_ref, w2b_ref, act_ref):
    i = pl.program_id(0)
    d_ff = w2_ref.shape[1]
    valid = tv_ref[i]

    @pl.when((i == 0) | (te_ref[i] != te_ref[jnp.maximum(i - 1, 0)]))
    def _():
        w1b_ref[...] = w1_ref[0].astype(BF16)
        w2b_ref[...] = w2_ref[0].astype(BF16)

    @pl.when(valid > 0)
    def _():
        rows = lax.broadcasted_iota(I32, x_ref.shape, 0)
        xb = jnp.where(rows < valid, x_ref[...].astype(F32), 0.0).astype(BF16)
        for j in range(d_ff // FF_CHUNK):
            gs = slice(j * FF_CHUNK, (j + 1) * FF_CHUNK)
            us = slice(d_ff + j * FF_CHUNK, d_ff + (j + 1) * FF_CHUNK)
            gate = jnp.dot(xb, w1b_ref[:, gs], preferred_element_type=F32) + b1_ref[0, :, gs]
            up = jnp.dot(xb, w1b_ref[:, us], preferred_element_type=F32) + b1_ref[0, :, us]
            gate = jnp.minimum(gate, SWIGLU_LIMIT)
            up = jnp.clip(up, -SWIGLU_LIMIT, SWIGLU_LIMIT)
            act_ref[:, gs] = ((up + 1.0) * (gate * jax.nn.sigmoid(SWIGLU_ALPHA * gate))).astype(BF16)
        y = jnp.dot(act_ref[...], w2b_ref[...], preferred_element_type=F32) + b2_ref[0]
        y_ref[...] = y.astype(BF16)

    @pl.when(valid <= 0)
    def _():
        y_ref[...] = jnp.zeros(y_ref.shape, BF16)


def _moe(tile_expert, tile_valid, xs, w1, b1, w2, b2, *, tmoe):
    p, d = xs.shape
    e, _, ff2 = w1.shape
    d_ff = ff2 // 2
    return pl.pallas_call(
        _moe_kernel,
        grid_spec=pltpu.PrefetchScalarGridSpec(
            num_scalar_prefetch=2,
            grid=(p // tmoe,),
            in_specs=[pl.BlockSpec((tmoe, d), lambda i, te, tv: (i, 0)),
                      pl.BlockSpec((1, d, ff2), lambda i, te, tv: (te[i], 0, 0)),
                      pl.BlockSpec((1, 1, ff2), lambda i, te, tv: (te[i], 0, 0)),
                      pl.BlockSpec((1, d_ff, d), lambda i, te, tv: (te[i], 0, 0)),
                      pl.BlockSpec((1, 1, d), lambda i, te, tv: (te[i], 0, 0))],
            out_specs=pl.BlockSpec((tmoe, d), lambda i, te, tv: (i, 0)),
            scratch_shapes=[pltpu.VMEM((d, ff2), BF16), pltpu.VMEM((d_ff, d), BF16), pltpu.VMEM((tmoe, d_ff), BF16)],
        ),
        out_shape=jax.ShapeDtypeStruct((p, d), BF16),
        compiler_params=_params(("arbitrary",)),
        name="moe",
    )(tile_expert, tile_valid, xs, w1, b1.reshape(e, 1, ff2), w2, b2.reshape(e, 1, d))


def _combine_kernel(seg_ref, len_ref, dst_ref, h1_ref, cpos_ref, gw_ref, g2_ref, b2_ref, ys_hbm, o_ref,
                    stage_ref, gate_ref, sem, *, tm):
    tile = pl.program_id(0)
    slot = tile % 2

    def copy_into(buf):
        def make_copy(seg, dst, size):
            return pltpu.make_async_copy(ys_hbm.at[pl.ds(dst, size), :], stage_ref.at[buf, pl.ds(seg, size), :],
                                         sem.at[buf])
        return make_copy

    @pl.when(tile == 0)
    def _():
        stage_ref[...] = jnp.zeros(stage_ref.shape, BF16)
        _chunk_copies(seg_ref, len_ref, dst_ref, tile, copy_into(slot), lambda c: c.start())

    @pl.when(tile + 1 < pl.num_programs(0))
    def _():
        _chunk_copies(seg_ref, len_ref, dst_ref, tile + 1, copy_into(1 - slot), lambda c: c.start())

    cpos = cpos_ref[...]
    gw = gw_ref[...]
    pos_cols = [cpos[:, k:k + 1] for k in range(TOP_K)]
    gw_cols = [gw[:, k:k + 1] for k in range(TOP_K)]

    def finish(n_blk):
        _chunk_copies(seg_ref, len_ref, dst_ref, tile, copy_into(slot), lambda c: c.wait())
        m = None
        for blk in range(n_blk):
            iota = blk * G_BLOCK + lax.broadcasted_iota(I32, (tm, G_BLOCK), 1)
            gate = _route_onehot(iota, pos_cols, gw_cols)
            part = jnp.dot(gate, stage_ref[slot, blk * G_BLOCK:(blk + 1) * G_BLOCK, :], preferred_element_type=F32)
            m = part if m is None else part + m
        o_ref[...] = _layer_norm(DEEPNORM_ALPHA * h1_ref[...] + m, g2_ref[...], b2_ref[...])

    _by_staged_blocks(seg_ref, len_ref, tile, stage_ref.shape[1] // G_BLOCK, finish)


def _combine(seg, ln, dst, h1, cpos_nt, gw_nt, g2, b2, ys, *, tm):
    n, d = h1.shape
    return pl.pallas_call(
        functools.partial(_combine_kernel, tm=tm),
        grid_spec=pltpu.PrefetchScalarGridSpec(
            num_scalar_prefetch=3,
            grid=(n // tm,),
            in_specs=[pl.BlockSpec((tm, d), lambda i, *_: (i, 0)),
                      pl.BlockSpec((tm, TOP_K), lambda i, *_: (i, 0)), pl.BlockSpec((tm, TOP_K), lambda i, *_: (i, 0)),
                      pl.BlockSpec((1, d), lambda i, *_: (0, 0)), pl.BlockSpec((1, d), lambda i, *_: (0, 0)),
                      pl.BlockSpec(memory_space=pl.ANY)],
            out_specs=pl.BlockSpec((tm, d), lambda i, *_: (i, 0)),
            scratch_shapes=[pltpu.VMEM((2, _stage_rows(tm), d), BF16), pltpu.VMEM((tm, _stage_rows(tm)), BF16),
                            pltpu.SemaphoreType.DMA((2,))],
        ),
        out_shape=jax.ShapeDtypeStruct((n, d), F32),
        compiler_params=_params(("arbitrary",)),
        name="combine",
    )(seg, ln, dst, h1, cpos_nt, gw_nt, g2, b2, ys)


def _block_tails(groups, wts, *, tm, tmoe):
    carry = jnp.zeros((N_EXPERTS, LANES), F32)
    plans, max_rows = [], 0
    for x2d, att, pool in groups:
        n = x2d.shape[0]
        tm_g = min(tm, n)
        h1, cpos, gw_t, seg_t, len_t, dst_t, carry = _tail(
            x2d, att, pool, wts["g0"], wts["b0"], wts["woa"], wts["wob"], wts["g1"], wts["b1"],
            wts["wrh"], wts["wrl"], wts["br"], carry, tm=tm_g)
        plans.append((tm_g, h1, cpos, gw_t, seg_t, len_t, dst_t))
        max_rows += TOP_K * n + (ROW_ALIGN - 1) * N_EXPERTS * (n // tm_g)
    used = carry[:, 0].astype(I32)
    cap = ((used + tmoe - 1) // tmoe) * tmoe
    ends = jnp.cumsum(cap)
    offs = ends - cap
    n_tiles = -(-max_rows // tmoe) + N_EXPERTS
    tile_start = jnp.arange(n_tiles, dtype=I32) * tmoe
    tile_expert = jnp.minimum(jnp.sum((ends[None, :] <= tile_start[:, None]).astype(I32), axis=1), N_EXPERTS - 1)
    tile_valid = jnp.clip(offs[tile_expert] + used[tile_expert] - tile_start, 0, tmoe).astype(I32)
    tile_valid = jnp.where(tile_start < ends[-1], tile_valid, 0)

    xs, tables = None, []
    for tm_g, h1, cpos, gw_t, seg_t, len_t, dst_t in plans:
        seg = seg_t[:, :, 0].astype(I32).reshape(-1)
        ln = len_t[:, :, 0].astype(I32).reshape(-1)
        dst = (dst_t[:, :, 0].astype(I32) + offs[None, :]).reshape(-1)
        tables.append((seg, ln, dst))
        xs = _dispatch(seg, ln, dst, h1, cpos, xs, tm=tm_g, p_rows=n_tiles * tmoe)
    ys = _moe(tile_expert, tile_valid, xs, wts["w1e"], wts["b1e"], wts["w2e"], wts["b2e"], tmoe=tmoe)
    return [_combine(seg, ln, dst, h1, cpos.T, gw_t.T, wts["g2"], wts["b2"], ys, tm=tm_g)
            for (seg, ln, dst), (tm_g, h1, cpos, gw_t, _, _, _) in zip(tables, plans)]


def kernel(x_prompt, x_sample, cache_k, cache_v, cache_kidx, state_pool, ln0_g, ln0_b, w_in, w_o,
           pool_w, pool_scale, ln1_g, ln1_b, w_router, b_router, w1, b1, w2, b2, ln2_g, ln2_b):
    bp, s_len, d = x_prompt.shape
    bs, t_len, _ = x_sample.shape
    l_past = cache_k.shape[2]
    aw = N_HEADS * HEAD_DIM
    pw = d - aw
    lyr = 0
    lc = 256

    k_off = aw
    v_off = k_off + HEAD_DIM
    qi_off = v_off + HEAD_DIM
    ki_off = qi_off + IDX_HEADS * IDX_DIM
    wi_off = ki_off + IDX_DIM
    u_off = wi_off + IDX_HEADS
    win = w_in[lyr]
    wa = jnp.concatenate([win[:, 0:k_off], win[:, qi_off:ki_off], win[:, u_off:u_off + pw]], axis=1).astype(BF16)
    wb = jnp.concatenate([win[:, k_off:v_off], win[:, v_off:qi_off], win[:, ki_off:wi_off], win[:, wi_off:u_off],
                          jnp.zeros((d, HEAD_DIM - IDX_HEADS), F32)], axis=1).astype(BF16)

    g0 = ln0_g.reshape(1, d)
    b0 = ln0_b.reshape(1, d)
    wrt = w_router[lyr].T
    wrh = wrt.astype(BF16)
    wts = dict(
        g0=g0, b0=b0,
        woa=w_o[lyr][:aw].astype(BF16), wob=w_o[lyr][aw:].astype(BF16),
        g1=ln1_g[lyr].reshape(1, d), b1=ln1_b[lyr].reshape(1, d),
        wrh=wrh, wrl=(wrt - wrh.astype(F32)).astype(BF16), br=b_router[lyr].reshape(N_EXPERTS, 1),
        w1e=w1[lyr], b1e=b1[lyr], w2e=w2[lyr], b2e=b2[lyr],
        g2=ln2_g[lyr].reshape(1, d), b2=ln2_b[lyr].reshape(1, d),
    )
    pool_w_b = pool_w[lyr].astype(BF16)
    pool_sc = pool_scale[lyr].reshape(1, pw)

    xp = x_prompt.reshape(bp * s_len, d)
    q, qi, u, k, v, ki, kb, kib, vt, wit = _proj(xp, g0, b0, wa, wb, tm=512, lc=lc)
    att_p = _dsa(q.reshape(bp, s_len, aw), qi.reshape(bp, s_len, aw), wit,
                 kb.reshape(bp, s_len, HEAD_DIM), kib.reshape(bp, s_len, IDX_DIM), vt,
                 tq=256, lc=lc, causal=True, l_valid=s_len, q_pos0=0, topk=min(TOPK_MAX, s_len // 4))
    u_p = u.reshape(bp, s_len, pw)
    pool_p = _pool(u_p, jnp.zeros((bp, POOL_PAST + 1, pw), F32), pool_w_b, pool_sc, pos0=0)

    xs = x_sample.reshape(bs * t_len, d)
    qs, qis, us, kn, vn, kin, _, _, _, wits = _proj(xs, g0, b0, wa, wb, tm=512, lc=lc)
    l_all = l_past + t_len
    l_pad = -(-l_all // lc) * lc
    tq_s = LANES
    pad_keys = lambda a: jnp.pad(a, ((0, 0), (0, l_pad - l_all), (0, 0)))
    k_all = pad_keys(jnp.concatenate([cache_k[lyr], kn.reshape(bs, t_len, HEAD_DIM)], axis=1))
    v_all = pad_keys(jnp.concatenate([cache_v[lyr], vn.reshape(bs, t_len, HEAD_DIM)], axis=1))
    ki_all = pad_keys(jnp.concatenate([cache_kidx[lyr], kin.reshape(bs, t_len, IDX_DIM)], axis=1))
    vt_all = v_all.reshape(bs, l_pad // lc, lc, HEAD_DIM).transpose(0, 1, 3, 2).reshape(-1, HEAD_DIM, lc)
    vt_all = jnp.concatenate([vt_all, jnp.broadcast_to(_denominator_rows(lc), (vt_all.shape[0], VT_ROWS - HEAD_DIM, lc))],
                             axis=1)
    pad_q = lambda a: jnp.pad(a.reshape(bs, t_len, aw), ((0, 0), (0, tq_s - t_len), (0, 0)))
    wit_s = jnp.pad(wits.reshape(IDX_HEADS, bs, t_len), ((0, 0), (0, 0), (0, tq_s - t_len))).reshape(IDX_HEADS, -1)
    att_s = _dsa(pad_q(qs), pad_q(qis), wit_s, k_all.astype(BF16), ki_all.astype(BF16), vt_all.astype(BF16),
                 tq=tq_s, lc=lc, causal=False, l_valid=l_all, q_pos0=l_past, topk=min(TOPK_MAX, l_all // 4))
    att_s = att_s[:, :t_len].reshape(bs * t_len, aw)
    us3 = us.reshape(bs, t_len, pw)
    prefix_s = jnp.concatenate([jnp.zeros((bs, 1, pw), F32), state_pool[lyr]], axis=1)
    pool_s = _pool(us3, prefix_s, pool_w_b, pool_sc, pos0=l_past)

    y_p, y_s = _block_tails([(xp, att_p.reshape(bp * s_len, aw), pool_p.reshape(bp * s_len, pw)),
                             (xs, att_s, pool_s.reshape(bs * t_len, pw))], wts, tm=512, tmoe=1024)

    pool_state_p = u_p[:, s_len - POOL_PAST:]
    pool_state_s = jnp.concatenate([state_pool[lyr], us3], axis=1)[:, -POOL_PAST:]
    return (y_p.reshape(bp, s_len, d), y_s.reshape(bs, t_len, d),
            k.reshape(1, bp, s_len, HEAD_DIM), v.reshape(1, bp, s_len, HEAD_DIM),
            ki.reshape(1, bp, s_len, IDX_DIM), pool_state_p[None],
            kn.reshape(1, bs, t_len, HEAD_DIM), vn.reshape(1, bs, t_len, HEAD_DIM),
            kin.reshape(1, bs, t_len, IDX_DIM), pool_state_s[None])
```

```python
import functools

import jax
import jax.numpy as jnp
from jax import lax
from jax.experimental import pallas as pl
from jax.experimental.pallas import tpu as pltpu

F32 = jnp.float32
BF16 = jnp.bfloat16
I32 = jnp.int32

CHUNK = 64
CHUNK_SHIFT = 6
assert 1 << CHUNK_SHIFT == CHUNK
N_HEADS = 8
HEAD_DIM = 64
IDX_HEADS = 8
IDX_DIM = 64
TOPK_MAX = 256
POOL_WINDOWS = (2, 4, 8, 16)
POOL_PAST = 15
N_EXPERTS = 32
TOP_K = 4
SWIGLU_LIMIT = 7.0
SWIGLU_ALPHA = 1.702
LN_EPS = 1e-5
DEPTH = 1
DEEPNORM_ALPHA = (2 * DEPTH) ** 0.25
LOG2_E = 1.4426950408889634

LANES = 128
SUBLANES = 8
BF16_ROWS = 16
VMEM_LIMIT_BYTES = 56 * 1024 * 1024

NEG_BIG = -1e30
KEY_NEG_INF = -2139095041
KEY_POS_INF = 2139095040

NT_DIMS = (((1,), (1,)), ((), ()))


def _layer_norm(x, g, b):
    mu = jnp.mean(x, axis=-1, keepdims=True)
    xc = x - mu
    var = jnp.mean(xc * xc, axis=-1, keepdims=True)
    return xc * lax.rsqrt(var + LN_EPS) * g + b


def _params(sem):
    return pltpu.CompilerParams(dimension_semantics=sem, vmem_limit_bytes=VMEM_LIMIT_BYTES)


VT_ROWS = HEAD_DIM + BF16_ROWS


def _denominator_rows(width):
    return jnp.where(lax.broadcasted_iota(I32, (VT_ROWS - HEAD_DIM, width), 0) == 0, 1.0, 0.0)


def _proj_kernel(x_ref, g_ref, b_ref, wa_ref, wb_ref,
                 q_ref, qi_ref, u_ref, k_ref, v_ref, ki_ref, kb_ref, kib_ref, vt_ref, wit_ref, *, lc):
    h = _layer_norm(x_ref[...], g_ref[...], b_ref[...])
    hb = h.astype(BF16)
    aw = N_HEADS * HEAD_DIM
    pa = jnp.dot(hb, wa_ref[...], preferred_element_type=F32)
    q_ref[...] = (pa[:, :aw] * (HEAD_DIM ** -0.5 * LOG2_E)).astype(BF16)
    qi_ref[...] = pa[:, aw:2 * aw].astype(BF16)
    u_ref[...] = pa[:, 2 * aw:]
    pb = jnp.dot(hb, wb_ref[...], preferred_element_type=F32)
    k = pb[:, 0:HEAD_DIM]
    v = pb[:, HEAD_DIM:2 * HEAD_DIM]
    ki = pb[:, 2 * HEAD_DIM:2 * HEAD_DIM + IDX_DIM]
    k_ref[...] = k
    v_ref[...] = v
    ki_ref[...] = ki
    kb_ref[...] = k.astype(BF16)
    kib_ref[...] = ki.astype(BF16)
    pt = pb.T
    ones_rows = _denominator_rows(lc)
    for c in range(vt_ref.shape[0]):
        vt_ref[c] = jnp.concatenate([pt[HEAD_DIM:2 * HEAD_DIM, c * lc:(c + 1) * lc], ones_rows],
                                    axis=0).astype(BF16)
    wi = pt[3 * HEAD_DIM:3 * HEAD_DIM + IDX_HEADS, :]
    wit_ref[...] = (wi * (IDX_HEADS ** -0.5)) * (IDX_DIM ** -0.5)


def _proj(x2d, g, b, wa, wb, *, tm, lc):
    n, d = x2d.shape
    tm = min(tm, n)
    aw = N_HEADS * HEAD_DIM
    uw = wa.shape[1] - 2 * aw
    row = lambda i: (i, 0)
    const = lambda i: (0, 0)
    out_shape = (
        jax.ShapeDtypeStruct((n, aw), BF16),
        jax.ShapeDtypeStruct((n, aw), BF16),
        jax.ShapeDtypeStruct((n, uw), F32),
        jax.ShapeDtypeStruct((n, HEAD_DIM), F32),
        jax.ShapeDtypeStruct((n, HEAD_DIM), F32),
        jax.ShapeDtypeStruct((n, IDX_DIM), F32),
        jax.ShapeDtypeStruct((n, HEAD_DIM), BF16),
        jax.ShapeDtypeStruct((n, IDX_DIM), BF16),
        jax.ShapeDtypeStruct((n // lc, VT_ROWS, lc), BF16),
        jax.ShapeDtypeStruct((IDX_HEADS, n), F32),
    )
    out_specs = (
        pl.BlockSpec((tm, aw), row), pl.BlockSpec((tm, aw), row), pl.BlockSpec((tm, uw), row),
        pl.BlockSpec((tm, HEAD_DIM), row), pl.BlockSpec((tm, HEAD_DIM), row), pl.BlockSpec((tm, IDX_DIM), row),
        pl.BlockSpec((tm, HEAD_DIM), row), pl.BlockSpec((tm, IDX_DIM), row),
        pl.BlockSpec((tm // lc, VT_ROWS, lc), lambda i: (i, 0, 0)),
        pl.BlockSpec((IDX_HEADS, tm), lambda i: (0, i)),
    )
    return pl.pallas_call(
        functools.partial(_proj_kernel, lc=lc),
        grid=(n // tm,),
        in_specs=[pl.BlockSpec((tm, d), row), pl.BlockSpec((1, d), const), pl.BlockSpec((1, d), const),
                  pl.BlockSpec(wa.shape, const), pl.BlockSpec(wb.shape, const)],
        out_specs=out_specs,
        out_shape=out_shape,
        compiler_params=_params(("parallel",)),
        name="proj",
    )(x2d, g, b, wa, wb)


def _key_to_float(key):
    bits = jnp.where(key >= 0, key, key ^ jnp.int32(0x7FFFFFFF))
    return lax.bitcast_convert_type(bits, F32)


def _dsa_kernel(q_ref, qi_ref, wit_ref, kb_ref, kib_ref, vt_ref, o_ref,
                sc_ref, sch_ref, s_ref, out_ref, *acc_refs, tq, lc, nk_static, causal, l_valid, q_pos0, topk):
    qb = pl.program_id(1)
    nk = qb * (tq // lc) + (tq // lc) if causal else nk_static
    q_chunk = (q_pos0 + qb * tq + lax.broadcasted_iota(I32, (1, tq), 1)) >> CHUNK_SHIFT

    def chunk_loop(body, init):
        if causal:
            return lax.fori_loop(0, nk, body, init)
        return lax.fori_loop(0, nk_static, body, init, unroll=True)

    def wide_chunk_loop(body, init):
        quad = lambda i, c: body(pl.multiple_of(i * 4 * lc, 4 * lc), 4 * lc, c)
        if not causal:
            carry = lax.fori_loop(0, nk_static // 4, quad, init, unroll=True)
            done = nk_static // 4 * 4
            for span in (2, 1):
                if (nk_static - done) >= span:
                    carry = body(done * lc, span * lc, carry)
                    done += span
            return carry
        carry = lax.fori_loop(0, nk // 4, quad, init)
        done = (nk // 4) * 4
        carry = lax.cond((nk & 2) != 0, lambda c: body(pl.multiple_of(done * lc, 2 * lc), 2 * lc, c), lambda c: c,
                         carry)
        done = done + (nk & 2)
        return lax.cond((nk & 1) != 0, lambda c: body(pl.multiple_of(done * lc, lc), lc, c), lambda c: c, carry)

    def score_span(off, rows, carry):
        kic = kib_ref[0, pl.ds(off, rows), :]
        acc = jnp.zeros((rows, tq), F32)
        for h in range(IDX_HEADS):
            s = lax.dot_general(kic, qi_ref[0, :, h * IDX_DIM:(h + 1) * IDX_DIM], NT_DIMS,
                                preferred_element_type=F32)
            acc = acc + wit_ref[h:h + 1, :] * jnp.maximum(s, 0.0)
        l_pos = off + lax.broadcasted_iota(I32, (rows, tq), 0)
        visible = ((l_pos >> CHUNK_SHIFT) <= q_chunk) & (l_pos < l_valid)
        score = jnp.where(visible, acc, -jnp.inf)
        sc_ref[pl.ds(off, rows), :] = score
        sch_ref[pl.ds(off, rows), :] = score.astype(BF16)
        return carry

    wide_chunk_loop(score_span, 0)

    def count(pred):
        ways = 4 * SUBLANES

        def body(kc, part):
            off = pl.multiple_of(kc * lc, lc)
            hit = jnp.where(pred(sc_ref[pl.ds(off, lc), :]), 1.0, 0.0)
            return part + jnp.sum(hit.reshape(lc // ways, ways, tq), axis=0)
        return jnp.sum(chunk_loop(body, jnp.zeros((ways, tq), F32)), axis=0, keepdims=True)

    def count_coarse(cand):
        ways = 2 * BF16_ROWS
        assert sc_ref.shape[0] // ways <= 256

        def body(kc, part):
            off = pl.multiple_of(kc * lc, lc)
            hit = jnp.where(sch_ref[pl.ds(off, lc), :] >= cand, jnp.ones((), BF16), jnp.zeros((), BF16))
            hit = hit.reshape(lc // ways, ways, tq)
            terms = [hit[j] for j in range(lc // ways)]
            while len(terms) > 1:
                terms = [a + b for a, b in zip(terms[::2], terms[1::2])]
            return part + terms[0]
        part = chunk_loop(body, jnp.zeros((ways, tq), BF16))
        return jnp.sum(part.astype(F32), axis=0, keepdims=True)

    def bisect(count_ge, key_to_value, lo, hi, steps):
        def step(_, carry):
            lo, hi = carry
            mid = (lo >> 1) + (hi >> 1) + (lo & hi & 1)
            ok = count_ge(key_to_value(mid)) >= topk
            return jnp.where(ok, mid, lo), jnp.where(ok, hi, mid)
        return lax.fori_loop(0, steps, step, (lo, hi))[0]

    def coarse_key_to_f32_key(k16):
        return jnp.where(k16 >= 0, k16 << 16, (k16 << 16) | 0xFFFF)

    full = lambda v: jnp.full((1, tq), v, I32)
    k16 = bisect(count_coarse, lambda k: _key_to_float(coarse_key_to_f32_key(k)).astype(BF16),
                 full(KEY_NEG_INF >> 16), full((KEY_POS_INF >> 16) + 1), 16)
    kb = coarse_key_to_f32_key(k16)
    span = 1 << 16
    lo = bisect(lambda cand: count(lambda blk: blk >= cand), _key_to_float,
                jnp.maximum(kb, KEY_NEG_INF + span) - span, jnp.minimum(kb, KEY_POS_INF + 1 - span) + span, 17)
    thr = _key_to_float(lo)
    n_above = count(lambda blk: blk > thr)
    n_ties = topk - n_above

    tri = jnp.where(lax.broadcasted_iota(I32, (lc, lc), 0) >= lax.broadcasted_iota(I32, (lc, lc), 1),
                    1.0, 0.0).astype(BF16)

    def logits_span(off, rows, carry):
        ties_before, m8 = carry
        biases = []
        for r0 in range(0, rows, lc):
            blk = sc_ref[pl.ds(off + r0, lc), :]
            tie = blk == thr
            tie_rank = (jnp.dot(tri, jnp.where(tie, 1.0, 0.0).astype(BF16), preferred_element_type=F32)
                        + ties_before)
            bias = jnp.where(blk > thr, 0.0, jnp.where(tie, jnp.where(tie_rank <= n_ties, 0.0, NEG_BIG), NEG_BIG))
            biases.append(jnp.where(blk == -jnp.inf, NEG_BIG, bias))
            ties_before = tie_rank[lc - 1:lc, :]
        bias = jnp.concatenate(biases, axis=0)
        kc_b = kb_ref[0, pl.ds(off, rows), :]
        m_rows = []
        for h in range(N_HEADS):
            s = lax.dot_general(kc_b, q_ref[0, :, h * HEAD_DIM:(h + 1) * HEAD_DIM], NT_DIMS,
                                preferred_element_type=F32) + bias
            s_ref[h, pl.ds(off, rows), :] = s
            s8 = jnp.max(s.reshape(rows // SUBLANES, SUBLANES, tq), axis=0)
            m_rows.append(jnp.maximum(m8[h * SUBLANES:(h + 1) * SUBLANES], s8))
        return ties_before, jnp.concatenate(m_rows, axis=0)

    _, m8 = wide_chunk_loop(logits_span,
                            (jnp.zeros((1, tq), F32), jnp.full((N_HEADS * SUBLANES, tq), NEG_BIG, F32)))
    m_all = jnp.max(m8.reshape(N_HEADS, SUBLANES, tq), axis=1)

    for acc_ref in acc_refs:
        acc_ref[...] = jnp.zeros(acc_ref.shape, F32)

    def pv_chunk(kc, carry):
        off = pl.multiple_of(kc * lc, lc)
        vt_c = vt_ref[kc]
        for h in range(N_HEADS):
            p = jnp.exp2(s_ref[h, pl.ds(off, lc), :] - m_all[h:h + 1, :])
            acc_refs[h][...] += jnp.dot(vt_c, p.astype(BF16), preferred_element_type=F32)
        return carry

    chunk_loop(pv_chunk, 0)

    for h in range(N_HEADS):
        acc = acc_refs[h][...]
        out_ref[h * HEAD_DIM:(h + 1) * HEAD_DIM, :] = acc[0:HEAD_DIM] / acc[HEAD_DIM:HEAD_DIM + 1]
    o_ref[0] = out_ref[...].T.astype(BF16)


def _dsa(q, qi, wit, kb, kib, vt, *, tq, lc, causal, l_valid, q_pos0, topk):
    bsz, tq_tot, aw = q.shape
    l_tot = kb.shape[1]
    nq = tq_tot // tq
    nkc = l_tot // lc
    kern = functools.partial(_dsa_kernel, tq=tq, lc=lc, nk_static=nkc, causal=causal, l_valid=l_valid,
                             q_pos0=q_pos0, topk=topk)
    return pl.pallas_call(
        kern,
        grid=(bsz, nq),
        in_specs=[
            pl.BlockSpec((1, tq, aw), lambda b, i: (b, i, 0)),
            pl.BlockSpec((1, tq, aw), lambda b, i: (b, i, 0)),
            pl.BlockSpec((IDX_HEADS, tq), lambda b, i: (0, b * nq + i)),
            pl.BlockSpec((1, l_tot, HEAD_DIM), lambda b, i: (b, 0, 0)),
            pl.BlockSpec((1, l_tot, IDX_DIM), lambda b, i: (b, 0, 0)),
            pl.BlockSpec((nkc, VT_ROWS, lc), lambda b, i: (b, 0, 0)),
        ],
        out_specs=pl.BlockSpec((1, tq, aw), lambda b, i: (b, i, 0)),
        out_shape=jax.ShapeDtypeStruct((bsz, tq_tot, aw), BF16),
        scratch_shapes=[
            pltpu.VMEM((l_tot, tq), F32),
            pltpu.VMEM((l_tot, tq), BF16),
            pltpu.VMEM((N_HEADS, l_tot, tq), F32),
            pltpu.VMEM((aw, tq), F32),
        ] + [pltpu.VMEM((VT_ROWS, tq), F32) for _ in range(N_HEADS)],
        compiler_params=_params(("parallel", "parallel")),
        name="dsa",
    )(q, qi, wit, kb, kib, vt)


def _pool_kernel(u_ref, pre_ref, pw_ref, sc_ref, o_ref, ext_ref, *, t_len, tt, pos0):
    pad = pre_ref.shape[1]
    gw = pw_ref.shape[1]
    ext_ref[0:pad, :] = pre_ref[0]
    ext_ref[pad:pad + t_len, :] = u_ref[0]
    row = lax.broadcasted_iota(I32, (tt, gw), 0)
    for t in range(t_len // tt):
        r0 = t * tt
        for g, w in enumerate(POOL_WINDOWS):
            cols = slice(g * gw, (g + 1) * gw)
            cur = ext_ref[pad + r0:pad + r0 + tt, cols]
            wsum = cur
            for j in range(1, w):
                wsum = wsum + ext_ref[pad + r0 - j:pad + r0 - j + tt, cols]
            cnt = jnp.minimum(w, pos0 + r0 + 1 + row).astype(F32)
            diff = (wsum / cnt - cur).astype(BF16)
            y = jnp.dot(diff, pw_ref[g], preferred_element_type=F32)
            o_ref[0, r0:r0 + tt, cols] = (y * sc_ref[:, cols]).astype(BF16)


def _pool(u, prefix, pool_w_b, pool_scale, *, pos0):
    bsz, t_len, c = u.shape
    pad = prefix.shape[1]
    tt = min(t_len, 256)
    return pl.pallas_call(
        functools.partial(_pool_kernel, t_len=t_len, tt=tt, pos0=pos0),
        grid=(bsz,),
        in_specs=[pl.BlockSpec((1, t_len, c), lambda b: (b, 0, 0)),
                  pl.BlockSpec((1, pad, c), lambda b: (b, 0, 0)),
                  pl.BlockSpec(pool_w_b.shape, lambda b: (0, 0, 0)),
                  pl.BlockSpec((1, c), lambda b: (0, 0))],
        out_specs=pl.BlockSpec((1, t_len, c), lambda b: (b, 0, 0)),
        out_shape=jax.ShapeDtypeStruct((bsz, t_len, c), BF16),
        scratch_shapes=[pltpu.VMEM((pad + t_len, c), F32)],
        compiler_params=_params(("parallel",)),
        name="pool",
    )(u, prefix, pool_w_b, pool_scale)


ROW_ALIGN = 16


def _tail_kernel(x_ref, att_ref, pool_ref, g0_ref, b0_ref, woa_ref, wob_ref, g1_ref, b1_ref,
                 wrh_ref, wrl_ref, br_ref,
                 carry0_ref, h1_ref, cpos_ref, gw_ref, seg_ref, len_ref, dst_ref, used_ref, carry_ref, *, tm, sub):
    @pl.when(pl.program_id(0) == 0)
    def _():
        carry_ref[...] = carry0_ref[...]

    chunk_lens = []
    for s in range(sub):
        rows = slice(s * tm, (s + 1) * tm)
        chunk_lens.append(_tail_tile(x_ref[rows, :], att_ref[rows, :], pool_ref[rows, :], g0_ref, b0_ref, woa_ref,
                                     wob_ref, g1_ref, b1_ref, wrh_ref, wrl_ref, br_ref,
                                     h1_ref.at[rows, :], cpos_ref.at[:, rows], gw_ref.at[:, rows], seg_ref.at[s],
                                     len_ref.at[s], tm=tm))
    carry = carry_ref[...]
    for s in range(sub):
        dst_ref[s] = carry
        carry = carry + chunk_lens[s]
    carry_ref[...] = carry
    used_ref[...] = carry


def _tail_tile(x, att, pool, g0_ref, b0_ref, woa_ref, wob_ref, g1_ref, b1_ref, wrh_ref, wrl_ref, br_ref,
               h1_ref, cpos_ref, gw_ref, seg_ref, len_ref, *, tm):
    h = _layer_norm(x, g0_ref[...], b0_ref[...])
    mix = (jnp.dot(att, woa_ref[...], preferred_element_type=F32)
           + jnp.dot(pool, wob_ref[...], preferred_element_type=F32))
    h1 = _layer_norm(DEEPNORM_ALPHA * h + mix, g1_ref[...], b1_ref[...])
    h1_ref[...] = h1

    hh = h1.astype(BF16)
    hl = (h1 - hh.astype(F32)).astype(BF16)
    logits = (lax.dot_general(wrh_ref[...], hh, NT_DIMS, preferred_element_type=F32)
              + lax.dot_general(wrh_ref[...], hl, NT_DIMS, preferred_element_type=F32)
              + lax.dot_general(wrl_ref[...], hh, NT_DIMS, preferred_element_type=F32)
              + br_ref[...])

    e_iota = lax.broadcasted_iota(I32, (N_EXPERTS, tm), 0).astype(F32)
    work = logits
    vals, hots = [], []
    for k in range(TOP_K):
        m = jnp.max(work, axis=0, keepdims=True)
        idx = jnp.min(jnp.where(work == m, e_iota, float(N_EXPERTS)), axis=0, keepdims=True)
        hot = e_iota == idx
        vals.append(m)
        hots.append(jnp.where(hot, 1.0, 0.0))
        work = jnp.where(hot, -jnp.inf, work)
    exps = [jnp.exp(v - vals[0]) for v in vals]
    den = exps[0] + exps[1] + exps[2] + exps[3]
    for k in range(TOP_K):
        gw_ref[k:k + 1, :] = exps[k] / den

    hot_all = (hots[0] + hots[1] + hots[2] + hots[3]).astype(BF16)
    count = jnp.dot(hot_all, jnp.ones((tm, LANES), BF16), preferred_element_type=F32)
    chunk_len = jnp.ceil(count * (1.0 / ROW_ALIGN)) * ROW_ALIGN
    lower = jnp.where(lax.broadcasted_iota(I32, (N_EXPERTS, N_EXPERTS), 1)
                      < lax.broadcasted_iota(I32, (N_EXPERTS, N_EXPERTS), 0), 1.0, 0.0).astype(BF16)
    seg_base = jnp.dot(lower, chunk_len.astype(BF16), preferred_element_type=F32)
    before = jnp.where(lax.broadcasted_iota(I32, (tm, tm), 0) < lax.broadcasted_iota(I32, (tm, tm), 1),
                       1.0, 0.0).astype(BF16)
    slot = jnp.dot(hot_all, before, preferred_element_type=F32) + jnp.tile(seg_base, (1, tm // LANES))
    for k in range(TOP_K):
        cpos_ref[k:k + 1, :] = jnp.sum(hots[k] * slot, axis=0, keepdims=True).astype(I32)
    seg_ref[...] = seg_base
    len_ref[...] = chunk_len
    return chunk_len


def _tail(x2d, att, pool, g0, b0, woa, wob, g1, b1, wrh, wrl, br, carry0, *, tm):
    n, d = x2d.shape
    aw = att.shape[1]
    nt = n // tm
    sub = 2 if nt % 2 == 0 else 1
    ts = sub * tm
    row = lambda i: (i, 0)
    const = lambda i: (0, 0)
    col = lambda i: (0, i)
    tab = lambda i: (i, 0, 0)
    tab_shape = jax.ShapeDtypeStruct((nt, N_EXPERTS, LANES), F32)
    tab_spec = pl.BlockSpec((sub, N_EXPERTS, LANES), tab)
    return pl.pallas_call(
        functools.partial(_tail_kernel, tm=tm, sub=sub),
        grid=(nt // sub,),
        in_specs=[pl.BlockSpec((ts, d), row), pl.BlockSpec((ts, aw), row), pl.BlockSpec((ts, d - aw), row),
                  pl.BlockSpec((1, d), const), pl.BlockSpec((1, d), const),
                  pl.BlockSpec(woa.shape, const), pl.BlockSpec(wob.shape, const),
                  pl.BlockSpec((1, d), const), pl.BlockSpec((1, d), const),
                  pl.BlockSpec(wrh.shape, const), pl.BlockSpec(wrl.shape, const), pl.BlockSpec(br.shape, const),
                  pl.BlockSpec((N_EXPERTS, LANES), const)],
        out_specs=(pl.BlockSpec((ts, d), row), pl.BlockSpec((TOP_K, ts), col), pl.BlockSpec((TOP_K, ts), col),
                   tab_spec, tab_spec, tab_spec, pl.BlockSpec((N_EXPERTS, LANES), const)),
        out_shape=(jax.ShapeDtypeStruct((n, d), F32), jax.ShapeDtypeStruct((TOP_K, n), I32),
                   jax.ShapeDtypeStruct((TOP_K, n), F32), tab_shape, tab_shape, tab_shape,
                   jax.ShapeDtypeStruct((N_EXPERTS, LANES), F32)),
        scratch_shapes=[pltpu.VMEM((N_EXPERTS, LANES), F32)],
        compiler_params=_params(("arbitrary",)),
        name="tail",
    )(x2d, att, pool, g0, b0, woa, wob, g1, b1, wrh, wrl, br, carry0)


G_BLOCK = 256


def _stage_rows(tm):
    raw = TOP_K * tm + (ROW_ALIGN - 1) * N_EXPERTS
    return -(-raw // G_BLOCK) * G_BLOCK


def _chunk_copies(seg_ref, len_ref, dst_ref, tile, make_copy, act):
    def per_expert(e, carry):
        idx = tile * N_EXPERTS + e
        seg, ln, dst = seg_ref[idx], len_ref[idx], dst_ref[idx]

        @pl.when(ln > 0)
        def _():
            act(make_copy(pl.multiple_of(seg, ROW_ALIGN), pl.multiple_of(dst, ROW_ALIGN),
                          pl.multiple_of(ln, ROW_ALIGN)))
        return carry

    lax.fori_loop(0, N_EXPERTS, per_expert, 0)


def _by_staged_blocks(seg_ref, len_ref, tile, n_full, body):
    last = tile * N_EXPERTS + N_EXPERTS - 1
    total = seg_ref[last] + len_ref[last]
    pl.when(total <= (n_full - 1) * G_BLOCK)(lambda: body(n_full - 1))
    pl.when(total > (n_full - 1) * G_BLOCK)(lambda: body(n_full))


def _route_onehot(iota, pos_list, val_list):
    g = jnp.zeros(iota.shape, F32)
    for pos, val in zip(pos_list, val_list):
        g = jnp.where(iota == pos, val, g)
    return g.astype(BF16)


def _dispatch_kernel(seg_ref, len_ref, dst_ref, fill_off_ref, fill_len_ref, h1_ref, cpos_ref, *rest, tm, fill):
    xs_hbm, stage_ref, zero_ref, sem, fill_sem = rest[-5:]
    tile = pl.program_id(0)
    slot = tile % 2
    h1b = h1_ref[...].astype(BF16)
    pos_rows = [cpos_ref[k:k + 1, :] for k in range(TOP_K)]

    def stage_blocks(n_blk):
        for blk in range(n_blk):
            iota = blk * G_BLOCK + lax.broadcasted_iota(I32, (G_BLOCK, tm), 0)
            g = _route_onehot(iota, pos_rows, [1.0] * TOP_K)
            stage_ref[slot, blk * G_BLOCK:(blk + 1) * G_BLOCK, :] = jnp.dot(
                g, h1b, preferred_element_type=F32).astype(BF16)

    _by_staged_blocks(seg_ref, len_ref, tile, stage_ref.shape[1] // G_BLOCK, stage_blocks)

    def copy_from(buf):
        def make_copy(seg, dst, size):
            return pltpu.make_async_copy(stage_ref.at[buf, pl.ds(seg, size), :], xs_hbm.at[pl.ds(dst, size), :],
                                         sem.at[buf])
        return make_copy

    _chunk_copies(seg_ref, len_ref, dst_ref, tile, copy_from(slot), lambda c: c.start())

    @pl.when(tile > 0)
    def _():
        _chunk_copies(seg_ref, len_ref, dst_ref, tile - 1, copy_from(1 - slot), lambda c: c.wait())

    @pl.when(tile == pl.num_programs(0) - 1)
    def _():
        _chunk_copies(seg_ref, len_ref, dst_ref, tile, copy_from(slot), lambda c: c.wait())
        if fill:
            zero_ref[...] = jnp.zeros(zero_ref.shape, BF16)

            def fill_copy(j):
                size = pl.multiple_of(fill_len_ref[j], ROW_ALIGN)
                return pltpu.make_async_copy(zero_ref.at[pl.ds(0, size), :],
                                             xs_hbm.at[pl.ds(pl.multiple_of(fill_off_ref[j], ROW_ALIGN), size), :],
                                             fill_sem)

            def each_fill(act):
                def body(j, carry):
                    pl.when(fill_len_ref[j] > 0)(lambda: act(fill_copy(j)))
                    return carry
                lax.fori_loop(0, fill_len_ref.shape[0], body, 0)

            each_fill(lambda c: c.start())
            each_fill(lambda c: c.wait())


def _dispatch(seg, ln, dst, fill_off, fill_len, h1, cpos, xs_prev, *, tm, p_rows, fill_rows):
    n, d = h1.shape
    in_specs = [pl.BlockSpec((tm, d), lambda i, *_: (i, 0)), pl.BlockSpec((TOP_K, tm), lambda i, *_: (0, i))]
    operands = [seg, ln, dst, fill_off, fill_len, h1, cpos]
    aliases = {}
    if xs_prev is not None:
        in_specs.append(pl.BlockSpec(memory_space=pl.ANY))
        aliases = {len(operands): 0}
        operands.append(xs_prev)
    return pl.pallas_call(
        functools.partial(_dispatch_kernel, tm=tm, fill=xs_prev is None),
        grid_spec=pltpu.PrefetchScalarGridSpec(
            num_scalar_prefetch=5,
            grid=(n // tm,),
            in_specs=in_specs,
            out_specs=pl.BlockSpec(memory_space=pl.ANY),
            scratch_shapes=[pltpu.VMEM((2, _stage_rows(tm), d), BF16), pltpu.VMEM((fill_rows, d), BF16),
                            pltpu.SemaphoreType.DMA((2,)), pltpu.SemaphoreType.DMA(())],
        ),
        out_shape=jax.ShapeDtypeStruct((p_rows, d), BF16),
        input_output_aliases=aliases,
        compiler_params=_params(("arbitrary",)),
        name="dispatch",
    )(*operands)


FF_CHUNK = 256


def _moe_kernel(te_ref, tv_ref, x_ref, w1_ref, b1_ref, w2_ref, b2_ref, y_ref, w1b_ref, w2b_ref, act_ref):
    i = pl.program_id(0)
    d_ff = w2_ref.shape[1]
    valid = tv_ref[i]

    @pl.when((i == 0) | (te_ref[i] != te_ref[jnp.maximum(i - 1, 0)]))
    def _():
        w1b_ref[...] = w1_ref[0].astype(BF16)
        w2b_ref[...] = w2_ref[0].astype(BF16)

    @pl.when(valid > 0)
    def _():
        xb = x_ref[...]
        for j in range(d_ff // FF_CHUNK):
            gs = slice(j * FF_CHUNK, (j + 1) * FF_CHUNK)
            us = slice(d_ff + j * FF_CHUNK, d_ff + (j + 1) * FF_CHUNK)
            gate = jnp.dot(xb, w1b_ref[:, gs], preferred_element_type=F32) + b1_ref[0, :, gs]
            up = jnp.dot(xb, w1b_ref[:, us], preferred_element_type=F32) + b1_ref[0, :, us]
            gate = jnp.minimum(gate, SWIGLU_LIMIT)
            up = jnp.clip(up, -SWIGLU_LIMIT, SWIGLU_LIMIT)
            act_ref[:, gs] = ((up + 1.0) * (gate * jax.nn.sigmoid(SWIGLU_ALPHA * gate))).astype(BF16)
        y = jnp.dot(act_ref[...], w2b_ref[...], preferred_element_type=F32) + b2_ref[0]
        y_ref[...] = y.astype(BF16)

    @pl.when(valid <= 0)
    def _():
        y_ref[...] = jnp.zeros(y_ref.shape, BF16)


def _moe(tile_expert, tile_valid, xs, w1, b1, w2, b2, *, tmoe):
    p, d = xs.shape
    e, _, ff2 = w1.shape
    d_ff = ff2 // 2
    return pl.pallas_call(
        _moe_kernel,
        grid_spec=pltpu.PrefetchScalarGridSpec(
            num_scalar_prefetch=2,
            grid=(p // tmoe,),
            in_specs=[pl.BlockSpec((tmoe, d), lambda i, te, tv: (i, 0)),
                      pl.BlockSpec((1, d, ff2), lambda i, te, tv: (te[i], 0, 0)),
                      pl.BlockSpec((1, 1, ff2), lambda i, te, tv: (te[i], 0, 0)),
                      pl.BlockSpec((1, d_ff, d), lambda i, te, tv: (te[i], 0, 0)),
                      pl.BlockSpec((1, 1, d), lambda i, te, tv: (te[i], 0, 0))],
            out_specs=pl.BlockSpec((tmoe, d), lambda i, te, tv: (i, 0)),
            scratch_shapes=[pltpu.VMEM((d, ff2), BF16), pltpu.VMEM((d_ff, d), BF16), pltpu.VMEM((tmoe, d_ff), BF16)],
        ),
        out_shape=jax.ShapeDtypeStruct((p, d), BF16),
        compiler_params=_params(("arbitrary",)),
        name="moe",
    )(tile_expert, tile_valid, xs, w1, b1.reshape(e, 1, ff2), w2, b2.reshape(e, 1, d))


def _combine_kernel(seg_ref, len_ref, dst_ref, h1_ref, cpos_ref, gw_ref, g2_ref, b2_ref, ys_hbm, o_ref,
                    stage_ref, gate_ref, sem, *, tm):
    tile = pl.program_id(0)
    slot = tile % 2

    def copy_into(buf):
        def make_copy(seg, dst, size):
            return pltpu.make_async_copy(ys_hbm.at[pl.ds(dst, size), :], stage_ref.at[buf, pl.ds(seg, size), :],
                                         sem.at[buf])
        return make_copy

    @pl.when(tile == 0)
    def _():
        stage_ref[...] = jnp.zeros(stage_ref.shape, BF16)
        _chunk_copies(seg_ref, len_ref, dst_ref, tile, copy_into(slot), lambda c: c.start())

    @pl.when(tile + 1 < pl.num_programs(0))
    def _():
        _chunk_copies(seg_ref, len_ref, dst_ref, tile + 1, copy_into(1 - slot), lambda c: c.start())

    cpos = cpos_ref[...]
    gw = gw_ref[...]
    pos_cols = [cpos[:, k:k + 1] for k in range(TOP_K)]
    gw_cols = [gw[:, k:k + 1] for k in range(TOP_K)]

    def finish(n_blk):
        _chunk_copies(seg_ref, len_ref, dst_ref, tile, copy_into(slot), lambda c: c.wait())
        m = None
        for blk in range(n_blk):
            iota = blk * G_BLOCK + lax.broadcasted_iota(I32, (tm, G_BLOCK), 1)
            gate = _route_onehot(iota, pos_cols, gw_cols)
            part = jnp.dot(gate, stage_ref[slot, blk * G_BLOCK:(blk + 1) * G_BLOCK, :], preferred_element_type=F32)
            m = part if m is None else part + m
        o_ref[...] = _layer_norm(DEEPNORM_ALPHA * h1_ref[...] + m, g2_ref[...], b2_ref[...])

    _by_staged_blocks(seg_ref, len_ref, tile, stage_ref.shape[1] // G_BLOCK, finish)


def _combine(seg, ln, dst, h1, cpos_nt, gw_nt, g2, b2, ys, *, tm):
    n, d = h1.shape
    return pl.pallas_call(
        functools.partial(_combine_kernel, tm=tm),
        grid_spec=pltpu.PrefetchScalarGridSpec(
            num_scalar_prefetch=3,
            grid=(n // tm,),
            in_specs=[pl.BlockSpec((tm, d), lambda i, *_: (i, 0)),
                      pl.BlockSpec((tm, TOP_K), lambda i, *_: (i, 0)), pl.BlockSpec((tm, TOP_K), lambda i, *_: (i, 0)),
                      pl.BlockSpec((1, d), lambda i, *_: (0, 0)), pl.BlockSpec((1, d), lambda i, *_: (0, 0)),
                      pl.BlockSpec(memory_space=pl.ANY)],
            out_specs=pl.BlockSpec((tm, d), lambda i, *_: (i, 0)),
            scratch_shapes=[pltpu.VMEM((2, _stage_rows(tm), d), BF16), pltpu.VMEM((tm, _stage_rows(tm)), BF16),
                            pltpu.SemaphoreType.DMA((2,))],
        ),
        out_shape=jax.ShapeDtypeStruct((n, d), F32),
        compiler_params=_params(("arbitrary",)),
        name="combine",
    )(seg, ln, dst, h1, cpos_nt, gw_nt, g2, b2, ys)


def _block_tails(groups, wts, *, tm, tmoe):
    carry = jnp.zeros((N_EXPERTS, LANES), F32)
    plans, max_rows, used_first = [], 0, None
    for x2d, att, pool in groups:
        n = x2d.shape[0]
        tm_g = min(tm, n)
        h1, cpos, gw_t, seg_t, len_t, dst_t, carry = _tail(
            x2d, att, pool, wts["g0"], wts["b0"], wts["woa"], wts["wob"], wts["g1"], wts["b1"],
            wts["wrh"], wts["wrl"], wts["br"], carry, tm=tm_g)
        plans.append((tm_g, h1, cpos, gw_t, seg_t, len_t, dst_t))
        max_rows += TOP_K * n + (ROW_ALIGN - 1) * N_EXPERTS * (n // tm_g)
        used_first = carry[:, 0].astype(I32) if used_first is None else used_first
    used = carry[:, 0].astype(I32)
    cap = ((used + tmoe - 1) // tmoe) * tmoe
    ends = jnp.cumsum(cap)
    offs = ends - cap
    n_tiles = -(-max_rows // tmoe) + N_EXPERTS
    tile_start = jnp.arange(n_tiles, dtype=I32) * tmoe
    tile_expert = jnp.minimum(jnp.sum((ends[None, :] <= tile_start[:, None]).astype(I32), axis=1), N_EXPERTS - 1)
    tile_valid = jnp.clip(offs[tile_expert] + used[tile_expert] - tile_start, 0, tmoe).astype(I32)
    tile_valid = jnp.where(tile_start < ends[-1], tile_valid, 0)

    written_end = (offs + used_first)[tile_expert]
    fill_off = jnp.where(tile_start < ends[-1], jnp.clip(written_end, tile_start, tile_start + tmoe), tile_start)
    fill_len = tile_start + tmoe - fill_off
    no_fill = jnp.zeros((1,), I32)

    xs, tables = None, []
    for tm_g, h1, cpos, gw_t, seg_t, len_t, dst_t in plans:
        seg = seg_t[:, :, 0].astype(I32).reshape(-1)
        ln = len_t[:, :, 0].astype(I32).reshape(-1)
        dst = (dst_t[:, :, 0].astype(I32) + offs[None, :]).reshape(-1)
        tables.append((seg, ln, dst))
        fills = (fill_off, fill_len) if xs is None else (no_fill, no_fill)
        xs = _dispatch(seg, ln, dst, *fills, h1, cpos, xs, tm=tm_g, p_rows=n_tiles * tmoe, fill_rows=tmoe)
    ys = _moe(tile_expert, tile_valid, xs, wts["w1e"], wts["b1e"], wts["w2e"], wts["b2e"], tmoe=tmoe)
    return [_combine(seg, ln, dst, h1, cpos.T, gw_t.T, wts["g2"], wts["b2"], ys, tm=tm_g)
            for (seg, ln, dst), (tm_g, h1, cpos, gw_t, _, _, _) in zip(tables, plans)]


def kernel(x_prompt, x_sample, cache_k, cache_v, cache_kidx, state_pool, ln0_g, ln0_b, w_in, w_o,
           pool_w, pool_scale, ln1_g, ln1_b, w_router, b_router, w1, b1, w2, b2, ln2_g, ln2_b):
    bp, s_len, d = x_prompt.shape
    bs, t_len, _ = x_sample.shape
    l_past = cache_k.shape[2]
    aw = N_HEADS * HEAD_DIM
    pw = d - aw
    lyr = 0
    lc = 256

    k_off = aw
    v_off = k_off + HEAD_DIM
    qi_off = v_off + HEAD_DIM
    ki_off = qi_off + IDX_HEADS * IDX_DIM
    wi_off = ki_off + IDX_DIM
    u_off = wi_off + IDX_HEADS
    win = w_in[lyr]
    wa = jnp.concatenate([win[:, 0:k_off], win[:, qi_off:ki_off], win[:, u_off:u_off + pw]], axis=1).astype(BF16)
    wb = jnp.concatenate([win[:, k_off:v_off], win[:, v_off:qi_off], win[:, ki_off:wi_off], win[:, wi_off:u_off],
                          jnp.zeros((d, HEAD_DIM - IDX_HEADS), F32)], axis=1).astype(BF16)

    g0 = ln0_g.reshape(1, d)
    b0 = ln0_b.reshape(1, d)
    wrt = w_router[lyr].T
    wrh = wrt.astype(BF16)
    wts = dict(
        g0=g0, b0=b0,
        woa=w_o[lyr][:aw].astype(BF16), wob=w_o[lyr][aw:].astype(BF16),
        g1=ln1_g[lyr].reshape(1, d), b1=ln1_b[lyr].reshape(1, d),
        wrh=wrh, wrl=(wrt - wrh.astype(F32)).astype(BF16), br=b_router[lyr].reshape(N_EXPERTS, 1),
        w1e=w1[lyr], b1e=b1[lyr], w2e=w2[lyr], b2e=b2[lyr],
        g2=ln2_g[lyr].reshape(1, d), b2=ln2_b[lyr].reshape(1, d),
    )
    pool_w_b = pool_w[lyr].astype(BF16)
    pool_sc = pool_scale[lyr].reshape(1, pw)

    xp = x_prompt.reshape(bp * s_len, d)
    q, qi, u, k, v, ki, kb, kib, vt, wit = _proj(xp, g0, b0, wa, wb, tm=512, lc=lc)
    att_p = _dsa(q.reshape(bp, s_len, aw), qi.reshape(bp, s_len, aw), wit,
                 kb.reshape(bp, s_len, HEAD_DIM), kib.reshape(bp, s_len, IDX_DIM), vt,
                 tq=256, lc=lc, causal=True, l_valid=s_len, q_pos0=0, topk=min(TOPK_MAX, s_len // 4))
    u_p = u.reshape(bp, s_len, pw)
    pool_p = _pool(u_p, jnp.zeros((bp, POOL_PAST + 1, pw), F32), pool_w_b, pool_sc, pos0=0)

    xs = x_sample.reshape(bs * t_len, d)
    qs, qis, us, kn, vn, kin, _, _, _, wits = _proj(xs, g0, b0, wa, wb, tm=512, lc=lc)
    l_all = l_past + t_len
    l_pad = -(-l_all // lc) * lc
    tq_s = LANES
    pad_keys = lambda a: jnp.pad(a, ((0, 0), (0, l_pad - l_all), (0, 0)))
    k_all = pad_keys(jnp.concatenate([cache_k[lyr], kn.reshape(bs, t_len, HEAD_DIM)], axis=1))
    v_all = pad_keys(jnp.concatenate([cache_v[lyr], vn.reshape(bs, t_len, HEAD_DIM)], axis=1))
    ki_all = pad_keys(jnp.concatenate([cache_kidx[lyr], kin.reshape(bs, t_len, IDX_DIM)], axis=1))
    vt_all = v_all.reshape(bs, l_pad // lc, lc, HEAD_DIM).transpose(0, 1, 3, 2).reshape(-1, HEAD_DIM, lc)
    vt_all = jnp.concatenate([vt_all, jnp.broadcast_to(_denominator_rows(lc), (vt_all.shape[0], VT_ROWS - HEAD_DIM, lc))],
                             axis=1)
    pad_q = lambda a: jnp.pad(a.reshape(bs, t_len, aw), ((0, 0), (0, tq_s - t_len), (0, 0)))
    wit_s = jnp.pad(wits.reshape(IDX_HEADS, bs, t_len), ((0, 0), (0, 0), (0, tq_s - t_len))).reshape(IDX_HEADS, -1)
    att_s = _dsa(pad_q(qs), pad_q(qis), wit_s, k_all.astype(BF16), ki_all.astype(BF16), vt_all.astype(BF16),
                 tq=tq_s, lc=lc, causal=False, l_valid=l_all, q_pos0=l_past, topk=min(TOPK_MAX, l_all // 4))
    att_s = att_s[:, :t_len].reshape(bs * t_len, aw)
    us3 = us.reshape(bs, t_len, pw)
    prefix_s = jnp.concatenate([jnp.zeros((bs, 1, pw), F32), state_pool[lyr]], axis=1)
    pool_s = _pool(us3, prefix_s, pool_w_b, pool_sc, pos0=l_past)

    y_p, y_s = _block_tails([(xp, att_p.reshape(bp * s_len, aw), pool_p.reshape(bp * s_len, pw)),
                             (xs, att_s, pool_s.reshape(bs * t_len, pw))], wts, tm=512, tmoe=1024)

    pool_state_p = u_p[:, s_len - POOL_PAST:]
    pool_state_s = jnp.concatenate([state_pool[lyr], us3], axis=1)[:, -POOL_PAST:]
    return (y_p.reshape(bp, s_len, d), y_s.reshape(bs, t_len, d),
            k.reshape(1, bp, s_len, HEAD_DIM), v.reshape(1, bp, s_len, HEAD_DIM),
            ki.reshape(1, bp, s_len, IDX_DIM), pool_state_p[None],
            kn.reshape(1, bs, t_len, HEAD_DIM), vn.reshape(1, bs, t_len, HEAD_DIM),
            kin.reshape(1, bs, t_len, IDX_DIM), pool_state_s[None])
```

```python
import functools

import jax
import jax.numpy as jnp
from jax import lax
from jax.experimental import pallas as pl
from jax.experimental.pallas import tpu as pltpu

F32 = jnp.float32
BF16 = jnp.bfloat16
I32 = jnp.int32

CHUNK = 64
CHUNK_SHIFT = 6
assert 1 << CHUNK_SHIFT == CHUNK
N_HEADS = 8
HEAD_DIM = 64
IDX_HEADS = 8
IDX_DIM = 64
TOPK_MAX = 256
POOL_WINDOWS = (2, 4, 8, 16)
POOL_PAST = 15
N_EXPERTS = 32
TOP_K = 4
SWIGLU_LIMIT = 7.0
SWIGLU_ALPHA = 1.702
LN_EPS = 1e-5
DEPTH = 1
DEEPNORM_ALPHA = (2 * DEPTH) ** 0.25
LOG2_E = 1.4426950408889634

LANES = 128
SUBLANES = 8
BF16_ROWS = 16
VMEM_LIMIT_BYTES = 56 * 1024 * 1024

NEG_BIG = -1e30
KEY_NEG_INF = -2139095041
KEY_POS_INF = 2139095040

NT_DIMS = (((1,), (1,)), ((), ()))


def _layer_norm(x, g, b):
    mu = jnp.mean(x, axis=-1, keepdims=True)
    xc = x - mu
    var = jnp.mean(xc * xc, axis=-1, keepdims=True)
    return xc * lax.rsqrt(var + LN_EPS) * g + b


def _params(sem):
    return pltpu.CompilerParams(dimension_semantics=sem, vmem_limit_bytes=VMEM_LIMIT_BYTES)


VT_ROWS = HEAD_DIM + BF16_ROWS


def _denominator_rows(width):
    return jnp.where(lax.broadcasted_iota(I32, (VT_ROWS - HEAD_DIM, width), 0) == 0, 1.0, 0.0)


def _proj_kernel(x_ref, g_ref, b_ref, wa_ref, wb_ref,
                 q_ref, qi_ref, u_ref, k_ref, v_ref, ki_ref, kb_ref, kib_ref, vt_ref, wit_ref, *, lc):
    h = _layer_norm(x_ref[...], g_ref[...], b_ref[...])
    hb = h.astype(BF16)
    aw = N_HEADS * HEAD_DIM
    pa = jnp.dot(hb, wa_ref[...], preferred_element_type=F32)
    q_ref[...] = (pa[:, :aw] * (HEAD_DIM ** -0.5 * LOG2_E)).astype(BF16)
    qi_ref[...] = pa[:, aw:2 * aw].astype(BF16)
    u_ref[...] = pa[:, 2 * aw:]
    pb = jnp.dot(hb, wb_ref[...], preferred_element_type=F32)
    k = pb[:, 0:HEAD_DIM]
    v = pb[:, HEAD_DIM:2 * HEAD_DIM]
    ki = pb[:, 2 * HEAD_DIM:2 * HEAD_DIM + IDX_DIM]
    k_ref[...] = k
    v_ref[...] = v
    ki_ref[...] = ki
    kb_ref[...] = k.astype(BF16)
    kib_ref[...] = ki.astype(BF16)
    pt = pb.T
    ones_rows = _denominator_rows(lc)
    for c in range(vt_ref.shape[0]):
        vt_ref[c] = jnp.concatenate([pt[HEAD_DIM:2 * HEAD_DIM, c * lc:(c + 1) * lc], ones_rows],
                                    axis=0).astype(BF16)
    wi = pt[3 * HEAD_DIM:3 * HEAD_DIM + IDX_HEADS, :]
    wit_ref[...] = (wi * (IDX_HEADS ** -0.5)) * (IDX_DIM ** -0.5)


def _proj(x2d, g, b, wa, wb, *, tm, lc):
    n, d = x2d.shape
    tm = min(tm, n)
    aw = N_HEADS * HEAD_DIM
    uw = wa.shape[1] - 2 * aw
    row = lambda i: (i, 0)
    const = lambda i: (0, 0)
    out_shape = (
        jax.ShapeDtypeStruct((n, aw), BF16),
        jax.ShapeDtypeStruct((n, aw), BF16),
        jax.ShapeDtypeStruct((n, uw), F32),
        jax.ShapeDtypeStruct((n, HEAD_DIM), F32),
        jax.ShapeDtypeStruct((n, HEAD_DIM), F32),
        jax.ShapeDtypeStruct((n, IDX_DIM), F32),
        jax.ShapeDtypeStruct((n, HEAD_DIM), BF16),
        jax.ShapeDtypeStruct((n, IDX_DIM), BF16),
        jax.ShapeDtypeStruct((n // lc, VT_ROWS, lc), BF16),
        jax.ShapeDtypeStruct((IDX_HEADS, n), F32),
    )
    out_specs = (
        pl.BlockSpec((tm, aw), row), pl.BlockSpec((tm, aw), row), pl.BlockSpec((tm, uw), row),
        pl.BlockSpec((tm, HEAD_DIM), row), pl.BlockSpec((tm, HEAD_DIM), row), pl.BlockSpec((tm, IDX_DIM), row),
        pl.BlockSpec((tm, HEAD_DIM), row), pl.BlockSpec((tm, IDX_DIM), row),
        pl.BlockSpec((tm // lc, VT_ROWS, lc), lambda i: (i, 0, 0)),
        pl.BlockSpec((IDX_HEADS, tm), lambda i: (0, i)),
    )
    return pl.pallas_call(
        functools.partial(_proj_kernel, lc=lc),
        grid=(n // tm,),
        in_specs=[pl.BlockSpec((tm, d), row), pl.BlockSpec((1, d), const), pl.BlockSpec((1, d), const),
                  pl.BlockSpec(wa.shape, const), pl.BlockSpec(wb.shape, const)],
        out_specs=out_specs,
        out_shape=out_shape,
        compiler_params=_params(("parallel",)),
        name="proj",
    )(x2d, g, b, wa, wb)


def _key_to_float(key):
    bits = jnp.where(key >= 0, key, key ^ jnp.int32(0x7FFFFFFF))
    return lax.bitcast_convert_type(bits, F32)


def _dsa_kernel(q_ref, qi_ref, wit_ref, kb_ref, kib_ref, vt_ref, o_ref,
                sc_ref, sch_ref, s_ref, out_ref, *acc_refs, tq, lc, nk_static, causal, l_valid, q_pos0, topk):
    qb = pl.program_id(1)
    nk = qb * (tq // lc) + (tq // lc) if causal else nk_static
    q_chunk = (q_pos0 + qb * tq + lax.broadcasted_iota(I32, (1, tq), 1)) >> CHUNK_SHIFT

    def chunk_loop(body, init):
        if causal:
            return lax.fori_loop(0, nk, body, init)
        return lax.fori_loop(0, nk_static, body, init, unroll=True)

    def wide_chunk_loop(body, init):
        quad = lambda i, c: body(pl.multiple_of(i * 4 * lc, 4 * lc), 4 * lc, c)
        if not causal:
            carry = lax.fori_loop(0, nk_static // 4, quad, init, unroll=True)
            done = nk_static // 4 * 4
            for span in (2, 1):
                if (nk_static - done) >= span:
                    carry = body(done * lc, span * lc, carry)
                    done += span
            return carry
        carry = lax.fori_loop(0, nk // 4, quad, init)
        done = (nk // 4) * 4
        carry = lax.cond((nk & 2) != 0, lambda c: body(pl.multiple_of(done * lc, 2 * lc), 2 * lc, c), lambda c: c,
                         carry)
        done = done + (nk & 2)
        return lax.cond((nk & 1) != 0, lambda c: body(pl.multiple_of(done * lc, lc), lc, c), lambda c: c, carry)

    def score_span(off, rows, carry):
        kic = kib_ref[0, pl.ds(off, rows), :]
        acc = jnp.zeros((rows, tq), F32)
        for h in range(IDX_HEADS):
            s = lax.dot_general(kic, qi_ref[0, :, h * IDX_DIM:(h + 1) * IDX_DIM], NT_DIMS,
                                preferred_element_type=F32)
            acc = acc + wit_ref[h:h + 1, :] * jnp.maximum(s, 0.0)
        l_pos = off + lax.broadcasted_iota(I32, (rows, tq), 0)
        visible = ((l_pos >> CHUNK_SHIFT) <= q_chunk) & (l_pos < l_valid)
        score = jnp.where(visible, acc, -jnp.inf)
        sc_ref[pl.ds(off, rows), :] = score
        sch_ref[pl.ds(off, rows), :] = score.astype(BF16)
        return carry

    wide_chunk_loop(score_span, 0)

    def count(pred):
        ways = 4 * SUBLANES

        def body(kc, part):
            off = pl.multiple_of(kc * lc, lc)
            hit = jnp.where(pred(sc_ref[pl.ds(off, lc), :]), 1.0, 0.0)
            return part + jnp.sum(hit.reshape(lc // ways, ways, tq), axis=0)
        return jnp.sum(chunk_loop(body, jnp.zeros((ways, tq), F32)), axis=0, keepdims=True)

    def count_coarse(cand):
        ways = 2 * BF16_ROWS
        assert sc_ref.shape[0] // ways <= 256

        def body(kc, part):
            off = pl.multiple_of(kc * lc, lc)
            hit = jnp.where(sch_ref[pl.ds(off, lc), :] >= cand, jnp.ones((), BF16), jnp.zeros((), BF16))
            hit = hit.reshape(lc // ways, ways, tq)
            terms = [hit[j] for j in range(lc // ways)]
            while len(terms) > 1:
                terms = [a + b for a, b in zip(terms[::2], terms[1::2])]
            return part + terms[0]
        part = chunk_loop(body, jnp.zeros((ways, tq), BF16))
        return jnp.sum(part.astype(F32), axis=0, keepdims=True)

    def bisect(count_ge, key_to_value, lo, hi, steps):
        def step(_, carry):
            lo, hi = carry
            mid = (lo >> 1) + (hi >> 1) + (lo & hi & 1)
            ok = count_ge(key_to_value(mid)) >= topk
            return jnp.where(ok, mid, lo), jnp.where(ok, hi, mid)
        return lax.fori_loop(0, steps, step, (lo, hi))[0]

    def coarse_key_to_f32_key(k16):
        return jnp.where(k16 >= 0, k16 << 16, (k16 << 16) | 0xFFFF)

    full = lambda v: jnp.full((1, tq), v, I32)
    k16 = bisect(count_coarse, lambda k: _key_to_float(coarse_key_to_f32_key(k)).astype(BF16),
                 full(KEY_NEG_INF >> 16), full((KEY_POS_INF >> 16) + 1), 16)
    kb = coarse_key_to_f32_key(k16)
    span = 1 << 16
    lo = bisect(lambda cand: count(lambda blk: blk >= cand), _key_to_float,
                jnp.maximum(kb, KEY_NEG_INF + span) - span, jnp.minimum(kb, KEY_POS_INF + 1 - span) + span, 17)
    thr = _key_to_float(lo)
    n_above = count(lambda blk: blk > thr)
    n_ties = topk - n_above

    tri = jnp.where(lax.broadcasted_iota(I32, (lc, lc), 0) >= lax.broadcasted_iota(I32, (lc, lc), 1),
                    1.0, 0.0).astype(BF16)

    def logits_span(off, rows, carry):
        ties_before, m8 = carry
        biases = []
        for r0 in range(0, rows, lc):
            blk = sc_ref[pl.ds(off + r0, lc), :]
            tie = blk == thr
            tie_rank = (jnp.dot(tri, jnp.where(tie, 1.0, 0.0).astype(BF16), preferred_element_type=F32)
                        + ties_before)
            bias = jnp.where(blk > thr, 0.0, jnp.where(tie, jnp.where(tie_rank <= n_ties, 0.0, NEG_BIG), NEG_BIG))
            biases.append(jnp.where(blk == -jnp.inf, NEG_BIG, bias))
            ties_before = tie_rank[lc - 1:lc, :]
        bias = jnp.concatenate(biases, axis=0)
        kc_b = kb_ref[0, pl.ds(off, rows), :]
        m_rows = []
        for h in range(N_HEADS):
            s = lax.dot_general(kc_b, q_ref[0, :, h * HEAD_DIM:(h + 1) * HEAD_DIM], NT_DIMS,
                                preferred_element_type=F32) + bias
            s_ref[h, pl.ds(off, rows), :] = s
            s8 = jnp.max(s.reshape(rows // SUBLANES, SUBLANES, tq), axis=0)
            m_rows.append(jnp.maximum(m8[h * SUBLANES:(h + 1) * SUBLANES], s8))
        return ties_before, jnp.concatenate(m_rows, axis=0)

    _, m8 = wide_chunk_loop(logits_span,
                            (jnp.zeros((1, tq), F32), jnp.full((N_HEADS * SUBLANES, tq), NEG_BIG, F32)))
    m_all = jnp.max(m8.reshape(N_HEADS, SUBLANES, tq), axis=1)

    for acc_ref in acc_refs:
        acc_ref[...] = jnp.zeros(acc_ref.shape, F32)

    def pv_chunk(kc, carry):
        off = pl.multiple_of(kc * lc, lc)
        vt_c = vt_ref[kc]
        for h in range(N_HEADS):
            p = jnp.exp2(s_ref[h, pl.ds(off, lc), :] - m_all[h:h + 1, :])
            acc_refs[h][...] += jnp.dot(vt_c, p.astype(BF16), preferred_element_type=F32)
        return carry

    chunk_loop(pv_chunk, 0)

    for h in range(N_HEADS):
        acc = acc_refs[h][...]
        out_ref[h * HEAD_DIM:(h + 1) * HEAD_DIM, :] = acc[0:HEAD_DIM] / acc[HEAD_DIM:HEAD_DIM + 1]
    o_ref[0] = out_ref[...].T.astype(BF16)


def _dsa(q, qi, wit, kb, kib, vt, *, tq, lc, causal, l_valid, q_pos0, topk):
    bsz, tq_tot, aw = q.shape
    l_tot = kb.shape[1]
    nq = tq_tot // tq
    nkc = l_tot // lc
    kern = functools.partial(_dsa_kernel, tq=tq, lc=lc, nk_static=nkc, causal=causal, l_valid=l_valid,
                             q_pos0=q_pos0, topk=topk)
    return pl.pallas_call(
        kern,
        grid=(bsz, nq),
        in_specs=[
            pl.BlockSpec((1, tq, aw), lambda b, i: (b, i, 0)),
            pl.BlockSpec((1, tq, aw), lambda b, i: (b, i, 0)),
            pl.BlockSpec((IDX_HEADS, tq), lambda b, i: (0, b * nq + i)),
            pl.BlockSpec((1, l_tot, HEAD_DIM), lambda b, i: (b, 0, 0)),
            pl.BlockSpec((1, l_tot, IDX_DIM), lambda b, i: (b, 0, 0)),
            pl.BlockSpec((nkc, VT_ROWS, lc), lambda b, i: (b, 0, 0)),
        ],
        out_specs=pl.BlockSpec((1, tq, aw), lambda b, i: (b, i, 0)),
        out_shape=jax.ShapeDtypeStruct((bsz, tq_tot, aw), BF16),
        scratch_shapes=[
            pltpu.VMEM((l_tot, tq), F32),
            pltpu.VMEM((l_tot, tq), BF16),
            pltpu.VMEM((N_HEADS, l_tot, tq), F32),
            pltpu.VMEM((aw, tq), F32),
        ] + [pltpu.VMEM((VT_ROWS, tq), F32) for _ in range(N_HEADS)],
        compiler_params=_params(("parallel", "parallel")),
        name="dsa",
    )(q, qi, wit, kb, kib, vt)


def _pool_kernel(u_ref, pre_ref, pw_ref, sc_ref, o_ref, ext_ref, *, t_len, tt, pos0):
    pad = pre_ref.shape[1]
    gw = pw_ref.shape[1]
    ext_ref[0:pad, :] = pre_ref[0]
    ext_ref[pad:pad + t_len, :] = u_ref[0]
    row = lax.broadcasted_iota(I32, (tt, gw), 0)
    for t in range(t_len // tt):
        r0 = t * tt
        for g, w in enumerate(POOL_WINDOWS):
            cols = slice(g * gw, (g + 1) * gw)
            cur = ext_ref[pad + r0:pad + r0 + tt, cols]
            wsum = cur
            for j in range(1, w):
                wsum = wsum + ext_ref[pad + r0 - j:pad + r0 - j + tt, cols]
            cnt = jnp.minimum(w, pos0 + r0 + 1 + row).astype(F32)
            diff = (wsum / cnt - cur).astype(BF16)
            y = jnp.dot(diff, pw_ref[g], preferred_element_type=F32)
            o_ref[0, r0:r0 + tt, cols] = (y * sc_ref[:, cols]).astype(BF16)


def _pool(u, prefix, pool_w_b, pool_scale, *, pos0):
    bsz, t_len, c = u.shape
    pad = prefix.shape[1]
    tt = min(t_len, 256)
    return pl.pallas_call(
        functools.partial(_pool_kernel, t_len=t_len, tt=tt, pos0=pos0),
        grid=(bsz,),
        in_specs=[pl.BlockSpec((1, t_len, c), lambda b: (b, 0, 0)),
                  pl.BlockSpec((1, pad, c), lambda b: (b, 0, 0)),
                  pl.BlockSpec(pool_w_b.shape, lambda b: (0, 0, 0)),
                  pl.BlockSpec((1, c), lambda b: (0, 0))],
        out_specs=pl.BlockSpec((1, t_len, c), lambda b: (b, 0, 0)),
        out_shape=jax.ShapeDtypeStruct((bsz, t_len, c), BF16),
        scratch_shapes=[pltpu.VMEM((pad + t_len, c), F32)],
        compiler_params=_params(("parallel",)),
        name="pool",
    )(u, prefix, pool_w_b, pool_scale)


ROW_ALIGN = 16


def _tail_kernel(x_ref, att_ref, pool_ref, g0_ref, b0_ref, woa_ref, wob_ref, g1_ref, b1_ref,
                 wrh_ref, wrl_ref, br_ref,
                 carry0_ref, h1_ref, cpos_ref, gw_ref, seg_ref, len_ref, dst_ref, used_ref, carry_ref, *, tm, sub):
    @pl.when(pl.program_id(0) == 0)
    def _():
        carry_ref[...] = carry0_ref[...]

    chunk_lens = []
    for s in range(sub):
        rows = slice(s * tm, (s + 1) * tm)
        chunk_lens.append(_tail_tile(x_ref[rows, :], att_ref[rows, :], pool_ref[rows, :], g0_ref, b0_ref, woa_ref,
                                     wob_ref, g1_ref, b1_ref, wrh_ref, wrl_ref, br_ref,
                                     h1_ref.at[rows, :], cpos_ref.at[:, rows], gw_ref.at[:, rows], seg_ref.at[s],
                                     len_ref.at[s], tm=tm))
    carry = carry_ref[...]
    for s in range(sub):
        dst_ref[s] = carry
        carry = carry + chunk_lens[s]
    carry_ref[...] = carry
    used_ref[...] = carry


def _tail_tile(x, att, pool, g0_ref, b0_ref, woa_ref, wob_ref, g1_ref, b1_ref, wrh_ref, wrl_ref, br_ref,
               h1_ref, cpos_ref, gw_ref, seg_ref, len_ref, *, tm):
    h = _layer_norm(x, g0_ref[...], b0_ref[...])
    mix = (jnp.dot(att, woa_ref[...], preferred_element_type=F32)
           + jnp.dot(pool, wob_ref[...], preferred_element_type=F32))
    h1 = _layer_norm(DEEPNORM_ALPHA * h + mix, g1_ref[...], b1_ref[...])
    h1_ref[...] = h1

    hh = h1.astype(BF16)
    hl = (h1 - hh.astype(F32)).astype(BF16)
    logits = (lax.dot_general(wrh_ref[...], hh, NT_DIMS, preferred_element_type=F32)
              + lax.dot_general(wrh_ref[...], hl, NT_DIMS, preferred_element_type=F32)
              + lax.dot_general(wrl_ref[...], hh, NT_DIMS, preferred_element_type=F32)
              + br_ref[...])

    e_iota = lax.broadcasted_iota(I32, (N_EXPERTS, tm), 0).astype(F32)
    work = logits
    vals, hots = [], []
    for k in range(TOP_K):
        m = jnp.max(work, axis=0, keepdims=True)
        idx = jnp.min(jnp.where(work == m, e_iota, float(N_EXPERTS)), axis=0, keepdims=True)
        hot = e_iota == idx
        vals.append(m)
        hots.append(jnp.where(hot, 1.0, 0.0))
        work = jnp.where(hot, -jnp.inf, work)
    exps = [jnp.exp(v - vals[0]) for v in vals]
    den = exps[0] + exps[1] + exps[2] + exps[3]
    for k in range(TOP_K):
        gw_ref[k:k + 1, :] = exps[k] / den

    hot_all = (hots[0] + hots[1] + hots[2] + hots[3]).astype(BF16)
    count = jnp.dot(hot_all, jnp.ones((tm, LANES), BF16), preferred_element_type=F32)
    chunk_len = jnp.ceil(count * (1.0 / ROW_ALIGN)) * ROW_ALIGN
    lower = jnp.where(lax.broadcasted_iota(I32, (N_EXPERTS, N_EXPERTS), 1)
                      < lax.broadcasted_iota(I32, (N_EXPERTS, N_EXPERTS), 0), 1.0, 0.0).astype(BF16)
    seg_base = jnp.dot(lower, chunk_len.astype(BF16), preferred_element_type=F32)
    before = jnp.where(lax.broadcasted_iota(I32, (tm, tm), 0) < lax.broadcasted_iota(I32, (tm, tm), 1),
                       1.0, 0.0).astype(BF16)
    slot = jnp.dot(hot_all, before, preferred_element_type=F32) + jnp.tile(seg_base, (1, tm // LANES))
    for k in range(TOP_K):
        cpos_ref[k:k + 1, :] = jnp.sum(hots[k] * slot, axis=0, keepdims=True).astype(I32)
    seg_ref[...] = seg_base
    len_ref[...] = chunk_len
    return chunk_len


def _tail(x2d, att, pool, g0, b0, woa, wob, g1, b1, wrh, wrl, br, carry0, *, tm):
    n, d = x2d.shape
    aw = att.shape[1]
    nt = n // tm
    sub = 2 if nt % 2 == 0 else 1
    ts = sub * tm
    row = lambda i: (i, 0)
    const = lambda i: (0, 0)
    col = lambda i: (0, i)
    tab = lambda i: (i, 0, 0)
    tab_shape = jax.ShapeDtypeStruct((nt, N_EXPERTS, LANES), F32)
    tab_spec = pl.BlockSpec((sub, N_EXPERTS, LANES), tab)
    return pl.pallas_call(
        functools.partial(_tail_kernel, tm=tm, sub=sub),
        grid=(nt // sub,),
        in_specs=[pl.BlockSpec((ts, d), row), pl.BlockSpec((ts, aw), row), pl.BlockSpec((ts, d - aw), row),
                  pl.BlockSpec((1, d), const), pl.BlockSpec((1, d), const),
                  pl.BlockSpec(woa.shape, const), pl.BlockSpec(wob.shape, const),
                  pl.BlockSpec((1, d), const), pl.BlockSpec((1, d), const),
                  pl.BlockSpec(wrh.shape, const), pl.BlockSpec(wrl.shape, const), pl.BlockSpec(br.shape, const),
                  pl.BlockSpec((N_EXPERTS, LANES), const)],
        out_specs=(pl.BlockSpec((ts, d), row), pl.BlockSpec((TOP_K, ts), col), pl.BlockSpec((TOP_K, ts), col),
                   tab_spec, tab_spec, tab_spec, pl.BlockSpec((N_EXPERTS, LANES), const)),
        out_shape=(jax.ShapeDtypeStruct((n, d), F32), jax.ShapeDtypeStruct((TOP_K, n), I32),
                   jax.ShapeDtypeStruct((TOP_K, n), F32), tab_shape, tab_shape, tab_shape,
                   jax.ShapeDtypeStruct((N_EXPERTS, LANES), F32)),
        scratch_shapes=[pltpu.VMEM((N_EXPERTS, LANES), F32)],
        compiler_params=_params(("arbitrary",)),
        name="tail",
    )(x2d, att, pool, g0, b0, woa, wob, g1, b1, wrh, wrl, br, carry0)


G_BLOCK = 256


def _stage_rows(tm):
    raw = TOP_K * tm + (ROW_ALIGN - 1) * N_EXPERTS
    return -(-raw // G_BLOCK) * G_BLOCK


def _chunk_copies(seg_ref, len_ref, dst_ref, tile, make_copy, act):
    def per_expert(e, carry):
        idx = tile * N_EXPERTS + e
        seg, ln, dst = seg_ref[idx], len_ref[idx], dst_ref[idx]

        @pl.when(ln > 0)
        def _():
            act(make_copy(pl.multiple_of(seg, ROW_ALIGN), pl.multiple_of(dst, ROW_ALIGN),
                          pl.multiple_of(ln, ROW_ALIGN)))
        return carry

    lax.fori_loop(0, N_EXPERTS, per_expert, 0)


def _by_staged_blocks(seg_ref, len_ref, tile, n_full, body):
    last = tile * N_EXPERTS + N_EXPERTS - 1
    total = seg_ref[last] + len_ref[last]
    pl.when(total <= (n_full - 1) * G_BLOCK)(lambda: body(n_full - 1))
    pl.when(total > (n_full - 1) * G_BLOCK)(lambda: body(n_full))


def _route_onehot(iota, pos_list, val_list):
    g = jnp.zeros(iota.shape, F32)
    for pos, val in zip(pos_list, val_list):
        g = jnp.where(iota == pos, val, g)
    return g.astype(BF16)


def _dispatch_kernel(seg_ref, len_ref, dst_ref, fill_off_ref, fill_len_ref, h1_ref, cpos_ref, *rest, tm, fill):
    xs_hbm, stage_ref, zero_ref, sem, fill_sem = rest[-5:]
    tile = pl.program_id(0)
    slot = tile % 2
    h1b = h1_ref[...].astype(BF16)
    pos_rows = [cpos_ref[k:k + 1, :] for k in range(TOP_K)]

    def stage_blocks(n_blk):
        for blk in range(n_blk):
            iota = blk * G_BLOCK + lax.broadcasted_iota(I32, (G_BLOCK, tm), 0)
            g = _route_onehot(iota, pos_rows, [1.0] * TOP_K)
            stage_ref[slot, blk * G_BLOCK:(blk + 1) * G_BLOCK, :] = jnp.dot(
                g, h1b, preferred_element_type=F32).astype(BF16)

    _by_staged_blocks(seg_ref, len_ref, tile, stage_ref.shape[1] // G_BLOCK, stage_blocks)

    def copy_from(buf):
        def make_copy(seg, dst, size):
            return pltpu.make_async_copy(stage_ref.at[buf, pl.ds(seg, size), :], xs_hbm.at[pl.ds(dst, size), :],
                                         sem.at[buf])
        return make_copy

    _chunk_copies(seg_ref, len_ref, dst_ref, tile, copy_from(slot), lambda c: c.start())

    @pl.when(tile > 0)
    def _():
        _chunk_copies(seg_ref, len_ref, dst_ref, tile - 1, copy_from(1 - slot), lambda c: c.wait())

    def fill_copy(j):
        size = pl.multiple_of(fill_len_ref[j], ROW_ALIGN)
        return pltpu.make_async_copy(zero_ref.at[pl.ds(0, size), :],
                                     xs_hbm.at[pl.ds(pl.multiple_of(fill_off_ref[j], ROW_ALIGN), size), :], fill_sem)

    def each_fill(act):
        def body(j, carry):
            pl.when(fill_len_ref[j] > 0)(lambda: act(fill_copy(j)))
            return carry
        lax.fori_loop(0, fill_len_ref.shape[0], body, 0)

    if fill:
        @pl.when(tile == 0)
        def _():
            zero_ref[...] = jnp.zeros(zero_ref.shape, BF16)
            each_fill(lambda c: c.start())

    @pl.when(tile == pl.num_programs(0) - 1)
    def _():
        _chunk_copies(seg_ref, len_ref, dst_ref, tile, copy_from(slot), lambda c: c.wait())
        if fill:
            each_fill(lambda c: c.wait())


def _dispatch(seg, ln, dst, fill_off, fill_len, h1, cpos, xs_prev, *, tm, p_rows, fill_rows):
    n, d = h1.shape
    in_specs = [pl.BlockSpec((tm, d), lambda i, *_: (i, 0)), pl.BlockSpec((TOP_K, tm), lambda i, *_: (0, i))]
    operands = [seg, ln, dst, fill_off, fill_len, h1, cpos]
    aliases = {}
    if xs_prev is not None:
        in_specs.append(pl.BlockSpec(memory_space=pl.ANY))
        aliases = {len(operands): 0}
        operands.append(xs_prev)
    return pl.pallas_call(
        functools.partial(_dispatch_kernel, tm=tm, fill=xs_prev is None),
        grid_spec=pltpu.PrefetchScalarGridSpec(
            num_scalar_prefetch=5,
            grid=(n // tm,),
            in_specs=in_specs,
            out_specs=pl.BlockSpec(memory_space=pl.ANY),
            scratch_shapes=[pltpu.VMEM((2, _stage_rows(tm), d), BF16), pltpu.VMEM((fill_rows, d), BF16),
                            pltpu.SemaphoreType.DMA((2,)), pltpu.SemaphoreType.DMA(())],
        ),
        out_shape=jax.ShapeDtypeStruct((p_rows, d), BF16),
        input_output_aliases=aliases,
        compiler_params=_params(("arbitrary",)),
        name="dispatch",
    )(*operands)


FF_CHUNK = 256


def _moe_kernel(te_ref, tv_ref, x_ref, w1_ref, b1_ref, w2_ref, b2_ref, y_ref, w1b_ref, w2b_ref, act_ref):
    i = pl.program_id(0)
    d_ff = w2_ref.shape[1]
    valid = tv_ref[i]

    @pl.when((i == 0) | (te_ref[i] != te_ref[jnp.maximum(i - 1, 0)]))
    def _():
        w1b_ref[...] = w1_ref[0].astype(BF16)
        w2b_ref[...] = w2_ref[0].astype(BF16)

    @pl.when(valid > 0)
    def _():
        xb = x_ref[...]
        for j in range(d_ff // FF_CHUNK):
            gs = slice(j * FF_CHUNK, (j + 1) * FF_CHUNK)
            us = slice(d_ff + j * FF_CHUNK, d_ff + (j + 1) * FF_CHUNK)
            gate = jnp.dot(xb, w1b_ref[:, gs], preferred_element_type=F32) + b1_ref[0, :, gs]
            up = jnp.dot(xb, w1b_ref[:, us], preferred_element_type=F32) + b1_ref[0, :, us]
            gate = jnp.minimum(gate, SWIGLU_LIMIT)
            up = jnp.clip(up, -SWIGLU_LIMIT, SWIGLU_LIMIT)
            act_ref[:, gs] = ((up + 1.0) * (gate * jax.nn.sigmoid(SWIGLU_ALPHA * gate))).astype(BF16)
        y = jnp.dot(act_ref[...], w2b_ref[...], preferred_element_type=F32) + b2_ref[0]
        y_ref[...] = y.astype(BF16)

    @pl.when(valid <= 0)
    def _():
        y_ref[...] = jnp.zeros(y_ref.shape, BF16)


def _moe(tile_expert, tile_valid, xs, w1, b1, w2, b2, *, tmoe):
    p, d = xs.shape
    e, _, ff2 = w1.shape
    d_ff = ff2 // 2
    return pl.pallas_call(
        _moe_kernel,
        grid_spec=pltpu.PrefetchScalarGridSpec(
            num_scalar_prefetch=2,
            grid=(p // tmoe,),
            in_specs=[pl.BlockSpec((tmoe, d), lambda i, te, tv: (i, 0)),
                      pl.BlockSpec((1, d, ff2), lambda i, te, tv: (te[i], 0, 0)),
                      pl.BlockSpec((1, 1, ff2), lambda i, te, tv: (te[i], 0, 0)),
                      pl.BlockSpec((1, d_ff, d), lambda i, te, tv: (te[i], 0, 0)),
                      pl.BlockSpec((1, 1, d), lambda i, te, tv: (te[i], 0, 0))],
            out_specs=pl.BlockSpec((tmoe, d), lambda i, te, tv: (i, 0)),
            scratch_shapes=[pltpu.VMEM((d, ff2), BF16), pltpu.VMEM((d_ff, d), BF16), pltpu.VMEM((tmoe, d_ff), BF16)],
        ),
        out_shape=jax.ShapeDtypeStruct((p, d), BF16),
        compiler_params=_params(("arbitrary",)),
        name="moe",
    )(tile_expert, tile_valid, xs, w1, b1.reshape(e, 1, ff2), w2, b2.reshape(e, 1, d))


def _combine_kernel(seg_ref, len_ref, dst_ref, h1_ref, cpos_ref, gw_ref, g2_ref, b2_ref, ys_hbm, o_ref,
                    stage_ref, gate_ref, sem, *, tm):
    tile = pl.program_id(0)
    slot = tile % 2

    def copy_into(buf):
        def make_copy(seg, dst, size):
            return pltpu.make_async_copy(ys_hbm.at[pl.ds(dst, size), :], stage_ref.at[buf, pl.ds(seg, size), :],
                                         sem.at[buf])
        return make_copy

    @pl.when(tile == 0)
    def _():
        stage_ref[...] = jnp.zeros(stage_ref.shape, BF16)
        _chunk_copies(seg_ref, len_ref, dst_ref, tile, copy_into(slot), lambda c: c.start())

    @pl.when(tile + 1 < pl.num_programs(0))
    def _():
        _chunk_copies(seg_ref, len_ref, dst_ref, tile + 1, copy_into(1 - slot), lambda c: c.start())

    cpos = cpos_ref[...]
    gw = gw_ref[...]
    pos_cols = [cpos[:, k:k + 1] for k in range(TOP_K)]
    gw_cols = [gw[:, k:k + 1] for k in range(TOP_K)]

    def finish(n_blk):
        _chunk_copies(seg_ref, len_ref, dst_ref, tile, copy_into(slot), lambda c: c.wait())
        m = None
        for blk in range(n_blk):
            iota = blk * G_BLOCK + lax.broadcasted_iota(I32, (tm, G_BLOCK), 1)
            gate = _route_onehot(iota, pos_cols, gw_cols)
            part = jnp.dot(gate, stage_ref[slot, blk * G_BLOCK:(blk + 1) * G_BLOCK, :], preferred_element_type=F32)
            m = part if m is None else part + m
        o_ref[...] = _layer_norm(DEEPNORM_ALPHA * h1_ref[...] + m, g2_ref[...], b2_ref[...])

    _by_staged_blocks(seg_ref, len_ref, tile, stage_ref.shape[1] // G_BLOCK, finish)


def _combine(seg, ln, dst, h1, cpos_nt, gw_nt, g2, b2, ys, *, tm):
    n, d = h1.shape
    return pl.pallas_call(
        functools.partial(_combine_kernel, tm=tm),
        grid_spec=pltpu.PrefetchScalarGridSpec(
            num_scalar_prefetch=3,
            grid=(n // tm,),
            in_specs=[pl.BlockSpec((tm, d), lambda i, *_: (i, 0)),
                      pl.BlockSpec((tm, TOP_K), lambda i, *_: (i, 0)), pl.BlockSpec((tm, TOP_K), lambda i, *_: (i, 0)),
                      pl.BlockSpec((1, d), lambda i, *_: (0, 0)), pl.BlockSpec((1, d), lambda i, *_: (0, 0)),
                      pl.BlockSpec(memory_space=pl.ANY)],
            out_specs=pl.BlockSpec((tm, d), lambda i, *_: (i, 0)),
            scratch_shapes=[pltpu.VMEM((2, _stage_rows(tm), d), BF16), pltpu.VMEM((tm, _stage_rows(tm)), BF16),
                            pltpu.SemaphoreType.DMA((2,))],
        ),
        out_shape=jax.ShapeDtypeStruct((n, d), F32),
        compiler_params=_params(("arbitrary",)),
        name="combine",
    )(seg, ln, dst, h1, cpos_nt, gw_nt, g2, b2, ys)


def _block_tails(groups, wts, *, tm, tmoe):
    carry = jnp.zeros((N_EXPERTS, LANES), F32)
    plans, max_rows, used_first = [], 0, None
    for x2d, att, pool in groups:
        n = x2d.shape[0]
        tm_g = min(tm, n)
        h1, cpos, gw_t, seg_t, len_t, dst_t, carry = _tail(
            x2d, att, pool, wts["g0"], wts["b0"], wts["woa"], wts["wob"], wts["g1"], wts["b1"],
            wts["wrh"], wts["wrl"], wts["br"], carry, tm=tm_g)
        plans.append((tm_g, h1, cpos, gw_t, seg_t, len_t, dst_t))
        max_rows += TOP_K * n + (ROW_ALIGN - 1) * N_EXPERTS * (n // tm_g)
        used_first = carry[:, 0].astype(I32) if used_first is None else used_first
    used = carry[:, 0].astype(I32)
    cap = ((used + tmoe - 1) // tmoe) * tmoe
    ends = jnp.cumsum(cap)
    offs = ends - cap
    n_tiles = -(-max_rows // tmoe) + N_EXPERTS
    tile_start = jnp.arange(n_tiles, dtype=I32) * tmoe
    tile_expert = jnp.minimum(jnp.sum((ends[None, :] <= tile_start[:, None]).astype(I32), axis=1), N_EXPERTS - 1)
    tile_valid = jnp.clip(offs[tile_expert] + used[tile_expert] - tile_start, 0, tmoe).astype(I32)
    tile_valid = jnp.where(tile_start < ends[-1], tile_valid, 0)

    written_end = (offs + used_first)[tile_expert]
    fill_off = jnp.where(tile_start < ends[-1], jnp.clip(written_end, tile_start, tile_start + tmoe), tile_start)
    fill_len = tile_start + tmoe - fill_off
    no_fill = jnp.zeros((1,), I32)

    xs, tables = None, []
    for tm_g, h1, cpos, gw_t, seg_t, len_t, dst_t in plans:
        seg = seg_t[:, :, 0].astype(I32).reshape(-1)
        ln = len_t[:, :, 0].astype(I32).reshape(-1)
        dst = (dst_t[:, :, 0].astype(I32) + offs[None, :]).reshape(-1)
        tables.append((seg, ln, dst))
        fills = (fill_off, fill_len) if xs is None else (no_fill, no_fill)
        xs = _dispatch(seg, ln, dst, *fills, h1, cpos, xs, tm=tm_g, p_rows=n_tiles * tmoe, fill_rows=tmoe)
    ys = _moe(tile_expert, tile_valid, xs, wts["w1e"], wts["b1e"], wts["w2e"], wts["b2e"], tmoe=tmoe)
    return [_combine(seg, ln, dst, h1, cpos.T, gw_t.T, wts["g2"], wts["b2"], ys, tm=tm_g)
            for (seg, ln, dst), (tm_g, h1, cpos, gw_t, _, _, _) in zip(tables, plans)]


def kernel(x_prompt, x_sample, cache_k, cache_v, cache_kidx, state_pool, ln0_g, ln0_b, w_in, w_o,
           pool_w, pool_scale, ln1_g, ln1_b, w_router, b_router, w1, b1, w2, b2, ln2_g, ln2_b):
    bp, s_len, d = x_prompt.shape
    bs, t_len, _ = x_sample.shape
    l_past = cache_k.shape[2]
    aw = N_HEADS * HEAD_DIM
    pw = d - aw
    lyr = 0
    lc = 256

    k_off = aw
    v_off = k_off + HEAD_DIM
    qi_off = v_off + HEAD_DIM
    ki_off = qi_off + IDX_HEADS * IDX_DIM
    wi_off = ki_off + IDX_DIM
    u_off = wi_off + IDX_HEADS
    win = w_in[lyr]
    wa = jnp.concatenate([win[:, 0:k_off], win[:, qi_off:ki_off], win[:, u_off:u_off + pw]], axis=1).astype(BF16)
    wb = jnp.concatenate([win[:, k_off:v_off], win[:, v_off:qi_off], win[:, ki_off:wi_off], win[:, wi_off:u_off],
                          jnp.zeros((d, HEAD_DIM - IDX_HEADS), F32)], axis=1).astype(BF16)

    g0 = ln0_g.reshape(1, d)
    b0 = ln0_b.reshape(1, d)
    wrt = w_router[lyr].T
    wrh = wrt.astype(BF16)
    wts = dict(
        g0=g0, b0=b0,
        woa=w_o[lyr][:aw].astype(BF16), wob=w_o[lyr][aw:].astype(BF16),
        g1=ln1_g[lyr].reshape(1, d), b1=ln1_b[lyr].reshape(1, d),
        wrh=wrh, wrl=(wrt - wrh.astype(F32)).astype(BF16), br=b_router[lyr].reshape(N_EXPERTS, 1),
        w1e=w1[lyr], b1e=b1[lyr], w2e=w2[lyr], b2e=b2[lyr],
        g2=ln2_g[lyr].reshape(1, d), b2=ln2_b[lyr].reshape(1, d),
    )
    pool_w_b = pool_w[lyr].astype(BF16)
    pool_sc = pool_scale[lyr].reshape(1, pw)

    xp = x_prompt.reshape(bp * s_len, d)
    q, qi, u, k, v, ki, kb, kib, vt, wit = _proj(xp, g0, b0, wa, wb, tm=512, lc=lc)
    att_p = _dsa(q.reshape(bp, s_len, aw), qi.reshape(bp, s_len, aw), wit,
                 kb.reshape(bp, s_len, HEAD_DIM), kib.reshape(bp, s_len, IDX_DIM), vt,
                 tq=256, lc=lc, causal=True, l_valid=s_len, q_pos0=0, topk=min(TOPK_MAX, s_len // 4))
    u_p = u.reshape(bp, s_len, pw)
    pool_p = _pool(u_p, jnp.zeros((bp, POOL_PAST + 1, pw), F32), pool_w_b, pool_sc, pos0=0)

    xs = x_sample.reshape(bs * t_len, d)
    qs, qis, us, kn, vn, kin, _, _, _, wits = _proj(xs, g0, b0, wa, wb, tm=512, lc=lc)
    l_all = l_past + t_len
    l_pad = -(-l_all // lc) * lc
    tq_s = LANES
    pad_keys = lambda a: jnp.pad(a, ((0, 0), (0, l_pad - l_all), (0, 0)))
    k_all = pad_keys(jnp.concatenate([cache_k[lyr], kn.reshape(bs, t_len, HEAD_DIM)], axis=1))
    v_all = pad_keys(jnp.concatenate([cache_v[lyr], vn.reshape(bs, t_len, HEAD_DIM)], axis=1))
    ki_all = pad_keys(jnp.concatenate([cache_kidx[lyr], kin.reshape(bs, t_len, IDX_DIM)], axis=1))
    vt_all = v_all.reshape(bs, l_pad // lc, lc, HEAD_DIM).transpose(0, 1, 3, 2).reshape(-1, HEAD_DIM, lc)
    vt_all = jnp.concatenate([vt_all, jnp.broadcast_to(_denominator_rows(lc), (vt_all.shape[0], VT_ROWS - HEAD_DIM, lc))],
                             axis=1)
    pad_q = lambda a: jnp.pad(a.reshape(bs, t_len, aw), ((0, 0), (0, tq_s - t_len), (0, 0)))
    wit_s = jnp.pad(wits.reshape(IDX_HEADS, bs, t_len), ((0, 0), (0, 0), (0, tq_s - t_len))).reshape(IDX_HEADS, -1)
    att_s = _dsa(pad_q(qs), pad_q(qis), wit_s, k_all.astype(BF16), ki_all.astype(BF16), vt_all.astype(BF16),
                 tq=tq_s, lc=lc, causal=False, l_valid=l_all, q_pos0=l_past, topk=min(TOPK_MAX, l_all // 4))
    att_s = att_s[:, :t_len].reshape(bs * t_len, aw)
    us3 = us.reshape(bs, t_len, pw)
    prefix_s = jnp.concatenate([jnp.zeros((bs, 1, pw), F32), state_pool[lyr]], axis=1)
    pool_s = _pool(us3, prefix_s, pool_w_b, pool_sc, pos0=l_past)

    y_p, y_s = _block_tails([(xp, att_p.reshape(bp * s_len, aw), pool_p.reshape(bp * s_len, pw)),
                             (xs, att_s, pool_s.reshape(bs * t_len, pw))], wts, tm=512, tmoe=1024)

    pool_state_p = u_p[:, s_len - POOL_PAST:]
    pool_state_s = jnp.concatenate([state_pool[lyr], us3], axis=1)[:, -POOL_PAST:]
    return (y_p.reshape(bp, s_len, d), y_s.reshape(bs, t_len, d),
            k.reshape(1, bp, s_len, HEAD_DIM), v.reshape(1, bp, s_len, HEAD_DIM),
            ki.reshape(1, bp, s_len, IDX_DIM), pool_state_p[None],
            kn.reshape(1, bs, t_len, HEAD_DIM), vn.reshape(1, bs, t_len, HEAD_DIM),
            kin.reshape(1, bs, t_len, IDX_DIM), pool_state_s[None])
```

```python
import functools

import jax
import jax.numpy as jnp
from jax import lax
from jax.experimental import pallas as pl
from jax.experimental.pallas import tpu as pltpu

F32 = jnp.float32
BF16 = jnp.bfloat16
I32 = jnp.int32

CHUNK = 64
CHUNK_SHIFT = 6
assert 1 << CHUNK_SHIFT == CHUNK
N_HEADS = 8
HEAD_DIM = 64
IDX_HEADS = 8
IDX_DIM = 64
TOPK_MAX = 256
POOL_WINDOWS = (2, 4, 8, 16)
POOL_PAST = 15
N_EXPERTS = 32
TOP_K = 4
SWIGLU_LIMIT = 7.0
SWIGLU_ALPHA = 1.702
LN_EPS = 1e-5
DEPTH = 1
DEEPNORM_ALPHA = (2 * DEPTH) ** 0.25
LOG2_E = 1.4426950408889634

LANES = 128
SUBLANES = 8
BF16_ROWS = 16
VMEM_LIMIT_BYTES = 56 * 1024 * 1024

NEG_BIG = -1e30
KEY_NEG_INF = -2139095041
KEY_POS_INF = 2139095040

NT_DIMS = (((1,), (1,)), ((), ()))


def _layer_norm(x, g, b):
    mu = jnp.mean(x, axis=-1, keepdims=True)
    xc = x - mu
    var = jnp.mean(xc * xc, axis=-1, keepdims=True)
    return xc * lax.rsqrt(var + LN_EPS) * g + b


def _params(sem):
    return pltpu.CompilerParams(dimension_semantics=sem, vmem_limit_bytes=VMEM_LIMIT_BYTES)


VT_ROWS = HEAD_DIM + BF16_ROWS


def _denominator_rows(width):
    return jnp.where(lax.broadcasted_iota(I32, (VT_ROWS - HEAD_DIM, width), 0) == 0, 1.0, 0.0)


def _proj_kernel(x_ref, g_ref, b_ref, wa_ref, wb_ref,
                 q_ref, qi_ref, u_ref, k_ref, v_ref, ki_ref, kb_ref, kib_ref, vt_ref, wit_ref, *, lc):
    h = _layer_norm(x_ref[...], g_ref[...], b_ref[...])
    hb = h.astype(BF16)
    aw = N_HEADS * HEAD_DIM
    pa = jnp.dot(hb, wa_ref[...], preferred_element_type=F32)
    q_ref[...] = (pa[:, :aw] * (HEAD_DIM ** -0.5 * LOG2_E)).astype(BF16)
    qi_ref[...] = pa[:, aw:2 * aw].astype(BF16)
    u_ref[...] = pa[:, 2 * aw:]
    pb = jnp.dot(hb, wb_ref[...], preferred_element_type=F32)
    k = pb[:, 0:HEAD_DIM]
    v = pb[:, HEAD_DIM:2 * HEAD_DIM]
    ki = pb[:, 2 * HEAD_DIM:2 * HEAD_DIM + IDX_DIM]
    k_ref[...] = k
    v_ref[...] = v
    ki_ref[...] = ki
    kb_ref[...] = k.astype(BF16)
    kib_ref[...] = ki.astype(BF16)
    pt = pb.T
    ones_rows = _denominator_rows(lc)
    for c in range(vt_ref.shape[0]):
        vt_ref[c] = jnp.concatenate([pt[HEAD_DIM:2 * HEAD_DIM, c * lc:(c + 1) * lc], ones_rows],
                                    axis=0).astype(BF16)
    wi = pt[3 * HEAD_DIM:3 * HEAD_DIM + IDX_HEADS, :]
    wit_ref[...] = (wi * (IDX_HEADS ** -0.5)) * (IDX_DIM ** -0.5)


def _proj(x2d, g, b, wa, wb, *, tm, lc):
    n, d = x2d.shape
    tm = min(tm, n)
    aw = N_HEADS * HEAD_DIM
    uw = wa.shape[1] - 2 * aw
    row = lambda i: (i, 0)
    const = lambda i: (0, 0)
    out_shape = (
        jax.ShapeDtypeStruct((n, aw), BF16),
        jax.ShapeDtypeStruct((n, aw), BF16),
        jax.ShapeDtypeStruct((n, uw), F32),
        jax.ShapeDtypeStruct((n, HEAD_DIM), F32),
        jax.ShapeDtypeStruct((n, HEAD_DIM), F32),
        jax.ShapeDtypeStruct((n, IDX_DIM), F32),
        jax.ShapeDtypeStruct((n, HEAD_DIM), BF16),
        jax.ShapeDtypeStruct((n, IDX_DIM), BF16),
        jax.ShapeDtypeStruct((n // lc, VT_ROWS, lc), BF16),
        jax.ShapeDtypeStruct((IDX_HEADS, n), F32),
    )
    out_specs = (
        pl.BlockSpec((tm, aw), row), pl.BlockSpec((tm, aw), row), pl.BlockSpec((tm, uw), row),
        pl.BlockSpec((tm, HEAD_DIM), row), pl.BlockSpec((tm, HEAD_DIM), row), pl.BlockSpec((tm, IDX_DIM), row),
        pl.BlockSpec((tm, HEAD_DIM), row), pl.BlockSpec((tm, IDX_DIM), row),
        pl.BlockSpec((tm // lc, VT_ROWS, lc), lambda i: (i, 0, 0)),
        pl.BlockSpec((IDX_HEADS, tm), lambda i: (0, i)),
    )
    return pl.pallas_call(
        functools.partial(_proj_kernel, lc=lc),
        grid=(n // tm,),
        in_specs=[pl.BlockSpec((tm, d), row), pl.BlockSpec((1, d), const), pl.BlockSpec((1, d), const),
                  pl.BlockSpec(wa.shape, const), pl.BlockSpec(wb.shape, const)],
        out_specs=out_specs,
        out_shape=out_shape,
        compiler_params=_params(("parallel",)),
        name="proj",
    )(x2d, g, b, wa, wb)


def _key_to_float(key):
    bits = jnp.where(key >= 0, key, key ^ jnp.int32(0x7FFFFFFF))
    return lax.bitcast_convert_type(bits, F32)


def _dsa_kernel(q_ref, qi_ref, wit_ref, kb_ref, kib_ref, vt_ref, o_ref,
                sc_ref, sch_ref, s_ref, out_ref, *acc_refs, tq, lc, nk_static, causal, l_valid, q_pos0, topk):
    qb = pl.program_id(1)
    nk = qb * (tq // lc) + (tq // lc) if causal else nk_static
    q_chunk = (q_pos0 + qb * tq + lax.broadcasted_iota(I32, (1, tq), 1)) >> CHUNK_SHIFT

    def chunk_loop(body, init):
        if causal:
            return lax.fori_loop(0, nk, body, init)
        return lax.fori_loop(0, nk_static, body, init, unroll=True)

    def wide_chunk_loop(body, init):
        quad = lambda i, c: body(pl.multiple_of(i * 4 * lc, 4 * lc), 4 * lc, c)
        if not causal:
            carry = lax.fori_loop(0, nk_static // 4, quad, init, unroll=True)
            done = nk_static // 4 * 4
            for span in (2, 1):
                if (nk_static - done) >= span:
                    carry = body(done * lc, span * lc, carry)
                    done += span
            return carry
        carry = lax.fori_loop(0, nk // 4, quad, init)
        done = (nk // 4) * 4
        carry = lax.cond((nk & 2) != 0, lambda c: body(pl.multiple_of(done * lc, 2 * lc), 2 * lc, c), lambda c: c,
                         carry)
        done = done + (nk & 2)
        return lax.cond((nk & 1) != 0, lambda c: body(pl.multiple_of(done * lc, lc), lc, c), lambda c: c, carry)

    def score_span(off, rows, carry):
        kic = kib_ref[0, pl.ds(off, rows), :]
        acc = jnp.zeros((rows, tq), F32)
        for h in range(IDX_HEADS):
            s = lax.dot_general(kic, qi_ref[0, :, h * IDX_DIM:(h + 1) * IDX_DIM], NT_DIMS,
                                preferred_element_type=F32)
            acc = acc + wit_ref[h:h + 1, :] * jnp.maximum(s, 0.0)
        l_pos = off + lax.broadcasted_iota(I32, (rows, tq), 0)
        visible = ((l_pos >> CHUNK_SHIFT) <= q_chunk) & (l_pos < l_valid)
        score = jnp.where(visible, acc, -jnp.inf)
        sc_ref[pl.ds(off, rows), :] = score
        sch_ref[pl.ds(off, rows), :] = score.astype(BF16)
        return carry

    wide_chunk_loop(score_span, 0)

    def count(pred):
        ways = 4 * SUBLANES

        def body(kc, part):
            off = pl.multiple_of(kc * lc, lc)
            hit = jnp.where(pred(sc_ref[pl.ds(off, lc), :]), 1.0, 0.0)
            return part + jnp.sum(hit.reshape(lc // ways, ways, tq), axis=0)
        return jnp.sum(chunk_loop(body, jnp.zeros((ways, tq), F32)), axis=0, keepdims=True)

    def count_coarse(cand):
        ways = 2 * BF16_ROWS
        assert sc_ref.shape[0] // ways <= 256

        def body(kc, part):
            off = pl.multiple_of(kc * lc, lc)
            hit = jnp.where(sch_ref[pl.ds(off, lc), :] >= cand, jnp.ones((), BF16), jnp.zeros((), BF16))
            hit = hit.reshape(lc // ways, ways, tq)
            terms = [hit[j] for j in range(lc // ways)]
            while len(terms) > 1:
                terms = [a + b for a, b in zip(terms[::2], terms[1::2])]
            return part + terms[0]
        part = chunk_loop(body, jnp.zeros((ways, tq), BF16))
        return jnp.sum(part.astype(F32), axis=0, keepdims=True)

    def bisect(count_ge, key_to_value, lo, hi, steps):
        def step(_, carry):
            lo, hi, n_lo = carry
            mid = (lo >> 1) + (hi >> 1) + (lo & hi & 1)
            n_mid = count_ge(key_to_value(mid))
            ok = n_mid >= topk
            return jnp.where(ok, mid, lo), jnp.where(ok, hi, mid), jnp.where(ok, n_mid, n_lo)
        lo, _, n_lo = lax.fori_loop(0, steps, step, (lo, hi, jnp.full(lo.shape, -1.0, F32)))
        return lo, n_lo

    def coarse_key_to_f32_key(k16):
        return jnp.where(k16 >= 0, k16 << 16, (k16 << 16) | 0xFFFF)

    full = lambda v: jnp.full((1, tq), v, I32)
    k16, _ = bisect(count_coarse, lambda k: _key_to_float(coarse_key_to_f32_key(k)).astype(BF16),
                    full(KEY_NEG_INF >> 16), full((KEY_POS_INF >> 16) + 1), 16)
    kb = coarse_key_to_f32_key(k16)
    span = 1 << 16
    lo, n_at_thr = bisect(lambda cand: count(lambda blk: blk >= cand), _key_to_float,
                          jnp.maximum(kb, KEY_NEG_INF + span) - span,
                          jnp.minimum(kb, KEY_POS_INF + 1 - span) + span, 17)
    thr = _key_to_float(lo)

    tri = jnp.where(lax.broadcasted_iota(I32, (lc, lc), 0) >= lax.broadcasted_iota(I32, (lc, lc), 1),
                    1.0, 0.0).astype(BF16)

    def logits_pass(n_ties):
        def logits_span(off, rows, carry):
            ties_before, m8 = carry
            biases = []
            for r0 in range(0, rows, lc):
                blk = sc_ref[pl.ds(off + r0, lc), :]
                if n_ties is None:
                    bias = jnp.where(blk >= thr, 0.0, NEG_BIG)
                else:
                    tie = blk == thr
                    tie_rank = (jnp.dot(tri, jnp.where(tie, 1.0, 0.0).astype(BF16), preferred_element_type=F32)
                                + ties_before)
                    bias = jnp.where(blk > thr, 0.0,
                                     jnp.where(tie, jnp.where(tie_rank <= n_ties, 0.0, NEG_BIG), NEG_BIG))
                    ties_before = tie_rank[lc - 1:lc, :]
                biases.append(jnp.where(blk == -jnp.inf, NEG_BIG, bias))
            bias = jnp.concatenate(biases, axis=0)
            kc_b = kb_ref[0, pl.ds(off, rows), :]
            m_rows = []
            for h in range(N_HEADS):
                s = lax.dot_general(kc_b, q_ref[0, :, h * HEAD_DIM:(h + 1) * HEAD_DIM], NT_DIMS,
                                    preferred_element_type=F32) + bias
                s_ref[h, pl.ds(off, rows), :] = s
                s8 = jnp.max(s.reshape(rows // SUBLANES, SUBLANES, tq), axis=0)
                m_rows.append(jnp.maximum(m8[h * SUBLANES:(h + 1) * SUBLANES], s8))
            return ties_before, jnp.concatenate(m_rows, axis=0)

        return wide_chunk_loop(logits_span,
                               (jnp.zeros((1, tq), F32), jnp.full((N_HEADS * SUBLANES, tq), NEG_BIG, F32)))[1]

    def with_ties():
        n_above = count(lambda blk: blk > thr)
        return logits_pass(topk - n_above)

    no_excess_ties = jnp.min(jnp.where(n_at_thr == topk, 1.0, 0.0)) > 0.5
    m8 = lax.cond(no_excess_ties, lambda: logits_pass(None), with_ties)
    m_all = jnp.max(m8.reshape(N_HEADS, SUBLANES, tq), axis=1)

    for acc_ref in acc_refs:
        acc_ref[...] = jnp.zeros(acc_ref.shape, F32)

    def pv_chunk(kc, carry):
        off = pl.multiple_of(kc * lc, lc)
        vt_c = vt_ref[kc]
        for h in range(N_HEADS):
            p = jnp.exp2(s_ref[h, pl.ds(off, lc), :] - m_all[h:h + 1, :])
            acc_refs[h][...] += jnp.dot(vt_c, p.astype(BF16), preferred_element_type=F32)
        return carry

    chunk_loop(pv_chunk, 0)

    for h in range(N_HEADS):
        acc = acc_refs[h][...]
        out_ref[h * HEAD_DIM:(h + 1) * HEAD_DIM, :] = acc[0:HEAD_DIM] / acc[HEAD_DIM:HEAD_DIM + 1]
    o_ref[0] = out_ref[...].T.astype(BF16)


def _dsa(q, qi, wit, kb, kib, vt, *, tq, lc, causal, l_valid, q_pos0, topk):
    bsz, tq_tot, aw = q.shape
    l_tot = kb.shape[1]
    nq = tq_tot // tq
    nkc = l_tot // lc
    kern = functools.partial(_dsa_kernel, tq=tq, lc=lc, nk_static=nkc, causal=causal, l_valid=l_valid,
                             q_pos0=q_pos0, topk=topk)
    return pl.pallas_call(
        kern,
        grid=(bsz, nq),
        in_specs=[
            pl.BlockSpec((1, tq, aw), lambda b, i: (b, i, 0)),
            pl.BlockSpec((1, tq, aw), lambda b, i: (b, i, 0)),
            pl.BlockSpec((IDX_HEADS, tq), lambda b, i: (0, b * nq + i)),
            pl.BlockSpec((1, l_tot, HEAD_DIM), lambda b, i: (b, 0, 0)),
            pl.BlockSpec((1, l_tot, IDX_DIM), lambda b, i: (b, 0, 0)),
            pl.BlockSpec((nkc, VT_ROWS, lc), lambda b, i: (b, 0, 0)),
        ],
        out_specs=pl.BlockSpec((1, tq, aw), lambda b, i: (b, i, 0)),
        out_shape=jax.ShapeDtypeStruct((bsz, tq_tot, aw), BF16),
        scratch_shapes=[
            pltpu.VMEM((l_tot, tq), F32),
            pltpu.VMEM((l_tot, tq), BF16),
            pltpu.VMEM((N_HEADS, l_tot, tq), F32),
            pltpu.VMEM((aw, tq), F32),
        ] + [pltpu.VMEM((VT_ROWS, tq), F32) for _ in range(N_HEADS)],
        compiler_params=_params(("parallel", "parallel")),
        name="dsa",
    )(q, qi, wit, kb, kib, vt)


def _pool_kernel(u_ref, pre_ref, pw_ref, sc_ref, o_ref, ext_ref, *, t_len, tt, pos0):
    pad = pre_ref.shape[1]
    gw = pw_ref.shape[1]
    ext_ref[0:pad, :] = pre_ref[0]
    ext_ref[pad:pad + t_len, :] = u_ref[0]
    row = lax.broadcasted_iota(I32, (tt, gw), 0)
    for t in range(t_len // tt):
        r0 = t * tt
        for g, w in enumerate(POOL_WINDOWS):
            cols = slice(g * gw, (g + 1) * gw)
            cur = ext_ref[pad + r0:pad + r0 + tt, cols]
            wsum = cur
            for j in range(1, w):
                wsum = wsum + ext_ref[pad + r0 - j:pad + r0 - j + tt, cols]
            cnt = jnp.minimum(w, pos0 + r0 + 1 + row).astype(F32)
            diff = (wsum / cnt - cur).astype(BF16)
            y = jnp.dot(diff, pw_ref[g], preferred_element_type=F32)
            o_ref[0, r0:r0 + tt, cols] = (y * sc_ref[:, cols]).astype(BF16)


def _pool(u, prefix, pool_w_b, pool_scale, *, pos0):
    bsz, t_len, c = u.shape
    pad = prefix.shape[1]
    tt = min(t_len, 256)
    return pl.pallas_call(
        functools.partial(_pool_kernel, t_len=t_len, tt=tt, pos0=pos0),
        grid=(bsz,),
        in_specs=[pl.BlockSpec((1, t_len, c), lambda b: (b, 0, 0)),
                  pl.BlockSpec((1, pad, c), lambda b: (b, 0, 0)),
                  pl.BlockSpec(pool_w_b.shape, lambda b: (0, 0, 0)),
                  pl.BlockSpec((1, c), lambda b: (0, 0))],
        out_specs=pl.BlockSpec((1, t_len, c), lambda b: (b, 0, 0)),
        out_shape=jax.ShapeDtypeStruct((bsz, t_len, c), BF16),
        scratch_shapes=[pltpu.VMEM((pad + t_len, c), F32)],
        compiler_params=_params(("parallel",)),
        name="pool",
    )(u, prefix, pool_w_b, pool_scale)


ROW_ALIGN = 16


def _tail_kernel(x_ref, att_ref, pool_ref, g0_ref, b0_ref, woa_ref, wob_ref, g1_ref, b1_ref,
                 wrh_ref, wrl_ref, br_ref,
                 carry0_ref, h1_ref, cpos_ref, gw_ref, seg_ref, len_ref, dst_ref, used_ref, carry_ref, *, tm, sub):
    @pl.when(pl.program_id(0) == 0)
    def _():
        carry_ref[...] = carry0_ref[...]

    chunk_lens = []
    for s in range(sub):
        rows = slice(s * tm, (s + 1) * tm)
        chunk_lens.append(_tail_tile(x_ref[rows, :], att_ref[rows, :], pool_ref[rows, :], g0_ref, b0_ref, woa_ref,
                                     wob_ref, g1_ref, b1_ref, wrh_ref, wrl_ref, br_ref,
                                     h1_ref.at[rows, :], cpos_ref.at[:, rows], gw_ref.at[:, rows], seg_ref.at[s],
                                     len_ref.at[s], tm=tm))
    carry = carry_ref[...]
    for s in range(sub):
        dst_ref[s] = carry
        carry = carry + chunk_lens[s]
    carry_ref[...] = carry
    used_ref[...] = carry


def _tail_tile(x, att, pool, g0_ref, b0_ref, woa_ref, wob_ref, g1_ref, b1_ref, wrh_ref, wrl_ref, br_ref,
               h1_ref, cpos_ref, gw_ref, seg_ref, len_ref, *, tm):
    h = _layer_norm(x, g0_ref[...], b0_ref[...])
    mix = (jnp.dot(att, woa_ref[...], preferred_element_type=F32)
           + jnp.dot(pool, wob_ref[...], preferred_element_type=F32))
    h1 = _layer_norm(DEEPNORM_ALPHA * h + mix, g1_ref[...], b1_ref[...])
    h1_ref[...] = h1

    hh = h1.astype(BF16)
    hl = (h1 - hh.astype(F32)).astype(BF16)
    logits = (lax.dot_general(wrh_ref[...], hh, NT_DIMS, preferred_element_type=F32)
              + lax.dot_general(wrh_ref[...], hl, NT_DIMS, preferred_element_type=F32)
              + lax.dot_general(wrl_ref[...], hh, NT_DIMS, preferred_element_type=F32)
              + br_ref[...])

    e_iota = lax.broadcasted_iota(I32, (N_EXPERTS, tm), 0).astype(F32)
    work = logits
    vals, hots = [], []
    for k in range(TOP_K):
        m = jnp.max(work, axis=0, keepdims=True)
        idx = jnp.min(jnp.where(work == m, e_iota, float(N_EXPERTS)), axis=0, keepdims=True)
        hot = e_iota == idx
        vals.append(m)
        hots.append(jnp.where(hot, 1.0, 0.0))
        work = jnp.where(hot, -jnp.inf, work)
    exps = [jnp.exp(v - vals[0]) for v in vals]
    den = exps[0] + exps[1] + exps[2] + exps[3]
    for k in range(TOP_K):
        gw_ref[k:k + 1, :] = exps[k] / den

    hot_all = (hots[0] + hots[1] + hots[2] + hots[3]).astype(BF16)
    count = jnp.dot(hot_all, jnp.ones((tm, LANES), BF16), preferred_element_type=F32)
    chunk_len = jnp.ceil(count * (1.0 / ROW_ALIGN)) * ROW_ALIGN
    lower = jnp.where(lax.broadcasted_iota(I32, (N_EXPERTS, N_EXPERTS), 1)
                      < lax.broadcasted_iota(I32, (N_EXPERTS, N_EXPERTS), 0), 1.0, 0.0).astype(BF16)
    seg_base = jnp.dot(lower, chunk_len.astype(BF16), preferred_element_type=F32)
    before = jnp.where(lax.broadcasted_iota(I32, (tm, tm), 0) < lax.broadcasted_iota(I32, (tm, tm), 1),
                       1.0, 0.0).astype(BF16)
    slot = jnp.dot(hot_all, before, preferred_element_type=F32) + jnp.tile(seg_base, (1, tm // LANES))
    for k in range(TOP_K):
        cpos_ref[k:k + 1, :] = jnp.sum(hots[k] * slot, axis=0, keepdims=True).astype(I32)
    seg_ref[...] = seg_base
    len_ref[...] = chunk_len
    return chunk_len


def _tail(x2d, att, pool, g0, b0, woa, wob, g1, b1, wrh, wrl, br, carry0, *, tm):
    n, d = x2d.shape
    aw = att.shape[1]
    nt = n // tm
    sub = 2 if nt % 2 == 0 else 1
    ts = sub * tm
    row = lambda i: (i, 0)
    const = lambda i: (0, 0)
    col = lambda i: (0, i)
    tab = lambda i: (i, 0, 0)
    tab_shape = jax.ShapeDtypeStruct((nt, N_EXPERTS, LANES), F32)
    tab_spec = pl.BlockSpec((sub, N_EXPERTS, LANES), tab)
    return pl.pallas_call(
        functools.partial(_tail_kernel, tm=tm, sub=sub),
        grid=(nt // sub,),
        in_specs=[pl.BlockSpec((ts, d), row), pl.BlockSpec((ts, aw), row), pl.BlockSpec((ts, d - aw), row),
                  pl.BlockSpec((1, d), const), pl.BlockSpec((1, d), const),
                  pl.BlockSpec(woa.shape, const), pl.BlockSpec(wob.shape, const),
                  pl.BlockSpec((1, d), const), pl.BlockSpec((1, d), const),
                  pl.BlockSpec(wrh.shape, const), pl.BlockSpec(wrl.shape, const), pl.BlockSpec(br.shape, const),
                  pl.BlockSpec((N_EXPERTS, LANES), const)],
        out_specs=(pl.BlockSpec((ts, d), row), pl.BlockSpec((TOP_K, ts), col), pl.BlockSpec((TOP_K, ts), col),
                   tab_spec, tab_spec, tab_spec, pl.BlockSpec((N_EXPERTS, LANES), const)),
        out_shape=(jax.ShapeDtypeStruct((n, d), F32), jax.ShapeDtypeStruct((TOP_K, n), I32),
                   jax.ShapeDtypeStruct((TOP_K, n), F32), tab_shape, tab_shape, tab_shape,
                   jax.ShapeDtypeStruct((N_EXPERTS, LANES), F32)),
        scratch_shapes=[pltpu.VMEM((N_EXPERTS, LANES), F32)],
        compiler_params=_params(("arbitrary",)),
        name="tail",
    )(x2d, att, pool, g0, b0, woa, wob, g1, b1, wrh, wrl, br, carry0)


G_BLOCK = 256


def _stage_rows(tm):
    raw = TOP_K * tm + (ROW_ALIGN - 1) * N_EXPERTS
    return -(-raw // G_BLOCK) * G_BLOCK


def _chunk_copies(seg_ref, len_ref, dst_ref, tile, make_copy, act):
    def per_expert(e, carry):
        idx = tile * N_EXPERTS + e
        seg, ln, dst = seg_ref[idx], len_ref[idx], dst_ref[idx]

        @pl.when(ln > 0)
        def _():
            act(make_copy(pl.multiple_of(seg, ROW_ALIGN), pl.multiple_of(dst, ROW_ALIGN),
                          pl.multiple_of(ln, ROW_ALIGN)))
        return carry

    lax.fori_loop(0, N_EXPERTS, per_expert, 0)


def _by_staged_blocks(seg_ref, len_ref, tile, n_full, body):
    last = tile * N_EXPERTS + N_EXPERTS - 1
    total = seg_ref[last] + len_ref[last]
    pl.when(total <= (n_full - 1) * G_BLOCK)(lambda: body(n_full - 1))
    pl.when(total > (n_full - 1) * G_BLOCK)(lambda: body(n_full))


def _route_onehot(iota, pos_list, val_list):
    g = jnp.zeros(iota.shape, F32)
    for pos, val in zip(pos_list, val_list):
        g = jnp.where(iota == pos, val, g)
    return g.astype(BF16)


def _dispatch_kernel(seg_ref, len_ref, dst_ref, fill_off_ref, fill_len_ref, h1_ref, cpos_ref, *rest, tm, fill):
    xs_hbm, stage_ref, zero_ref, sem, fill_sem = rest[-5:]
    tile = pl.program_id(0)
    slot = tile % 2
    h1b = h1_ref[...].astype(BF16)
    pos_rows = [cpos_ref[k:k + 1, :] for k in range(TOP_K)]

    def stage_blocks(n_blk):
        for blk in range(n_blk):
            iota = blk * G_BLOCK + lax.broadcasted_iota(I32, (G_BLOCK, tm), 0)
            g = _route_onehot(iota, pos_rows, [1.0] * TOP_K)
            stage_ref[slot, blk * G_BLOCK:(blk + 1) * G_BLOCK, :] = jnp.dot(
                g, h1b, preferred_element_type=F32).astype(BF16)

    _by_staged_blocks(seg_ref, len_ref, tile, stage_ref.shape[1] // G_BLOCK, stage_blocks)

    def copy_from(buf):
        def make_copy(seg, dst, size):
            return pltpu.make_async_copy(stage_ref.at[buf, pl.ds(seg, size), :], xs_hbm.at[pl.ds(dst, size), :],
                                         sem.at[buf])
        return make_copy

    _chunk_copies(seg_ref, len_ref, dst_ref, tile, copy_from(slot), lambda c: c.start())

    @pl.when(tile > 0)
    def _():
        _chunk_copies(seg_ref, len_ref, dst_ref, tile - 1, copy_from(1 - slot), lambda c: c.wait())

    @pl.when(tile == pl.num_programs(0) - 1)
    def _():
        _chunk_copies(seg_ref, len_ref, dst_ref, tile, copy_from(slot), lambda c: c.wait())
        if fill:
            zero_ref[...] = jnp.zeros(zero_ref.shape, BF16)

            def fill_copy(j):
                size = pl.multiple_of(fill_len_ref[j], ROW_ALIGN)
                return pltpu.make_async_copy(zero_ref.at[pl.ds(0, size), :],
                                             xs_hbm.at[pl.ds(pl.multiple_of(fill_off_ref[j], ROW_ALIGN), size), :],
                                             fill_sem)

            def each_fill(act):
                def body(j, carry):
                    pl.when(fill_len_ref[j] > 0)(lambda: act(fill_copy(j)))
                    return carry
                lax.fori_loop(0, fill_len_ref.shape[0], body, 0)

            each_fill(lambda c: c.start())
            each_fill(lambda c: c.wait())


def _dispatch(seg, ln, dst, fill_off, fill_len, h1, cpos, xs_prev, *, tm, p_rows, fill_rows):
    n, d = h1.shape
    in_specs = [pl.BlockSpec((tm, d), lambda i, *_: (i, 0)), pl.BlockSpec((TOP_K, tm), lambda i, *_: (0, i))]
    operands = [seg, ln, dst, fill_off, fill_len, h1, cpos]
    aliases = {}
    if xs_prev is not None:
        in_specs.append(pl.BlockSpec(memory_space=pl.ANY))
        aliases = {len(operands): 0}
        operands.append(xs_prev)
    return pl.pallas_call(
        functools.partial(_dispatch_kernel, tm=tm, fill=xs_prev is None),
        grid_spec=pltpu.PrefetchScalarGridSpec(
            num_scalar_prefetch=5,
            grid=(n // tm,),
            in_specs=in_specs,
            out_specs=pl.BlockSpec(memory_space=pl.ANY),
            scratch_shapes=[pltpu.VMEM((2, _stage_rows(tm), d), BF16), pltpu.VMEM((fill_rows, d), BF16),
                            pltpu.SemaphoreType.DMA((2,)), pltpu.SemaphoreType.DMA(())],
        ),
        out_shape=jax.ShapeDtypeStruct((p_rows, d), BF16),
        input_output_aliases=aliases,
        compiler_params=_params(("arbitrary",)),
        name="dispatch",
    )(*operands)


FF_CHUNK = 256


def _moe_kernel(te_ref, tv_ref, x_ref, w1_ref, b1_ref, w2_ref, b2_ref, y_ref, w1b_ref, w2b_ref, act_ref):
    i = pl.program_id(0)
    d_ff = w2_ref.shape[1]
    valid = tv_ref[i]

    @pl.when((i == 0) | (te_ref[i] != te_ref[jnp.maximum(i - 1, 0)]))
    def _():
        w1b_ref[...] = w1_ref[0].astype(BF16)
        w2b_ref[...] = w2_ref[0].astype(BF16)

    @pl.when(valid > 0)
    def _():
        xb = x_ref[...]
        for j in range(d_ff // FF_CHUNK):
            gs = slice(j * FF_CHUNK, (j + 1) * FF_CHUNK)
            us = slice(d_ff + j * FF_CHUNK, d_ff + (j + 1) * FF_CHUNK)
            gate = jnp.dot(xb, w1b_ref[:, gs], preferred_element_type=F32) + b1_ref[0, :, gs]
            up = jnp.dot(xb, w1b_ref[:, us], preferred_element_type=F32) + b1_ref[0, :, us]
            gate = jnp.minimum(gate, SWIGLU_LIMIT)
            up = jnp.clip(up, -SWIGLU_LIMIT, SWIGLU_LIMIT)
            act_ref[:, gs] = ((up + 1.0) * (gate * jax.nn.sigmoid(SWIGLU_ALPHA * gate))).astype(BF16)
        y = jnp.dot(act_ref[...], w2b_ref[...], preferred_element_type=F32) + b2_ref[0]
        y_ref[...] = y.astype(BF16)

    @pl.when(valid <= 0)
    def _():
        y_ref[...] = jnp.zeros(y_ref.shape, BF16)


def _moe(tile_expert, tile_valid, xs, w1, b1, w2, b2, *, tmoe):
    p, d = xs.shape
    e, _, ff2 = w1.shape
    d_ff = ff2 // 2
    return pl.pallas_call(
        _moe_kernel,
        grid_spec=pltpu.PrefetchScalarGridSpec(
            num_scalar_prefetch=2,
            grid=(p // tmoe,),
            in_specs=[pl.BlockSpec((tmoe, d), lambda i, te, tv: (i, 0)),
                      pl.BlockSpec((1, d, ff2), lambda i, te, tv: (te[i], 0, 0)),
                      pl.BlockSpec((1, 1, ff2), lambda i, te, tv: (te[i], 0, 0)),
                      pl.BlockSpec((1, d_ff, d), lambda i, te, tv: (te[i], 0, 0)),
                      pl.BlockSpec((1, 1, d), lambda i, te, tv: (te[i], 0, 0))],
            out_specs=pl.BlockSpec((tmoe, d), lambda i, te, tv: (i, 0)),
            scratch_shapes=[pltpu.VMEM((d, ff2), BF16), pltpu.VMEM((d_ff, d), BF16), pltpu.VMEM((tmoe, d_ff), BF16)],
        ),
        out_shape=jax.ShapeDtypeStruct((p, d), BF16),
        compiler_params=_params(("arbitrary",)),
        name="moe",
    )(tile_expert, tile_valid, xs, w1, b1.reshape(e, 1, ff2), w2, b2.reshape(e, 1, d))


def _combine_kernel(seg_ref, len_ref, dst_ref, h1_ref, cpos_ref, gw_ref, g2_ref, b2_ref, ys_hbm, o_ref,
                    stage_ref, gate_ref, sem, *, tm):
    tile = pl.program_id(0)
    slot = tile % 2

    def copy_into(buf):
        def make_copy(seg, dst, size):
            return pltpu.make_async_copy(ys_hbm.at[pl.ds(dst, size), :], stage_ref.at[buf, pl.ds(seg, size), :],
                                         sem.at[buf])
        return make_copy

    @pl.when(tile == 0)
    def _():
        stage_ref[...] = jnp.zeros(stage_ref.shape, BF16)
        _chunk_copies(seg_ref, len_ref, dst_ref, tile, copy_into(slot), lambda c: c.start())

    @pl.when(tile + 1 < pl.num_programs(0))
    def _():
        _chunk_copies(seg_ref, len_ref, dst_ref, tile + 1, copy_into(1 - slot), lambda c: c.start())

    cpos = cpos_ref[...]
    gw = gw_ref[...]
    pos_cols = [cpos[:, k:k + 1] for k in range(TOP_K)]
    gw_cols = [gw[:, k:k + 1] for k in range(TOP_K)]

    def finish(n_blk):
        _chunk_copies(seg_ref, len_ref, dst_ref, tile, copy_into(slot), lambda c: c.wait())
        m = None
        for blk in range(n_blk):
            iota = blk * G_BLOCK + lax.broadcasted_iota(I32, (tm, G_BLOCK), 1)
            gate = _route_onehot(iota, pos_cols, gw_cols)
            part = jnp.dot(gate, stage_ref[slot, blk * G_BLOCK:(blk + 1) * G_BLOCK, :], preferred_element_type=F32)
            m = part if m is None else part + m
        o_ref[...] = _layer_norm(DEEPNORM_ALPHA * h1_ref[...] + m, g2_ref[...], b2_ref[...])

    _by_staged_blocks(seg_ref, len_ref, tile, stage_ref.shape[1] // G_BLOCK, finish)


def _combine(seg, ln, dst, h1, cpos_nt, gw_nt, g2, b2, ys, *, tm):
    n, d = h1.shape
    return pl.pallas_call(
        functools.partial(_combine_kernel, tm=tm),
        grid_spec=pltpu.PrefetchScalarGridSpec(
            num_scalar_prefetch=3,
            grid=(n // tm,),
            in_specs=[pl.BlockSpec((tm, d), lambda i, *_: (i, 0)),
                      pl.BlockSpec((tm, TOP_K), lambda i, *_: (i, 0)), pl.BlockSpec((tm, TOP_K), lambda i, *_: (i, 0)),
                      pl.BlockSpec((1, d), lambda i, *_: (0, 0)), pl.BlockSpec((1, d), lambda i, *_: (0, 0)),
                      pl.BlockSpec(memory_space=pl.ANY)],
            out_specs=pl.BlockSpec((tm, d), lambda i, *_: (i, 0)),
            scratch_shapes=[pltpu.VMEM((2, _stage_rows(tm), d), BF16), pltpu.VMEM((tm, _stage_rows(tm)), BF16),
                            pltpu.SemaphoreType.DMA((2,))],
        ),
        out_shape=jax.ShapeDtypeStruct((n, d), F32),
        compiler_params=_params(("arbitrary",)),
        name="combine",
    )(seg, ln, dst, h1, cpos_nt, gw_nt, g2, b2, ys)


def _block_tails(groups, wts, *, tm, tmoe):
    carry = jnp.zeros((N_EXPERTS, LANES), F32)
    plans, max_rows, used_first = [], 0, None
    for x2d, att, pool in groups:
        n = x2d.shape[0]
        tm_g = min(tm, n)
        h1, cpos, gw_t, seg_t, len_t, dst_t, carry = _tail(
            x2d, att, pool, wts["g0"], wts["b0"], wts["woa"], wts["wob"], wts["g1"], wts["b1"],
            wts["wrh"], wts["wrl"], wts["br"], carry, tm=tm_g)
        plans.append((tm_g, h1, cpos, gw_t, seg_t, len_t, dst_t))
        max_rows += TOP_K * n + (ROW_ALIGN - 1) * N_EXPERTS * (n // tm_g)
        used_first = carry[:, 0].astype(I32) if used_first is None else used_first
    used = carry[:, 0].astype(I32)
    cap = ((used + tmoe - 1) // tmoe) * tmoe
    ends = jnp.cumsum(cap)
    offs = ends - cap
    n_tiles = -(-max_rows // tmoe) + N_EXPERTS
    tile_start = jnp.arange(n_tiles, dtype=I32) * tmoe
    tile_expert = jnp.minimum(jnp.sum((ends[None, :] <= tile_start[:, None]).astype(I32), axis=1), N_EXPERTS - 1)
    tile_valid = jnp.clip(offs[tile_expert] + used[tile_expert] - tile_start, 0, tmoe).astype(I32)
    tile_valid = jnp.where(tile_start < ends[-1], tile_valid, 0)

    written_end = (offs + used_first)[tile_expert]
    fill_off = jnp.where(tile_start < ends[-1], jnp.clip(written_end, tile_start, tile_start + tmoe), tile_start)
    fill_len = tile_start + tmoe - fill_off
    no_fill = jnp.zeros((1,), I32)

    xs, tables = None, []
    for tm_g, h1, cpos, gw_t, seg_t, len_t, dst_t in plans:
        seg = seg_t[:, :, 0].astype(I32).reshape(-1)
        ln = len_t[:, :, 0].astype(I32).reshape(-1)
        dst = (dst_t[:, :, 0].astype(I32) + offs[None, :]).reshape(-1)
        tables.append((seg, ln, dst))
        fills = (fill_off, fill_len) if xs is None else (no_fill, no_fill)
        xs = _dispatch(seg, ln, dst, *fills, h1, cpos, xs, tm=tm_g, p_rows=n_tiles * tmoe, fill_rows=tmoe)
    ys = _moe(tile_expert, tile_valid, xs, wts["w1e"], wts["b1e"], wts["w2e"], wts["b2e"], tmoe=tmoe)
    return [_combine(seg, ln, dst, h1, cpos.T, gw_t.T, wts["g2"], wts["b2"], ys, tm=tm_g)
            for (seg, ln, dst), (tm_g, h1, cpos, gw_t, _, _, _) in zip(tables, plans)]


def kernel(x_prompt, x_sample, cache_k, cache_v, cache_kidx, state_pool, ln0_g, ln0_b, w_in, w_o,
           pool_w, pool_scale, ln1_g, ln1_b, w_router, b_router, w1, b1, w2, b2, ln2_g, ln2_b):
    bp, s_len, d = x_prompt.shape
    bs, t_len, _ = x_sample.shape
    l_past = cache_k.shape[2]
    aw = N_HEADS * HEAD_DIM
    pw = d - aw
    lyr = 0
    lc = 256

    k_off = aw
    v_off = k_off + HEAD_DIM
    qi_off = v_off + HEAD_DIM
    ki_off = qi_off + IDX_HEADS * IDX_DIM
    wi_off = ki_off + IDX_DIM
    u_off = wi_off + IDX_HEADS
    win = w_in[lyr]
    wa = jnp.concatenate([win[:, 0:k_off], win[:, qi_off:ki_off], win[:, u_off:u_off + pw]], axis=1).astype(BF16)
    wb = jnp.concatenate([win[:, k_off:v_off], win[:, v_off:qi_off], win[:, ki_off:wi_off], win[:, wi_off:u_off],
                          jnp.zeros((d, HEAD_DIM - IDX_HEADS), F32)], axis=1).astype(BF16)

    g0 = ln0_g.reshape(1, d)
    b0 = ln0_b.reshape(1, d)
    wrt = w_router[lyr].T
    wrh = wrt.astype(BF16)
    wts = dict(
        g0=g0, b0=b0,
        woa=w_o[lyr][:aw].astype(BF16), wob=w_o[lyr][aw:].astype(BF16),
        g1=ln1_g[lyr].reshape(1, d), b1=ln1_b[lyr].reshape(1, d),
        wrh=wrh, wrl=(wrt - wrh.astype(F32)).astype(BF16), br=b_router[lyr].reshape(N_EXPERTS, 1),
        w1e=w1[lyr], b1e=b1[lyr], w2e=w2[lyr], b2e=b2[lyr],
        g2=ln2_g[lyr].reshape(1, d), b2=ln2_b[lyr].reshape(1, d),
    )
    pool_w_b = pool_w[lyr].astype(BF16)
    pool_sc = pool_scale[lyr].reshape(1, pw)

    xp = x_prompt.reshape(bp * s_len, d)
    q, qi, u, k, v, ki, kb, kib, vt, wit = _proj(xp, g0, b0, wa, wb, tm=512, lc=lc)
    att_p = _dsa(q.reshape(bp, s_len, aw), qi.reshape(bp, s_len, aw), wit,
                 kb.reshape(bp, s_len, HEAD_DIM), kib.reshape(bp, s_len, IDX_DIM), vt,
                 tq=256, lc=lc, causal=True, l_valid=s_len, q_pos0=0, topk=min(TOPK_MAX, s_len // 4))
    u_p = u.reshape(bp, s_len, pw)
    pool_p = _pool(u_p, jnp.zeros((bp, POOL_PAST + 1, pw), F32), pool_w_b, pool_sc, pos0=0)

    xs = x_sample.reshape(bs * t_len, d)
    qs, qis, us, kn, vn, kin, _, _, _, wits = _proj(xs, g0, b0, wa, wb, tm=512, lc=lc)
    l_all = l_past + t_len
    l_pad = -(-l_all // lc) * lc
    tq_s = LANES
    pad_keys = lambda a: jnp.pad(a, ((0, 0), (0, l_pad - l_all), (0, 0)))
    k_all = pad_keys(jnp.concatenate([cache_k[lyr], kn.reshape(bs, t_len, HEAD_DIM)], axis=1))
    v_all = pad_keys(jnp.concatenate([cache_v[lyr], vn.reshape(bs, t_len, HEAD_DIM)], axis=1))
    ki_all = pad_keys(jnp.concatenate([cache_kidx[lyr], kin.reshape(bs, t_len, IDX_DIM)], axis=1))
    vt_all = v_all.reshape(bs, l_pad // lc, lc, HEAD_DIM).transpose(0, 1, 3, 2).reshape(-1, HEAD_DIM, lc)
    vt_all = jnp.concatenate([vt_all, jnp.broadcast_to(_denominator_rows(lc), (vt_all.shape[0], VT_ROWS - HEAD_DIM, lc))],
                             axis=1)
    pad_q = lambda a: jnp.pad(a.reshape(bs, t_len, aw), ((0, 0), (0, tq_s - t_len), (0, 0)))
    wit_s = jnp.pad(wits.reshape(IDX_HEADS, bs, t_len), ((0, 0), (0, 0), (0, tq_s - t_len))).reshape(IDX_HEADS, -1)
    att_s = _dsa(pad_q(qs), pad_q(qis), wit_s, k_all.astype(BF16), ki_all.astype(BF16), vt_all.astype(BF16),
                 tq=tq_s, lc=lc, causal=False, l_valid=l_all, q_pos0=l_past, topk=min(TOPK_MAX, l_all // 4))
    att_s = att_s[:, :t_len].reshape(bs * t_len, aw)
    us3 = us.reshape(bs, t_len, pw)
    prefix_s = jnp.concatenate([jnp.zeros((bs, 1, pw), F32), state_pool[lyr]], axis=1)
    pool_s = _pool(us3, prefix_s, pool_w_b, pool_sc, pos0=l_past)

    y_p, y_s = _block_tails([(xp, att_p.reshape(bp * s_len, aw), pool_p.reshape(bp * s_len, pw)),
                             (xs, att_s, pool_s.reshape(bs * t_len, pw))], wts, tm=512, tmoe=1024)

    pool_state_p = u_p[:, s_len - POOL_PAST:]
    pool_state_s = jnp.concatenate([state_pool[lyr], us3], axis=1)[:, -POOL_PAST:]
    return (y_p.reshape(bp, s_len, d), y_s.reshape(bs, t_len, d),
            k.reshape(1, bp, s_len, HEAD_DIM), v.reshape(1, bp, s_len, HEAD_DIM),
            ki.reshape(1, bp, s_len, IDX_DIM), pool_state_p[None],
            kn.reshape(1, bs, t_len, HEAD_DIM), vn.reshape(1, bs, t_len, HEAD_DIM),
            kin.reshape(1, bs, t_len, IDX_DIM), pool_state_s[None])
```

```python
import functools

import jax
import jax.numpy as jnp
from jax import lax
from jax.experimental import pallas as pl
from jax.experimental.pallas import tpu as pltpu

F32 = jnp.float32
BF16 = jnp.bfloat16
I32 = jnp.int32

CHUNK = 64
CHUNK_SHIFT = 6
assert 1 << CHUNK_SHIFT == CHUNK
N_HEADS = 8
HEAD_DIM = 64
IDX_HEADS = 8
IDX_DIM = 64
TOPK_MAX = 256
POOL_WINDOWS = (2, 4, 8, 16)
POOL_PAST = 15
N_EXPERTS = 32
TOP_K = 4
SWIGLU_LIMIT = 7.0
SWIGLU_ALPHA = 1.702
LN_EPS = 1e-5
DEPTH = 1
DEEPNORM_ALPHA = (2 * DEPTH) ** 0.25
LOG2_E = 1.4426950408889634

LANES = 128
SUBLANES = 8
BF16_ROWS = 16
VMEM_LIMIT_BYTES = 56 * 1024 * 1024

NEG_BIG = -1e30
KEY_NEG_INF = -2139095041
KEY_POS_INF = 2139095040

NT_DIMS = (((1,), (1,)), ((), ()))


def _layer_norm(x, g, b):
    mu = jnp.mean(x, axis=-1, keepdims=True)
    xc = x - mu
    var = jnp.mean(xc * xc, axis=-1, keepdims=True)
    return xc * lax.rsqrt(var + LN_EPS) * g + b


def _params(sem):
    return pltpu.CompilerParams(dimension_semantics=sem, vmem_limit_bytes=VMEM_LIMIT_BYTES)


VT_ROWS = HEAD_DIM + BF16_ROWS


def _denominator_rows(width):
    return jnp.where(lax.broadcasted_iota(I32, (VT_ROWS - HEAD_DIM, width), 0) == 0, 1.0, 0.0)


def _proj_kernel(x_ref, g_ref, b_ref, wa_ref, wb_ref,
                 q_ref, qi_ref, u_ref, k_ref, v_ref, ki_ref, kb_ref, kib_ref, vt_ref, wit_ref, *, lc):
    h = _layer_norm(x_ref[...], g_ref[...], b_ref[...])
    hb = h.astype(BF16)
    aw = N_HEADS * HEAD_DIM
    pa = jnp.dot(hb, wa_ref[...], preferred_element_type=F32)
    q_ref[...] = (pa[:, :aw] * (HEAD_DIM ** -0.5 * LOG2_E)).astype(BF16)
    qi_ref[...] = pa[:, aw:2 * aw].astype(BF16)
    u_ref[...] = pa[:, 2 * aw:]
    pb = jnp.dot(hb, wb_ref[...], preferred_element_type=F32)
    k = pb[:, 0:HEAD_DIM]
    v = pb[:, HEAD_DIM:2 * HEAD_DIM]
    ki = pb[:, 2 * HEAD_DIM:2 * HEAD_DIM + IDX_DIM]
    k_ref[...] = k
    v_ref[...] = v
    ki_ref[...] = ki
    kb_ref[...] = k.astype(BF16)
    kib_ref[...] = ki.astype(BF16)
    pt = pb.T
    ones_rows = _denominator_rows(lc)
    for c in range(vt_ref.shape[0]):
        vt_ref[c] = jnp.concatenate([pt[HEAD_DIM:2 * HEAD_DIM, c * lc:(c + 1) * lc], ones_rows],
                                    axis=0).astype(BF16)
    wi = pt[3 * HEAD_DIM:3 * HEAD_DIM + IDX_HEADS, :]
    wit_ref[...] = (wi * (IDX_HEADS ** -0.5)) * (IDX_DIM ** -0.5)


def _proj(x2d, g, b, wa, wb, *, tm, lc):
    n, d = x2d.shape
    tm = min(tm, n)
    aw = N_HEADS * HEAD_DIM
    uw = wa.shape[1] - 2 * aw
    row = lambda i: (i, 0)
    const = lambda i: (0, 0)
    out_shape = (
        jax.ShapeDtypeStruct((n, aw), BF16),
        jax.ShapeDtypeStruct((n, aw), BF16),
        jax.ShapeDtypeStruct((n, uw), F32),
        jax.ShapeDtypeStruct((n, HEAD_DIM), F32),
        jax.ShapeDtypeStruct((n, HEAD_DIM), F32),
        jax.ShapeDtypeStruct((n, IDX_DIM), F32),
        jax.ShapeDtypeStruct((n, HEAD_DIM), BF16),
        jax.ShapeDtypeStruct((n, IDX_DIM), BF16),
        jax.ShapeDtypeStruct((n // lc, VT_ROWS, lc), BF16),
        jax.ShapeDtypeStruct((IDX_HEADS, n), F32),
    )
    out_specs = (
        pl.BlockSpec((tm, aw), row), pl.BlockSpec((tm, aw), row), pl.BlockSpec((tm, uw), row),
        pl.BlockSpec((tm, HEAD_DIM), row), pl.BlockSpec((tm, HEAD_DIM), row), pl.BlockSpec((tm, IDX_DIM), row),
        pl.BlockSpec((tm, HEAD_DIM), row), pl.BlockSpec((tm, IDX_DIM), row),
        pl.BlockSpec((tm // lc, VT_ROWS, lc), lambda i: (i, 0, 0)),
        pl.BlockSpec((IDX_HEADS, tm), lambda i: (0, i)),
    )
    return pl.pallas_call(
        functools.partial(_proj_kernel, lc=lc),
        grid=(n // tm,),
        in_specs=[pl.BlockSpec((tm, d), row), pl.BlockSpec((1, d), const), pl.BlockSpec((1, d), const),
                  pl.BlockSpec(wa.shape, const), pl.BlockSpec(wb.shape, const)],
        out_specs=out_specs,
        out_shape=out_shape,
        compiler_params=_params(("parallel",)),
        name="proj",
    )(x2d, g, b, wa, wb)


def _key_to_float(key):
    bits = jnp.where(key >= 0, key, key ^ jnp.int32(0x7FFFFFFF))
    return lax.bitcast_convert_type(bits, F32)


def _dsa_kernel(q_ref, qi_ref, wit_ref, kb_ref, kib_ref, vt_ref, o_ref,
                sc_ref, sch_ref, s_ref, out_ref, *acc_refs, tq, lc, nk_static, causal, l_valid, q_pos0, topk):
    qb = pl.program_id(1)
    nk = qb * (tq // lc) + (tq // lc) if causal else nk_static
    q_chunk = (q_pos0 + qb * tq + lax.broadcasted_iota(I32, (1, tq), 1)) >> CHUNK_SHIFT

    def chunk_loop(body, init):
        if causal:
            return lax.fori_loop(0, nk, body, init)
        return lax.fori_loop(0, nk_static, body, init, unroll=True)

    def wide_chunk_loop(body, init):
        quad = lambda i, c: body(pl.multiple_of(i * 4 * lc, 4 * lc), 4 * lc, c)
        if not causal:
            carry = lax.fori_loop(0, nk_static // 4, quad, init, unroll=True)
            done = nk_static // 4 * 4
            for span in (2, 1):
                if (nk_static - done) >= span:
                    carry = body(done * lc, span * lc, carry)
                    done += span
            return carry
        carry = lax.fori_loop(0, nk // 4, quad, init)
        done = (nk // 4) * 4
        carry = lax.cond((nk & 2) != 0, lambda c: body(pl.multiple_of(done * lc, 2 * lc), 2 * lc, c), lambda c: c,
                         carry)
        done = done + (nk & 2)
        return lax.cond((nk & 1) != 0, lambda c: body(pl.multiple_of(done * lc, lc), lc, c), lambda c: c, carry)

    def score_span(off, rows, carry):
        kic = kib_ref[0, pl.ds(off, rows), :]
        acc = jnp.zeros((rows, tq), F32)
        for h in range(IDX_HEADS):
            s = lax.dot_general(kic, qi_ref[0, :, h * IDX_DIM:(h + 1) * IDX_DIM], NT_DIMS,
                                preferred_element_type=F32)
            acc = acc + wit_ref[h:h + 1, :] * jnp.maximum(s, 0.0)
        l_pos = off + lax.broadcasted_iota(I32, (rows, tq), 0)
        visible = ((l_pos >> CHUNK_SHIFT) <= q_chunk) & (l_pos < l_valid)
        score = jnp.where(visible, acc, -jnp.inf)
        sc_ref[pl.ds(off, rows), :] = score
        sch_ref[pl.ds(off, rows), :] = score.astype(BF16)
        return carry

    wide_chunk_loop(score_span, 0)

    def count(pred):
        ways = 4 * SUBLANES

        def body(kc, part):
            off = pl.multiple_of(kc * lc, lc)
            hit = jnp.where(pred(sc_ref[pl.ds(off, lc), :]), 1.0, 0.0)
            return part + jnp.sum(hit.reshape(lc // ways, ways, tq), axis=0)
        return jnp.sum(chunk_loop(body, jnp.zeros((ways, tq), F32)), axis=0, keepdims=True)

    def count_coarse(cand):
        ways = 2 * BF16_ROWS
        assert sc_ref.shape[0] // ways <= 256

        def body(kc, part):
            off = pl.multiple_of(kc * lc, lc)
            hit = jnp.where(sch_ref[pl.ds(off, lc), :] >= cand, jnp.ones((), BF16), jnp.zeros((), BF16))
            hit = hit.reshape(lc // ways, ways, tq)
            terms = [hit[j] for j in range(lc // ways)]
            while len(terms) > 1:
                terms = [a + b for a, b in zip(terms[::2], terms[1::2])]
            return part + terms[0]
        part = chunk_loop(body, jnp.zeros((ways, tq), BF16))
        return jnp.sum(part.astype(F32), axis=0, keepdims=True)

    def bisect(count_ge, key_to_value, lo, hi, steps):
        def step(_, carry):
            lo, hi = carry
            mid = (lo >> 1) + (hi >> 1) + (lo & hi & 1)
            ok = count_ge(key_to_value(mid)) >= topk
            return jnp.where(ok, mid, lo), jnp.where(ok, hi, mid)
        return lax.fori_loop(0, steps, step, (lo, hi))[0]

    def coarse_key_to_f32_key(k16):
        return jnp.where(k16 >= 0, k16 << 16, (k16 << 16) | 0xFFFF)

    full = lambda v: jnp.full((1, tq), v, I32)
    k16 = bisect(count_coarse, lambda k: _key_to_float(coarse_key_to_f32_key(k)).astype(BF16),
                 full(KEY_NEG_INF >> 16), full((KEY_POS_INF >> 16) + 1), 16)
    kb = coarse_key_to_f32_key(k16)
    span = 1 << 16
    lo = bisect(lambda cand: count(lambda blk: blk >= cand), _key_to_float,
                jnp.maximum(kb, KEY_NEG_INF + span) - span, jnp.minimum(kb, KEY_POS_INF + 1 - span) + span, 17)
    thr = _key_to_float(lo)
    n_above = count(lambda blk: blk > thr)
    n_ties = topk - n_above

    tri = jnp.where(lax.broadcasted_iota(I32, (lc, lc), 0) >= lax.broadcasted_iota(I32, (lc, lc), 1),
                    1.0, 0.0).astype(BF16)

    def logits_span(off, rows, carry):
        ties_before, m8 = carry
        biases = []
        for r0 in range(0, rows, lc):
            blk = sc_ref[pl.ds(off + r0, lc), :]
            tie = blk == thr
            tie_rank = (jnp.dot(tri, jnp.where(tie, 1.0, 0.0).astype(BF16), preferred_element_type=F32)
                        + ties_before)
            bias = jnp.where(blk > thr, 0.0, jnp.where(tie, jnp.where(tie_rank <= n_ties, 0.0, NEG_BIG), NEG_BIG))
            biases.append(jnp.where(blk == -jnp.inf, NEG_BIG, bias))
            ties_before = tie_rank[lc - 1:lc, :]
        bias = jnp.concatenate(biases, axis=0)
        kc_b = kb_ref[0, pl.ds(off, rows), :]
        m_rows = []
        for h in range(N_HEADS):
            s = lax.dot_general(kc_b, q_ref[0, :, h * HEAD_DIM:(h + 1) * HEAD_DIM], NT_DIMS,
                                preferred_element_type=F32) + bias
            s_ref[h, pl.ds(off, rows), :] = s
            s8 = jnp.max(s.reshape(rows // SUBLANES, SUBLANES, tq), axis=0)
            m_rows.append(jnp.maximum(m8[h * SUBLANES:(h + 1) * SUBLANES], s8))
        return ties_before, jnp.concatenate(m_rows, axis=0)

    _, m8 = wide_chunk_loop(logits_span,
                            (jnp.zeros((1, tq), F32), jnp.full((N_HEADS * SUBLANES, tq), NEG_BIG, F32)))
    m_all = jnp.max(m8.reshape(N_HEADS, SUBLANES, tq), axis=1)

    for acc_ref in acc_refs:
        acc_ref[...] = jnp.zeros(acc_ref.shape, F32)

    def pv_chunk(kc, carry):
        off = pl.multiple_of(kc * lc, lc)
        vt_c = vt_ref[kc]
        for h in range(N_HEADS):
            p = jnp.exp2(s_ref[h, pl.ds(off, lc), :] - m_all[h:h + 1, :])
            acc_refs[h][...] += jnp.dot(vt_c, p.astype(BF16), preferred_element_type=F32)
        return carry

    chunk_loop(pv_chunk, 0)

    for h in range(N_HEADS):
        acc = acc_refs[h][...]
        out_ref[h * HEAD_DIM:(h + 1) * HEAD_DIM, :] = acc[0:HEAD_DIM] / acc[HEAD_DIM:HEAD_DIM + 1]
    o_ref[0] = out_ref[...].T.astype(BF16)


def _dsa(q, qi, wit, kb, kib, vt, *, tq, lc, causal, l_valid, q_pos0, topk):
    bsz, tq_tot, aw = q.shape
    l_tot = kb.shape[1]
    nq = tq_tot // tq
    nkc = l_tot // lc
    kern = functools.partial(_dsa_kernel, tq=tq, lc=lc, nk_static=nkc, causal=causal, l_valid=l_valid,
                             q_pos0=q_pos0, topk=topk)
    return pl.pallas_call(
        kern,
        grid=(bsz, nq),
        in_specs=[
            pl.BlockSpec((1, tq, aw), lambda b, i: (b, i, 0)),
            pl.BlockSpec((1, tq, aw), lambda b, i: (b, i, 0)),
            pl.BlockSpec((IDX_HEADS, tq), lambda b, i: (0, b * nq + i)),
            pl.BlockSpec((1, l_tot, HEAD_DIM), lambda b, i: (b, 0, 0)),
            pl.BlockSpec((1, l_tot, IDX_DIM), lambda b, i: (b, 0, 0)),
            pl.BlockSpec((nkc, VT_ROWS, lc), lambda b, i: (b, 0, 0)),
        ],
        out_specs=pl.BlockSpec((1, tq, aw), lambda b, i: (b, i, 0)),
        out_shape=jax.ShapeDtypeStruct((bsz, tq_tot, aw), BF16),
        scratch_shapes=[
            pltpu.VMEM((l_tot, tq), F32),
            pltpu.VMEM((l_tot, tq), BF16),
            pltpu.VMEM((N_HEADS, l_tot, tq), F32),
            pltpu.VMEM((aw, tq), F32),
        ] + [pltpu.VMEM((VT_ROWS, tq), F32) for _ in range(N_HEADS)],
        compiler_params=_params(("parallel", "parallel")),
        name="dsa",
    )(q, qi, wit, kb, kib, vt)


def _pool_kernel(u_ref, pre_ref, pw_ref, sc_ref, o_ref, ext_ref, *, t_len, tt, pos0):
    pad = pre_ref.shape[1]
    gw = pw_ref.shape[1]
    ext_ref[0:pad, :] = pre_ref[0]
    ext_ref[pad:pad + t_len, :] = u_ref[0]
    row = lax.broadcasted_iota(I32, (tt, gw), 0)
    for t in range(t_len // tt):
        r0 = t * tt
        for g, w in enumerate(POOL_WINDOWS):
            cols = slice(g * gw, (g + 1) * gw)
            cur = ext_ref[pad + r0:pad + r0 + tt, cols]
            wsum = cur
            for j in range(1, w):
                wsum = wsum + ext_ref[pad + r0 - j:pad + r0 - j + tt, cols]
            cnt = jnp.minimum(w, pos0 + r0 + 1 + row).astype(F32)
            diff = (wsum / cnt - cur).astype(BF16)
            y = jnp.dot(diff, pw_ref[g], preferred_element_type=F32)
            o_ref[0, r0:r0 + tt, cols] = (y * sc_ref[:, cols]).astype(BF16)


def _pool(u, prefix, pool_w_b, pool_scale, *, pos0):
    bsz, t_len, c = u.shape
    pad = prefix.shape[1]
    tt = min(t_len, 256)
    return pl.pallas_call(
        functools.partial(_pool_kernel, t_len=t_len, tt=tt, pos0=pos0),
        grid=(bsz,),
        in_specs=[pl.BlockSpec((1, t_len, c), lambda b: (b, 0, 0)),
                  pl.BlockSpec((1, pad, c), lambda b: (b, 0, 0)),
                  pl.BlockSpec(pool_w_b.shape, lambda b: (0, 0, 0)),
                  pl.BlockSpec((1, c), lambda b: (0, 0))],
        out_specs=pl.BlockSpec((1, t_len, c), lambda b: (b, 0, 0)),
        out_shape=jax.ShapeDtypeStruct((bsz, t_len, c), BF16),
        scratch_shapes=[pltpu.VMEM((pad + t_len, c), F32)],
        compiler_params=_params(("parallel",)),
        name="pool",
    )(u, prefix, pool_w_b, pool_scale)


ROW_ALIGN = 16


def _tail_kernel(x_ref, att_ref, pool_ref, g0_ref, b0_ref, woa_ref, wob_ref, g1_ref, b1_ref,
                 wrh_ref, wrl_ref, br_ref,
                 carry0_ref, h1_ref, cpos_ref, gw_ref, seg_ref, len_ref, dst_ref, used_ref, carry_ref, *, tm, sub):
    @pl.when(pl.program_id(0) == 0)
    def _():
        carry_ref[...] = carry0_ref[...]

    chunk_lens = []
    for s in range(sub):
        rows = slice(s * tm, (s + 1) * tm)
        chunk_lens.append(_tail_tile(x_ref[rows, :], att_ref[rows, :], pool_ref[rows, :], g0_ref, b0_ref, woa_ref,
                                     wob_ref, g1_ref, b1_ref, wrh_ref, wrl_ref, br_ref,
                                     h1_ref.at[rows, :], cpos_ref.at[:, rows], gw_ref.at[:, rows], seg_ref.at[s],
                                     len_ref.at[s], tm=tm))
    carry = carry_ref[...]
    for s in range(sub):
        dst_ref[s] = carry
        carry = carry + chunk_lens[s]
    carry_ref[...] = carry
    used_ref[...] = carry


def _tail_tile(x, att, pool, g0_ref, b0_ref, woa_ref, wob_ref, g1_ref, b1_ref, wrh_ref, wrl_ref, br_ref,
               h1_ref, cpos_ref, gw_ref, seg_ref, len_ref, *, tm):
    h = _layer_norm(x, g0_ref[...], b0_ref[...])
    mix = (jnp.dot(att, woa_ref[...], preferred_element_type=F32)
           + jnp.dot(pool, wob_ref[...], preferred_element_type=F32))
    h1 = _layer_norm(DEEPNORM_ALPHA * h + mix, g1_ref[...], b1_ref[...])
    h1_ref[...] = h1

    hh = h1.astype(BF16)
    hl = (h1 - hh.astype(F32)).astype(BF16)
    logits = (lax.dot_general(wrh_ref[...], hh, NT_DIMS, preferred_element_type=F32)
              + lax.dot_general(wrh_ref[...], hl, NT_DIMS, preferred_element_type=F32)
              + lax.dot_general(wrl_ref[...], hh, NT_DIMS, preferred_element_type=F32)
              + br_ref[...])

    e_iota = lax.broadcasted_iota(I32, (N_EXPERTS, tm), 0).astype(F32)
    work = logits
    vals, hots = [], []
    for k in range(TOP_K):
        m = jnp.max(work, axis=0, keepdims=True)
        idx = jnp.min(jnp.where(work == m, e_iota, float(N_EXPERTS)), axis=0, keepdims=True)
        hot = e_iota == idx
        vals.append(m)
        hots.append(jnp.where(hot, 1.0, 0.0))
        work = jnp.where(hot, -jnp.inf, work)
    exps = [jnp.exp(v - vals[0]) for v in vals]
    den = exps[0] + exps[1] + exps[2] + exps[3]
    for k in range(TOP_K):
        gw_ref[k:k + 1, :] = exps[k] / den

    hot_all = (hots[0] + hots[1] + hots[2] + hots[3]).astype(BF16)
    count = jnp.dot(hot_all, jnp.ones((tm, LANES), BF16), preferred_element_type=F32)
    chunk_len = jnp.ceil(count * (1.0 / ROW_ALIGN)) * ROW_ALIGN
    lower = jnp.where(lax.broadcasted_iota(I32, (N_EXPERTS, N_EXPERTS), 1)
                      < lax.broadcasted_iota(I32, (N_EXPERTS, N_EXPERTS), 0), 1.0, 0.0).astype(BF16)
    seg_base = jnp.dot(lower, chunk_len.astype(BF16), preferred_element_type=F32)
    before = jnp.where(lax.broadcasted_iota(I32, (tm, tm), 0) < lax.broadcasted_iota(I32, (tm, tm), 1),
                       1.0, 0.0).astype(BF16)
    slot = jnp.dot(hot_all, before, preferred_element_type=F32) + jnp.tile(seg_base, (1, tm // LANES))
    for k in range(TOP_K):
        cpos_ref[k:k + 1, :] = jnp.sum(hots[k] * slot, axis=0, keepdims=True).astype(I32)
    seg_ref[...] = seg_base
    len_ref[...] = chunk_len
    return chunk_len


def _tail(x2d, att, pool, g0, b0, woa, wob, g1, b1, wrh, wrl, br, carry0, *, tm):
    n, d = x2d.shape
    aw = att.shape[1]
    nt = n // tm
    sub = 2 if nt % 2 == 0 else 1
    ts = sub * tm
    row = lambda i: (i, 0)
    const = lambda i: (0, 0)
    col = lambda i: (0, i)
    tab = lambda i: (i, 0, 0)
    tab_shape = jax.ShapeDtypeStruct((nt, N_EXPERTS, LANES), F32)
    tab_spec = pl.BlockSpec((sub, N_EXPERTS, LANES), tab)
    return pl.pallas_call(
        functools.partial(_tail_kernel, tm=tm, sub=sub),
        grid=(nt // sub,),
        in_specs=[pl.BlockSpec((ts, d), row), pl.BlockSpec((ts, aw), row), pl.BlockSpec((ts, d - aw), row),
                  pl.BlockSpec((1, d), const), pl.BlockSpec((1, d), const),
                  pl.BlockSpec(woa.shape, const), pl.BlockSpec(wob.shape, const),
                  pl.BlockSpec((1, d), const), pl.BlockSpec((1, d), const),
                  pl.BlockSpec(wrh.shape, const), pl.BlockSpec(wrl.shape, const), pl.BlockSpec(br.shape, const),
                  pl.BlockSpec((N_EXPERTS, LANES), const)],
        out_specs=(pl.BlockSpec((ts, d), row), pl.BlockSpec((TOP_K, ts), col), pl.BlockSpec((TOP_K, ts), col),
                   tab_spec, tab_spec, tab_spec, pl.BlockSpec((N_EXPERTS, LANES), const)),
        out_shape=(jax.ShapeDtypeStruct((n, d), F32), jax.ShapeDtypeStruct((TOP_K, n), I32),
                   jax.ShapeDtypeStruct((TOP_K, n), F32), tab_shape, tab_shape, tab_shape,
                   jax.ShapeDtypeStruct((N_EXPERTS, LANES), F32)),
        scratch_shapes=[pltpu.VMEM((N_EXPERTS, LANES), F32)],
        compiler_params=_params(("arbitrary",)),
        name="tail",
    )(x2d, att, pool, g0, b0, woa, wob, g1, b1, wrh, wrl, br, carry0)


G_BLOCK = 256


def _stage_rows(tm):
    raw = TOP_K * tm + (ROW_ALIGN - 1) * N_EXPERTS
    return -(-raw // G_BLOCK) * G_BLOCK


def _chunk_copies(seg_ref, len_ref, dst_ref, tile, make_copy, act):
    def per_expert(e, carry):
        idx = tile * N_EXPERTS + e
        seg, ln, dst = seg_ref[idx], len_ref[idx], dst_ref[idx]

        @pl.when(ln > 0)
        def _():
            act(make_copy(pl.multiple_of(seg, ROW_ALIGN), pl.multiple_of(dst, ROW_ALIGN),
                          pl.multiple_of(ln, ROW_ALIGN)))
        return carry

    lax.fori_loop(0, N_EXPERTS, per_expert, 0)


def _by_staged_blocks(seg_ref, len_ref, tile, n_full, body):
    last = tile * N_EXPERTS + N_EXPERTS - 1
    total = seg_ref[last] + len_ref[last]
    pl.when(total <= (n_full - 1) * G_BLOCK)(lambda: body(n_full - 1))
    pl.when(total > (n_full - 1) * G_BLOCK)(lambda: body(n_full))


def _route_onehot(iota, pos_list, val_list):
    g = jnp.zeros(iota.shape, F32)
    for pos, val in zip(pos_list, val_list):
        g = jnp.where(iota == pos, val, g)
    return g.astype(BF16)


def _dispatch_kernel(seg_ref, len_ref, dst_ref, fill_off_ref, fill_len_ref, h1_ref, cpos_ref, *rest, tm, fill):
    xs_hbm, stage_ref, zero_ref, sem, fill_sem = rest[-5:]
    tile = pl.program_id(0)
    slot = tile % 2
    h1b = h1_ref[...].astype(BF16)
    pos_rows = [cpos_ref[k:k + 1, :] for k in range(TOP_K)]

    def stage_blocks(n_blk):
        for blk in range(n_blk):
            iota = blk * G_BLOCK + lax.broadcasted_iota(I32, (G_BLOCK, tm), 0)
            g = _route_onehot(iota, pos_rows, [1.0] * TOP_K)
            stage_ref[slot, blk * G_BLOCK:(blk + 1) * G_BLOCK, :] = jnp.dot(
                g, h1b, preferred_element_type=F32).astype(BF16)

    _by_staged_blocks(seg_ref, len_ref, tile, stage_ref.shape[1] // G_BLOCK, stage_blocks)

    def copy_from(buf):
        def make_copy(seg, dst, size):
            return pltpu.make_async_copy(stage_ref.at[buf, pl.ds(seg, size), :], xs_hbm.at[pl.ds(dst, size), :],
                                         sem.at[buf])
        return make_copy

    _chunk_copies(seg_ref, len_ref, dst_ref, tile, copy_from(slot), lambda c: c.start())

    @pl.when(tile > 0)
    def _():
        _chunk_copies(seg_ref, len_ref, dst_ref, tile - 1, copy_from(1 - slot), lambda c: c.wait())

    def fill_copy(j):
        size = pl.multiple_of(fill_len_ref[j], ROW_ALIGN)
        return pltpu.make_async_copy(zero_ref.at[pl.ds(0, size), :],
                                     xs_hbm.at[pl.ds(pl.multiple_of(fill_off_ref[j], ROW_ALIGN), size), :], fill_sem)

    def each_fill(act):
        def body(j, carry):
            pl.when(fill_len_ref[j] > 0)(lambda: act(fill_copy(j)))
            return carry
        lax.fori_loop(0, fill_len_ref.shape[0], body, 0)

    if fill:
        @pl.when(tile == 0)
        def _():
            zero_ref[...] = jnp.zeros(zero_ref.shape, BF16)
            each_fill(lambda c: c.start(priority=1))

    @pl.when(tile == pl.num_programs(0) - 1)
    def _():
        _chunk_copies(seg_ref, len_ref, dst_ref, tile, copy_from(slot), lambda c: c.wait())
        if fill:
            each_fill(lambda c: c.wait())


def _dispatch(seg, ln, dst, fill_off, fill_len, h1, cpos, xs_prev, *, tm, p_rows, fill_rows):
    n, d = h1.shape
    in_specs = [pl.BlockSpec((tm, d), lambda i, *_: (i, 0)), pl.BlockSpec((TOP_K, tm), lambda i, *_: (0, i))]
    operands = [seg, ln, dst, fill_off, fill_len, h1, cpos]
    aliases = {}
    if xs_prev is not None:
        in_specs.append(pl.BlockSpec(memory_space=pl.ANY))
        aliases = {len(operands): 0}
        operands.append(xs_prev)
    return pl.pallas_call(
        functools.partial(_dispatch_kernel, tm=tm, fill=xs_prev is None),
        grid_spec=pltpu.PrefetchScalarGridSpec(
            num_scalar_prefetch=5,
            grid=(n // tm,),
            in_specs=in_specs,
            out_specs=pl.BlockSpec(memory_space=pl.ANY),
            scratch_shapes=[pltpu.VMEM((2, _stage_rows(tm), d), BF16), pltpu.VMEM((fill_rows, d), BF16),
                            pltpu.SemaphoreType.DMA((2,)), pltpu.SemaphoreType.DMA(())],
        ),
        out_shape=jax.ShapeDtypeStruct((p_rows, d), BF16),
        input_output_aliases=aliases,
        compiler_params=_params(("arbitrary",)),
        name="dispatch",
    )(*operands)


FF_CHUNK = 256


def _moe_kernel(te_ref, tv_ref, x_ref, w1_ref, b1_ref, w2_ref, b2_ref, y_ref, w1b_ref, w2b_ref, act_ref):
    i = pl.program_id(0)
    d_ff = w2_ref.shape[1]
    valid = tv_ref[i]

    @pl.when((i == 0) | (te_ref[i] != te_ref[jnp.maximum(i - 1, 0)]))
    def _():
        w1b_ref[...] = w1_ref[0].astype(BF16)
        w2b_ref[...] = w2_ref[0].astype(BF16)

    @pl.when(valid > 0)
    def _():
        xb = x_ref[...]
        for j in range(d_ff // FF_CHUNK):
            gs = slice(j * FF_CHUNK, (j + 1) * FF_CHUNK)
            us = slice(d_ff + j * FF_CHUNK, d_ff + (j + 1) * FF_CHUNK)
            gate = jnp.dot(xb, w1b_ref[:, gs], preferred_element_type=F32) + b1_ref[0, :, gs]
            up = jnp.dot(xb, w1b_ref[:, us], preferred_element_type=F32) + b1_ref[0, :, us]
            gate = jnp.minimum(gate, SWIGLU_LIMIT)
            up = jnp.clip(up, -SWIGLU_LIMIT, SWIGLU_LIMIT)
            act_ref[:, gs] = ((up + 1.0) * (gate * jax.nn.sigmoid(SWIGLU_ALPHA * gate))).astype(BF16)
        y = jnp.dot(act_ref[...], w2b_ref[...], preferred_element_type=F32) + b2_ref[0]
        y_ref[...] = y.astype(BF16)

    @pl.when(valid <= 0)
    def _():
        y_ref[...] = jnp.zeros(y_ref.shape, BF16)


def _moe(tile_expert, tile_valid, xs, w1, b1, w2, b2, *, tmoe):
    p, d = xs.shape
    e, _, ff2 = w1.shape
    d_ff = ff2 // 2
    return pl.pallas_call(
        _moe_kernel,
        grid_spec=pltpu.PrefetchScalarGridSpec(
            num_scalar_prefetch=2,
            grid=(p // tmoe,),
            in_specs=[pl.BlockSpec((tmoe, d), lambda i, te, tv: (i, 0)),
                      pl.BlockSpec((1, d, ff2), lambda i, te, tv: (te[i], 0, 0)),
                      pl.BlockSpec((1, 1, ff2), lambda i, te, tv: (te[i], 0, 0)),
                      pl.BlockSpec((1, d_ff, d), lambda i, te, tv: (te[i], 0, 0)),
                      pl.BlockSpec((1, 1, d), lambda i, te, tv: (te[i], 0, 0))],
            out_specs=pl.BlockSpec((tmoe, d), lambda i, te, tv: (i, 0)),
            scratch_shapes=[pltpu.VMEM((d, ff2), BF16), pltpu.VMEM((d_ff, d), BF16), pltpu.VMEM((tmoe, d_ff), BF16)],
        ),
        out_shape=jax.ShapeDtypeStruct((p, d), BF16),
        compiler_params=_params(("arbitrary",)),
        name="moe",
    )(tile_expert, tile_valid, xs, w1, b1.reshape(e, 1, ff2), w2, b2.reshape(e, 1, d))


def _combine_kernel(seg_ref, len_ref, dst_ref, h1_ref, cpos_ref, gw_ref, g2_ref, b2_ref, ys_hbm, o_ref,
                    stage_ref, gate_ref, sem, *, tm):
    tile = pl.program_id(0)
    slot = tile % 2

    def copy_into(buf):
        def make_copy(seg, dst, size):
            return pltpu.make_async_copy(ys_hbm.at[pl.ds(dst, size), :], stage_ref.at[buf, pl.ds(seg, size), :],
                                         sem.at[buf])
        return make_copy

    @pl.when(tile == 0)
    def _():
        stage_ref[...] = jnp.zeros(stage_ref.shape, BF16)
        _chunk_copies(seg_ref, len_ref, dst_ref, tile, copy_into(slot), lambda c: c.start())

    @pl.when(tile + 1 < pl.num_programs(0))
    def _():
        _chunk_copies(seg_ref, len_ref, dst_ref, tile + 1, copy_into(1 - slot), lambda c: c.start())

    cpos = cpos_ref[...]
    gw = gw_ref[...]
    pos_cols = [cpos[:, k:k + 1] for k in range(TOP_K)]
    gw_cols = [gw[:, k:k + 1] for k in range(TOP_K)]

    def finish(n_blk):
        _chunk_copies(seg_ref, len_ref, dst_ref, tile, copy_into(slot), lambda c: c.wait())
        m = None
        for blk in range(n_blk):
            iota = blk * G_BLOCK + lax.broadcasted_iota(I32, (tm, G_BLOCK), 1)
            gate = _route_onehot(iota, pos_cols, gw_cols)
            part = jnp.dot(gate, stage_ref[slot, blk * G_BLOCK:(blk + 1) * G_BLOCK, :], preferred_element_type=F32)
            m = part if m is None else part + m
        o_ref[...] = _layer_norm(DEEPNORM_ALPHA * h1_ref[...] + m, g2_ref[...], b2_ref[...])

    _by_staged_blocks(seg_ref, len_ref, tile, stage_ref.shape[1] // G_BLOCK, finish)


def _combine(seg, ln, dst, h1, cpos_nt, gw_nt, g2, b2, ys, *, tm):
    n, d = h1.shape
    return pl.pallas_call(
        functools.partial(_combine_kernel, tm=tm),
        grid_spec=pltpu.PrefetchScalarGridSpec(
            num_scalar_prefetch=3,
            grid=(n // tm,),
            in_specs=[pl.BlockSpec((tm, d), lambda i, *_: (i, 0)),
                      pl.BlockSpec((tm, TOP_K), lambda i, *_: (i, 0)), pl.BlockSpec((tm, TOP_K), lambda i, *_: (i, 0)),
                      pl.BlockSpec((1, d), lambda i, *_: (0, 0)), pl.BlockSpec((1, d), lambda i, *_: (0, 0)),
                      pl.BlockSpec(memory_space=pl.ANY)],
            out_specs=pl.BlockSpec((tm, d), lambda i, *_: (i, 0)),
            scratch_shapes=[pltpu.VMEM((2, _stage_rows(tm), d), BF16), pltpu.VMEM((tm, _stage_rows(tm)), BF16),
                            pltpu.SemaphoreType.DMA((2,))],
        ),
        out_shape=jax.ShapeDtypeStruct((n, d), F32),
        compiler_params=_params(("arbitrary",)),
        name="combine",
    )(seg, ln, dst, h1, cpos_nt, gw_nt, g2, b2, ys)


def _block_tails(groups, wts, *, tm, tmoe):
    carry = jnp.zeros((N_EXPERTS, LANES), F32)
    plans, max_rows, used_first = [], 0, None
    for x2d, att, pool in groups:
        n = x2d.shape[0]
        tm_g = min(tm, n)
        h1, cpos, gw_t, seg_t, len_t, dst_t, carry = _tail(
            x2d, att, pool, wts["g0"], wts["b0"], wts["woa"], wts["wob"], wts["g1"], wts["b1"],
            wts["wrh"], wts["wrl"], wts["br"], carry, tm=tm_g)
        plans.append((tm_g, h1, cpos, gw_t, seg_t, len_t, dst_t))
        max_rows += TOP_K * n + (ROW_ALIGN - 1) * N_EXPERTS * (n // tm_g)
        used_first = carry[:, 0].astype(I32) if used_first is None else used_first
    used = carry[:, 0].astype(I32)
    cap = ((used + tmoe - 1) // tmoe) * tmoe
    ends = jnp.cumsum(cap)
    offs = ends - cap
    n_tiles = -(-max_rows // tmoe) + N_EXPERTS
    tile_start = jnp.arange(n_tiles, dtype=I32) * tmoe
    tile_expert = jnp.minimum(jnp.sum((ends[None, :] <= tile_start[:, None]).astype(I32), axis=1), N_EXPERTS - 1)
    tile_valid = jnp.clip(offs[tile_expert] + used[tile_expert] - tile_start, 0, tmoe).astype(I32)
    tile_valid = jnp.where(tile_start < ends[-1], tile_valid, 0)

    written_end = (offs + used_first)[tile_expert]
    fill_off = jnp.where(tile_start < ends[-1], jnp.clip(written_end, tile_start, tile_start + tmoe), tile_start)
    fill_len = tile_start + tmoe - fill_off
    no_fill = jnp.zeros((1,), I32)

    xs, tables = None, []
    for tm_g, h1, cpos, gw_t, seg_t, len_t, dst_t in plans:
        seg = seg_t[:, :, 0].astype(I32).reshape(-1)
        ln = len_t[:, :, 0].astype(I32).reshape(-1)
        dst = (dst_t[:, :, 0].astype(I32) + offs[None, :]).reshape(-1)
        tables.append((seg, ln, dst))
        fills = (fill_off, fill_len) if xs is None else (no_fill, no_fill)
        xs = _dispatch(seg, ln, dst, *fills, h1, cpos, xs, tm=tm_g, p_rows=n_tiles * tmoe, fill_rows=tmoe)
    ys = _moe(tile_expert, tile_valid, xs, wts["w1e"], wts["b1e"], wts["w2e"], wts["b2e"], tmoe=tmoe)
    return [_combine(seg, ln, dst, h1, cpos.T, gw_t.T, wts["g2"], wts["b2"], ys, tm=tm_g)
            for (seg, ln, dst), (tm_g, h1, cpos, gw_t, _, _, _) in zip(tables, plans)]


def kernel(x_prompt, x_sample, cache_k, cache_v, cache_kidx, state_pool, ln0_g, ln0_b, w_in, w_o,
           pool_w, pool_scale, ln1_g, ln1_b, w_router, b_router, w1, b1, w2, b2, ln2_g, ln2_b):
    bp, s_len, d = x_prompt.shape
    bs, t_len, _ = x_sample.shape
    l_past = cache_k.shape[2]
    aw = N_HEADS * HEAD_DIM
    pw = d - aw
    lyr = 0
    lc = 256

    k_off = aw
    v_off = k_off + HEAD_DIM
    qi_off = v_off + HEAD_DIM
    ki_off = qi_off + IDX_HEADS * IDX_DIM
    wi_off = ki_off + IDX_DIM
    u_off = wi_off + IDX_HEADS
    win = w_in[lyr]
    wa = jnp.concatenate([win[:, 0:k_off], win[:, qi_off:ki_off], win[:, u_off:u_off + pw]], axis=1).astype(BF16)
    wb = jnp.concatenate([win[:, k_off:v_off], win[:, v_off:qi_off], win[:, ki_off:wi_off], win[:, wi_off:u_off],
                          jnp.zeros((d, HEAD_DIM - IDX_HEADS), F32)], axis=1).astype(BF16)

    g0 = ln0_g.reshape(1, d)
    b0 = ln0_b.reshape(1, d)
    wrt = w_router[lyr].T
    wrh = wrt.astype(BF16)
    wts = dict(
        g0=g0, b0=b0,
        woa=w_o[lyr][:aw].astype(BF16), wob=w_o[lyr][aw:].astype(BF16),
        g1=ln1_g[lyr].reshape(1, d), b1=ln1_b[lyr].reshape(1, d),
        wrh=wrh, wrl=(wrt - wrh.astype(F32)).astype(BF16), br=b_router[lyr].reshape(N_EXPERTS, 1),
        w1e=w1[lyr], b1e=b1[lyr], w2e=w2[lyr], b2e=b2[lyr],
        g2=ln2_g[lyr].reshape(1, d), b2=ln2_b[lyr].reshape(1, d),
    )
    pool_w_b = pool_w[lyr].astype(BF16)
    pool_sc = pool_scale[lyr].reshape(1, pw)

    xp = x_prompt.reshape(bp * s_len, d)
    q, qi, u, k, v, ki, kb, kib, vt, wit = _proj(xp, g0, b0, wa, wb, tm=512, lc=lc)
    att_p = _dsa(q.reshape(bp, s_len, aw), qi.reshape(bp, s_len, aw), wit,
                 kb.reshape(bp, s_len, HEAD_DIM), kib.reshape(bp, s_len, IDX_DIM), vt,
                 tq=256, lc=lc, causal=True, l_valid=s_len, q_pos0=0, topk=min(TOPK_MAX, s_len // 4))
    u_p = u.reshape(bp, s_len, pw)
    pool_p = _pool(u_p, jnp.zeros((bp, POOL_PAST + 1, pw), F32), pool_w_b, pool_sc, pos0=0)

    xs = x_sample.reshape(bs * t_len, d)
    qs, qis, us, kn, vn, kin, _, _, _, wits = _proj(xs, g0, b0, wa, wb, tm=512, lc=lc)
    l_all = l_past + t_len
    l_pad = -(-l_all // lc) * lc
    tq_s = LANES
    pad_keys = lambda a: jnp.pad(a, ((0, 0), (0, l_pad - l_all), (0, 0)))
    k_all = pad_keys(jnp.concatenate([cache_k[lyr], kn.reshape(bs, t_len, HEAD_DIM)], axis=1))
    v_all = pad_keys(jnp.concatenate([cache_v[lyr], vn.reshape(bs, t_len, HEAD_DIM)], axis=1))
    ki_all = pad_keys(jnp.concatenate([cache_kidx[lyr], kin.reshape(bs, t_len, IDX_DIM)], axis=1))
    vt_all = v_all.reshape(bs, l_pad // lc, lc, HEAD_DIM).transpose(0, 1, 3, 2).reshape(-1, HEAD_DIM, lc)
    vt_all = jnp.concatenate([vt_all, jnp.broadcast_to(_denominator_rows(lc), (vt_all.shape[0], VT_ROWS - HEAD_DIM, lc))],
                             axis=1)
    pad_q = lambda a: jnp.pad(a.reshape(bs, t_len, aw), ((0, 0), (0, tq_s - t_len), (0, 0)))
    wit_s = jnp.pad(wits.reshape(IDX_HEADS, bs, t_len), ((0, 0), (0, 0), (0, tq_s - t_len))).reshape(IDX_HEADS, -1)
    att_s = _dsa(pad_q(qs), pad_q(qis), wit_s, k_all.astype(BF16), ki_all.astype(BF16), vt_all.astype(BF16),
                 tq=tq_s, lc=lc, causal=False, l_valid=l_all, q_pos0=l_past, topk=min(TOPK_MAX, l_all // 4))
    att_s = att_s[:, :t_len].reshape(bs * t_len, aw)
    us3 = us.reshape(bs, t_len, pw)
    prefix_s = jnp.concatenate([jnp.zeros((bs, 1, pw), F32), state_pool[lyr]], axis=1)
    pool_s = _pool(us3, prefix_s, pool_w_b, pool_sc, pos0=l_past)

    y_p, y_s = _block_tails([(xp, att_p.reshape(bp * s_len, aw), pool_p.reshape(bp * s_len, pw)),
                             (xs, att_s, pool_s.reshape(bs * t_len, pw))], wts, tm=512, tmoe=1024)

    pool_state_p = u_p[:, s_len - POOL_PAST:]
    pool_state_s = jnp.concatenate([state_pool[lyr], us3], axis=1)[:, -POOL_PAST:]
    return (y_p.reshape(bp, s_len, d), y_s.reshape(bs, t_len, d),
            k.reshape(1, bp, s_len, HEAD_DIM), v.reshape(1, bp, s_len, HEAD_DIM),
            ki.reshape(1, bp, s_len, IDX_DIM), pool_state_p[None],
            kn.reshape(1, bs, t_len, HEAD_DIM), vn.reshape(1, bs, t_len, HEAD_DIM),
            kin.reshape(1, bs, t_len, IDX_DIM), pool_state_s[None])
```
